```python
import jax, jax.numpy as jnp
from jax import lax
import numpy as np

D_MODEL = 2048
BATCH = 1
SEQ = 8192
DEPTH = 4

CTX_LEN = 256
GRID_W = 64
HEAD_DIM = 128
ATTN_HEADS = D_MODEL // (2 * HEAD_DIM)
ATTN_KV_HEADS = ATTN_HEADS // 4
ATTN_GROUP = ATTN_HEADS // ATTN_KV_HEADS
ATTN_WIDTH = ATTN_HEADS * HEAD_DIM
KV_WIDTH = ATTN_KV_HEADS * HEAD_DIM
HG_DK = 128
HG_DV = 128
HG_HEADS = D_MODEL // (2 * HG_DV)
HG_KWIDTH = HG_HEADS * HG_DK
HG_WIDTH = HG_HEADS * HG_DV
MIX_WIDTH = ATTN_WIDTH + HG_WIDTH
IN_WIDTH = ATTN_WIDTH + 2 * KV_WIDTH + 3 * HG_KWIDTH + 2 * HG_WIDTH
Q_BLOCK = 128
CHUNK = 64
ROPE_THETA = 10000.0
ROPE_FREQS = HEAD_DIM // 4
D_FF = ((8 * D_MODEL // 3 + 255) // 256) * 256
CONV_W = 3
EPS = 1e-6

kernel_name = 'hybrid_gqa_hgrn2_convffn_dit'


def rms_norm(x, w):
    xf = x.astype(jnp.float32)
    y = xf * lax.rsqrt(jnp.mean(xf * xf, axis=-1, keepdims=True) + EPS)
    return (y * w.astype(jnp.float32)).astype(x.dtype)


def modulate(h, shift, scale):
    return h * (1 + scale) + shift


def axial_rope_angles(L):
    rows = L // GRID_W
    row = jnp.repeat(jnp.arange(rows, dtype=jnp.float32), GRID_W)
    col = jnp.tile(jnp.arange(GRID_W, dtype=jnp.float32), rows)
    freqs = ROPE_THETA ** (-jnp.arange(ROPE_FREQS, dtype=jnp.float32) / ROPE_FREQS)
    ang = jnp.stack([row[:, None] * freqs, col[:, None] * freqs], axis=1)
    return jnp.cos(ang), jnp.sin(ang)


def apply_rope(x, cos, sin):
    B, L, H, _ = x.shape
    xs = x.astype(jnp.float32).reshape(B, L, H, 2, 2, ROPE_FREQS)
    x1, x2 = xs[..., 0, :], xs[..., 1, :]
    c = cos[None, :, None]
    s = sin[None, :, None]
    out = jnp.stack([x1 * c - x2 * s, x2 * c + x1 * s], axis=-2)
    return out.reshape(B, L, H, HEAD_DIM).astype(x.dtype)


def gqa_attend(q, k, v):
    s = jnp.einsum('bqgrd,bkgd->bgrqk', q, k).astype(jnp.float32) * (HEAD_DIM ** -0.5)
    p = jax.nn.softmax(s, axis=-1).astype(v.dtype)
    return jnp.einsum('bgrqk,bkgd->bqgrd', p, v)


def blocked_attention(q, k, v):
    B, L = q.shape[0], q.shape[1]
    nb = L // Q_BLOCK
    qb = q.reshape(B, nb, Q_BLOCK, ATTN_KV_HEADS, ATTN_GROUP, HEAD_DIM).swapaxes(0, 1)
    o = lax.map(lambda blk: gqa_attend(blk, k, v), qb)
    return o.swapaxes(0, 1).reshape(B, L, ATTN_WIDTH)


def hgrn2_scan(q, k, v, logf, s0):
    B, L, H, _ = q.shape
    n = L // CHUNK

    def to_chunks(a):
        return a.reshape(B, n, CHUNK, H, a.shape[-1]).transpose(1, 0, 3, 2, 4)

    lower = jnp.tril(jnp.ones((CHUNK, CHUNK), dtype=bool))[:, :, None]

    def step(S, inp):
        qc, kc, vc, gc = inp
        b = jnp.cumsum(gc, axis=2)
        b_last = b[:, :, -1:, :]
        o_inter = jnp.einsum('bhtd,bhdv->bhtv', qc * jnp.exp(b), S)
        diff = b[:, :, :, None, :] - b[:, :, None, :, :]
        decay = jnp.exp(jnp.where(lower, diff, -jnp.inf))
        scores = jnp.einsum('bhtd,bhtsd,bhsd->bhts', qc, decay, kc)
        o = o_inter + jnp.einsum('bhts,bhsv->bhtv', scores, vc)
        S_new = (jnp.exp(b_last[:, :, 0, :, None]) * S
                 + jnp.einsum('bhsd,bhsv->bhdv', kc * jnp.exp(b_last - b), vc))
        return S_new, o

    s_fin, o = lax.scan(step, s0, (to_chunks(q), to_chunks(k), to_chunks(v), to_chunks(logf)))
    return o.transpose(1, 0, 3, 2, 4).reshape(B, L, H, v.shape[-1]), s_fin


def project(h, w_in, q_norm_w, k_norm_w, lb_f, lb_b):
    B, L, _ = h.shape
    p = h @ w_in
    idx = [ATTN_WIDTH, ATTN_WIDTH + KV_WIDTH, ATTN_WIDTH + 2 * KV_WIDTH,
           ATTN_WIDTH + 2 * KV_WIDTH + HG_KWIDTH, ATTN_WIDTH + 2 * KV_WIDTH + 2 * HG_KWIDTH,
           ATTN_WIDTH + 2 * KV_WIDTH + 3 * HG_KWIDTH, ATTN_WIDTH + 2 * KV_WIDTH + 3 * HG_KWIDTH + HG_WIDTH]
    aq, ak, av, hq, hff, hfb, hi, hg = jnp.split(p, idx, axis=-1)
    aq = rms_norm(aq.reshape(B, L, ATTN_HEADS, HEAD_DIM), q_norm_w)
    ak = rms_norm(ak.reshape(B, L, ATTN_KV_HEADS, HEAD_DIM), k_norm_w)
    av = av.reshape(B, L, ATTN_KV_HEADS, HEAD_DIM)
    hq = jax.nn.silu(hq.astype(jnp.float32)).reshape(B, L, HG_HEADS, HG_DK)

    def forget(f, lb):
        fg = lb + (1 - lb) * jax.nn.sigmoid(f.astype(jnp.float32))
        fg = fg.reshape(B, L, HG_HEADS, HG_DK)
        return 1 - fg, jnp.log(fg)

    k_f, g_f = forget(hff, lb_f)
    k_b, g_b = forget(hfb, lb_b)
    hi = hi.astype(jnp.float32).reshape(B, L, HG_HEADS, HG_DV)
    return aq, ak, av, hq, k_f, g_f, k_b, g_b, hi, hg


def bidir_hgrn2(q, k_f, g_f, k_b, g_b, v, s0_f, s0_b):
    flip = lambda a: jnp.flip(a, axis=1)
    o_f, s_f = hgrn2_scan(q, k_f, v, g_f, s0_f)
    o_b, s_b = hgrn2_scan(flip(q), flip(k_b), flip(v), flip(g_b), s0_b)
    return o_f + flip(o_b), s_f, s_b


def hgrn2_output(o, gate, norm_w):
    B, L = o.shape[0], o.shape[1]
    o = rms_norm(o, norm_w).reshape(B, L, HG_WIDTH).astype(gate.dtype)
    return o * jax.nn.silu(gate)


def dwconv3(u, w, b):
    up = jnp.pad(u, ((0, 0), (1, 1), (0, 0)))
    return up[:, :-2] * w[0] + up[:, 1:-1] * w[1] + up[:, 2:] * w[2] + b


def conv_ffn(h, w_up, conv_w, conv_b, w_down):
    gate, up = jnp.split(h @ w_up, 2, axis=-1)
    gate = dwconv3(gate, conv_w, conv_b)
    return (jax.nn.silu(gate) * up) @ w_down


def setup_inputs(seed: int = 0) -> dict:
    key = jax.random.key(seed)
    ks = jax.random.split(key, 20)
    n = jax.random.normal
    f32 = jnp.float32
    return {
        'x': n(ks[0], (BATCH, SEQ, D_MODEL), f32),
        'c': n(ks[1], (BATCH, D_MODEL), f32),
        'ctx': n(ks[2], (BATCH, CTX_LEN, D_MODEL), f32),
        'c_ctx': n(ks[3], (D_MODEL,), f32),
        'w_mod': n(ks[4], (DEPTH, D_MODEL, 6 * D_MODEL), f32) * (0.5 * D_MODEL ** -0.5),
        'b_mod': n(ks[5], (DEPTH, 6 * D_MODEL), f32) * 0.02,
        'norm1_w': 1.0 + 0.1 * n(ks[6], (DEPTH, D_MODEL), f32),
        'norm2_w': 1.0 + 0.1 * n(ks[7], (DEPTH, D_MODEL), f32),
        'w_in': n(ks[8], (DEPTH, D_MODEL, IN_WIDTH), f32) * (D_MODEL ** -0.5),
        'q_norm_w': 1.0 + 0.1 * n(ks[9], (DEPTH, HEAD_DIM), f32),
        'k_norm_w': 1.0 + 0.1 * n(ks[10], (DEPTH, HEAD_DIM), f32),
        'hg_lb_logits': n(ks[11], (2, DEPTH, HG_KWIDTH), f32),
        'hg_norm_w': 1.0 + 0.1 * n(ks[12], (DEPTH, HG_DV), f32),
        'w_out': n(ks[13], (DEPTH, MIX_WIDTH, D_MODEL), f32) * (MIX_WIDTH ** -0.5),
        'w_up': n(ks[14], (DEPTH, D_MODEL, 2 * D_FF), f32) * (D_MODEL ** -0.5),
        'conv_w': n(ks[15], (DEPTH, CONV_W, D_FF), f32) * (CONV_W ** -0.5),
        'conv_b': n(ks[16], (DEPTH, D_FF), f32) * 0.02,
        'w_down': n(ks[17], (DEPTH, D_FF, D_MODEL), f32) * (D_FF ** -0.5),
        'final_norm_w': 1.0 + 0.1 * n(ks[18], (D_MODEL,), f32),
    }


def reference(x, c, ctx, c_ctx, w_mod, b_mod, norm1_w, norm2_w, w_in, q_norm_w, k_norm_w,
              hg_lb_logits, hg_norm_w, w_out, w_up, conv_w, conv_b, w_down, final_norm_w):
    B, L = x.shape[0], x.shape[1]
    Lc = ctx.shape[1]
    cos, sin = axial_rope_angles(L)
    lb_sm = jax.nn.softmax(hg_lb_logits.astype(jnp.float32), axis=1)
    lb_all = jnp.cumsum(lb_sm, axis=1) - lb_sm[:, :1]
    s_c = jax.nn.silu(c)
    s_cc = jax.nn.silu(c_ctx)
    s0 = jnp.zeros((B, HG_HEADS, HG_DK, HG_DV), jnp.float32)

    for l in range(DEPTH):
        last = l == DEPTH - 1
        mod = (s_c @ w_mod[l] + b_mod[l])[:, None, :]
        mod_c = s_cc @ w_mod[l] + b_mod[l]
        sh1, sc1, g1, sh2, sc2, g2 = jnp.split(mod, 6, axis=-1)
        sh1c, sc1c, g1c, sh2c, sc2c, g2c = jnp.split(mod_c, 6, axis=-1)
        lb_f, lb_b = lb_all[0, l], lb_all[1, l]

        hl = modulate(rms_norm(x, norm1_w[l]), sh1, sc1)
        hc = modulate(rms_norm(ctx, norm1_w[l]), sh1c, sc1c)
        aql, akl, avl, hql, kfl, gfl, kbl, gbl, vl, ggl = project(hl, w_in[l], q_norm_w[l], k_norm_w[l], lb_f, lb_b)
        aqc, akc, avc, hqc, kfc, gfc, kbc, gbc, vc, ggc = project(hc, w_in[l], q_norm_w[l], k_norm_w[l], lb_f, lb_b)

        aql = apply_rope(aql, cos, sin)
        akl = apply_rope(akl, cos, sin)
        k_all = jnp.concatenate([akc, akl], axis=1)
        v_all = jnp.concatenate([avc, avl], axis=1)
        attn_l = blocked_attention(aql, k_all, v_all)

        o_c, s_f, s_b = bidir_hgrn2(hqc, kfc, gfc, kbc, gbc, vc, s0, s0)
        o_l, _, _ = bidir_hgrn2(hql, kfl, gfl, kbl, gbl, vl, s_f, s_b)
        hg_l = hgrn2_output(o_l, ggl, hg_norm_w[l])

        x = x + g1 * (jnp.concatenate([attn_l, hg_l], axis=-1) @ w_out[l])
        h2 = modulate(rms_norm(x, norm2_w[l]), sh2, sc2)
        x = x + g2 * conv_ffn(h2, w_up[l], conv_w[l], conv_b[l], w_down[l])

        if not last:
            attn_c = gqa_attend(aqc.reshape(B, Lc, ATTN_KV_HEADS, ATTN_GROUP, HEAD_DIM), akc, avc)
            attn_c = attn_c.reshape(B, Lc, ATTN_WIDTH)
            hg_c = hgrn2_output(o_c, ggc, hg_norm_w[l])
            ctx = ctx + g1c * (jnp.concatenate([attn_c, hg_c], axis=-1) @ w_out[l])
            h2c = modulate(rms_norm(ctx, norm2_w[l]), sh2c, sc2c)
            ctx = ctx + g2c * conv_ffn(h2c, w_up[l], conv_w[l], conv_b[l], w_down[l])

    return rms_norm(x, final_norm_w)
```

```python
import functools

import jax
import jax.numpy as jnp
from jax import lax
from jax.experimental import pallas as pl
from jax.experimental.pallas import tpu as pltpu

F32 = jnp.float32
BF16 = jnp.bfloat16

D_MODEL = 2048
SEQ = 8192
DEPTH = 4
CTX_LEN = 256
T_ROWS = CTX_LEN + SEQ
GRID_W = 64
HEAD_DIM = 128
ATTN_HEADS = 8
ATTN_KV_HEADS = 2
ATTN_GROUP = 4
ATTN_WIDTH = 1024
KV_WIDTH = 256
HG_HEADS = 8
HG_WIDTH = 1024
IN_WIDTH = 6656
ROPE_THETA = 10000.0
ROPE_FREQS = 32
D_FF = 5632
EPS = 1e-6

LANES = 128
SUBLANES = 8
VMEM_LIMIT = 56 * 1024 * 1024

ROW_TILE = 768
OUT_ROW_TILE = 384
IN_COL_TILE = 512
FF_COL_TILE = 512
ATTN_Q_TILE = 256
ATTN_K_TILE = 256
HG_BLOCK = 256
HG_CHUNK = 64
HG_DIAG = 8
CONV_HALO = 16


def _cparams(n_axes):
    return pltpu.CompilerParams(dimension_semantics=("arbitrary",) * n_axes,
                                vmem_limit_bytes=VMEM_LIMIT)


def _rms(x, w):
    return x * lax.rsqrt(jnp.mean(x * x, axis=-1, keepdims=True) + EPS) * w


def _silu(x):
    return x * jax.nn.sigmoid(x)


def _row_is_ctx(tile_idx, rows):
    r = tile_idx * rows + lax.broadcasted_iota(jnp.int32, (rows, 1), 0)
    return r < CTX_LEN


def _mod_rows(mod_ref, is_ctx, k):
    lat = mod_ref[0:1, k * D_MODEL:(k + 1) * D_MODEL]
    ctx = mod_ref[1:2, k * D_MODEL:(k + 1) * D_MODEL]
    return jnp.where(is_ctx, ctx, lat)


def _mod_kernel(s_ref, w_ref, b_ref, o_ref):
    bn = w_ref.shape[2]
    rc = 64

    def body(i, acc):
        a0, a1 = acc
        r0 = pl.multiple_of(i * rc, rc)
        w = w_ref[0, pl.ds(r0, rc), :]
        s = _silu(s_ref[pl.ds(r0, rc), :])
        a0 = a0 + (w * s[:, 0:1]).reshape(rc // SUBLANES, SUBLANES, bn).sum(axis=0)
        a1 = a1 + (w * s[:, 1:2]).reshape(rc // SUBLANES, SUBLANES, bn).sum(axis=0)
        return a0, a1

    z = jnp.zeros((SUBLANES, bn), F32)
    a0, a1 = lax.fori_loop(0, D_MODEL // rc, body, (z, z))
    bias = b_ref[0]
    r0 = a0.sum(axis=0, keepdims=True) + bias
    r1 = a1.sum(axis=0, keepdims=True) + bias
    o_ref[0] = jnp.concatenate([r0, r1, jnp.zeros((SUBLANES - 2, bn), F32)], axis=0)


def _modulation(s_in, w_mod, b_mod):
    bn = 1024
    n_out = 6 * D_MODEL
    return pl.pallas_call(
        _mod_kernel,
        out_shape=jax.ShapeDtypeStruct((DEPTH, SUBLANES, n_out), F32),
        grid=(DEPTH, n_out // bn),
        in_specs=[
            pl.BlockSpec((D_MODEL, 2), lambda l, n: (0, 0)),
            pl.BlockSpec((1, D_MODEL, bn), lambda l, n: (l, 0, n)),
            pl.BlockSpec((1, 1, bn), lambda l, n: (l, 0, n)),
        ],
        out_specs=pl.BlockSpec((1, SUBLANES, bn), lambda l, n: (l, 0, n)),
        compiler_params=_cparams(2),
        name="adaln_modulation",
    )(s_in, w_mod, b_mod.reshape(DEPTH, 1, n_out))


def _rope(y, cos, sin_signed):
    lane = lax.broadcasted_iota(jnp.int32, y.shape, 1)
    partner = jnp.where((lane // ROPE_FREQS) % 2 == 0,
                        pltpu.roll(y, LANES - ROPE_FREQS, axis=1),
                        pltpu.roll(y, ROPE_FREQS, axis=1))
    return y * cos + partner * sin_signed


def _inproj_kernel(x_ref, mod_ref, n1w_ref, w_ref, qnw_ref, knw_ref, lb_ref, cos_ref, sin_ref,
                   q_ref, k_ref, v_ref, hq_ref, kf_ref, gf_ref, kb_ref, gb_ref, hv_ref, gate_ref,
                   h_scr):
    m = pl.program_id(0)
    n = pl.program_id(1)
    bm = x_ref.shape[0]
    heads = IN_COL_TILE // HEAD_DIM

    @pl.when(n == 0)
    def _():
        is_ctx = _row_is_ctx(m, bm)
        y = _rms(x_ref[...], n1w_ref[...])
        h = y * (1.0 + _mod_rows(mod_ref, is_ctx, 1)) + _mod_rows(mod_ref, is_ctx, 0)
        h_scr[...] = h.astype(BF16)

    p = jnp.dot(h_scr[...], w_ref[...], preferred_element_type=F32)

    def head(j):
        return p[:, j * HEAD_DIM:(j + 1) * HEAD_DIM]

    @pl.when(n < 2)
    def _():
        for j in range(heads):
            y = _rope(_rms(head(j), qnw_ref[...]), cos_ref[...], sin_ref[...])
            q_ref[j] = (y * (HEAD_DIM ** -0.5)).astype(BF16)

    @pl.when(n == 2)
    def _():
        for j in range(ATTN_KV_HEADS):
            y = _rope(_rms(head(j), knw_ref[...]), cos_ref[...], sin_ref[...])
            k_ref[j] = y.astype(BF16)
            v_ref[j] = head(ATTN_KV_HEADS + j).astype(BF16)

    def silu_out(ref):
        for j in range(heads):
            ref[j] = _silu(head(j)).astype(BF16)

    def forget_out(k_out, g_out):
        for j in range(heads):
            lb = lb_ref[:, j * HEAD_DIM:(j + 1) * HEAD_DIM]
            fg = lb + (1.0 - lb) * jax.nn.sigmoid(head(j))
            k_out[j] = (1.0 - fg).astype(BF16)
            g_out[j] = jnp.log(fg)

    @pl.when((n == 3) | (n == 4))
    def _():
        silu_out(hq_ref)

    @pl.when((n == 5) | (n == 6))
    def _():
        forget_out(kf_ref, gf_ref)

    @pl.when((n == 7) | (n == 8))
    def _():
        forget_out(kb_ref, gb_ref)

    @pl.when((n == 9) | (n == 10))
    def _():
        for j in range(heads):
            hv_ref[j] = head(j).astype(BF16)

    @pl.when(n >= 11)
    def _():
        silu_out(gate_ref)


def _input_projection(xs, mod_l, n1w, w_in, qnw, knw, lb, cos_t, sin_t):
    bm, bn = ROW_TILE, IN_COL_TILE
    n_m, n_n = T_ROWS // bm, IN_WIDTH // bn
    hpt = bn // HEAD_DIM

    def head_spec(first_tile):
        return pl.BlockSpec((hpt, bm, HEAD_DIM),
                            lambda m, n: (jnp.clip(n - first_tile, 0, 1), m, 0))

    def head_shape(dtype):
        return jax.ShapeDtypeStruct((HG_HEADS, T_ROWS, HEAD_DIM), dtype)

    kv_spec = pl.BlockSpec((ATTN_KV_HEADS, bm, HEAD_DIM), lambda m, n: (0, m, 0))
    kv_shape = jax.ShapeDtypeStruct((ATTN_KV_HEADS, T_ROWS, HEAD_DIM), BF16)
    return pl.pallas_call(
        _inproj_kernel,
        out_shape=(head_shape(BF16), kv_shape, kv_shape, head_shape(BF16),
                   head_shape(BF16), head_shape(F32), head_shape(BF16), head_shape(F32),
                   head_shape(BF16), head_shape(BF16)),
        grid=(n_m, n_n),
        in_specs=[
            pl.BlockSpec((bm, D_MODEL), lambda m, n: (m, 0)),
            pl.BlockSpec((SUBLANES, 6 * D_MODEL), lambda m, n: (0, 0)),
            pl.BlockSpec((1, D_MODEL), lambda m, n: (0, 0)),
            pl.BlockSpec((D_MODEL, bn), lambda m, n: (0, n)),
            pl.BlockSpec((1, HEAD_DIM), lambda m, n: (0, 0)),
            pl.BlockSpec((1, HEAD_DIM), lambda m, n: (0, 0)),
            pl.BlockSpec((1, bn), lambda m, n: (0, jnp.clip(n - 5, 0, 3))),
            pl.BlockSpec((bm, HEAD_DIM), lambda m, n: (m, 0)),
            pl.BlockSpec((bm, HEAD_DIM), lambda m, n: (m, 0)),
        ],
        out_specs=(head_spec(0), kv_spec, kv_spec, head_spec(3),
                   head_spec(5), head_spec(5), head_spec(7), head_spec(7),
                   head_spec(9), head_spec(11)),
        scratch_shapes=[pltpu.VMEM((bm, D_MODEL), BF16)],
        compiler_params=_cparams(2),
        name="input_projection",
    )(xs, mod_l, n1w, w_in, qnw, knw, lb, cos_t, sin_t)


def _attn_kernel(q_ref, k_ref, v_ref, o_ref, m_scr, l_scr, acc_scr):
    i = pl.program_id(1)
    bq = q_ref.shape[1]
    bk = ATTN_K_TILE
    m_scr[...] = jnp.full(m_scr.shape, -jnp.inf, F32)
    l_scr[...] = jnp.zeros(l_scr.shape, F32)
    acc_scr[...] = jnp.zeros(acc_scr.shape, F32)
    n_kv = jnp.where(i == 0, CTX_LEN // bk, T_ROWS // bk)

    def body(j, carry):
        r0 = pl.multiple_of(j * bk, bk)
        k = k_ref[0, pl.ds(r0, bk), :]
        v = v_ref[0, pl.ds(r0, bk), :]
        for r in range(ATTN_GROUP):
            s = lax.dot_general(q_ref[r], k, (((1,), (1,)), ((), ())),
                                preferred_element_type=F32)
            m_prev = m_scr[r]
            m_new = jnp.maximum(m_prev, s.max(axis=-1, keepdims=True))
            alpha = jnp.exp(m_prev - m_new)
            p = jnp.exp(s - jnp.tile(m_new, (1, bk // LANES)))
            l_scr[r] = alpha * l_scr[r] + p.sum(axis=-1, keepdims=True)
            acc_scr[r] = alpha * acc_scr[r] + jnp.dot(p.astype(BF16), v, preferred_element_type=F32)
            m_scr[r] = m_new
        return carry

    lax.fori_loop(0, n_kv, body, 0)
    for r in range(ATTN_GROUP):
        o_ref[:, r * HEAD_DIM:(r + 1) * HEAD_DIM] = (acc_scr[r] / l_scr[r]).astype(BF16)


def _attention(q, k, v):
    bq = ATTN_Q_TILE
    return pl.pallas_call(
        _attn_kernel,
        out_shape=jax.ShapeDtypeStruct((T_ROWS, ATTN_WIDTH), BF16),
        grid=(ATTN_KV_HEADS, T_ROWS // bq),
        in_specs=[
            pl.BlockSpec((ATTN_GROUP, bq, HEAD_DIM), lambda g, i: (g, i, 0)),
            pl.BlockSpec((1, T_ROWS, HEAD_DIM), lambda g, i: (g, 0, 0)),
            pl.BlockSpec((1, T_ROWS, HEAD_DIM), lambda g, i: (g, 0, 0)),
        ],
        out_specs=pl.BlockSpec((bq, ATTN_GROUP * HEAD_DIM), lambda g, i: (i, g)),
        scratch_shapes=[pltpu.VMEM((ATTN_GROUP, bq, LANES), F32),
                        pltpu.VMEM((ATTN_GROUP, bq, LANES), F32),
                        pltpu.VMEM((ATTN_GROUP, bq, HEAD_DIM), F32)],
        compiler_params=_cparams(2),
        name="gqa_attention",
    )(q, k, v)


def _hgrn_consts(rev):
    c = HG_CHUNK
    t = lax.broadcasted_iota(jnp.int32, (c, c), 0)
    s = lax.broadcasted_iota(jnp.int32, (c, c), 1)
    if rev:
        t, s = s, t
    tri = (s <= t).astype(F32)
    levels = []
    half = c // 2
    while half >= HG_DIAG:
        blk = 2 * half
        levels.append((half, (t // blk == s // blk) & (t % blk >= half) & (s % blk < half)))
        half //= 2
    diag = (t // HG_DIAG == s // HG_DIAG) & (s <= t)
    lane = lax.broadcasted_iota(jnp.int32, (HG_DIAG, c), 1)
    return tri, levels, diag, lane


def _hgrn_chunk(q, k, g, v, st, rev, consts):
    tri, levels, diag_mask, lane = consts
    c = HG_CHUNK
    nt = (((1,), (1,)), ((), ()))
    b = jnp.dot(tri, g, preferred_element_type=F32, precision=lax.Precision.HIGHEST)
    tot = b[0:1] if rev else b[c - 1:c]
    qf = q.astype(F32)
    kf = k.astype(F32)

    o = lax.dot_general((qf * jnp.exp(b)).astype(BF16), st.astype(BF16), nt,
                        preferred_element_type=F32)
    k_out = (kf * jnp.exp(tot - b)).astype(BF16)
    st_new = jnp.exp(tot) * st + lax.dot_general(v, k_out, (((0,), (0,)), ((), ())),
                                                 preferred_element_type=F32)

    a = jnp.zeros((c, c), F32)
    for half, mask in levels:
        blk = 2 * half
        refs = []
        for j in range(c // blk):
            r = j * blk + (half if rev else half - 1)
            refs.append(jnp.broadcast_to(b[r:r + 1], (blk, HEAD_DIM)))
        ref = refs[0] if len(refs) == 1 else jnp.concatenate(refs, axis=0)
        qs = (qf * jnp.exp(jnp.minimum(b - ref, 0.0))).astype(BF16)
        ks = (kf * jnp.exp(jnp.minimum(ref - b, 0.0))).astype(BF16)
        a = jnp.where(mask, lax.dot_general(qs, ks, nt, preferred_element_type=F32), a)

    pieces = []
    for s in range(c):
        rows = slice((s // HG_DIAG) * HG_DIAG, (s // HG_DIAG + 1) * HG_DIAG)
        pieces.append(qf[rows] * jnp.exp(jnp.minimum(b[rows] - b[s:s + 1], 0.0)))
    pk = lax.dot_general(jnp.concatenate(pieces, axis=0).astype(BF16), k, nt,
                         preferred_element_type=F32)
    dsc = jnp.zeros((HG_DIAG, c), F32)
    for s in range(c):
        dsc = jnp.where(lane == s, pk[s * HG_DIAG:(s + 1) * HG_DIAG], dsc)
    a = jnp.where(diag_mask, jnp.tile(dsc, (c // HG_DIAG, 1)), a)

    o = o + jnp.dot(a.astype(BF16), v, preferred_element_type=F32)
    return o, st_new


def _hgrn_kernel(qf_ref, kf_ref, gf_ref, vf_ref, qb_ref, kb_ref, gb_ref, vb_ref,
                 of_ref, ob_ref, s_scr):
    @pl.when(pl.program_id(0) == 0)
    def _():
        s_scr[...] = jnp.zeros(s_scr.shape, F32)

    n_chunks = HG_BLOCK // HG_CHUNK
    consts_f = _hgrn_consts(False)
    consts_b = _hgrn_consts(True)

    def head_body(h, carry):
        st = s_scr[0, h]
        for ci in range(n_chunks):
            rows = slice(ci * HG_CHUNK, (ci + 1) * HG_CHUNK)
            o, st = _hgrn_chunk(qf_ref[h, rows, :], kf_ref[h, rows, :], gf_ref[h, rows, :],
                                vf_ref[h, rows, :], st, False, consts_f)
            of_ref[h, rows, :] = o
        s_scr[0, h] = st
        st = s_scr[1, h]
        for ci in reversed(range(n_chunks)):
            rows = slice(ci * HG_CHUNK, (ci + 1) * HG_CHUNK)
            o, st = _hgrn_chunk(qb_ref[h, rows, :], kb_ref[h, rows, :], gb_ref[h, rows, :],
                                vb_ref[h, rows, :], st, True, consts_b)
            ob_ref[h, rows, :] = o
        s_scr[1, h] = st
        return carry

    lax.fori_loop(0, HG_HEADS, head_body, 0)


def _hgrn2(hq, kf, gf, kb, gb, hv):
    nb = T_ROWS // HG_BLOCK
    fwd = lambda i: (0, i, 0)
    bwd = lambda i: (0, jnp.where(i == 0, 0, nb - i), 0)
    blk = (HG_HEADS, HG_BLOCK, HEAD_DIM)
    out = jax.ShapeDtypeStruct((HG_HEADS, T_ROWS, HEAD_DIM), F32)
    return pl.pallas_call(
        _hgrn_kernel,
        out_shape=(out, out),
        grid=(nb,),
        in_specs=[pl.BlockSpec(blk, fwd)] * 4 + [pl.BlockSpec(blk, bwd)] * 4,
        out_specs=(pl.BlockSpec(blk, fwd), pl.BlockSpec(blk, bwd)),
        scratch_shapes=[pltpu.VMEM((2, HG_HEADS, HEAD_DIM, HEAD_DIM), F32)],
        compiler_params=_cparams(1),
        name="hgrn2_scan",
    )(hq, kf, gf, hv, hq, kb, gb, hv)


def _outproj_kernel(attn_ref, of_ref, ob_ref, gate_ref, hnw_ref, w_ref, x_ref, mod_ref, n2w_ref,
                    xo_ref, h2_ref, mix_scr):
    m = pl.program_id(0)
    bm = x_ref.shape[0]
    for h in range(HG_HEADS):
        y = _rms(of_ref[h] + ob_ref[h], hnw_ref[...])
        mix_scr[:, h * HEAD_DIM:(h + 1) * HEAD_DIM] = (y * gate_ref[h].astype(F32)).astype(BF16)
    acc = jnp.dot(attn_ref[...], w_ref[0:ATTN_WIDTH, :], preferred_element_type=F32)
    acc = acc + jnp.dot(mix_scr[...], w_ref[ATTN_WIDTH:ATTN_WIDTH + HG_WIDTH, :],
                        preferred_element_type=F32)
    is_ctx = _row_is_ctx(m, bm)
    x_new = x_ref[...] + _mod_rows(mod_ref, is_ctx, 2) * acc
    xo_ref[...] = x_new
    h2 = _rms(x_new, n2w_ref[...]) * (1.0 + _mod_rows(mod_ref, is_ctx, 4)) + _mod_rows(mod_ref, is_ctx, 3)
    h2_ref[...] = h2.astype(BF16)


def _output_projection(attn, o_f, o_b, gate, hnw, w_out, xs, mod_l, n2w):
    bm = OUT_ROW_TILE
    head_spec = pl.BlockSpec((HG_HEADS, bm, HEAD_DIM), lambda m: (0, m, 0))
    row_spec = pl.BlockSpec((bm, D_MODEL), lambda m: (m, 0))
    return pl.pallas_call(
        _outproj_kernel,
        out_shape=(jax.ShapeDtypeStruct((T_ROWS, D_MODEL), F32),
                   jax.ShapeDtypeStruct((T_ROWS, D_MODEL), BF16)),
        grid=(T_ROWS // bm,),
        in_specs=[
            pl.BlockSpec((bm, ATTN_WIDTH), lambda m: (m, 0)),
            head_spec, head_spec, head_spec,
            pl.BlockSpec((1, HEAD_DIM), lambda m: (0, 0)),
            pl.BlockSpec((ATTN_WIDTH + HG_WIDTH, D_MODEL), lambda m: (0, 0)),
            row_spec,
            pl.BlockSpec((SUBLANES, 6 * D_MODEL), lambda m: (0, 0)),
            pl.BlockSpec((1, D_MODEL), lambda m: (0, 0)),
        ],
        out_specs=(row_spec, row_spec),
        scratch_shapes=[pltpu.VMEM((bm, HG_WIDTH), BF16)],
        compiler_params=_cparams(1),
        name="output_projection",
    )(attn, o_f, o_b, gate, hnw, w_out, xs, mod_l, n2w)


def _ffn_up_kernel(h_ref, hp_ref, hn_ref, wg_ref, wu_ref, cw_ref, cb_ref, u_ref, hcat_scr, g_scr):
    m = pl.program_id(0)
    n = pl.program_id(1)
    bm = h_ref.shape[0]
    hl = CONV_HALO

    @pl.when(n == 0)
    def _():
        hcat_scr[0:hl, :] = hp_ref[...]
        hcat_scr[hl:hl + bm, :] = h_ref[...]
        hcat_scr[hl + bm:hl + bm + hl, :] = hn_ref[...]

    g_scr[...] = jnp.dot(hcat_scr[...], wg_ref[...], preferred_element_type=F32)
    up = jnp.dot(h_ref[...], wu_ref[...], preferred_element_type=F32)
    r = m * bm + lax.broadcasted_iota(jnp.int32, (bm, 1), 0)
    has_prev = (r != 0) & (r != CTX_LEN)
    has_next = (r != CTX_LEN - 1) & (r != T_ROWS - 1)
    g_prev = jnp.where(has_prev, g_scr[hl - 1:hl - 1 + bm, :], 0.0)
    g_next = jnp.where(has_next, g_scr[hl + 1:hl + 1 + bm, :], 0.0)
    gate = (g_prev * cw_ref[0:1, :] + g_scr[hl:hl + bm, :] * cw_ref[1:2, :]
            + g_next * cw_ref[2:3, :] + cb_ref[...])
    u_ref[...] = (_silu(gate) * up).astype(BF16)


def _ffn_up(h2, w_up, conv_w, conv_b):
    bm, bn, hl = ROW_TILE, FF_COL_TILE, CONV_HALO
    n_n = D_FF // bn
    per = bm // hl
    last = T_ROWS // hl - 1
    return pl.pallas_call(
        _ffn_up_kernel,
        out_shape=jax.ShapeDtypeStruct((T_ROWS, D_FF), BF16),
        grid=(T_ROWS // bm, n_n),
        in_specs=[
            pl.BlockSpec((bm, D_MODEL), lambda m, n: (m, 0)),
            pl.BlockSpec((hl, D_MODEL), lambda m, n: (jnp.maximum(m * per - 1, 0), 0)),
            pl.BlockSpec((hl, D_MODEL), lambda m, n: (jnp.minimum((m + 1) * per, last), 0)),
            pl.BlockSpec((D_MODEL, bn), lambda m, n: (0, n)),
            pl.BlockSpec((D_MODEL, bn), lambda m, n: (0, n_n + n)),
            pl.BlockSpec((3, bn), lambda m, n: (0, n)),
            pl.BlockSpec((1, bn), lambda m, n: (0, n)),
        ],
        out_specs=pl.BlockSpec((bm, bn), lambda m, n: (m, n)),
        scratch_shapes=[pltpu.VMEM((bm + 2 * hl, D_MODEL), BF16),
                        pltpu.VMEM((bm + 2 * hl, bn), F32)],
        compiler_params=_cparams(2),
        name="ffn_up_conv",
    )(h2, h2, h2, w_up, w_up, conv_w, conv_b)


def _ffn_down_kernel(u_ref, w_ref, x_ref, mod_ref, fw_ref, o_ref, acc_scr, *, final):
    m = pl.program_id(0)
    k = pl.program_id(1)
    bm = x_ref.shape[0]

    @pl.when(k == 0)
    def _():
        acc_scr[...] = jnp.zeros(acc_scr.shape, F32)

    acc_scr[...] += jnp.dot(u_ref[...], w_ref[...], preferred_element_type=F32)

    @pl.when(k == pl.num_programs(1) - 1)
    def _():
        x_new = x_ref[...] + _mod_rows(mod_ref, _row_is_ctx(m, bm), 5) * acc_scr[...]
        o_ref[...] = _rms(x_new, fw_ref[...]) if final else x_new


def _ffn_down(u, w_down, xs, mod_l, final_w, final):
    bm, bk = ROW_TILE, FF_COL_TILE
    row_spec = pl.BlockSpec((bm, D_MODEL), lambda m, k: (m, 0))
    return pl.pallas_call(
        functools.partial(_ffn_down_kernel, final=final),
        out_shape=jax.ShapeDtypeStruct((T_ROWS, D_MODEL), F32),
        grid=(T_ROWS // bm, D_FF // bk),
        in_specs=[
            pl.BlockSpec((bm, bk), lambda m, k: (m, k)),
            pl.BlockSpec((bk, D_MODEL), lambda m, k: (k, 0)),
            row_spec,
            pl.BlockSpec((SUBLANES, 6 * D_MODEL), lambda m, k: (0, 0)),
            pl.BlockSpec((1, D_MODEL), lambda m, k: (0, 0)),
        ],
        out_specs=row_spec,
        scratch_shapes=[pltpu.VMEM((bm, D_MODEL), F32)],
        compiler_params=_cparams(2),
        name="ffn_down",
    )(u, w_down, xs, mod_l, final_w)


def _rope_tables():
    rows = SEQ // GRID_W
    row = jnp.repeat(jnp.arange(rows, dtype=F32), GRID_W)
    col = jnp.tile(jnp.arange(GRID_W, dtype=F32), rows)
    freqs = ROPE_THETA ** (-jnp.arange(ROPE_FREQS, dtype=F32) / ROPE_FREQS)
    ar, ac = row[:, None] * freqs, col[:, None] * freqs
    cos = jnp.concatenate([jnp.cos(ar), jnp.cos(ar), jnp.cos(ac), jnp.cos(ac)], axis=1)
    sin = jnp.concatenate([-jnp.sin(ar), jnp.sin(ar), -jnp.sin(ac), jnp.sin(ac)], axis=1)
    cos = jnp.concatenate([jnp.ones((CTX_LEN, HEAD_DIM), F32), cos], axis=0)
    sin = jnp.concatenate([jnp.zeros((CTX_LEN, HEAD_DIM), F32), sin], axis=0)
    return cos, sin


def kernel(x, c, ctx, c_ctx, w_mod, b_mod, norm1_w, norm2_w, w_in, q_norm_w, k_norm_w, hg_lb_logits,
           hg_norm_w, w_out, w_up, conv_w, conv_b, w_down, final_norm_w):
    assert x.shape == (1, SEQ, D_MODEL) and ctx.shape == (1, CTX_LEN, D_MODEL)
    xs = jnp.concatenate([ctx[0], x[0]], axis=0)
    mod = _modulation(jnp.stack([c[0], c_ctx], axis=1), w_mod, b_mod)
    cos_t, sin_t = _rope_tables()
    lb_sm = jax.nn.softmax(hg_lb_logits.astype(F32), axis=1)
    lb_all = jnp.cumsum(lb_sm, axis=1) - lb_sm[:, :1]

    for l in range(DEPTH):
        lb = jnp.concatenate([lb_all[0, l], lb_all[1, l]])[None, :]
        q, k, v, hq, kf, gf, kb, gb, hv, gate = _input_projection(
            xs, mod[l], norm1_w[l][None, :], w_in[l].astype(BF16), q_norm_w[l][None, :],
            k_norm_w[l][None, :], lb, cos_t, sin_t)
        attn = _attention(q, k, v)
        o_f, o_b = _hgrn2(hq, kf, gf, kb, gb, hv)
        xs, h2 = _output_projection(attn, o_f, o_b, gate, hg_norm_w[l][None, :],
                                    w_out[l].astype(BF16), xs, mod[l], norm2_w[l][None, :])
        u = _ffn_up(h2, w_up[l].astype(BF16), conv_w[l], conv_b[l][None, :])
        xs = _ffn_down(u, w_down[l].astype(BF16), xs, mod[l], final_norm_w[None, :],
                       final=(l == DEPTH - 1))
    return xs[CTX_LEN:][None]
```

```python
import functools

import jax
import jax.numpy as jnp
from jax import lax
from jax.experimental import pallas as pl
from jax.experimental.pallas import tpu as pltpu

F32 = jnp.float32
BF16 = jnp.bfloat16

D_MODEL = 2048
SEQ = 8192
DEPTH = 4
CTX_LEN = 256
T_ROWS = CTX_LEN + SEQ
GRID_W = 64
HEAD_DIM = 128
ATTN_HEADS = 8
ATTN_KV_HEADS = 2
ATTN_GROUP = 4
ATTN_WIDTH = 1024
KV_WIDTH = 256
HG_HEADS = 8
HG_WIDTH = 1024
IN_WIDTH = 6656
ROPE_THETA = 10000.0
ROPE_FREQS = 32
D_FF = 5632
EPS = 1e-6
Q_SCALE = HEAD_DIM ** -0.5 * 1.4426950408889634

LANES = 128
SUBLANES = 8
VMEM_LIMIT = 56 * 1024 * 1024

ROW_TILE = 768
OUT_ROW_TILE = 384
IN_COL_TILE = 512
FF_COL_TILE = 512
ATTN_Q_TILE = 256
ATTN_K_TILE = 768
HG_BLOCK = 256
HG_CHUNK = 64
HG_DIAG = 8
CONV_HALO = 16


def _cparams(n_axes):
    return pltpu.CompilerParams(dimension_semantics=("arbitrary",) * n_axes,
                                vmem_limit_bytes=VMEM_LIMIT)


def _rms(x, w):
    return x * lax.rsqrt(jnp.mean(x * x, axis=-1, keepdims=True) + EPS) * w


def _silu(x):
    return x * jax.nn.sigmoid(x)


def _row_is_ctx(tile_idx, rows):
    r = tile_idx * rows + lax.broadcasted_iota(jnp.int32, (rows, 1), 0)
    return r < CTX_LEN


def _mod_rows(mod_ref, is_ctx, k):
    lat = mod_ref[0:1, k * D_MODEL:(k + 1) * D_MODEL]
    ctx = mod_ref[1:2, k * D_MODEL:(k + 1) * D_MODEL]
    return jnp.where(is_ctx, ctx, lat)


def _mod_kernel(s_ref, w_ref, b_ref, o_ref):
    bn = w_ref.shape[2]
    rc = 64

    def body(i, acc):
        a0, a1 = acc
        r0 = pl.multiple_of(i * rc, rc)
        w = w_ref[0, pl.ds(r0, rc), :]
        s = _silu(s_ref[pl.ds(r0, rc), :])
        a0 = a0 + (w * s[:, 0:1]).reshape(rc // SUBLANES, SUBLANES, bn).sum(axis=0)
        a1 = a1 + (w * s[:, 1:2]).reshape(rc // SUBLANES, SUBLANES, bn).sum(axis=0)
        return a0, a1

    z = jnp.zeros((SUBLANES, bn), F32)
    a0, a1 = lax.fori_loop(0, D_MODEL // rc, body, (z, z))
    bias = b_ref[0]
    r0 = a0.sum(axis=0, keepdims=True) + bias
    r1 = a1.sum(axis=0, keepdims=True) + bias
    o_ref[0] = jnp.concatenate([r0, r1, jnp.zeros((SUBLANES - 2, bn), F32)], axis=0)


def _modulation(s_in, w_mod, b_mod):
    bn = 1024
    n_out = 6 * D_MODEL
    return pl.pallas_call(
        _mod_kernel,
        out_shape=jax.ShapeDtypeStruct((DEPTH, SUBLANES, n_out), F32),
        grid=(DEPTH, n_out // bn),
        in_specs=[
            pl.BlockSpec((D_MODEL, 2), lambda l, n: (0, 0)),
            pl.BlockSpec((1, D_MODEL, bn), lambda l, n: (l, 0, n)),
            pl.BlockSpec((1, 1, bn), lambda l, n: (l, 0, n)),
        ],
        out_specs=pl.BlockSpec((1, SUBLANES, bn), lambda l, n: (l, 0, n)),
        compiler_params=_cparams(2),
        name="adaln_modulation",
    )(s_in, w_mod, b_mod.reshape(DEPTH, 1, n_out))


def _rope(y, cos, sin_signed):
    lane = lax.broadcasted_iota(jnp.int32, y.shape, 1)
    partner = jnp.where((lane // ROPE_FREQS) % 2 == 0,
                        pltpu.roll(y, LANES - ROPE_FREQS, axis=1),
                        pltpu.roll(y, ROPE_FREQS, axis=1))
    return y * cos + partner * sin_signed


def _inproj_kernel(x_ref, mod_ref, n1w_ref, w_ref, qnw_ref, knw_ref, lb_ref, cos_ref, sin_ref,
                   q_ref, k_ref, v_ref, hq_ref, kf_ref, gf_ref, kb_ref, gb_ref, hv_ref, gate_ref,
                   h_scr):
    m = pl.program_id(0)
    n = pl.program_id(1)
    bm = x_ref.shape[0]
    heads = IN_COL_TILE // HEAD_DIM

    @pl.when(n == 0)
    def _():
        is_ctx = _row_is_ctx(m, bm)
        y = _rms(x_ref[...], n1w_ref[...])
        h = y * (1.0 + _mod_rows(mod_ref, is_ctx, 1)) + _mod_rows(mod_ref, is_ctx, 0)
        h_scr[...] = h.astype(BF16)

    p = jnp.dot(h_scr[...], w_ref[...], preferred_element_type=F32)

    def head(j):
        return p[:, j * HEAD_DIM:(j + 1) * HEAD_DIM]

    @pl.when(n < 2)
    def _():
        for j in range(heads):
            y = _rope(_rms(head(j), qnw_ref[...]), cos_ref[...], sin_ref[...])
            q_ref[j] = (y * Q_SCALE).astype(BF16)

    @pl.when(n == 2)
    def _():
        for j in range(ATTN_KV_HEADS):
            y = _rope(_rms(head(j), knw_ref[...]), cos_ref[...], sin_ref[...])
            k_ref[j] = y.astype(BF16)
            v_ref[j] = head(ATTN_KV_HEADS + j).astype(BF16)

    def silu_out(ref):
        for j in range(heads):
            ref[j] = _silu(head(j)).astype(BF16)

    def forget_out(k_out, g_out):
        for j in range(heads):
            lb = lb_ref[:, j * HEAD_DIM:(j + 1) * HEAD_DIM]
            fg = lb + (1.0 - lb) * jax.nn.sigmoid(head(j))
            k_out[j] = (1.0 - fg).astype(BF16)
            g_out[j] = jnp.log(fg)

    @pl.when((n == 3) | (n == 4))
    def _():
        silu_out(hq_ref)

    @pl.when((n == 5) | (n == 6))
    def _():
        forget_out(kf_ref, gf_ref)

    @pl.when((n == 7) | (n == 8))
    def _():
        forget_out(kb_ref, gb_ref)

    @pl.when((n == 9) | (n == 10))
    def _():
        for j in range(heads):
            hv_ref[j] = head(j).astype(BF16)

    @pl.when(n >= 11)
    def _():
        silu_out(gate_ref)


def _input_projection(xs, mod_l, n1w, w_in, qnw, knw, lb, cos_t, sin_t):
    bm, bn = ROW_TILE, IN_COL_TILE
    n_m, n_n = T_ROWS // bm, IN_WIDTH // bn
    hpt = bn // HEAD_DIM

    def head_spec(first_tile):
        return pl.BlockSpec((hpt, bm, HEAD_DIM),
                            lambda m, n: (jnp.clip(n - first_tile, 0, 1), m, 0))

    def head_shape(dtype):
        return jax.ShapeDtypeStruct((HG_HEADS, T_ROWS, HEAD_DIM), dtype)

    kv_spec = pl.BlockSpec((ATTN_KV_HEADS, bm, HEAD_DIM), lambda m, n: (0, m, 0))
    kv_shape = jax.ShapeDtypeStruct((ATTN_KV_HEADS, T_ROWS, HEAD_DIM), BF16)
    return pl.pallas_call(
        _inproj_kernel,
        out_shape=(head_shape(BF16), kv_shape, kv_shape, head_shape(BF16),
                   head_shape(BF16), head_shape(F32), head_shape(BF16), head_shape(F32),
                   head_shape(BF16), head_shape(BF16)),
        grid=(n_m, n_n),
        in_specs=[
            pl.BlockSpec((bm, D_MODEL), lambda m, n: (m, 0)),
            pl.BlockSpec((SUBLANES, 6 * D_MODEL), lambda m, n: (0, 0)),
            pl.BlockSpec((1, D_MODEL), lambda m, n: (0, 0)),
            pl.BlockSpec((D_MODEL, bn), lambda m, n: (0, n)),
            pl.BlockSpec((1, HEAD_DIM), lambda m, n: (0, 0)),
            pl.BlockSpec((1, HEAD_DIM), lambda m, n: (0, 0)),
            pl.BlockSpec((1, bn), lambda m, n: (0, jnp.clip(n - 5, 0, 3))),
            pl.BlockSpec((bm, HEAD_DIM), lambda m, n: (m, 0)),
            pl.BlockSpec((bm, HEAD_DIM), lambda m, n: (m, 0)),
        ],
        out_specs=(head_spec(0), kv_spec, kv_spec, head_spec(3),
                   head_spec(5), head_spec(5), head_spec(7), head_spec(7),
                   head_spec(9), head_spec(11)),
        scratch_shapes=[pltpu.VMEM((bm, D_MODEL), BF16)],
        compiler_params=_cparams(2),
        name="input_projection",
    )(xs, mod_l, n1w, w_in, qnw, knw, lb, cos_t, sin_t)


def _attn_kernel(q_ref, k_ref, vt_ref, o_ref, acc_scr, st_scr):
    i = pl.program_id(1)
    bq = q_ref.shape[1]
    bk = ATTN_K_TILE
    n_chunks = T_ROWS // bk

    def scores(kc, r, slot):
        half = kc.shape[0] // 2
        mx = []
        for a in range(2):
            st = lax.dot_general(kc[a * half:(a + 1) * half], q_ref[r], (((1,), (1,)), ((), ())),
                                 preferred_element_type=F32)
            st_scr[slot, a * half:(a + 1) * half, :] = st
            mx.append(st.max(axis=0, keepdims=True))
        return jnp.maximum(mx[0], mx[1])

    def update(vtc, r, slot, m_cur, m, l):
        width = vtc.shape[1]
        m_new = jnp.maximum(m, m_cur)
        alpha = jnp.exp2(m - m_new)
        p = jnp.exp2(st_scr[slot, 0:width, :] - m_new)
        l = alpha * l + p.sum(axis=0, keepdims=True)
        acc_scr[r] = alpha * acc_scr[r] + jnp.dot(vtc, p.astype(BF16), preferred_element_type=F32)
        return m_new, l

    def finish(l):
        for r in range(ATTN_GROUP):
            o_ref[:, r * HEAD_DIM:(r + 1) * HEAD_DIM] = (acc_scr[r] / l[r]).T.astype(BF16)

    acc_scr[...] = jnp.zeros(acc_scr.shape, F32)
    m0 = tuple(jnp.full((1, bq), -jnp.inf, F32) for _ in range(ATTN_GROUP))
    l0 = tuple(jnp.zeros((1, bq), F32) for _ in range(ATTN_GROUP))

    @pl.when(i == 0)
    def _():
        kc = k_ref[0, 0:CTX_LEN, :]
        vtc = vt_ref[0, 0, :, 0:CTX_LEN]
        l = []
        mc = scores(kc, 0, 0)
        for r in range(ATTN_GROUP):
            mc_next = scores(kc, r + 1, (r + 1) % 2) if r + 1 < ATTN_GROUP else None
            l.append(update(vtc, r, r % 2, mc, m0[r], l0[r])[1])
            mc = mc_next
        finish(l)

    @pl.when(i > 0)
    def _():
        def chunk(j):
            return k_ref[0, pl.ds(pl.multiple_of(j * bk, bk), bk), :]

        def body(j, carry):
            mc, m, l = carry
            m, l = list(m), list(l)
            vtc = vt_ref[0, j]
            for r in range(ATTN_GROUP):
                if r + 1 < ATTN_GROUP:
                    mc_next = scores(chunk(j), r + 1, (r + 1) % 2)
                else:
                    mc_next = scores(chunk(jnp.minimum(j + 1, n_chunks - 1)), 0, 0)
                m[r], l[r] = update(vtc, r, r % 2, mc, m[r], l[r])
                mc = mc_next
            return mc, tuple(m), tuple(l)

        _, _, l = lax.fori_loop(0, n_chunks, body, (scores(chunk(0), 0, 0), m0, l0), unroll=True)
        finish(l)


def _attention(q, k, v):
    bq, bk = ATTN_Q_TILE, ATTN_K_TILE
    n_chunks = T_ROWS // bk
    vt = v.reshape(ATTN_KV_HEADS, n_chunks, bk, HEAD_DIM).transpose(0, 1, 3, 2)
    return pl.pallas_call(
        _attn_kernel,
        out_shape=jax.ShapeDtypeStruct((T_ROWS, ATTN_WIDTH), BF16),
        grid=(ATTN_KV_HEADS, T_ROWS // bq),
        in_specs=[
            pl.BlockSpec((ATTN_GROUP, bq, HEAD_DIM), lambda g, i: (g, i, 0)),
            pl.BlockSpec((1, T_ROWS, HEAD_DIM), lambda g, i: (g, 0, 0)),
            pl.BlockSpec((1, n_chunks, HEAD_DIM, bk), lambda g, i: (g, 0, 0, 0)),
        ],
        out_specs=pl.BlockSpec((bq, ATTN_GROUP * HEAD_DIM), lambda g, i: (i, g)),
        scratch_shapes=[pltpu.VMEM((ATTN_GROUP, HEAD_DIM, bq), F32),
                        pltpu.VMEM((2, bk, bq), F32)],
        compiler_params=_cparams(2),
        name="gqa_attention",
    )(q, k, vt)


def _hgrn_consts(rev):
    c = HG_CHUNK
    t = lax.broadcasted_iota(jnp.int32, (c, c), 0)
    s = lax.broadcasted_iota(jnp.int32, (c, c), 1)
    if rev:
        t, s = s, t
    tri = (s <= t).astype(F32)
    levels = []
    half = c // 2
    while half >= HG_DIAG:
        blk = 2 * half
        levels.append((half, (t // blk == s // blk) & (t % blk >= half) & (s % blk < half)))
        half //= 2
    diag = (t // HG_DIAG == s // HG_DIAG) & (s <= t)
    lane = lax.broadcasted_iota(jnp.int32, (HG_DIAG, c), 1)
    return tri, levels, diag, lane


def _hgrn_chunk(q, k, g, v, st, rev, consts):
    tri, levels, diag_mask, lane = consts
    c = HG_CHUNK
    nt = (((1,), (1,)), ((), ()))
    b = jnp.dot(tri, g, preferred_element_type=F32, precision=lax.Precision.HIGHEST)
    tot = b[0:1] if rev else b[c - 1:c]
    qf = q.astype(F32)
    kf = k.astype(F32)

    o = lax.dot_general((qf * jnp.exp(b)).astype(BF16), st.astype(BF16), nt,
                        preferred_element_type=F32)
    k_out = (kf * jnp.exp(tot - b)).astype(BF16)
    st_new = jnp.exp(tot) * st + lax.dot_general(v, k_out, (((0,), (0,)), ((), ())),
                                                 preferred_element_type=F32)

    a = jnp.zeros((c, c), F32)
    for half, mask in levels:
        blk = 2 * half
        refs = []
        for j in range(c // blk):
            r = j * blk + (half if rev else half - 1)
            refs.append(jnp.broadcast_to(b[r:r + 1], (blk, HEAD_DIM)))
        ref = refs[0] if len(refs) == 1 else jnp.concatenate(refs, axis=0)
        qs = (qf * jnp.exp(jnp.minimum(b - ref, 0.0))).astype(BF16)
        ks = (kf * jnp.exp(jnp.minimum(ref - b, 0.0))).astype(BF16)
        a = jnp.where(mask, lax.dot_general(qs, ks, nt, preferred_element_type=F32), a)

    pieces = []
    for s in range(c):
        rows = slice((s // HG_DIAG) * HG_DIAG, (s // HG_DIAG + 1) * HG_DIAG)
        pieces.append(qf[rows] * jnp.exp(jnp.minimum(b[rows] - b[s:s + 1], 0.0)))
    pk = lax.dot_general(jnp.concatenate(pieces, axis=0).astype(BF16), k, nt,
                         preferred_element_type=F32)
    dsc = jnp.zeros((HG_DIAG, c), F32)
    for s in range(c):
        dsc = jnp.where(lane == s, pk[s * HG_DIAG:(s + 1) * HG_DIAG], dsc)
    a = jnp.where(diag_mask, jnp.tile(dsc, (c // HG_DIAG, 1)), a)

    o = o + jnp.dot(a.astype(BF16), v, preferred_element_type=F32)
    return o, st_new


def _hgrn_kernel(qf_ref, kf_ref, gf_ref, vf_ref, qb_ref, kb_ref, gb_ref, vb_ref,
                 of_ref, ob_ref, s_scr):
    @pl.when(pl.program_id(0) == 0)
    def _():
        s_scr[...] = jnp.zeros(s_scr.shape, F32)

    n_chunks = HG_BLOCK // HG_CHUNK
    consts_f = _hgrn_consts(False)
    consts_b = _hgrn_consts(True)

    def head_body(h, carry):
        st = s_scr[0, h]
        for ci in range(n_chunks):
            rows = slice(ci * HG_CHUNK, (ci + 1) * HG_CHUNK)
            o, st = _hgrn_chunk(qf_ref[h, rows, :], kf_ref[h, rows, :], gf_ref[h, rows, :],
                                vf_ref[h, rows, :], st, False, consts_f)
            of_ref[h, rows, :] = o
        s_scr[0, h] = st
        st = s_scr[1, h]
        for ci in reversed(range(n_chunks)):
            rows = slice(ci * HG_CHUNK, (ci + 1) * HG_CHUNK)
            o, st = _hgrn_chunk(qb_ref[h, rows, :], kb_ref[h, rows, :], gb_ref[h, rows, :],
                                vb_ref[h, rows, :], st, True, consts_b)
            ob_ref[h, rows, :] = o
        s_scr[1, h] = st
        return carry

    lax.fori_loop(0, HG_HEADS, head_body, 0)


def _hgrn2(hq, kf, gf, kb, gb, hv):
    nb = T_ROWS // HG_BLOCK
    fwd = lambda i: (0, i, 0)
    bwd = lambda i: (0, jnp.where(i == 0, 0, nb - i), 0)
    blk = (HG_HEADS, HG_BLOCK, HEAD_DIM)
    out = jax.ShapeDtypeStruct((HG_HEADS, T_ROWS, HEAD_DIM), F32)
    return pl.pallas_call(
        _hgrn_kernel,
        out_shape=(out, out),
        grid=(nb,),
        in_specs=[pl.BlockSpec(blk, fwd)] * 4 + [pl.BlockSpec(blk, bwd)] * 4,
        out_specs=(pl.BlockSpec(blk, fwd), pl.BlockSpec(blk, bwd)),
        scratch_shapes=[pltpu.VMEM((2, HG_HEADS, HEAD_DIM, HEAD_DIM), F32)],
        compiler_params=_cparams(1),
        name="hgrn2_scan",
    )(hq, kf, gf, hv, hq, kb, gb, hv)


def _outproj_kernel(attn_ref, of_ref, ob_ref, gate_ref, hnw_ref, w_ref, x_ref, mod_ref, n2w_ref,
                    xo_ref, h2_ref, mix_scr):
    m = pl.program_id(0)
    bm = x_ref.shape[0]
    for h in range(HG_HEADS):
        y = _rms(of_ref[h] + ob_ref[h], hnw_ref[...])
        mix_scr[:, h * HEAD_DIM:(h + 1) * HEAD_DIM] = (y * gate_ref[h].astype(F32)).astype(BF16)
    acc = jnp.dot(attn_ref[...], w_ref[0:ATTN_WIDTH, :], preferred_element_type=F32)
    acc = acc + jnp.dot(mix_scr[...], w_ref[ATTN_WIDTH:ATTN_WIDTH + HG_WIDTH, :],
                        preferred_element_type=F32)
    is_ctx = _row_is_ctx(m, bm)
    x_new = x_ref[...] + _mod_rows(mod_ref, is_ctx, 2) * acc
    xo_ref[...] = x_new
    h2 = _rms(x_new, n2w_ref[...]) * (1.0 + _mod_rows(mod_ref, is_ctx, 4)) + _mod_rows(mod_ref, is_ctx, 3)
    h2_ref[...] = h2.astype(BF16)


def _output_projection(attn, o_f, o_b, gate, hnw, w_out, xs, mod_l, n2w):
    bm = OUT_ROW_TILE
    head_spec = pl.BlockSpec((HG_HEADS, bm, HEAD_DIM), lambda m: (0, m, 0))
    row_spec = pl.BlockSpec((bm, D_MODEL), lambda m: (m, 0))
    return pl.pallas_call(
        _outproj_kernel,
        out_shape=(jax.ShapeDtypeStruct((T_ROWS, D_MODEL), F32),
                   jax.ShapeDtypeStruct((T_ROWS, D_MODEL), BF16)),
        grid=(T_ROWS // bm,),
        in_specs=[
            pl.BlockSpec((bm, ATTN_WIDTH), lambda m: (m, 0)),
            head_spec, head_spec, head_spec,
            pl.BlockSpec((1, HEAD_DIM), lambda m: (0, 0)),
            pl.BlockSpec((ATTN_WIDTH + HG_WIDTH, D_MODEL), lambda m: (0, 0)),
            row_spec,
            pl.BlockSpec((SUBLANES, 6 * D_MODEL), lambda m: (0, 0)),
            pl.BlockSpec((1, D_MODEL), lambda m: (0, 0)),
        ],
        out_specs=(row_spec, row_spec),
        scratch_shapes=[pltpu.VMEM((bm, HG_WIDTH), BF16)],
        compiler_params=_cparams(1),
        name="output_projection",
    )(attn, o_f, o_b, gate, hnw, w_out, xs, mod_l, n2w)


def _ffn_up_kernel(h_ref, hp_ref, hn_ref, wg_ref, wu_ref, cw_ref, cb_ref, u_ref, hcat_scr, g_scr):
    m = pl.program_id(0)
    n = pl.program_id(1)
    bm = h_ref.shape[0]
    hl = CONV_HALO

    @pl.when(n == 0)
    def _():
        hcat_scr[0:hl, :] = hp_ref[...]
        hcat_scr[hl:hl + bm, :] = h_ref[...]
        hcat_scr[hl + bm:hl + bm + hl, :] = hn_ref[...]

    g_scr[...] = jnp.dot(hcat_scr[...], wg_ref[...], preferred_element_type=F32)
    up = jnp.dot(h_ref[...], wu_ref[...], preferred_element_type=F32)
    r = m * bm + lax.broadcasted_iota(jnp.int32, (bm, 1), 0)
    has_prev = (r != 0) & (r != CTX_LEN)
    has_next = (r != CTX_LEN - 1) & (r != T_ROWS - 1)
    g_prev = jnp.where(has_prev, g_scr[hl - 1:hl - 1 + bm, :], 0.0)
    g_next = jnp.where(has_next, g_scr[hl + 1:hl + 1 + bm, :], 0.0)
    gate = (g_prev * cw_ref[0:1, :] + g_scr[hl:hl + bm, :] * cw_ref[1:2, :]
            + g_next * cw_ref[2:3, :] + cb_ref[...])
    u_ref[...] = (_silu(gate) * up).astype(BF16)


def _ffn_up(h2, w_up, conv_w, conv_b):
    bm, bn, hl = ROW_TILE, FF_COL_TILE, CONV_HALO
    n_n = D_FF // bn
    per = bm // hl
    last = T_ROWS // hl - 1
    return pl.pallas_call(
        _ffn_up_kernel,
        out_shape=jax.ShapeDtypeStruct((T_ROWS, D_FF), BF16),
        grid=(T_ROWS // bm, n_n),
        in_specs=[
            pl.BlockSpec((bm, D_MODEL), lambda m, n: (m, 0)),
            pl.BlockSpec((hl, D_MODEL), lambda m, n: (jnp.maximum(m * per - 1, 0), 0)),
            pl.BlockSpec((hl, D_MODEL), lambda m, n: (jnp.minimum((m + 1) * per, last), 0)),
            pl.BlockSpec((D_MODEL, bn), lambda m, n: (0, n)),
            pl.BlockSpec((D_MODEL, bn), lambda m, n: (0, n_n + n)),
            pl.BlockSpec((3, bn), lambda m, n: (0, n)),
            pl.BlockSpec((1, bn), lambda m, n: (0, n)),
        ],
        out_specs=pl.BlockSpec((bm, bn), lambda m, n: (m, n)),
        scratch_shapes=[pltpu.VMEM((bm + 2 * hl, D_MODEL), BF16),
                        pltpu.VMEM((bm + 2 * hl, bn), F32)],
        compiler_params=_cparams(2),
        name="ffn_up_conv",
    )(h2, h2, h2, w_up, w_up, conv_w, conv_b)


def _ffn_down_kernel(u_ref, w_ref, x_ref, mod_ref, fw_ref, o_ref, acc_scr, *, final):
    m = pl.program_id(0)
    k = pl.program_id(1)
    bm = x_ref.shape[0]

    @pl.when(k == 0)
    def _():
        acc_scr[...] = jnp.zeros(acc_scr.shape, F32)

    acc_scr[...] += jnp.dot(u_ref[...], w_ref[...], preferred_element_type=F32)

    @pl.when(k == pl.num_programs(1) - 1)
    def _():
        x_new = x_ref[...] + _mod_rows(mod_ref, _row_is_ctx(m, bm), 5) * acc_scr[...]
        o_ref[...] = _rms(x_new, fw_ref[...]) if final else x_new


def _ffn_down(u, w_down, xs, mod_l, final_w, final):
    bm, bk = ROW_TILE, FF_COL_TILE
    row_spec = pl.BlockSpec((bm, D_MODEL), lambda m, k: (m, 0))
    return pl.pallas_call(
        functools.partial(_ffn_down_kernel, final=final),
        out_shape=jax.ShapeDtypeStruct((T_ROWS, D_MODEL), F32),
        grid=(T_ROWS // bm, D_FF // bk),
        in_specs=[
            pl.BlockSpec((bm, bk), lambda m, k: (m, k)),
            pl.BlockSpec((bk, D_MODEL), lambda m, k: (k, 0)),
            row_spec,
            pl.BlockSpec((SUBLANES, 6 * D_MODEL), lambda m, k: (0, 0)),
            pl.BlockSpec((1, D_MODEL), lambda m, k: (0, 0)),
        ],
        out_specs=row_spec,
        scratch_shapes=[pltpu.VMEM((bm, D_MODEL), F32)],
        compiler_params=_cparams(2),
        name="ffn_down",
    )(u, w_down, xs, mod_l, final_w)


def _rope_tables():
    rows = SEQ // GRID_W
    row = jnp.repeat(jnp.arange(rows, dtype=F32), GRID_W)
    col = jnp.tile(jnp.arange(GRID_W, dtype=F32), rows)
    freqs = ROPE_THETA ** (-jnp.arange(ROPE_FREQS, dtype=F32) / ROPE_FREQS)
    ar, ac = row[:, None] * freqs, col[:, None] * freqs
    cos = jnp.concatenate([jnp.cos(ar), jnp.cos(ar), jnp.cos(ac), jnp.cos(ac)], axis=1)
    sin = jnp.concatenate([-jnp.sin(ar), jnp.sin(ar), -jnp.sin(ac), jnp.sin(ac)], axis=1)
    cos = jnp.concatenate([jnp.ones((CTX_LEN, HEAD_DIM), F32), cos], axis=0)
    sin = jnp.concatenate([jnp.zeros((CTX_LEN, HEAD_DIM), F32), sin], axis=0)
    return cos, sin


def kernel(x, c, ctx, c_ctx, w_mod, b_mod, norm1_w, norm2_w, w_in, q_norm_w, k_norm_w, hg_lb_logits,
           hg_norm_w, w_out, w_up, conv_w, conv_b, w_down, final_norm_w):
    assert x.shape == (1, SEQ, D_MODEL) and ctx.shape == (1, CTX_LEN, D_MODEL)
    xs = jnp.concatenate([ctx[0], x[0]], axis=0)
    mod = _modulation(jnp.stack([c[0], c_ctx], axis=1), w_mod, b_mod)
    cos_t, sin_t = _rope_tables()
    lb_sm = jax.nn.softmax(hg_lb_logits.astype(F32), axis=1)
    lb_all = jnp.cumsum(lb_sm, axis=1) - lb_sm[:, :1]

    for l in range(DEPTH):
        lb = jnp.concatenate([lb_all[0, l], lb_all[1, l]])[None, :]
        q, k, v, hq, kf, gf, kb, gb, hv, gate = _input_projection(
            xs, mod[l], norm1_w[l][None, :], w_in[l].astype(BF16), q_norm_w[l][None, :],
            k_norm_w[l][None, :], lb, cos_t, sin_t)
        attn = _attention(q, k, v)
        o_f, o_b = _hgrn2(hq, kf, gf, kb, gb, hv)
        xs, h2 = _output_projection(attn, o_f, o_b, gate, hg_norm_w[l][None, :],
                                    w_out[l].astype(BF16), xs, mod[l], norm2_w[l][None, :])
        u = _ffn_up(h2, w_up[l].astype(BF16), conv_w[l], conv_b[l][None, :])
        xs = _ffn_down(u, w_down[l].astype(BF16), xs, mod[l], final_norm_w[None, :],
                       final=(l == DEPTH - 1))
    return xs[CTX_LEN:][None]
```

```python
import functools

import jax
import jax.numpy as jnp
from jax import lax
from jax.experimental import pallas as pl
from jax.experimental.pallas import tpu as pltpu

F32 = jnp.float32
BF16 = jnp.bfloat16

D_MODEL = 2048
SEQ = 8192
DEPTH = 4
CTX_LEN = 256
T_ROWS = CTX_LEN + SEQ
GRID_W = 64
HEAD_DIM = 128
ATTN_HEADS = 8
ATTN_KV_HEADS = 2
ATTN_GROUP = 4
ATTN_WIDTH = 1024
KV_WIDTH = 256
HG_HEADS = 8
HG_WIDTH = 1024
IN_WIDTH = 6656
ROPE_THETA = 10000.0
ROPE_FREQS = 32
D_FF = 5632
EPS = 1e-6
Q_SCALE = HEAD_DIM ** -0.5 * 1.4426950408889634

LANES = 128
SUBLANES = 8
VMEM_LIMIT = 56 * 1024 * 1024

ROW_TILE = 768
OUT_ROW_TILE = 384
IN_COL_TILE = 512
FF_COL_TILE = 512
ATTN_Q_TILE = 256
ATTN_K_TILE = 768
HG_BLOCK = 256
HG_CHUNK = 64
HG_DIAG = 2
CONV_HALO = 16


def _cparams(n_axes):
    return pltpu.CompilerParams(dimension_semantics=("arbitrary",) * n_axes,
                                vmem_limit_bytes=VMEM_LIMIT)


def _rms(x, w):
    return x * lax.rsqrt(jnp.mean(x * x, axis=-1, keepdims=True) + EPS) * w


def _silu(x):
    return x * jax.nn.sigmoid(x)


def _row_is_ctx(tile_idx, rows):
    r = tile_idx * rows + lax.broadcasted_iota(jnp.int32, (rows, 1), 0)
    return r < CTX_LEN


def _mod_rows(mod_ref, is_ctx, k):
    lat = mod_ref[0:1, k * D_MODEL:(k + 1) * D_MODEL]
    ctx = mod_ref[1:2, k * D_MODEL:(k + 1) * D_MODEL]
    return jnp.where(is_ctx, ctx, lat)


def _mod_kernel(s_ref, w_ref, b_ref, o_ref):
    bn = w_ref.shape[2]
    rc = 64

    def body(i, acc):
        a0, a1 = acc
        r0 = pl.multiple_of(i * rc, rc)
        w = w_ref[0, pl.ds(r0, rc), :]
        s = _silu(s_ref[pl.ds(r0, rc), :])
        a0 = a0 + (w * s[:, 0:1]).reshape(rc // SUBLANES, SUBLANES, bn).sum(axis=0)
        a1 = a1 + (w * s[:, 1:2]).reshape(rc // SUBLANES, SUBLANES, bn).sum(axis=0)
        return a0, a1

    z = jnp.zeros((SUBLANES, bn), F32)
    a0, a1 = lax.fori_loop(0, D_MODEL // rc, body, (z, z))
    bias = b_ref[0]
    r0 = a0.sum(axis=0, keepdims=True) + bias
    r1 = a1.sum(axis=0, keepdims=True) + bias
    o_ref[0] = jnp.concatenate([r0, r1, jnp.zeros((SUBLANES - 2, bn), F32)], axis=0)


def _modulation(s_in, w_mod, b_mod):
    bn = 1024
    n_out = 6 * D_MODEL
    return pl.pallas_call(
        _mod_kernel,
        out_shape=jax.ShapeDtypeStruct((DEPTH, SUBLANES, n_out), F32),
        grid=(DEPTH, n_out // bn),
        in_specs=[
            pl.BlockSpec((D_MODEL, 2), lambda l, n: (0, 0)),
            pl.BlockSpec((1, D_MODEL, bn), lambda l, n: (l, 0, n)),
            pl.BlockSpec((1, 1, bn), lambda l, n: (l, 0, n)),
        ],
        out_specs=pl.BlockSpec((1, SUBLANES, bn), lambda l, n: (l, 0, n)),
        compiler_params=_cparams(2),
        name="adaln_modulation",
    )(s_in, w_mod, b_mod.reshape(DEPTH, 1, n_out))


def _rope(y, cos, sin_signed):
    lane = lax.broadcasted_iota(jnp.int32, y.shape, 1)
    partner = jnp.where((lane // ROPE_FREQS) % 2 == 0,
                        pltpu.roll(y, LANES - ROPE_FREQS, axis=1),
                        pltpu.roll(y, ROPE_FREQS, axis=1))
    return y * cos + partner * sin_signed


def _inproj_kernel(x_ref, mod_ref, n1w_ref, w_ref, qnw_ref, knw_ref, lb_ref, cos_ref, sin_ref,
                   q_ref, k_ref, v_ref, hq_ref, kf_ref, bf_ref, kb_ref, bb_ref, hv_ref, gate_ref,
                   h_scr):
    m = pl.program_id(0)
    n = pl.program_id(1)
    bm = x_ref.shape[0]
    heads = IN_COL_TILE // HEAD_DIM

    @pl.when(n == 0)
    def _():
        is_ctx = _row_is_ctx(m, bm)
        y = _rms(x_ref[...], n1w_ref[...])
        h = y * (1.0 + _mod_rows(mod_ref, is_ctx, 1)) + _mod_rows(mod_ref, is_ctx, 0)
        h_scr[...] = h.astype(BF16)

    p = jnp.dot(h_scr[...], w_ref[...], preferred_element_type=F32)

    def head(j):
        return p[:, j * HEAD_DIM:(j + 1) * HEAD_DIM]

    @pl.when(n < 2)
    def _():
        for j in range(heads):
            y = _rope(_rms(head(j), qnw_ref[...]), cos_ref[...], sin_ref[...])
            q_ref[j] = (y * Q_SCALE).astype(BF16)

    @pl.when(n == 2)
    def _():
        for j in range(ATTN_KV_HEADS):
            y = _rope(_rms(head(j), knw_ref[...]), cos_ref[...], sin_ref[...])
            k_ref[j] = y.astype(BF16)
            v_ref[j] = head(ATTN_KV_HEADS + j).astype(BF16)

    def silu_out(ref):
        for j in range(heads):
            ref[j] = _silu(head(j)).astype(BF16)

    def forget_out(k_out, b_out, rev):
        for j in range(heads):
            lb = lb_ref[:, j * HEAD_DIM:(j + 1) * HEAD_DIM]
            fg = lb + (1.0 - lb) * jax.nn.sigmoid(head(j))
            k_out[j] = (1.0 - fg).astype(BF16)
            b_out[j] = _chunk_cumsum(jnp.log2(fg), rev)

    @pl.when((n == 3) | (n == 4))
    def _():
        silu_out(hq_ref)

    @pl.when((n == 5) | (n == 6))
    def _():
        forget_out(kf_ref, bf_ref, False)

    @pl.when((n == 7) | (n == 8))
    def _():
        forget_out(kb_ref, bb_ref, True)

    @pl.when((n == 9) | (n == 10))
    def _():
        for j in range(heads):
            hv_ref[j] = head(j).astype(BF16)

    @pl.when(n >= 11)
    def _():
        silu_out(gate_ref)


def _input_projection(xs, mod_l, n1w, w_in, qnw, knw, lb, cos_t, sin_t):
    bm, bn = ROW_TILE, IN_COL_TILE
    n_m, n_n = T_ROWS // bm, IN_WIDTH // bn
    hpt = bn // HEAD_DIM

    def head_spec(first_tile):
        return pl.BlockSpec((hpt, bm, HEAD_DIM),
                            lambda m, n: (jnp.clip(n - first_tile, 0, 1), m, 0))

    def head_shape(dtype):
        return jax.ShapeDtypeStruct((HG_HEADS, T_ROWS, HEAD_DIM), dtype)

    kv_spec = pl.BlockSpec((ATTN_KV_HEADS, bm, HEAD_DIM), lambda m, n: (0, m, 0))
    kv_shape = jax.ShapeDtypeStruct((ATTN_KV_HEADS, T_ROWS, HEAD_DIM), BF16)
    return pl.pallas_call(
        _inproj_kernel,
        out_shape=(head_shape(BF16), kv_shape, kv_shape, head_shape(BF16),
                   head_shape(BF16), head_shape(F32), head_shape(BF16), head_shape(F32),
                   head_shape(BF16), head_shape(BF16)),
        grid=(n_m, n_n),
        in_specs=[
            pl.BlockSpec((bm, D_MODEL), lambda m, n: (m, 0)),
            pl.BlockSpec((SUBLANES, 6 * D_MODEL), lambda m, n: (0, 0)),
            pl.BlockSpec((1, D_MODEL), lambda m, n: (0, 0)),
            pl.BlockSpec((D_MODEL, bn), lambda m, n: (0, n)),
            pl.BlockSpec((1, HEAD_DIM), lambda m, n: (0, 0)),
            pl.BlockSpec((1, HEAD_DIM), lambda m, n: (0, 0)),
            pl.BlockSpec((1, bn), lambda m, n: (0, jnp.clip(n - 5, 0, 3))),
            pl.BlockSpec((bm, HEAD_DIM), lambda m, n: (m, 0)),
            pl.BlockSpec((bm, HEAD_DIM), lambda m, n: (m, 0)),
        ],
        out_specs=(head_spec(0), kv_spec, kv_spec, head_spec(3),
                   head_spec(5), head_spec(5), head_spec(7), head_spec(7),
                   head_spec(9), head_spec(11)),
        scratch_shapes=[pltpu.VMEM((bm, D_MODEL), BF16)],
        compiler_params=_cparams(2),
        name="input_projection",
    )(xs, mod_l, n1w, w_in, qnw, knw, lb, cos_t, sin_t)


def _attn_kernel(q_ref, k_ref, vt_ref, o_ref, acc_scr, st_scr):
    i = pl.program_id(1)
    bq = q_ref.shape[1]
    bk = ATTN_K_TILE
    n_chunks = T_ROWS // bk

    def scores(kc, r, slot):
        half = kc.shape[0] // 2
        mx = []
        for a in range(2):
            st = lax.dot_general(kc[a * half:(a + 1) * half], q_ref[r], (((1,), (1,)), ((), ())),
                                 preferred_element_type=F32)
            st_scr[slot, a * half:(a + 1) * half, :] = st
            mx.append(st.max(axis=0, keepdims=True))
        return jnp.maximum(mx[0], mx[1])

    def update(vtc, r, slot, m_cur, m, l):
        width = vtc.shape[1]
        m_new = jnp.maximum(m, m_cur)
        alpha = jnp.exp2(m - m_new)
        p = jnp.exp2(st_scr[slot, 0:width, :] - m_new)
        l = alpha * l + p.sum(axis=0, keepdims=True)
        acc_scr[r] = alpha * acc_scr[r] + jnp.dot(vtc, p.astype(BF16), preferred_element_type=F32)
        return m_new, l

    def finish(l):
        for r in range(ATTN_GROUP):
            o_ref[:, r * HEAD_DIM:(r + 1) * HEAD_DIM] = (acc_scr[r] / l[r]).T.astype(BF16)

    acc_scr[...] = jnp.zeros(acc_scr.shape, F32)
    m0 = tuple(jnp.full((1, bq), -jnp.inf, F32) for _ in range(ATTN_GROUP))
    l0 = tuple(jnp.zeros((1, bq), F32) for _ in range(ATTN_GROUP))

    @pl.when(i == 0)
    def _():
        kc = k_ref[0, 0:CTX_LEN, :]
        vtc = vt_ref[0, 0, :, 0:CTX_LEN]
        l = []
        mc = scores(kc, 0, 0)
        for r in range(ATTN_GROUP):
            mc_next = scores(kc, r + 1, (r + 1) % 2) if r + 1 < ATTN_GROUP else None
            l.append(update(vtc, r, r % 2, mc, m0[r], l0[r])[1])
            mc = mc_next
        finish(l)

    @pl.when(i > 0)
    def _():
        def chunk(j):
            return k_ref[0, pl.ds(pl.multiple_of(j * bk, bk), bk), :]

        def body(j, carry):
            mc, m, l = carry
            m, l = list(m), list(l)
            vtc = vt_ref[0, j]
            for r in range(ATTN_GROUP):
                if r + 1 < ATTN_GROUP:
                    mc_next = scores(chunk(j), r + 1, (r + 1) % 2)
                else:
                    mc_next = scores(chunk(jnp.minimum(j + 1, n_chunks - 1)), 0, 0)
                m[r], l[r] = update(vtc, r, r % 2, mc, m[r], l[r])
                mc = mc_next
            return mc, tuple(m), tuple(l)

        _, _, l = lax.fori_loop(0, n_chunks, body, (scores(chunk(0), 0, 0), m0, l0), unroll=True)
        finish(l)


def _attention(q, k, v):
    bq, bk = ATTN_Q_TILE, ATTN_K_TILE
    n_chunks = T_ROWS // bk
    vt = v.reshape(ATTN_KV_HEADS, n_chunks, bk, HEAD_DIM).transpose(0, 1, 3, 2)
    return pl.pallas_call(
        _attn_kernel,
        out_shape=jax.ShapeDtypeStruct((T_ROWS, ATTN_WIDTH), BF16),
        grid=(ATTN_KV_HEADS, T_ROWS // bq),
        in_specs=[
            pl.BlockSpec((ATTN_GROUP, bq, HEAD_DIM), lambda g, i: (g, i, 0)),
            pl.BlockSpec((1, T_ROWS, HEAD_DIM), lambda g, i: (g, 0, 0)),
            pl.BlockSpec((1, n_chunks, HEAD_DIM, bk), lambda g, i: (g, 0, 0, 0)),
        ],
        out_specs=pl.BlockSpec((bq, ATTN_GROUP * HEAD_DIM), lambda g, i: (i, g)),
        scratch_shapes=[pltpu.VMEM((ATTN_GROUP, HEAD_DIM, bq), F32),
                        pltpu.VMEM((2, bk, bq), F32)],
        compiler_params=_cparams(2),
        name="gqa_attention",
    )(q, k, vt)


def _chunk_cumsum(g, rev):
    rows = g.shape[0]
    pos = lax.broadcasted_iota(jnp.int32, g.shape, 0) % HG_CHUNK
    x = g
    step = 1
    while step < HG_CHUNK:
        if rev:
            x = x + jnp.where(pos < HG_CHUNK - step, pltpu.roll(x, rows - step, axis=0), 0.0)
        else:
            x = x + jnp.where(pos >= step, pltpu.roll(x, step, axis=0), 0.0)
        step *= 2
    return x


def _hgrn_levels():
    halves = []
    half = HG_CHUNK // 2
    while half >= HG_DIAG:
        halves.append(half)
        half //= 2
    return halves


def _hgrn_consts(rev):
    c = HG_CHUNK
    levels = {}
    for half in _hgrn_levels():
        blk = 2 * half
        levels[half] = []
        for g in range(c // SUBLANES):
            t = lax.broadcasted_iota(jnp.int32, (SUBLANES, c), 0) + g * SUBLANES
            s = lax.broadcasted_iota(jnp.int32, (SUBLANES, c), 1)
            same = t // blk == s // blk
            if rev:
                levels[half].append(same & (t % blk < half) & (s % blk >= half))
            else:
                levels[half].append(same & (t % blk >= half) & (s % blk < half))
    t = lax.broadcasted_iota(jnp.int32, (c, c), 0)
    s = lax.broadcasted_iota(jnp.int32, (c, c), 1)
    band = []
    for d in range(HG_DIAG):
        if rev:
            band.append((s == t + d) & (t % HG_DIAG + d < HG_DIAG))
        else:
            band.append((s == t - d) & (t % HG_DIAG >= d))
    return levels, band


def _hgrn_scores(q, k, b, brow, rev, consts):
    levels, band = consts
    c = HG_CHUNK
    nt = (((1,), (1,)), ((), ()))
    groups = c // SUBLANES
    row8 = lax.broadcasted_iota(jnp.int32, (SUBLANES, HEAD_DIM), 0)
    rows = [slice(g * SUBLANES, (g + 1) * SUBLANES) for g in range(groups)]

    a = jnp.zeros((c, c), F32)
    for d in range(HG_DIAG):
        if d == 0:
            prod = q * k
        else:
            shift = c - d if rev else d
            prod = q * pltpu.roll(k, shift, axis=0) * jnp.exp2(b - pltpu.roll(b, shift, axis=0))
        a = jnp.where(band[d], prod.sum(axis=-1, keepdims=True), a)
    rows_a = [a[r] for r in rows]

    for half in _hgrn_levels():
        blk = 2 * half
        ref_off = half if rev else half - 1
        zs = []
        for g in range(groups):
            t0 = g * SUBLANES
            if half >= SUBLANES:
                ref = brow(t0 // blk * blk + ref_off)
                attending = (t0 % blk < half) if rev else (t0 % blk >= half)
                x = q[rows[g]] if attending else k[rows[g]]
                e = jnp.exp2(b[rows[g]] - ref if attending else ref - b[rows[g]])
            else:
                ref = brow(t0 + ref_off)
                for i in range(1, SUBLANES // blk):
                    ref = jnp.where(row8 >= i * blk, brow(t0 + i * blk + ref_off), ref)
                attending = (row8 % blk < half) if rev else (row8 % blk >= half)
                x = jnp.where(attending, q[rows[g]], k[rows[g]])
                e = jnp.exp2(-jnp.abs(b[rows[g]] - ref))
            zs.append(x * e)
        z = jnp.concatenate(zs, axis=0).astype(BF16)
        lv = lax.dot_general(z, z, nt, preferred_element_type=F32)
        for g in range(groups):
            rows_a[g] = jnp.where(levels[half][g], lv[rows[g]], rows_a[g])
    return jnp.concatenate(rows_a, axis=0)


def _hgrn_kernel(qf_ref, kf_ref, bf_ref, vf_ref, qb_ref, kb_ref, bb_ref, vb_ref,
                 of_ref, ob_ref, s_scr):
    @pl.when(pl.program_id(0) == 0)
    def _():
        s_scr[...] = jnp.zeros(s_scr.shape, F32)

    c = HG_CHUNK
    n_chunks = HG_BLOCK // c
    consts = (_hgrn_consts(False), _hgrn_consts(True))
    nt = (((1,), (1,)), ((), ()))
    tn = (((0,), (0,)), ((), ()))
    dirs = ((qf_ref, kf_ref, bf_ref, vf_ref, of_ref, False),
            (qb_ref, kb_ref, bb_ref, vb_ref, ob_ref, True))

    def head_body(h, carry):
        jobs = []
        for di, (q_ref, k_ref, b_ref, v_ref, o_ref, rev) in enumerate(dirs):
            order = reversed(range(n_chunks)) if rev else range(n_chunks)
            for ci in order:
                rows = slice(ci * c, (ci + 1) * c)
                q = q_ref[h, rows, :].astype(F32)
                k = k_ref[h, rows, :].astype(F32)
                b = b_ref[h, rows, :]
                v = v_ref[h, rows, :]
                tot = b[0:1] if rev else b[c - 1:c]
                q_in = (q * jnp.exp2(b)).astype(BF16)
                k_out = (k * jnp.exp2(tot - b)).astype(BF16)
                upd = lax.dot_general(v, k_out, tn, preferred_element_type=F32)

                def brow(r, b_ref=b_ref, base=ci * c):
                    return b_ref[h, base + r:base + r + 1, :]

                a = _hgrn_scores(q, k, b, brow, rev, consts[di]).astype(BF16)
                jobs.append((di, o_ref, rows, q_in, jnp.exp2(tot), upd, a, v))
        st = [s_scr[0, h], s_scr[1, h]]
        for di, o_ref, rows, q_in, decay, upd, a, v in jobs:
            o = lax.dot_general(q_in, st[di].astype(BF16), nt, preferred_element_type=F32)
            st[di] = decay * st[di] + upd
            o_ref[h, rows, :] = o + jnp.dot(a, v, preferred_element_type=F32)
        s_scr[0, h] = st[0]
        s_scr[1, h] = st[1]
        return carry

    lax.fori_loop(0, HG_HEADS, head_body, 0)


def _hgrn2(hq, kf, bf, kb, bb, hv):
    nb = T_ROWS // HG_BLOCK
    fwd = lambda i: (0, i, 0)
    bwd = lambda i: (0, jnp.where(i == 0, 0, nb - i), 0)
    blk = (HG_HEADS, HG_BLOCK, HEAD_DIM)
    out = jax.ShapeDtypeStruct((HG_HEADS, T_ROWS, HEAD_DIM), F32)
    return pl.pallas_call(
        _hgrn_kernel,
        out_shape=(out, out),
        grid=(nb,),
        in_specs=[pl.BlockSpec(blk, fwd)] * 4 + [pl.BlockSpec(blk, bwd)] * 4,
        out_specs=(pl.BlockSpec(blk, fwd), pl.BlockSpec(blk, bwd)),
        scratch_shapes=[pltpu.VMEM((2, HG_HEADS, HEAD_DIM, HEAD_DIM), F32)],
        compiler_params=_cparams(1),
        name="hgrn2_scan",
    )(hq, kf, bf, hv, hq, kb, bb, hv)


def _outproj_kernel(attn_ref, of_ref, ob_ref, gate_ref, hnw_ref, w_ref, x_ref, mod_ref, n2w_ref,
                    xo_ref, h2_ref, mix_scr):
    m = pl.program_id(0)
    bm = x_ref.shape[0]
    for h in range(HG_HEADS):
        y = _rms(of_ref[h] + ob_ref[h], hnw_ref[...])
        mix_scr[:, h * HEAD_DIM:(h + 1) * HEAD_DIM] = (y * gate_ref[h].astype(F32)).astype(BF16)
    acc = jnp.dot(attn_ref[...], w_ref[0:ATTN_WIDTH, :], preferred_element_type=F32)
    acc = acc + jnp.dot(mix_scr[...], w_ref[ATTN_WIDTH:ATTN_WIDTH + HG_WIDTH, :],
                        preferred_element_type=F32)
    is_ctx = _row_is_ctx(m, bm)
    x_new = x_ref[...] + _mod_rows(mod_ref, is_ctx, 2) * acc
    xo_ref[...] = x_new
    h2 = _rms(x_new, n2w_ref[...]) * (1.0 + _mod_rows(mod_ref, is_ctx, 4)) + _mod_rows(mod_ref, is_ctx, 3)
    h2_ref[...] = h2.astype(BF16)


def _output_projection(attn, o_f, o_b, gate, hnw, w_out, xs, mod_l, n2w):
    bm = OUT_ROW_TILE
    head_spec = pl.BlockSpec((HG_HEADS, bm, HEAD_DIM), lambda m: (0, m, 0))
    row_spec = pl.BlockSpec((bm, D_MODEL), lambda m: (m, 0))
    return pl.pallas_call(
        _outproj_kernel,
        out_shape=(jax.ShapeDtypeStruct((T_ROWS, D_MODEL), F32),
                   jax.ShapeDtypeStruct((T_ROWS, D_MODEL), BF16)),
        grid=(T_ROWS // bm,),
        in_specs=[
            pl.BlockSpec((bm, ATTN_WIDTH), lambda m: (m, 0)),
            head_spec, head_spec, head_spec,
            pl.BlockSpec((1, HEAD_DIM), lambda m: (0, 0)),
            pl.BlockSpec((ATTN_WIDTH + HG_WIDTH, D_MODEL), lambda m: (0, 0)),
            row_spec,
            pl.BlockSpec((SUBLANES, 6 * D_MODEL), lambda m: (0, 0)),
            pl.BlockSpec((1, D_MODEL), lambda m: (0, 0)),
        ],
        out_specs=(row_spec, row_spec),
        scratch_shapes=[pltpu.VMEM((bm, HG_WIDTH), BF16)],
        compiler_params=_cparams(1),
        name="output_projection",
    )(attn, o_f, o_b, gate, hnw, w_out, xs, mod_l, n2w)


def _ffn_up_kernel(h_ref, hp_ref, hn_ref, wg_ref, wu_ref, cw_ref, cb_ref, u_ref, hcat_scr, g_scr):
    m = pl.program_id(0)
    n = pl.program_id(1)
    bm = h_ref.shape[0]
    hl = CONV_HALO

    @pl.when(n == 0)
    def _():
        hcat_scr[0:hl, :] = hp_ref[...]
        hcat_scr[hl:hl + bm, :] = h_ref[...]
        hcat_scr[hl + bm:hl + bm + hl, :] = hn_ref[...]

    g_scr[...] = jnp.dot(hcat_scr[...], wg_ref[...], preferred_element_type=F32)
    up = jnp.dot(h_ref[...], wu_ref[...], preferred_element_type=F32)
    r = m * bm + lax.broadcasted_iota(jnp.int32, (bm, 1), 0)
    has_prev = (r != 0) & (r != CTX_LEN)
    has_next = (r != CTX_LEN - 1) & (r != T_ROWS - 1)
    g_prev = jnp.where(has_prev, g_scr[hl - 1:hl - 1 + bm, :], 0.0)
    g_next = jnp.where(has_next, g_scr[hl + 1:hl + 1 + bm, :], 0.0)
    gate = (g_prev * cw_ref[0:1, :] + g_scr[hl:hl + bm, :] * cw_ref[1:2, :]
            + g_next * cw_ref[2:3, :] + cb_ref[...])
    u_ref[...] = (_silu(gate) * up).astype(BF16)


def _ffn_up(h2, w_up, conv_w, conv_b):
    bm, bn, hl = ROW_TILE, FF_COL_TILE, CONV_HALO
    n_n = D_FF // bn
    per = bm // hl
    last = T_ROWS // hl - 1
    return pl.pallas_call(
        _ffn_up_kernel,
        out_shape=jax.ShapeDtypeStruct((T_ROWS, D_FF), BF16),
        grid=(T_ROWS // bm, n_n),
        in_specs=[
            pl.BlockSpec((bm, D_MODEL), lambda m, n: (m, 0)),
            pl.BlockSpec((hl, D_MODEL), lambda m, n: (jnp.maximum(m * per - 1, 0), 0)),
            pl.BlockSpec((hl, D_MODEL), lambda m, n: (jnp.minimum((m + 1) * per, last), 0)),
            pl.BlockSpec((D_MODEL, bn), lambda m, n: (0, n)),
            pl.BlockSpec((D_MODEL, bn), lambda m, n: (0, n_n + n)),
            pl.BlockSpec((3, bn), lambda m, n: (0, n)),
            pl.BlockSpec((1, bn), lambda m, n: (0, n)),
        ],
        out_specs=pl.BlockSpec((bm, bn), lambda m, n: (m, n)),
        scratch_shapes=[pltpu.VMEM((bm + 2 * hl, D_MODEL), BF16),
                        pltpu.VMEM((bm + 2 * hl, bn), F32)],
        compiler_params=_cparams(2),
        name="ffn_up_conv",
    )(h2, h2, h2, w_up, w_up, conv_w, conv_b)


def _ffn_down_kernel(u_ref, w_ref, x_ref, mod_ref, fw_ref, o_ref, acc_scr, *, final):
    m = pl.program_id(0)
    k = pl.program_id(1)
    bm = x_ref.shape[0]

    @pl.when(k == 0)
    def _():
        acc_scr[...] = jnp.zeros(acc_scr.shape, F32)

    acc_scr[...] += jnp.dot(u_ref[...], w_ref[...], preferred_element_type=F32)

    @pl.when(k == pl.num_programs(1) - 1)
    def _():
        x_new = x_ref[...] + _mod_rows(mod_ref, _row_is_ctx(m, bm), 5) * acc_scr[...]
        o_ref[...] = _rms(x_new, fw_ref[...]) if final else x_new


def _ffn_down(u, w_down, xs, mod_l, final_w, final):
    bm, bk = ROW_TILE, FF_COL_TILE
    row_spec = pl.BlockSpec((bm, D_MODEL), lambda m, k: (m, 0))
    return pl.pallas_call(
        functools.partial(_ffn_down_kernel, final=final),
        out_shape=jax.ShapeDtypeStruct((T_ROWS, D_MODEL), F32),
        grid=(T_ROWS // bm, D_FF // bk),
        in_specs=[
            pl.BlockSpec((bm, bk), lambda m, k: (m, k)),
            pl.BlockSpec((bk, D_MODEL), lambda m, k: (k, 0)),
            row_spec,
            pl.BlockSpec((SUBLANES, 6 * D_MODEL), lambda m, k: (0, 0)),
            pl.BlockSpec((1, D_MODEL), lambda m, k: (0, 0)),
        ],
        out_specs=row_spec,
        scratch_shapes=[pltpu.VMEM((bm, D_MODEL), F32)],
        compiler_params=_cparams(2),
        name="ffn_down",
    )(u, w_down, xs, mod_l, final_w)


def _rope_tables():
    rows = SEQ // GRID_W
    row = jnp.repeat(jnp.arange(rows, dtype=F32), GRID_W)
    col = jnp.tile(jnp.arange(GRID_W, dtype=F32), rows)
    freqs = ROPE_THETA ** (-jnp.arange(ROPE_FREQS, dtype=F32) / ROPE_FREQS)
    ar, ac = row[:, None] * freqs, col[:, None] * freqs
    cos = jnp.concatenate([jnp.cos(ar), jnp.cos(ar), jnp.cos(ac), jnp.cos(ac)], axis=1)
    sin = jnp.concatenate([-jnp.sin(ar), jnp.sin(ar), -jnp.sin(ac), jnp.sin(ac)], axis=1)
    cos = jnp.concatenate([jnp.ones((CTX_LEN, HEAD_DIM), F32), cos], axis=0)
    sin = jnp.concatenate([jnp.zeros((CTX_LEN, HEAD_DIM), F32), sin], axis=0)
    return cos, sin


def kernel(x, c, ctx, c_ctx, w_mod, b_mod, norm1_w, norm2_w, w_in, q_norm_w, k_norm_w, hg_lb_logits,
           hg_norm_w, w_out, w_up, conv_w, conv_b, w_down, final_norm_w):
    assert x.shape == (1, SEQ, D_MODEL) and ctx.shape == (1, CTX_LEN, D_MODEL)
    xs = jnp.concatenate([ctx[0], x[0]], axis=0)
    mod = _modulation(jnp.stack([c[0], c_ctx], axis=1), w_mod, b_mod)
    cos_t, sin_t = _rope_tables()
    lb_sm = jax.nn.softmax(hg_lb_logits.astype(F32), axis=1)
    lb_all = jnp.cumsum(lb_sm, axis=1) - lb_sm[:, :1]

    for l in range(DEPTH):
        lb = jnp.concatenate([lb_all[0, l], lb_all[1, l]])[None, :]
        q, k, v, hq, kf, bf, kb, bb, hv, gate = _input_projection(
            xs, mod[l], norm1_w[l][None, :], w_in[l].astype(BF16), q_norm_w[l][None, :],
            k_norm_w[l][None, :], lb, cos_t, sin_t)
        attn = _attention(q, k, v)
        o_f, o_b = _hgrn2(hq, kf, bf, kb, bb, hv)
        xs, h2 = _output_projection(attn, o_f, o_b, gate, hg_norm_w[l][None, :],
                                    w_out[l].astype(BF16), xs, mod[l], norm2_w[l][None, :])
        u = _ffn_up(h2, w_up[l].astype(BF16), conv_w[l], conv_b[l][None, :])
        xs = _ffn_down(u, w_down[l].astype(BF16), xs, mod[l], final_norm_w[None, :],
                       final=(l == DEPTH - 1))
    return xs[CTX_LEN:][None]
```

```python
import functools

import jax
import jax.numpy as jnp
from jax import lax
from jax.experimental import pallas as pl
from jax.experimental.pallas import tpu as pltpu

F32 = jnp.float32
BF16 = jnp.bfloat16

D_MODEL = 2048
SEQ = 8192
DEPTH = 4
CTX_LEN = 256
T_ROWS = CTX_LEN + SEQ
GRID_W = 64
HEAD_DIM = 128
ATTN_HEADS = 8
ATTN_KV_HEADS = 2
ATTN_GROUP = 4
ATTN_WIDTH = 1024
KV_WIDTH = 256
HG_HEADS = 8
HG_WIDTH = 1024
IN_WIDTH = 6656
ROPE_THETA = 10000.0
ROPE_FREQS = 32
D_FF = 5632
EPS = 1e-6
Q_SCALE = HEAD_DIM ** -0.5 * 1.4426950408889634

LANES = 128
SUBLANES = 8
VMEM_LIMIT = 56 * 1024 * 1024

ROW_TILE = 768
OUT_ROW_TILE = 384
IN_COL_TILE = 512
FF_COL_TILE = 512
ATTN_Q_TILE = 256
ATTN_K_TILE = 768
HG_BLOCK = 256
HG_CHUNK = 64
HG_DIAG = 2
CONV_HALO = 16


def _cparams(n_axes):
    return pltpu.CompilerParams(dimension_semantics=("arbitrary",) * n_axes,
                                vmem_limit_bytes=VMEM_LIMIT)


def _rms(x, w):
    return x * lax.rsqrt(jnp.mean(x * x, axis=-1, keepdims=True) + EPS) * w


def _silu(x):
    return x * jax.nn.sigmoid(x)


def _row_is_ctx(tile_idx, rows):
    r = tile_idx * rows + lax.broadcasted_iota(jnp.int32, (rows, 1), 0)
    return r < CTX_LEN


def _mod_rows(mod_ref, is_ctx, k):
    lat = mod_ref[0:1, k * D_MODEL:(k + 1) * D_MODEL]
    ctx = mod_ref[1:2, k * D_MODEL:(k + 1) * D_MODEL]
    return jnp.where(is_ctx, ctx, lat)


def _mod_kernel(s_ref, w_ref, b_ref, o_ref):
    bn = w_ref.shape[2]
    rc = 64

    def body(i, acc):
        a0, a1 = acc
        r0 = pl.multiple_of(i * rc, rc)
        w = w_ref[0, pl.ds(r0, rc), :]
        s = _silu(s_ref[pl.ds(r0, rc), :])
        a0 = a0 + (w * s[:, 0:1]).reshape(rc // SUBLANES, SUBLANES, bn).sum(axis=0)
        a1 = a1 + (w * s[:, 1:2]).reshape(rc // SUBLANES, SUBLANES, bn).sum(axis=0)
        return a0, a1

    z = jnp.zeros((SUBLANES, bn), F32)
    a0, a1 = lax.fori_loop(0, D_MODEL // rc, body, (z, z))
    bias = b_ref[0]
    r0 = a0.sum(axis=0, keepdims=True) + bias
    r1 = a1.sum(axis=0, keepdims=True) + bias
    o_ref[0] = jnp.concatenate([r0, r1, jnp.zeros((SUBLANES - 2, bn), F32)], axis=0)


def _modulation(s_in, w_mod, b_mod):
    bn = 1024
    n_out = 6 * D_MODEL
    return pl.pallas_call(
        _mod_kernel,
        out_shape=jax.ShapeDtypeStruct((DEPTH, SUBLANES, n_out), F32),
        grid=(DEPTH, n_out // bn),
        in_specs=[
            pl.BlockSpec((D_MODEL, 2), lambda l, n: (0, 0)),
            pl.BlockSpec((1, D_MODEL, bn), lambda l, n: (l, 0, n)),
            pl.BlockSpec((1, 1, bn), lambda l, n: (l, 0, n)),
        ],
        out_specs=pl.BlockSpec((1, SUBLANES, bn), lambda l, n: (l, 0, n)),
        compiler_params=_cparams(2),
        name="adaln_modulation",
    )(s_in, w_mod, b_mod.reshape(DEPTH, 1, n_out))


def _norm_rope_matrix():
    i = lax.broadcasted_iota(jnp.int32, (2 * HEAD_DIM, 2 * HEAD_DIM), 0)
    j = lax.broadcasted_iota(jnp.int32, (2 * HEAD_DIM, 2 * HEAD_DIM), 1)
    ones = (i < HEAD_DIM) & (j < HEAD_DIM)
    ii, jj = i - HEAD_DIM, j - HEAD_DIM
    low = (jj // ROPE_FREQS) % 2 == 0
    rot = jnp.where(low & (ii == jj + ROPE_FREQS), -1.0,
                    jnp.where(~low & (ii == jj - ROPE_FREQS), 1.0, 0.0))
    rot = jnp.where((i >= HEAD_DIM) & (j >= HEAD_DIM), rot, 0.0)
    return jnp.where(ones, 1.0, rot).astype(BF16)


def _norm_rope(p, w, mat_ref, cos, sin, scale):
    pw = p * w
    lhs = jnp.concatenate([(p * p).astype(BF16), pw.astype(BF16)], axis=1)
    red = jnp.dot(lhs, mat_ref[...], preferred_element_type=F32)
    r = lax.rsqrt(red[:, 0:HEAD_DIM] * (1.0 / HEAD_DIM) + EPS) * scale
    return r * (pw * cos + red[:, HEAD_DIM:2 * HEAD_DIM] * sin)


def _inproj_kernel(x_ref, mod_ref, n1w_ref, w_ref, qnw_ref, knw_ref, lb_ref, cos_ref, sin_ref, mat_ref,
                   q_ref, k_ref, v_ref, hq_ref, kf_ref, bf_ref, kb_ref, bb_ref, hv_ref, gate_ref,
                   h_scr):
    m = pl.program_id(0)
    n = pl.program_id(1)

    def modulated(rows, mod_row):
        x = x_ref[rows, :]
        r = lax.rsqrt(jnp.mean(x * x, axis=-1, keepdims=True) + EPS)
        gain = n1w_ref[...] * (1.0 + mod_ref[mod_row:mod_row + 1, D_MODEL:2 * D_MODEL])
        return (x * r * gain + mod_ref[mod_row:mod_row + 1, 0:D_MODEL]).astype(BF16)

    @pl.when(n == 0)
    def _():
        h_scr[...] = modulated(slice(None), 0)

    @pl.when((n == 0) & (m == 0))
    def _():
        h_scr[0:CTX_LEN, :] = modulated(slice(0, CTX_LEN), 1)

    def project(epilogue):
        half = IN_COL_TILE // 2
        hpt = half // HEAD_DIM
        h = h_scr[...]
        ps = [jnp.dot(h, w_ref[:, a * half:(a + 1) * half], preferred_element_type=F32)
              for a in range(2)]
        for a, p in enumerate(ps):
            for j in range(hpt):
                epilogue(a * hpt + j, p[:, j * HEAD_DIM:(j + 1) * HEAD_DIM])

    def q_out(j, p):
        q_ref[j] = _norm_rope(p, qnw_ref[...], mat_ref, cos_ref[...], sin_ref[...],
                              Q_SCALE).astype(BF16)

    def kv_out(j, p):
        if j < ATTN_KV_HEADS:
            k_ref[j] = _norm_rope(p, knw_ref[...], mat_ref, cos_ref[...], sin_ref[...],
                                  1.0).astype(BF16)
        else:
            v_ref[j - ATTN_KV_HEADS] = p.astype(BF16)

    def silu_out(ref):
        def out(j, p):
            ref[j] = _silu(p).astype(BF16)
        return out

    def forget_out(k_out, b_out, rev):
        def out(j, p):
            lb = lb_ref[:, j * HEAD_DIM:(j + 1) * HEAD_DIM]
            fg = lb + (1.0 - lb) * jax.nn.sigmoid(p)
            k_out[j] = (1.0 - fg).astype(BF16)
            b_out[j] = _chunk_cumsum(jnp.log2(fg), rev)
        return out

    def plain_out(j, p):
        hv_ref[j] = p.astype(BF16)

    pl.when(n < 2)(lambda: project(q_out))
    pl.when(n == 2)(lambda: project(kv_out))
    pl.when((n == 3) | (n == 4))(lambda: project(silu_out(hq_ref)))
    pl.when((n == 5) | (n == 6))(lambda: project(forget_out(kf_ref, bf_ref, False)))
    pl.when((n == 7) | (n == 8))(lambda: project(forget_out(kb_ref, bb_ref, True)))
    pl.when((n == 9) | (n == 10))(lambda: project(plain_out))
    pl.when(n >= 11)(lambda: project(silu_out(gate_ref)))


def _input_projection(layer, xs, mod, n1w, w_in, qnw, knw, lb, cos_t, sin_t):
    bm, bn = ROW_TILE, IN_COL_TILE
    n_m, n_n = T_ROWS // bm, IN_WIDTH // bn
    hpt = bn // HEAD_DIM

    def head_spec(first_tile):
        return pl.BlockSpec((hpt, bm, HEAD_DIM),
                            lambda m, n: (jnp.clip(n - first_tile, 0, 1), m, 0))

    def head_shape(dtype):
        return jax.ShapeDtypeStruct((HG_HEADS, T_ROWS, HEAD_DIM), dtype)

    kv_spec = pl.BlockSpec((ATTN_KV_HEADS, bm, HEAD_DIM), lambda m, n: (0, m, 0))
    kv_shape = jax.ShapeDtypeStruct((ATTN_KV_HEADS, T_ROWS, HEAD_DIM), BF16)
    return pl.pallas_call(
        _inproj_kernel,
        out_shape=(head_shape(BF16), kv_shape, kv_shape, head_shape(BF16),
                   head_shape(BF16), head_shape(F32), head_shape(BF16), head_shape(F32),
                   head_shape(BF16), head_shape(BF16)),
        grid=(n_m, n_n),
        in_specs=[
            pl.BlockSpec((bm, D_MODEL), lambda m, n: (m, 0)),
            pl.BlockSpec((None, SUBLANES, 6 * D_MODEL), lambda m, n: (layer, 0, 0)),
            pl.BlockSpec((1, D_MODEL), lambda m, n: (0, 0)),
            pl.BlockSpec((None, D_MODEL, bn), lambda m, n: (layer, 0, n)),
            pl.BlockSpec((1, HEAD_DIM), lambda m, n: (0, 0)),
            pl.BlockSpec((1, HEAD_DIM), lambda m, n: (0, 0)),
            pl.BlockSpec((1, bn), lambda m, n: (0, jnp.clip(n - 5, 0, 3))),
            pl.BlockSpec((bm, HEAD_DIM), lambda m, n: (m, 0)),
            pl.BlockSpec((bm, HEAD_DIM), lambda m, n: (m, 0)),
            pl.BlockSpec((2 * HEAD_DIM, 2 * HEAD_DIM), lambda m, n: (0, 0)),
        ],
        out_specs=(head_spec(0), kv_spec, kv_spec, head_spec(3),
                   head_spec(5), head_spec(5), head_spec(7), head_spec(7),
                   head_spec(9), head_spec(11)),
        scratch_shapes=[pltpu.VMEM((bm, D_MODEL), BF16)],
        compiler_params=_cparams(2),
        name="input_projection",
    )(xs, mod, n1w, w_in, qnw, knw, lb, cos_t, sin_t, _norm_rope_matrix())


def _attn_kernel(q_ref, k_ref, vt_ref, o_ref, acc_scr, st_scr):
    i = pl.program_id(1)
    bq = q_ref.shape[1]
    bk = ATTN_K_TILE
    n_chunks = T_ROWS // bk

    def scores(kc, r, slot):
        half = kc.shape[0] // 2
        mx = []
        for a in range(2):
            st = lax.dot_general(kc[a * half:(a + 1) * half], q_ref[r], (((1,), (1,)), ((), ())),
                                 preferred_element_type=F32)
            st_scr[slot, a * half:(a + 1) * half, :] = st
            mx.append(st.max(axis=0, keepdims=True))
        return jnp.maximum(mx[0], mx[1])

    def update(vtc, r, slot, m_cur, m, l):
        width = vtc.shape[1]
        m_new = jnp.maximum(m, m_cur)
        alpha = jnp.exp2(m - m_new)
        p = jnp.exp2(st_scr[slot, 0:width, :] - m_new)
        l = alpha * l + p.sum(axis=0, keepdims=True)
        acc_scr[r] = alpha * acc_scr[r] + jnp.dot(vtc, p.astype(BF16), preferred_element_type=F32)
        return m_new, l

    def finish(l):
        for r in range(ATTN_GROUP):
            o_ref[:, r * HEAD_DIM:(r + 1) * HEAD_DIM] = (acc_scr[r] / l[r]).T.astype(BF16)

    acc_scr[...] = jnp.zeros(acc_scr.shape, F32)
    m0 = tuple(jnp.full((1, bq), -jnp.inf, F32) for _ in range(ATTN_GROUP))
    l0 = tuple(jnp.zeros((1, bq), F32) for _ in range(ATTN_GROUP))

    @pl.when(i == 0)
    def _():
        kc = k_ref[0, 0:CTX_LEN, :]
        vtc = vt_ref[0, 0, :, 0:CTX_LEN]
        l = []
        mc = scores(kc, 0, 0)
        for r in range(ATTN_GROUP):
            mc_next = scores(kc, r + 1, (r + 1) % 2) if r + 1 < ATTN_GROUP else None
            l.append(update(vtc, r, r % 2, mc, m0[r], l0[r])[1])
            mc = mc_next
        finish(l)

    @pl.when(i > 0)
    def _():
        def chunk(j):
            return k_ref[0, pl.ds(pl.multiple_of(j * bk, bk), bk), :]

        def body(j, carry):
            mc, m, l = carry
            m, l = list(m), list(l)
            vtc = vt_ref[0, j]
            for r in range(ATTN_GROUP):
                if r + 1 < ATTN_GROUP:
                    mc_next = scores(chunk(j), r + 1, (r + 1) % 2)
                else:
                    mc_next = scores(chunk(jnp.minimum(j + 1, n_chunks - 1)), 0, 0)
                m[r], l[r] = update(vtc, r, r % 2, mc, m[r], l[r])
                mc = mc_next
            return mc, tuple(m), tuple(l)

        _, _, l = lax.fori_loop(0, n_chunks, body, (scores(chunk(0), 0, 0), m0, l0), unroll=True)
        finish(l)


def _attention(q, k, v):
    bq, bk = ATTN_Q_TILE, ATTN_K_TILE
    n_chunks = T_ROWS // bk
    vt = v.reshape(ATTN_KV_HEADS, n_chunks, bk, HEAD_DIM).transpose(0, 1, 3, 2)
    return pl.pallas_call(
        _attn_kernel,
        out_shape=jax.ShapeDtypeStruct((T_ROWS, ATTN_WIDTH), BF16),
        grid=(ATTN_KV_HEADS, T_ROWS // bq),
        in_specs=[
            pl.BlockSpec((ATTN_GROUP, bq, HEAD_DIM), lambda g, i: (g, i, 0)),
            pl.BlockSpec((1, T_ROWS, HEAD_DIM), lambda g, i: (g, 0, 0)),
            pl.BlockSpec((1, n_chunks, HEAD_DIM, bk), lambda g, i: (g, 0, 0, 0)),
        ],
        out_specs=pl.BlockSpec((bq, ATTN_GROUP * HEAD_DIM), lambda g, i: (i, g)),
        scratch_shapes=[pltpu.VMEM((ATTN_GROUP, HEAD_DIM, bq), F32),
                        pltpu.VMEM((2, bk, bq), F32)],
        compiler_params=_cparams(2),
        name="gqa_attention",
    )(q, k, vt)


def _chunk_cumsum(g, rev):
    rows = g.shape[0]
    pos = lax.broadcasted_iota(jnp.int32, g.shape, 0) % HG_CHUNK
    x = g
    step = 1
    while step < HG_CHUNK:
        if rev:
            x = x + jnp.where(pos < HG_CHUNK - step, pltpu.roll(x, rows - step, axis=0), 0.0)
        else:
            x = x + jnp.where(pos >= step, pltpu.roll(x, step, axis=0), 0.0)
        step *= 2
    return x


def _hgrn_levels():
    halves = []
    half = HG_CHUNK // 2
    while half >= HG_DIAG:
        halves.append(half)
        half //= 2
    return halves


def _hgrn_consts(rev):
    c = HG_CHUNK
    levels = {}
    for half in _hgrn_levels():
        blk = 2 * half
        levels[half] = []
        for g in range(c // SUBLANES):
            t = lax.broadcasted_iota(jnp.int32, (SUBLANES, c), 0) + g * SUBLANES
            s = lax.broadcasted_iota(jnp.int32, (SUBLANES, c), 1)
            same = t // blk == s // blk
            if rev:
                levels[half].append(same & (t % blk < half) & (s % blk >= half))
            else:
                levels[half].append(same & (t % blk >= half) & (s % blk < half))
    t = lax.broadcasted_iota(jnp.int32, (c, c), 0)
    s = lax.broadcasted_iota(jnp.int32, (c, c), 1)
    band = []
    for d in range(HG_DIAG):
        if rev:
            band.append((s == t + d) & (t % HG_DIAG + d < HG_DIAG))
        else:
            band.append((s == t - d) & (t % HG_DIAG >= d))
    return levels, band


def _hgrn_scores(q, k, b, brow, rev, consts):
    levels, band = consts
    c = HG_CHUNK
    nt = (((1,), (1,)), ((), ()))
    groups = c // SUBLANES
    row8 = lax.broadcasted_iota(jnp.int32, (SUBLANES, HEAD_DIM), 0)
    rows = [slice(g * SUBLANES, (g + 1) * SUBLANES) for g in range(groups)]

    a = jnp.zeros((c, c), F32)
    for d in range(HG_DIAG):
        if d == 0:
            prod = q * k
        else:
            shift = c - d if rev else d
            prod = q * pltpu.roll(k, shift, axis=0) * jnp.exp2(b - pltpu.roll(b, shift, axis=0))
        a = jnp.where(band[d], prod.sum(axis=-1, keepdims=True), a)
    rows_a = [a[r] for r in rows]

    for half in _hgrn_levels():
        blk = 2 * half
        ref_off = half if rev else half - 1
        zs = []
        for g in range(groups):
            t0 = g * SUBLANES
            if half >= SUBLANES:
                ref = brow(t0 // blk * blk + ref_off)
                attending = (t0 % blk < half) if rev else (t0 % blk >= half)
                x = q[rows[g]] if attending else k[rows[g]]
                e = jnp.exp2(b[rows[g]] - ref if attending else ref - b[rows[g]])
            else:
                ref = brow(t0 + ref_off)
                for i in range(1, SUBLANES // blk):
                    ref = jnp.where(row8 >= i * blk, brow(t0 + i * blk + ref_off), ref)
                attending = (row8 % blk < half) if rev else (row8 % blk >= half)
                x = jnp.where(attending, q[rows[g]], k[rows[g]])
                e = jnp.exp2(-jnp.abs(b[rows[g]] - ref))
            zs.append(x * e)
        z = jnp.concatenate(zs, axis=0).astype(BF16)
        lv = lax.dot_general(z, z, nt, preferred_element_type=F32)
        for g in range(groups):
            rows_a[g] = jnp.where(levels[half][g], lv[rows[g]], rows_a[g])
    return jnp.concatenate(rows_a, axis=0)


def _hgrn_kernel(qf_ref, kf_ref, bf_ref, vf_ref, qb_ref, kb_ref, bb_ref, vb_ref,
                 of_ref, ob_ref, s_scr):
    @pl.when(pl.program_id(0) == 0)
    def _():
        s_scr[...] = jnp.zeros(s_scr.shape, F32)

    c = HG_CHUNK
    n_chunks = HG_BLOCK // c
    consts = (_hgrn_consts(False), _hgrn_consts(True))
    nt = (((1,), (1,)), ((), ()))
    tn = (((0,), (0,)), ((), ()))
    dirs = ((qf_ref, kf_ref, bf_ref, vf_ref, of_ref, False),
            (qb_ref, kb_ref, bb_ref, vb_ref, ob_ref, True))

    def head_body(h, carry):
        jobs = []
        for di, (q_ref, k_ref, b_ref, v_ref, o_ref, rev) in enumerate(dirs):
            order = reversed(range(n_chunks)) if rev else range(n_chunks)
            for ci in order:
                rows = slice(ci * c, (ci + 1) * c)
                q = q_ref[h, rows, :].astype(F32)
                k = k_ref[h, rows, :].astype(F32)
                b = b_ref[h, rows, :]
                v = v_ref[h, rows, :]
                tot = b[0:1] if rev else b[c - 1:c]
                q_in = (q * jnp.exp2(b)).astype(BF16)
                k_out = (k * jnp.exp2(tot - b)).astype(BF16)
                upd = lax.dot_general(v, k_out, tn, preferred_element_type=F32)

                def brow(r, b_ref=b_ref, base=ci * c):
                    return b_ref[h, base + r:base + r + 1, :]

                a = _hgrn_scores(q, k, b, brow, rev, consts[di]).astype(BF16)
                jobs.append((di, o_ref, rows, q_in, jnp.exp2(tot), upd, a, v))
        st = [s_scr[0, h], s_scr[1, h]]
        for di, o_ref, rows, q_in, decay, upd, a, v in jobs:
            o = lax.dot_general(q_in, st[di].astype(BF16), nt, preferred_element_type=F32)
            st[di] = decay * st[di] + upd
            o_ref[h, rows, :] = o + jnp.dot(a, v, preferred_element_type=F32)
        s_scr[0, h] = st[0]
        s_scr[1, h] = st[1]
        return carry

    lax.fori_loop(0, HG_HEADS, head_body, 0)


def _hgrn2(hq, kf, bf, kb, bb, hv):
    nb = T_ROWS // HG_BLOCK
    fwd = lambda i: (0, i, 0)
    bwd = lambda i: (0, jnp.where(i == 0, 0, nb - i), 0)
    blk = (HG_HEADS, HG_BLOCK, HEAD_DIM)
    out = jax.ShapeDtypeStruct((HG_HEADS, T_ROWS, HEAD_DIM), F32)
    return pl.pallas_call(
        _hgrn_kernel,
        out_shape=(out, out),
        grid=(nb,),
        in_specs=[pl.BlockSpec(blk, fwd)] * 4 + [pl.BlockSpec(blk, bwd)] * 4,
        out_specs=(pl.BlockSpec(blk, fwd), pl.BlockSpec(blk, bwd)),
        scratch_shapes=[pltpu.VMEM((2, HG_HEADS, HEAD_DIM, HEAD_DIM), F32)],
        compiler_params=_cparams(1),
        name="hgrn2_scan",
    )(hq, kf, bf, hv, hq, kb, bb, hv)


def _outproj_kernel(attn_ref, of_ref, ob_ref, gate_ref, hnw_ref, w_ref, x_ref, mod_ref, n2w_ref,
                    xo_ref, h2_ref, mix_scr):
    m = pl.program_id(0)
    bm = x_ref.shape[0]
    acc = jnp.dot(attn_ref[...], w_ref[0:ATTN_WIDTH, :], preferred_element_type=F32)
    for h in range(HG_HEADS):
        y = _rms(of_ref[h] + ob_ref[h], hnw_ref[...])
        mix_scr[:, h * HEAD_DIM:(h + 1) * HEAD_DIM] = (y * gate_ref[h].astype(F32)).astype(BF16)
    acc = acc + jnp.dot(mix_scr[...], w_ref[ATTN_WIDTH:ATTN_WIDTH + HG_WIDTH, :],
                        preferred_element_type=F32)
    is_ctx = _row_is_ctx(m, bm)
    x_new = x_ref[...] + _mod_rows(mod_ref, is_ctx, 2) * acc
    xo_ref[...] = x_new
    h2 = _rms(x_new, n2w_ref[...]) * (1.0 + _mod_rows(mod_ref, is_ctx, 4)) + _mod_rows(mod_ref, is_ctx, 3)
    h2_ref[...] = h2.astype(BF16)


def _output_projection(layer, attn, o_f, o_b, gate, hnw, w_out, xs, mod, n2w):
    bm = OUT_ROW_TILE
    head_spec = pl.BlockSpec((HG_HEADS, bm, HEAD_DIM), lambda m: (0, m, 0))
    row_spec = pl.BlockSpec((bm, D_MODEL), lambda m: (m, 0))
    return pl.pallas_call(
        _outproj_kernel,
        out_shape=(jax.ShapeDtypeStruct((T_ROWS, D_MODEL), F32),
                   jax.ShapeDtypeStruct((T_ROWS, D_MODEL), BF16)),
        grid=(T_ROWS // bm,),
        in_specs=[
            pl.BlockSpec((bm, ATTN_WIDTH), lambda m: (m, 0)),
            head_spec, head_spec, head_spec,
            pl.BlockSpec((1, HEAD_DIM), lambda m: (0, 0)),
            pl.BlockSpec((None, ATTN_WIDTH + HG_WIDTH, D_MODEL), lambda m: (layer, 0, 0)),
            row_spec,
            pl.BlockSpec((None, SUBLANES, 6 * D_MODEL), lambda m: (layer, 0, 0)),
            pl.BlockSpec((1, D_MODEL), lambda m: (0, 0)),
        ],
        out_specs=(row_spec, row_spec),
        scratch_shapes=[pltpu.VMEM((bm, HG_WIDTH), BF16)],
        compiler_params=_cparams(1),
        name="output_projection",
    )(attn, o_f, o_b, gate, hnw, w_out, xs, mod, n2w)


def _ffn_up_kernel(h_ref, hp_ref, hn_ref, wg_ref, wu_ref, cw_ref, cb_ref, u_ref, hcat_scr, g_scr):
    m = pl.program_id(0)
    n = pl.program_id(1)
    bm = h_ref.shape[0]
    hl = CONV_HALO

    @pl.when(n == 0)
    def _():
        hcat_scr[0:hl, :] = hp_ref[...]
        hcat_scr[hl:hl + bm, :] = h_ref[...]
        hcat_scr[hl + bm:hl + bm + hl, :] = hn_ref[...]

    g_scr[...] = jnp.dot(hcat_scr[...], wg_ref[...], preferred_element_type=F32)
    up = jnp.dot(h_ref[...], wu_ref[...], preferred_element_type=F32)
    r = m * bm + lax.broadcasted_iota(jnp.int32, (bm, 1), 0)
    has_prev = (r != 0) & (r != CTX_LEN)
    has_next = (r != CTX_LEN - 1) & (r != T_ROWS - 1)
    g_prev = jnp.where(has_prev, g_scr[hl - 1:hl - 1 + bm, :], 0.0)
    g_next = jnp.where(has_next, g_scr[hl + 1:hl + 1 + bm, :], 0.0)
    gate = (g_prev * cw_ref[0:1, :] + g_scr[hl:hl + bm, :] * cw_ref[1:2, :]
            + g_next * cw_ref[2:3, :] + cb_ref[...])
    u_ref[...] = (_silu(gate) * up).astype(BF16)


def _ffn_up(layer, h2, w_up, conv_w, conv_b):
    bm, bn, hl = ROW_TILE, FF_COL_TILE, CONV_HALO
    n_n = D_FF // bn
    per = bm // hl
    last = T_ROWS // hl - 1
    return pl.pallas_call(
        _ffn_up_kernel,
        out_shape=jax.ShapeDtypeStruct((T_ROWS, D_FF), BF16),
        grid=(T_ROWS // bm, n_n),
        in_specs=[
            pl.BlockSpec((bm, D_MODEL), lambda m, n: (m, 0)),
            pl.BlockSpec((hl, D_MODEL), lambda m, n: (jnp.maximum(m * per - 1, 0), 0)),
            pl.BlockSpec((hl, D_MODEL), lambda m, n: (jnp.minimum((m + 1) * per, last), 0)),
            pl.BlockSpec((None, D_MODEL, bn), lambda m, n: (layer, 0, n)),
            pl.BlockSpec((None, D_MODEL, bn), lambda m, n: (layer, 0, n_n + n)),
            pl.BlockSpec((3, bn), lambda m, n: (0, n)),
            pl.BlockSpec((1, bn), lambda m, n: (0, n)),
        ],
        out_specs=pl.BlockSpec((bm, bn), lambda m, n: (m, n)),
        scratch_shapes=[pltpu.VMEM((bm + 2 * hl, D_MODEL), BF16),
                        pltpu.VMEM((bm + 2 * hl, bn), F32)],
        compiler_params=_cparams(2),
        name="ffn_up_conv",
    )(h2, h2, h2, w_up, w_up, conv_w, conv_b)


def _ffn_down_kernel(u_ref, w_ref, x_ref, mod_ref, fw_ref, o_ref, *, final):
    m = pl.program_id(0)
    bm = x_ref.shape[0]
    acc = jnp.dot(u_ref[...], w_ref[...], preferred_element_type=F32)
    x_new = x_ref[...] + _mod_rows(mod_ref, _row_is_ctx(m, bm), 5) * acc
    o_ref[...] = _rms(x_new, fw_ref[...]) if final else x_new


def _ffn_down(layer, u, w_down, xs, mod, final_w, final):
    bm = OUT_ROW_TILE
    row_spec = pl.BlockSpec((bm, D_MODEL), lambda m: (m, 0))
    return pl.pallas_call(
        functools.partial(_ffn_down_kernel, final=final),
        out_shape=jax.ShapeDtypeStruct((T_ROWS, D_MODEL), F32),
        grid=(T_ROWS // bm,),
        in_specs=[
            pl.BlockSpec((bm, D_FF), lambda m: (m, 0)),
            pl.BlockSpec((None, D_FF, D_MODEL), lambda m: (layer, 0, 0),
                         pipeline_mode=pl.Buffered(1)),
            row_spec,
            pl.BlockSpec((None, SUBLANES, 6 * D_MODEL), lambda m: (layer, 0, 0)),
            pl.BlockSpec((1, D_MODEL), lambda m: (0, 0)),
        ],
        out_specs=row_spec,
        compiler_params=_cparams(1),
        name="ffn_down",
    )(u, w_down, xs, mod, final_w)


def _rope_tables():
    rows = SEQ // GRID_W
    row = jnp.repeat(jnp.arange(rows, dtype=F32), GRID_W)
    col = jnp.tile(jnp.arange(GRID_W, dtype=F32), rows)
    freqs = ROPE_THETA ** (-jnp.arange(ROPE_FREQS, dtype=F32) / ROPE_FREQS)
    ar, ac = row[:, None] * freqs, col[:, None] * freqs
    cos = jnp.concatenate([jnp.cos(ar), jnp.cos(ar), jnp.cos(ac), jnp.cos(ac)], axis=1)
    sin = jnp.concatenate([jnp.sin(ar), jnp.sin(ar), jnp.sin(ac), jnp.sin(ac)], axis=1)
    cos = jnp.concatenate([jnp.ones((CTX_LEN, HEAD_DIM), F32), cos], axis=0)
    sin = jnp.concatenate([jnp.zeros((CTX_LEN, HEAD_DIM), F32), sin], axis=0)
    return cos, sin


def kernel(x, c, ctx, c_ctx, w_mod, b_mod, norm1_w, norm2_w, w_in, q_norm_w, k_norm_w, hg_lb_logits,
           hg_norm_w, w_out, w_up, conv_w, conv_b, w_down, final_norm_w):
    assert x.shape == (1, SEQ, D_MODEL) and ctx.shape == (1, CTX_LEN, D_MODEL)
    xs = jnp.concatenate([ctx[0], x[0]], axis=0)
    mod = _modulation(jnp.stack([c[0], c_ctx], axis=1), w_mod, b_mod)
    cos_t, sin_t = _rope_tables()
    lb_sm = jax.nn.softmax(hg_lb_logits.astype(F32), axis=1)
    lb_all = jnp.cumsum(lb_sm, axis=1) - lb_sm[:, :1]

    w_in, w_out, w_up, w_down = (w.astype(BF16) for w in (w_in, w_out, w_up, w_down))
    for l in range(DEPTH):
        lb = jnp.concatenate([lb_all[0, l], lb_all[1, l]])[None, :]
        q, k, v, hq, kf, bf, kb, bb, hv, gate = _input_projection(
            l, xs, mod, norm1_w[l][None, :], w_in, q_norm_w[l][None, :],
            k_norm_w[l][None, :], lb, cos_t, sin_t)
        attn = _attention(q, k, v)
        o_f, o_b = _hgrn2(hq, kf, bf, kb, bb, hv)
        xs, h2 = _output_projection(l, attn, o_f, o_b, gate, hg_norm_w[l][None, :],
                                    w_out, xs, mod, norm2_w[l][None, :])
        u = _ffn_up(l, h2, w_up, conv_w[l], conv_b[l][None, :])
        xs = _ffn_down(l, u, w_down, xs, mod, final_norm_w[None, :], final=(l == DEPTH - 1))
    return xs[CTX_LEN:][None]
```

```python
import functools

import jax
import jax.numpy as jnp
import numpy as np
from jax import lax
from jax.experimental import pallas as pl
from jax.experimental.pallas import tpu as pltpu

F32 = jnp.float32
BF16 = jnp.bfloat16

D_MODEL = 2048
SEQ = 8192
DEPTH = 4
CTX_LEN = 256
T_ROWS = CTX_LEN + SEQ
GRID_W = 64
HEAD_DIM = 128
ATTN_HEADS = 8
ATTN_KV_HEADS = 2
ATTN_GROUP = 4
ATTN_WIDTH = 1024
KV_WIDTH = 256
HG_HEADS = 8
HG_WIDTH = 1024
IN_WIDTH = 6656
ROPE_THETA = 10000.0
ROPE_FREQS = 32
D_FF = 5632
EPS = 1e-6
Q_SCALE = HEAD_DIM ** -0.5 * 1.4426950408889634

LANES = 128
SUBLANES = 8
VMEM_LIMIT = 56 * 1024 * 1024

MOD_SLAB = 256
ROW_TILE = 768
OUT_ROW_TILE = 384
IN_COL_TILE = 512
FF_COL_TILE = 512
ATTN_Q_TILE = 256
ATTN_K_TILE = 768
HG_BLOCK = 256
HG_CHUNK = 64
HG_DIAG = 2
CONV_HALO = 16


def _cparams(n_axes):
    return pltpu.CompilerParams(dimension_semantics=("arbitrary",) * n_axes,
                                vmem_limit_bytes=VMEM_LIMIT)


def _rms(x, w):
    return x * lax.rsqrt(jnp.mean(x * x, axis=-1, keepdims=True) + EPS) * w


def _silu(x):
    return x * jax.nn.sigmoid(x)


def _row_is_ctx(tile_idx, rows):
    r = tile_idx * rows + lax.broadcasted_iota(jnp.int32, (rows, 1), 0)
    return r < CTX_LEN


def _mod_rows(mod_ref, is_ctx, k):
    lat = mod_ref[0:1, k * D_MODEL:(k + 1) * D_MODEL]
    ctx = mod_ref[1:2, k * D_MODEL:(k + 1) * D_MODEL]
    return jnp.where(is_ctx, ctx, lat)


def _mod_kernel(s_ref, w_ref, b_ref, o_ref, acc_scr):
    k = pl.program_id(1)
    n_out = w_ref.shape[2]
    cw = 4 * LANES

    @pl.when(k == 0)
    def _():
        acc_scr[...] = jnp.zeros(acc_scr.shape, F32)

    r0 = pl.multiple_of(k * MOD_SLAB, MOD_SLAB)
    s = _silu(s_ref[pl.ds(r0, MOD_SLAB), :])
    sb = [jnp.tile(jnp.broadcast_to(s[:, i:i + 1], (MOD_SLAB, LANES)), (1, cw // LANES))
          for i in range(2)]
    for c0 in range(0, n_out, cw):
        w = w_ref[0, :, c0:c0 + cw]
        for i in range(2):
            part = (w * sb[i]).reshape(MOD_SLAB // SUBLANES, SUBLANES, cw).sum(axis=0)
            acc_scr[i, :, c0:c0 + cw] += part

    @pl.when(k == pl.num_programs(1) - 1)
    def _():
        bias = b_ref[0]
        rows = [acc_scr[i].sum(axis=0, keepdims=True) + bias for i in range(2)]
        o_ref[0] = jnp.concatenate(rows + [jnp.zeros((SUBLANES - 2, n_out), F32)], axis=0)


def _modulation(s_in, w_mod, b_mod):
    n_out = 6 * D_MODEL
    return pl.pallas_call(
        _mod_kernel,
        out_shape=jax.ShapeDtypeStruct((DEPTH, SUBLANES, n_out), F32),
        grid=(DEPTH, D_MODEL // MOD_SLAB),
        in_specs=[
            pl.BlockSpec((D_MODEL, 2), lambda l, k: (0, 0)),
            pl.BlockSpec((1, MOD_SLAB, n_out), lambda l, k: (l, k, 0)),
            pl.BlockSpec((1, 1, n_out), lambda l, k: (l, 0, 0)),
        ],
        out_specs=pl.BlockSpec((1, SUBLANES, n_out), lambda l, k: (l, 0, 0)),
        scratch_shapes=[pltpu.VMEM((2, SUBLANES, n_out), F32)],
        compiler_params=_cparams(2),
        name="adaln_modulation",
    )(s_in, w_mod, b_mod.reshape(DEPTH, 1, n_out))


def _norm_rope_matrix():
    i = lax.broadcasted_iota(jnp.int32, (2 * HEAD_DIM, 2 * HEAD_DIM), 0)
    j = lax.broadcasted_iota(jnp.int32, (2 * HEAD_DIM, 2 * HEAD_DIM), 1)
    ones = (i < HEAD_DIM) & (j < HEAD_DIM)
    ii, jj = i - HEAD_DIM, j - HEAD_DIM
    low = (jj // ROPE_FREQS) % 2 == 0
    rot = jnp.where(low & (ii == jj + ROPE_FREQS), -1.0,
                    jnp.where(~low & (ii == jj - ROPE_FREQS), 1.0, 0.0))
    rot = jnp.where((i >= HEAD_DIM) & (j >= HEAD_DIM), rot, 0.0)
    return jnp.where(ones, 1.0, rot).astype(BF16)


def _norm_rope(p, w, mat_ref, cos, sin, scale):
    pw = p * w
    lhs = jnp.concatenate([(p * p).astype(BF16), pw.astype(BF16)], axis=1)
    red = jnp.dot(lhs, mat_ref[...], preferred_element_type=F32)
    r = lax.rsqrt(red[:, 0:HEAD_DIM] * (1.0 / HEAD_DIM) + EPS) * scale
    return r * (pw * cos + red[:, HEAD_DIM:2 * HEAD_DIM] * sin)


def _inproj_kernel(x_ref, mod_ref, n1w_ref, w_ref, qnw_ref, knw_ref, lb_ref, cos_ref, sin_ref, mat_ref,
                   q_ref, k_ref, v_ref, hq_ref, kf_ref, bf_ref, kb_ref, bb_ref, hv_ref, gate_ref,
                   h_scr):
    m = pl.program_id(0)
    n = pl.program_id(1)

    def modulated(rows, mod_row):
        x = x_ref[rows, :]
        r = lax.rsqrt(jnp.mean(x * x, axis=-1, keepdims=True) + EPS)
        gain = n1w_ref[...] * (1.0 + mod_ref[mod_row:mod_row + 1, D_MODEL:2 * D_MODEL])
        return (x * r * gain + mod_ref[mod_row:mod_row + 1, 0:D_MODEL]).astype(BF16)

    @pl.when(n == 0)
    def _():
        h_scr[...] = modulated(slice(None), 0)

    @pl.when((n == 0) & (m == 0))
    def _():
        h_scr[0:CTX_LEN, :] = modulated(slice(0, CTX_LEN), 1)

    def project(epilogue):
        half = IN_COL_TILE // 2
        hpt = half // HEAD_DIM
        h = h_scr[...]
        ps = [jnp.dot(h, w_ref[:, a * half:(a + 1) * half], preferred_element_type=F32)
              for a in range(2)]
        for a, p in enumerate(ps):
            for j in range(hpt):
                epilogue(a * hpt + j, p[:, j * HEAD_DIM:(j + 1) * HEAD_DIM])

    def q_out(j, p):
        q_ref[j] = _norm_rope(p, qnw_ref[...], mat_ref, cos_ref[...], sin_ref[...],
                              Q_SCALE).astype(BF16)

    def kv_out(j, p):
        if j < ATTN_KV_HEADS:
            k_ref[j] = _norm_rope(p, knw_ref[...], mat_ref, cos_ref[...], sin_ref[...],
                                  1.0).astype(BF16)
        else:
            v_ref[j - ATTN_KV_HEADS] = p.astype(BF16)

    def silu_out(ref):
        def out(j, p):
            ref[j] = _silu(p).astype(BF16)
        return out

    def forget_out(k_out, b_out, rev):
        def out(j, p):
            lb = lb_ref[:, j * HEAD_DIM:(j + 1) * HEAD_DIM]
            fg = lb + (1.0 - lb) * jax.nn.sigmoid(p)
            k_out[j] = (1.0 - fg).astype(BF16)
            b_out[j] = _chunk_cumsum(jnp.log2(fg), rev)
        return out

    def plain_out(j, p):
        hv_ref[j] = p.astype(BF16)

    pl.when(n < 2)(lambda: project(q_out))
    pl.when(n == 2)(lambda: project(kv_out))
    pl.when((n == 3) | (n == 4))(lambda: project(silu_out(hq_ref)))
    pl.when((n == 5) | (n == 6))(lambda: project(forget_out(kf_ref, bf_ref, False)))
    pl.when((n == 7) | (n == 8))(lambda: project(forget_out(kb_ref, bb_ref, True)))
    pl.when((n == 9) | (n == 10))(lambda: project(plain_out))
    pl.when(n >= 11)(lambda: project(silu_out(gate_ref)))


def _input_projection(layer, xs, mod, n1w, w_in, qnw, knw, lb, cos_t, sin_t):
    bm, bn = ROW_TILE, IN_COL_TILE
    n_m, n_n = T_ROWS // bm, IN_WIDTH // bn
    hpt = bn // HEAD_DIM

    def head_spec(first_tile):
        return pl.BlockSpec((hpt, bm, HEAD_DIM),
                            lambda m, n: (jnp.clip(n - first_tile, 0, 1), m, 0))

    def head_shape(dtype):
        return jax.ShapeDtypeStruct((HG_HEADS, T_ROWS, HEAD_DIM), dtype)

    kv_spec = pl.BlockSpec((ATTN_KV_HEADS, bm, HEAD_DIM), lambda m, n: (0, m, 0))
    kv_shape = jax.ShapeDtypeStruct((ATTN_KV_HEADS, T_ROWS, HEAD_DIM), BF16)
    return pl.pallas_call(
        _inproj_kernel,
        out_shape=(head_shape(BF16), kv_shape, kv_shape, head_shape(BF16),
                   head_shape(BF16), head_shape(F32), head_shape(BF16), head_shape(F32),
                   head_shape(BF16), head_shape(BF16)),
        grid=(n_m, n_n),
        in_specs=[
            pl.BlockSpec((bm, D_MODEL), lambda m, n: (m, 0)),
            pl.BlockSpec((None, SUBLANES, 6 * D_MODEL), lambda m, n: (layer, 0, 0)),
            pl.BlockSpec((1, D_MODEL), lambda m, n: (0, 0)),
            pl.BlockSpec((None, D_MODEL, bn), lambda m, n: (layer, 0, n)),
            pl.BlockSpec((1, HEAD_DIM), lambda m, n: (0, 0)),
            pl.BlockSpec((1, HEAD_DIM), lambda m, n: (0, 0)),
            pl.BlockSpec((1, bn), lambda m, n: (0, jnp.clip(n - 5, 0, 3))),
            pl.BlockSpec((bm, HEAD_DIM), lambda m, n: (m, 0)),
            pl.BlockSpec((bm, HEAD_DIM), lambda m, n: (m, 0)),
            pl.BlockSpec((2 * HEAD_DIM, 2 * HEAD_DIM), lambda m, n: (0, 0)),
        ],
        out_specs=(head_spec(0), kv_spec, kv_spec, head_spec(3),
                   head_spec(5), head_spec(5), head_spec(7), head_spec(7),
                   head_spec(9), head_spec(11)),
        scratch_shapes=[pltpu.VMEM((bm, D_MODEL), BF16)],
        compiler_params=_cparams(2),
        name="input_projection",
    )(xs, mod, n1w, w_in, qnw, knw, lb, cos_t, sin_t, _norm_rope_matrix())


def _attn_kernel(q_ref, k_ref, vt_ref, o_ref, acc_scr, st_scr):
    i = pl.program_id(1)
    bq = q_ref.shape[1]
    bk = ATTN_K_TILE
    n_chunks = T_ROWS // bk

    def scores(kc, r, slot):
        half = kc.shape[0] // 2
        mx = []
        for a in range(2):
            st = lax.dot_general(kc[a * half:(a + 1) * half], q_ref[r], (((1,), (1,)), ((), ())),
                                 preferred_element_type=F32)
            st_scr[slot, a * half:(a + 1) * half, :] = st
            mx.append(st.max(axis=0, keepdims=True))
        return jnp.maximum(mx[0], mx[1])

    def update(vtc, r, slot, m_cur, m, l):
        width = vtc.shape[1]
        m_new = jnp.maximum(m, m_cur)
        alpha = jnp.exp2(m - m_new)
        p = jnp.exp2(st_scr[slot, 0:width, :] - m_new)
        l = alpha * l + p.sum(axis=0, keepdims=True)
        acc_scr[r] = alpha * acc_scr[r] + jnp.dot(vtc, p.astype(BF16), preferred_element_type=F32)
        return m_new, l

    def finish(l):
        for r in range(ATTN_GROUP):
            o_ref[:, r * HEAD_DIM:(r + 1) * HEAD_DIM] = (acc_scr[r] / l[r]).T.astype(BF16)

    acc_scr[...] = jnp.zeros(acc_scr.shape, F32)
    m0 = tuple(jnp.full((1, bq), -jnp.inf, F32) for _ in range(ATTN_GROUP))
    l0 = tuple(jnp.zeros((1, bq), F32) for _ in range(ATTN_GROUP))

    @pl.when(i == 0)
    def _():
        kc = k_ref[0, 0:CTX_LEN, :]
        vtc = vt_ref[0, 0, :, 0:CTX_LEN]
        l = []
        mc = scores(kc, 0, 0)
        for r in range(ATTN_GROUP):
            mc_next = scores(kc, r + 1, (r + 1) % 2) if r + 1 < ATTN_GROUP else None
            l.append(update(vtc, r, r % 2, mc, m0[r], l0[r])[1])
            mc = mc_next
        finish(l)

    @pl.when(i > 0)
    def _():
        def chunk(j):
            return k_ref[0, pl.ds(pl.multiple_of(j * bk, bk), bk), :]

        def body(j, carry):
            mc, m, l = carry
            m, l = list(m), list(l)
            vtc = vt_ref[0, j]
            for r in range(ATTN_GROUP):
                if r + 1 < ATTN_GROUP:
                    mc_next = scores(chunk(j), r + 1, (r + 1) % 2)
                else:
                    mc_next = scores(chunk(jnp.minimum(j + 1, n_chunks - 1)), 0, 0)
                m[r], l[r] = update(vtc, r, r % 2, mc, m[r], l[r])
                mc = mc_next
            return mc, tuple(m), tuple(l)

        _, _, l = lax.fori_loop(0, n_chunks, body, (scores(chunk(0), 0, 0), m0, l0), unroll=True)
        finish(l)


def _attention(q, k, v):
    bq, bk = ATTN_Q_TILE, ATTN_K_TILE
    n_chunks = T_ROWS // bk
    vt = v.reshape(ATTN_KV_HEADS, n_chunks, bk, HEAD_DIM).transpose(0, 1, 3, 2)
    return pl.pallas_call(
        _attn_kernel,
        out_shape=jax.ShapeDtypeStruct((T_ROWS, ATTN_WIDTH), BF16),
        grid=(ATTN_KV_HEADS, T_ROWS // bq),
        in_specs=[
            pl.BlockSpec((ATTN_GROUP, bq, HEAD_DIM), lambda g, i: (g, i, 0)),
            pl.BlockSpec((1, T_ROWS, HEAD_DIM), lambda g, i: (g, 0, 0)),
            pl.BlockSpec((1, n_chunks, HEAD_DIM, bk), lambda g, i: (g, 0, 0, 0)),
        ],
        out_specs=pl.BlockSpec((bq, ATTN_GROUP * HEAD_DIM), lambda g, i: (i, g)),
        scratch_shapes=[pltpu.VMEM((ATTN_GROUP, HEAD_DIM, bq), F32),
                        pltpu.VMEM((2, bk, bq), F32)],
        compiler_params=_cparams(2),
        name="gqa_attention",
    )(q, k, vt)


def _chunk_cumsum(g, rev):
    rows = g.shape[0]
    pos = lax.broadcasted_iota(jnp.int32, g.shape, 0) % HG_CHUNK
    x = g
    step = 1
    while step < HG_CHUNK:
        if rev:
            x = x + jnp.where(pos < HG_CHUNK - step, pltpu.roll(x, rows - step, axis=0), 0.0)
        else:
            x = x + jnp.where(pos >= step, pltpu.roll(x, step, axis=0), 0.0)
        step *= 2
    return x


def _hgrn_levels():
    halves = []
    half = HG_CHUNK // 2
    while half >= HG_DIAG:
        halves.append(half)
        half //= 2
    return halves


def _hgrn_consts(rev):
    c = HG_CHUNK
    levels = {}
    for half in _hgrn_levels():
        blk = 2 * half
        levels[half] = []
        for g in range(c // SUBLANES):
            t = lax.broadcasted_iota(jnp.int32, (SUBLANES, c), 0) + g * SUBLANES
            s = lax.broadcasted_iota(jnp.int32, (SUBLANES, c), 1)
            same = t // blk == s // blk
            if rev:
                levels[half].append(same & (t % blk < half) & (s % blk >= half))
            else:
                levels[half].append(same & (t % blk >= half) & (s % blk < half))
    t = lax.broadcasted_iota(jnp.int32, (c, c), 0)
    s = lax.broadcasted_iota(jnp.int32, (c, c), 1)
    band = []
    for d in range(HG_DIAG):
        if rev:
            band.append((s == t + d) & (t % HG_DIAG + d < HG_DIAG))
        else:
            band.append((s == t - d) & (t % HG_DIAG >= d))
    return levels, band


def _hgrn_scores(q, k, b, brow, rev, consts):
    levels, band = consts
    c = HG_CHUNK
    nt = (((1,), (1,)), ((), ()))
    groups = c // SUBLANES
    row8 = lax.broadcasted_iota(jnp.int32, (SUBLANES, HEAD_DIM), 0)
    rows = [slice(g * SUBLANES, (g + 1) * SUBLANES) for g in range(groups)]

    a = jnp.zeros((c, c), F32)
    for d in range(HG_DIAG):
        if d == 0:
            prod = q * k
        else:
            shift = c - d if rev else d
            prod = q * pltpu.roll(k, shift, axis=0) * jnp.exp2(b - pltpu.roll(b, shift, axis=0))
        a = jnp.where(band[d], prod.sum(axis=-1, keepdims=True), a)
    rows_a = [a[r] for r in rows]

    for half in _hgrn_levels():
        blk = 2 * half
        ref_off = half if rev else half - 1
        zs = []
        for g in range(groups):
            t0 = g * SUBLANES
            if half >= SUBLANES:
                ref = brow(t0 // blk * blk + ref_off)
                attending = (t0 % blk < half) if rev else (t0 % blk >= half)
                x = q[rows[g]] if attending else k[rows[g]]
                e = jnp.exp2(b[rows[g]] - ref if attending else ref - b[rows[g]])
            else:
                ref = brow(t0 + ref_off)
                for i in range(1, SUBLANES // blk):
                    ref = jnp.where(row8 >= i * blk, brow(t0 + i * blk + ref_off), ref)
                attending = (row8 % blk < half) if rev else (row8 % blk >= half)
                x = jnp.where(attending, q[rows[g]], k[rows[g]])
                e = jnp.exp2(-jnp.abs(b[rows[g]] - ref))
            zs.append(x * e)
        z = jnp.concatenate(zs, axis=0).astype(BF16)
        lv = lax.dot_general(z, z, nt, preferred_element_type=F32)
        for g in range(groups):
            rows_a[g] = jnp.where(levels[half][g], lv[rows[g]], rows_a[g])
    return jnp.concatenate(rows_a, axis=0)


def _hgrn_kernel(qf_ref, kf_ref, bf_ref, vf_ref, qb_ref, kb_ref, bb_ref, vb_ref,
                 of_ref, ob_ref, s_scr):
    @pl.when(pl.program_id(0) == 0)
    def _():
        s_scr[...] = jnp.zeros(s_scr.shape, F32)

    c = HG_CHUNK
    n_chunks = HG_BLOCK // c
    consts = (_hgrn_consts(False), _hgrn_consts(True))
    nt = (((1,), (1,)), ((), ()))
    tn = (((0,), (0,)), ((), ()))
    dirs = ((qf_ref, kf_ref, bf_ref, vf_ref, of_ref, False),
            (qb_ref, kb_ref, bb_ref, vb_ref, ob_ref, True))

    def head_body(h, carry):
        jobs = []
        for di, (q_ref, k_ref, b_ref, v_ref, o_ref, rev) in enumerate(dirs):
            order = reversed(range(n_chunks)) if rev else range(n_chunks)
            for ci in order:
                rows = slice(ci * c, (ci + 1) * c)
                q = q_ref[h, rows, :].astype(F32)
                k = k_ref[h, rows, :].astype(F32)
                b = b_ref[h, rows, :]
                v = v_ref[h, rows, :]
                tot = b[0:1] if rev else b[c - 1:c]
                q_in = (q * jnp.exp2(b)).astype(BF16)
                k_out = (k * jnp.exp2(tot - b)).astype(BF16)
                upd = lax.dot_general(v, k_out, tn, preferred_element_type=F32)

                def brow(r, b_ref=b_ref, base=ci * c):
                    return b_ref[h, base + r:base + r + 1, :]

                a = _hgrn_scores(q, k, b, brow, rev, consts[di]).astype(BF16)
                jobs.append((di, o_ref, rows, q_in, jnp.exp2(tot), upd, a, v))
        st = [s_scr[0, h], s_scr[1, h]]
        for di, o_ref, rows, q_in, decay, upd, a, v in jobs:
            o = lax.dot_general(q_in, st[di].astype(BF16), nt, preferred_element_type=F32)
            st[di] = decay * st[di] + upd
            o_ref[h, rows, :] = o + jnp.dot(a, v, preferred_element_type=F32)
        s_scr[0, h] = st[0]
        s_scr[1, h] = st[1]
        return carry

    lax.fori_loop(0, HG_HEADS, head_body, 0)


def _hgrn2(hq, kf, bf, kb, bb, hv):
    nb = T_ROWS // HG_BLOCK
    fwd = lambda i: (0, i, 0)
    bwd = lambda i: (0, jnp.where(i == 0, 0, nb - i), 0)
    blk = (HG_HEADS, HG_BLOCK, HEAD_DIM)
    out = jax.ShapeDtypeStruct((HG_HEADS, T_ROWS, HEAD_DIM), F32)
    return pl.pallas_call(
        _hgrn_kernel,
        out_shape=(out, out),
        grid=(nb,),
        in_specs=[pl.BlockSpec(blk, fwd)] * 4 + [pl.BlockSpec(blk, bwd)] * 4,
        out_specs=(pl.BlockSpec(blk, fwd), pl.BlockSpec(blk, bwd)),
        scratch_shapes=[pltpu.VMEM((2, HG_HEADS, HEAD_DIM, HEAD_DIM), F32)],
        compiler_params=_cparams(1),
        name="hgrn2_scan",
    )(hq, kf, bf, hv, hq, kb, bb, hv)


def _outproj_kernel(attn_ref, of_ref, ob_ref, gate_ref, hnw_ref, w_ref, x_ref, mod_ref, n2w_ref,
                    xo_ref, h2_ref, mix_scr):
    m = pl.program_id(0)
    bm = x_ref.shape[0]
    acc = jnp.dot(attn_ref[...], w_ref[0:ATTN_WIDTH, :], preferred_element_type=F32)
    for h in range(HG_HEADS):
        y = _rms(of_ref[h] + ob_ref[h], hnw_ref[...])
        mix_scr[:, h * HEAD_DIM:(h + 1) * HEAD_DIM] = (y * gate_ref[h].astype(F32)).astype(BF16)
    acc = acc + jnp.dot(mix_scr[...], w_ref[ATTN_WIDTH:ATTN_WIDTH + HG_WIDTH, :],
                        preferred_element_type=F32)
    is_ctx = _row_is_ctx(m, bm)
    x_new = x_ref[...] + _mod_rows(mod_ref, is_ctx, 2) * acc
    xo_ref[...] = x_new
    h2 = _rms(x_new, n2w_ref[...]) * (1.0 + _mod_rows(mod_ref, is_ctx, 4)) + _mod_rows(mod_ref, is_ctx, 3)
    h2_ref[...] = h2.astype(BF16)


def _output_projection(layer, attn, o_f, o_b, gate, hnw, w_out, xs, mod, n2w):
    bm = OUT_ROW_TILE
    head_spec = pl.BlockSpec((HG_HEADS, bm, HEAD_DIM), lambda m: (0, m, 0))
    row_spec = pl.BlockSpec((bm, D_MODEL), lambda m: (m, 0))
    return pl.pallas_call(
        _outproj_kernel,
        out_shape=(jax.ShapeDtypeStruct((T_ROWS, D_MODEL), F32),
                   jax.ShapeDtypeStruct((T_ROWS, D_MODEL), BF16)),
        grid=(T_ROWS // bm,),
        in_specs=[
            pl.BlockSpec((bm, ATTN_WIDTH), lambda m: (m, 0)),
            head_spec, head_spec, head_spec,
            pl.BlockSpec((1, HEAD_DIM), lambda m: (0, 0)),
            pl.BlockSpec((None, ATTN_WIDTH + HG_WIDTH, D_MODEL), lambda m: (layer, 0, 0)),
            row_spec,
            pl.BlockSpec((None, SUBLANES, 6 * D_MODEL), lambda m: (layer, 0, 0)),
            pl.BlockSpec((1, D_MODEL), lambda m: (0, 0)),
        ],
        out_specs=(row_spec, row_spec),
        scratch_shapes=[pltpu.VMEM((bm, HG_WIDTH), BF16)],
        compiler_params=_cparams(1),
        name="output_projection",
    )(attn, o_f, o_b, gate, hnw, w_out, xs, mod, n2w)


def _ffn_up_kernel(h_ref, hp_ref, hn_ref, wg_ref, wu_ref, cw_ref, cb_ref, u_ref, hcat_scr, g_scr):
    m = pl.program_id(0)
    n = pl.program_id(1)
    bm = h_ref.shape[0]
    hl = CONV_HALO

    @pl.when(n == 0)
    def _():
        hcat_scr[0:hl, :] = hp_ref[...]
        hcat_scr[hl:hl + bm, :] = h_ref[...]
        hcat_scr[hl + bm:hl + bm + hl, :] = hn_ref[...]

    g_scr[...] = jnp.dot(hcat_scr[...], wg_ref[...], preferred_element_type=F32)
    up = jnp.dot(h_ref[...], wu_ref[...], preferred_element_type=F32)
    r = m * bm + lax.broadcasted_iota(jnp.int32, (bm, 1), 0)
    has_prev = (r != 0) & (r != CTX_LEN)
    has_next = (r != CTX_LEN - 1) & (r != T_ROWS - 1)
    g_prev = jnp.where(has_prev, g_scr[hl - 1:hl - 1 + bm, :], 0.0)
    g_next = jnp.where(has_next, g_scr[hl + 1:hl + 1 + bm, :], 0.0)
    gate = (g_prev * cw_ref[0:1, :] + g_scr[hl:hl + bm, :] * cw_ref[1:2, :]
            + g_next * cw_ref[2:3, :] + cb_ref[...])
    u_ref[...] = (_silu(gate) * up).astype(BF16)


def _ffn_up(layer, h2, w_up, conv_w, conv_b):
    bm, bn, hl = ROW_TILE, FF_COL_TILE, CONV_HALO
    n_n = D_FF // bn
    per = bm // hl
    last = T_ROWS // hl - 1
    return pl.pallas_call(
        _ffn_up_kernel,
        out_shape=jax.ShapeDtypeStruct((T_ROWS, D_FF), BF16),
        grid=(T_ROWS // bm, n_n),
        in_specs=[
            pl.BlockSpec((bm, D_MODEL), lambda m, n: (m, 0)),
            pl.BlockSpec((hl, D_MODEL), lambda m, n: (jnp.maximum(m * per - 1, 0), 0)),
            pl.BlockSpec((hl, D_MODEL), lambda m, n: (jnp.minimum((m + 1) * per, last), 0)),
            pl.BlockSpec((None, D_MODEL, bn), lambda m, n: (layer, 0, n)),
            pl.BlockSpec((None, D_MODEL, bn), lambda m, n: (layer, 0, n_n + n)),
            pl.BlockSpec((3, bn), lambda m, n: (0, n)),
            pl.BlockSpec((1, bn), lambda m, n: (0, n)),
        ],
        out_specs=pl.BlockSpec((bm, bn), lambda m, n: (m, n)),
        scratch_shapes=[pltpu.VMEM((bm + 2 * hl, D_MODEL), BF16),
                        pltpu.VMEM((bm + 2 * hl, bn), F32)],
        compiler_params=_cparams(2),
        name="ffn_up_conv",
    )(h2, h2, h2, w_up, w_up, conv_w, conv_b)


def _ffn_down_kernel(u_ref, w_ref, x_ref, mod_ref, fw_ref, o_ref, *, final):
    m = pl.program_id(0)
    bm = x_ref.shape[0]

    def run():
        acc = jnp.dot(u_ref[...], w_ref[...], preferred_element_type=F32)
        x_new = x_ref[...] + _mod_rows(mod_ref, _row_is_ctx(m, bm), 5) * acc
        o_ref[...] = _rms(x_new, fw_ref[...]) if final else x_new

    if final:
        pl.when(m > 0)(run)
    else:
        run()


def _ffn_down(layer, u, w_down, xs, mod, final_w, final):
    bm = CTX_LEN if final else OUT_ROW_TILE
    row_spec = pl.BlockSpec((bm, D_MODEL), lambda m: (m, 0))
    if final:
        out_rows = SEQ
        out_spec = pl.BlockSpec((bm, D_MODEL), lambda m: (jnp.maximum(m - 1, 0), 0))
    else:
        out_rows, out_spec = T_ROWS, row_spec
    return pl.pallas_call(
        functools.partial(_ffn_down_kernel, final=final),
        out_shape=jax.ShapeDtypeStruct((out_rows, D_MODEL), F32),
        grid=(T_ROWS // bm,),
        in_specs=[
            pl.BlockSpec((bm, D_FF), lambda m: (m, 0)),
            pl.BlockSpec((None, D_FF, D_MODEL), lambda m: (layer, 0, 0),
                         pipeline_mode=pl.Buffered(1)),
            row_spec,
            pl.BlockSpec((None, SUBLANES, 6 * D_MODEL), lambda m: (layer, 0, 0)),
            pl.BlockSpec((1, D_MODEL), lambda m: (0, 0)),
        ],
        out_specs=out_spec,
        compiler_params=_cparams(1),
        name="ffn_down",
    )(u, w_down, xs, mod, final_w)


def _rope_tables():
    f32 = np.float32
    rows = SEQ // GRID_W
    row = np.repeat(np.arange(rows, dtype=f32), GRID_W)
    col = np.tile(np.arange(GRID_W, dtype=f32), rows)
    freqs = np.power(f32(ROPE_THETA), -np.arange(ROPE_FREQS, dtype=f32) / f32(ROPE_FREQS)).astype(f32)
    ar, ac = row[:, None] * freqs, col[:, None] * freqs
    cos = np.concatenate([np.cos(ar), np.cos(ar), np.cos(ac), np.cos(ac)], axis=1)
    sin = np.concatenate([np.sin(ar), np.sin(ar), np.sin(ac), np.sin(ac)], axis=1)
    cos = np.concatenate([np.ones((CTX_LEN, HEAD_DIM), f32), cos], axis=0)
    sin = np.concatenate([np.zeros((CTX_LEN, HEAD_DIM), f32), sin], axis=0)
    return jnp.asarray(cos, F32), jnp.asarray(sin, F32)


def kernel(x, c, ctx, c_ctx, w_mod, b_mod, norm1_w, norm2_w, w_in, q_norm_w, k_norm_w, hg_lb_logits,
           hg_norm_w, w_out, w_up, conv_w, conv_b, w_down, final_norm_w):
    assert x.shape == (1, SEQ, D_MODEL) and ctx.shape == (1, CTX_LEN, D_MODEL)
    xs = jnp.concatenate([ctx[0], x[0]], axis=0)
    mod = _modulation(jnp.stack([c[0], c_ctx], axis=1), w_mod, b_mod)
    cos_t, sin_t = _rope_tables()
    lb_sm = jax.nn.softmax(hg_lb_logits.astype(F32), axis=1)
    lb_all = jnp.cumsum(lb_sm, axis=1) - lb_sm[:, :1]

    w_in, w_out, w_up, w_down = (w.astype(BF16) for w in (w_in, w_out, w_up, w_down))
    for l in range(DEPTH):
        lb = jnp.concatenate([lb_all[0, l], lb_all[1, l]])[None, :]
        q, k, v, hq, kf, bf, kb, bb, hv, gate = _input_projection(
            l, xs, mod, norm1_w[l][None, :], w_in, q_norm_w[l][None, :],
            k_norm_w[l][None, :], lb, cos_t, sin_t)
        attn = _attention(q, k, v)
        o_f, o_b = _hgrn2(hq, kf, bf, kb, bb, hv)
        xs, h2 = _output_projection(l, attn, o_f, o_b, gate, hg_norm_w[l][None, :],
                                    w_out, xs, mod, norm2_w[l][None, :])
        u = _ffn_up(l, h2, w_up, conv_w[l], conv_b[l][None, :])
        xs = _ffn_down(l, u, w_down, xs, mod, final_norm_w[None, :], final=(l == DEPTH - 1))
    return xs[None]
```

```python
import functools

import jax
import jax.numpy as jnp
import numpy as np
from jax import lax
from jax.experimental import pallas as pl
from jax.experimental.pallas import tpu as pltpu

F32 = jnp.float32
BF16 = jnp.bfloat16

D_MODEL = 2048
SEQ = 8192
DEPTH = 4
CTX_LEN = 256
T_ROWS = CTX_LEN + SEQ
GRID_W = 64
HEAD_DIM = 128
ATTN_HEADS = 8
ATTN_KV_HEADS = 2
ATTN_GROUP = 4
ATTN_WIDTH = 1024
KV_WIDTH = 256
HG_HEADS = 8
HG_WIDTH = 1024
IN_WIDTH = 6656
ROPE_THETA = 10000.0
ROPE_FREQS = 32
D_FF = 5632
EPS = 1e-6
Q_SCALE = HEAD_DIM ** -0.5 * 1.4426950408889634

LANES = 128
SUBLANES = 8
VMEM_LIMIT = 56 * 1024 * 1024

MOD_SLAB = 256
ROW_TILE = 768
OUT_ROW_TILE = 384
IN_COL_TILE = 512
FF_COL_TILE = 512
ATTN_Q_TILE = 256
ATTN_K_TILE = 768
HG_BLOCK = 256
HG_CHUNK = 64
HG_DIAG = 2
CONV_HALO = 16


def _cparams(n_axes):
    return pltpu.CompilerParams(dimension_semantics=("arbitrary",) * n_axes,
                                vmem_limit_bytes=VMEM_LIMIT)


def _rms(x, w):
    return x * lax.rsqrt(jnp.mean(x * x, axis=-1, keepdims=True) + EPS) * w


def _silu(x):
    h = 0.5 * x
    return h * jnp.tanh(h) + h


def _row_is_ctx(tile_idx, rows):
    r = tile_idx * rows + lax.broadcasted_iota(jnp.int32, (rows, 1), 0)
    return r < CTX_LEN


def _mod_rows(mod_ref, is_ctx, k):
    lat = mod_ref[0:1, k * D_MODEL:(k + 1) * D_MODEL]
    ctx = mod_ref[1:2, k * D_MODEL:(k + 1) * D_MODEL]
    return jnp.where(is_ctx, ctx, lat)


def _mod_kernel(s_ref, w_ref, b_ref, o_ref, acc_scr):
    k = pl.program_id(1)
    n_out = w_ref.shape[2]
    cw = 4 * LANES

    @pl.when(k == 0)
    def _():
        acc_scr[...] = jnp.zeros(acc_scr.shape, F32)

    r0 = pl.multiple_of(k * MOD_SLAB, MOD_SLAB)
    s = _silu(s_ref[pl.ds(r0, MOD_SLAB), :])
    sb = [jnp.tile(jnp.broadcast_to(s[:, i:i + 1], (MOD_SLAB, LANES)), (1, cw // LANES))
          for i in range(2)]
    for c0 in range(0, n_out, cw):
        w = w_ref[0, :, c0:c0 + cw]
        for i in range(2):
            part = (w * sb[i]).reshape(MOD_SLAB // SUBLANES, SUBLANES, cw).sum(axis=0)
            acc_scr[i, :, c0:c0 + cw] += part

    @pl.when(k == pl.num_programs(1) - 1)
    def _():
        bias = b_ref[0]
        rows = [acc_scr[i].sum(axis=0, keepdims=True) + bias for i in range(2)]
        o_ref[0] = jnp.concatenate(rows + [jnp.zeros((SUBLANES - 2, n_out), F32)], axis=0)


def _modulation(s_in, w_mod, b_mod):
    n_out = 6 * D_MODEL
    return pl.pallas_call(
        _mod_kernel,
        out_shape=jax.ShapeDtypeStruct((DEPTH, SUBLANES, n_out), F32),
        grid=(DEPTH, D_MODEL // MOD_SLAB),
        in_specs=[
            pl.BlockSpec((D_MODEL, 2), lambda l, k: (0, 0)),
            pl.BlockSpec((1, MOD_SLAB, n_out), lambda l, k: (l, k, 0)),
            pl.BlockSpec((1, 1, n_out), lambda l, k: (l, 0, 0)),
        ],
        out_specs=pl.BlockSpec((1, SUBLANES, n_out), lambda l, k: (l, 0, 0)),
        scratch_shapes=[pltpu.VMEM((2, SUBLANES, n_out), F32)],
        compiler_params=_cparams(2),
        name="adaln_modulation",
    )(s_in, w_mod, b_mod.reshape(DEPTH, 1, n_out))


def _norm_rope_matrix():
    i = lax.broadcasted_iota(jnp.int32, (2 * HEAD_DIM, 2 * HEAD_DIM), 0)
    j = lax.broadcasted_iota(jnp.int32, (2 * HEAD_DIM, 2 * HEAD_DIM), 1)
    ones = (i < HEAD_DIM) & (j < HEAD_DIM)
    ii, jj = i - HEAD_DIM, j - HEAD_DIM
    low = (jj // ROPE_FREQS) % 2 == 0
    rot = jnp.where(low & (ii == jj + ROPE_FREQS), -1.0,
                    jnp.where(~low & (ii == jj - ROPE_FREQS), 1.0, 0.0))
    rot = jnp.where((i >= HEAD_DIM) & (j >= HEAD_DIM), rot, 0.0)
    return jnp.where(ones, 1.0, rot).astype(BF16)


def _norm_rope(p, w, mat_ref, cos, sin, scale):
    pw = p * w
    lhs = jnp.concatenate([(p * p).astype(BF16), pw.astype(BF16)], axis=1)
    red = jnp.dot(lhs, mat_ref[...], preferred_element_type=F32)
    r = lax.rsqrt(red[:, 0:HEAD_DIM] * (1.0 / HEAD_DIM) + EPS) * scale
    return r * (pw * cos + red[:, HEAD_DIM:2 * HEAD_DIM] * sin)


def _inproj_kernel(x_ref, mod_ref, n1w_ref, w_ref, qnw_ref, knw_ref, lb_ref, cos_ref, sin_ref, mat_ref,
                   q_ref, k_ref, v_ref, hq_ref, kf_ref, bf_ref, kb_ref, bb_ref, hv_ref, gate_ref,
                   h_scr):
    m = pl.program_id(0)
    n = pl.program_id(1)
    bm = x_ref.shape[0]

    def modulated(rows, mod_row):
        x = x_ref[rows, :]
        r = lax.rsqrt(jnp.mean(x * x, axis=-1, keepdims=True) + EPS)
        gain = n1w_ref[...] * (1.0 + mod_ref[mod_row:mod_row + 1, D_MODEL:2 * D_MODEL])
        return (x * r * gain + mod_ref[mod_row:mod_row + 1, 0:D_MODEL]).astype(BF16)

    @pl.when(n == 0)
    def _():
        h_scr[...] = modulated(slice(None), 0)

    @pl.when((n == 0) & (m == 0))
    def _():
        h_scr[0:CTX_LEN, :] = modulated(slice(0, CTX_LEN), 1)

    def project(epilogue):
        ch = IN_COL_TILE // 2
        hpt = ch // HEAD_DIM
        rh = bm // 2
        ps = []
        for ri in range(2):
            rows = slice(ri * rh, (ri + 1) * rh)
            h = h_scr[rows, :]
            ps.append((rows, [jnp.dot(h, w_ref[:, a * ch:(a + 1) * ch], preferred_element_type=F32)
                              for a in range(2)]))
        for rows, halves in ps:
            for a, p in enumerate(halves):
                for j in range(hpt):
                    epilogue(a * hpt + j, rows, p[:, j * HEAD_DIM:(j + 1) * HEAD_DIM])

    def q_out(j, rows, p):
        q_ref[j, rows, :] = _norm_rope(p, qnw_ref[...], mat_ref, cos_ref[rows, :], sin_ref[rows, :],
                                       Q_SCALE).astype(BF16)

    def kv_out(j, rows, p):
        if j < ATTN_KV_HEADS:
            k_ref[j, rows, :] = _norm_rope(p, knw_ref[...], mat_ref, cos_ref[rows, :],
                                           sin_ref[rows, :], 1.0).astype(BF16)
        else:
            v_ref[j - ATTN_KV_HEADS, rows, :] = p.astype(BF16)

    def silu_out(ref):
        def out(j, rows, p):
            ref[j, rows, :] = _silu(p).astype(BF16)
        return out

    def forget_out(k_out, b_out, rev):
        def out(j, rows, p):
            lb = lb_ref[:, j * HEAD_DIM:(j + 1) * HEAD_DIM]
            fg = lb + (1.0 - lb) * jax.nn.sigmoid(p)
            k_out[j, rows, :] = (1.0 - fg).astype(BF16)
            b_out[j, rows, :] = _chunk_cumsum(jnp.log2(fg), rev)
        return out

    def plain_out(j, rows, p):
        hv_ref[j, rows, :] = p.astype(BF16)

    pl.when(n < 2)(lambda: project(q_out))
    pl.when(n == 2)(lambda: project(kv_out))
    pl.when((n == 3) | (n == 4))(lambda: project(silu_out(hq_ref)))
    pl.when((n == 5) | (n == 6))(lambda: project(forget_out(kf_ref, bf_ref, False)))
    pl.when((n == 7) | (n == 8))(lambda: project(forget_out(kb_ref, bb_ref, True)))
    pl.when((n == 9) | (n == 10))(lambda: project(plain_out))
    pl.when(n >= 11)(lambda: project(silu_out(gate_ref)))


def _input_projection(layer, xs, mod, n1w, w_in, qnw, knw, lb, cos_t, sin_t):
    bm, bn = ROW_TILE, IN_COL_TILE
    n_m, n_n = T_ROWS // bm, IN_WIDTH // bn
    hpt = bn // HEAD_DIM

    def head_spec(first_tile):
        return pl.BlockSpec((hpt, bm, HEAD_DIM),
                            lambda m, n: (jnp.clip(n - first_tile, 0, 1), m, 0))

    def head_shape(dtype):
        return jax.ShapeDtypeStruct((HG_HEADS, T_ROWS, HEAD_DIM), dtype)

    kv_spec = pl.BlockSpec((ATTN_KV_HEADS, bm, HEAD_DIM), lambda m, n: (0, m, 0))
    kv_shape = jax.ShapeDtypeStruct((ATTN_KV_HEADS, T_ROWS, HEAD_DIM), BF16)
    return pl.pallas_call(
        _inproj_kernel,
        out_shape=(head_shape(BF16), kv_shape, kv_shape, head_shape(BF16),
                   head_shape(BF16), head_shape(F32), head_shape(BF16), head_shape(F32),
                   head_shape(BF16), head_shape(BF16)),
        grid=(n_m, n_n),
        in_specs=[
            pl.BlockSpec((bm, D_MODEL), lambda m, n: (m, 0)),
            pl.BlockSpec((None, SUBLANES, 6 * D_MODEL), lambda m, n: (layer, 0, 0)),
            pl.BlockSpec((1, D_MODEL), lambda m, n: (0, 0)),
            pl.BlockSpec((None, D_MODEL, bn), lambda m, n: (layer, 0, n)),
            pl.BlockSpec((1, HEAD_DIM), lambda m, n: (0, 0)),
            pl.BlockSpec((1, HEAD_DIM), lambda m, n: (0, 0)),
            pl.BlockSpec((1, bn), lambda m, n: (0, jnp.clip(n - 5, 0, 3))),
            pl.BlockSpec((bm, HEAD_DIM), lambda m, n: (m, 0)),
            pl.BlockSpec((bm, HEAD_DIM), lambda m, n: (m, 0)),
            pl.BlockSpec((2 * HEAD_DIM, 2 * HEAD_DIM), lambda m, n: (0, 0)),
        ],
        out_specs=(head_spec(0), kv_spec, kv_spec, head_spec(3),
                   head_spec(5), head_spec(5), head_spec(7), head_spec(7),
                   head_spec(9), head_spec(11)),
        scratch_shapes=[pltpu.VMEM((bm, D_MODEL), BF16)],
        compiler_params=_cparams(2),
        name="input_projection",
    )(xs, mod, n1w, w_in, qnw, knw, lb, cos_t, sin_t, _norm_rope_matrix())


def _attn_kernel(q_ref, k_ref, vt_ref, o_ref, acc_scr, st_scr):
    i = pl.program_id(1)
    bq = q_ref.shape[1]
    bk = ATTN_K_TILE
    n_chunks = T_ROWS // bk

    def scores(kc, r, slot):
        half = kc.shape[0] // 2
        mx = []
        for a in range(2):
            st = lax.dot_general(kc[a * half:(a + 1) * half], q_ref[r], (((1,), (1,)), ((), ())),
                                 preferred_element_type=F32)
            st_scr[slot, a * half:(a + 1) * half, :] = st
            mx.append(st.max(axis=0, keepdims=True))
        return jnp.maximum(mx[0], mx[1])

    def update(vtc, r, slot, m_cur, m, l):
        width = vtc.shape[1]
        m_new = jnp.maximum(m, m_cur)
        alpha = jnp.exp2(m - m_new)
        p = jnp.exp2(st_scr[slot, 0:width, :] - m_new)
        l = alpha * l + p.sum(axis=0, keepdims=True)
        acc_scr[r] = alpha * acc_scr[r] + jnp.dot(vtc, p.astype(BF16), preferred_element_type=F32)
        return m_new, l

    def finish(l):
        for r in range(ATTN_GROUP):
            o_ref[:, r * HEAD_DIM:(r + 1) * HEAD_DIM] = (acc_scr[r] / l[r]).T.astype(BF16)

    acc_scr[...] = jnp.zeros(acc_scr.shape, F32)
    m0 = tuple(jnp.full((1, bq), -jnp.inf, F32) for _ in range(ATTN_GROUP))
    l0 = tuple(jnp.zeros((1, bq), F32) for _ in range(ATTN_GROUP))

    @pl.when(i == 0)
    def _():
        kc = k_ref[0, 0:CTX_LEN, :]
        vtc = vt_ref[0, 0, :, 0:CTX_LEN]
        l = []
        mc = scores(kc, 0, 0)
        for r in range(ATTN_GROUP):
            mc_next = scores(kc, r + 1, (r + 1) % 2) if r + 1 < ATTN_GROUP else None
            l.append(update(vtc, r, r % 2, mc, m0[r], l0[r])[1])
            mc = mc_next
        finish(l)

    @pl.when(i > 0)
    def _():
        def chunk(j):
            return k_ref[0, pl.ds(pl.multiple_of(j * bk, bk), bk), :]

        def body(j, carry):
            mc, m, l = carry
            m, l = list(m), list(l)
            vtc = vt_ref[0, j]
            for r in range(ATTN_GROUP):
                if r + 1 < ATTN_GROUP:
                    mc_next = scores(chunk(j), r + 1, (r + 1) % 2)
                else:
                    mc_next = scores(chunk(jnp.minimum(j + 1, n_chunks - 1)), 0, 0)
                m[r], l[r] = update(vtc, r, r % 2, mc, m[r], l[r])
                mc = mc_next
            return mc, tuple(m), tuple(l)

        _, _, l = lax.fori_loop(0, n_chunks, body, (scores(chunk(0), 0, 0), m0, l0), unroll=True)
        finish(l)


def _attention(q, k, v):
    bq, bk = ATTN_Q_TILE, ATTN_K_TILE
    n_chunks = T_ROWS // bk
    vt = v.reshape(ATTN_KV_HEADS, n_chunks, bk, HEAD_DIM).transpose(0, 1, 3, 2)
    return pl.pallas_call(
        _attn_kernel,
        out_shape=jax.ShapeDtypeStruct((T_ROWS, ATTN_WIDTH), BF16),
        grid=(ATTN_KV_HEADS, T_ROWS // bq),
        in_specs=[
            pl.BlockSpec((ATTN_GROUP, bq, HEAD_DIM), lambda g, i: (g, i, 0)),
            pl.BlockSpec((1, T_ROWS, HEAD_DIM), lambda g, i: (g, 0, 0)),
            pl.BlockSpec((1, n_chunks, HEAD_DIM, bk), lambda g, i: (g, 0, 0, 0)),
        ],
        out_specs=pl.BlockSpec((bq, ATTN_GROUP * HEAD_DIM), lambda g, i: (i, g)),
        scratch_shapes=[pltpu.VMEM((ATTN_GROUP, HEAD_DIM, bq), F32),
                        pltpu.VMEM((2, bk, bq), F32)],
        compiler_params=_cparams(2),
        name="gqa_attention",
    )(q, k, vt)


def _chunk_cumsum(g, rev):
    n_groups = g.shape[0] // SUBLANES
    x = g.reshape(n_groups, SUBLANES, g.shape[1])
    sub = lax.broadcasted_iota(jnp.int32, x.shape, 1)
    step = 1
    while step < SUBLANES:
        if rev:
            x = x + jnp.where(sub < SUBLANES - step, pltpu.roll(x, SUBLANES - step, axis=1), 0.0)
        else:
            x = x + jnp.where(sub >= step, pltpu.roll(x, step, axis=1), 0.0)
        step *= 2
    per = HG_CHUNK // SUBLANES
    xs = [x[v] for v in range(n_groups)]
    for c0 in range(0, n_groups, per):
        if rev:
            for v in range(c0 + per - 2, c0 - 1, -1):
                xs[v] = xs[v] + xs[v + 1][0:1]
        else:
            for v in range(c0 + 1, c0 + per):
                xs[v] = xs[v] + xs[v - 1][SUBLANES - 1:SUBLANES]
    return jnp.concatenate(xs, axis=0)


def _hgrn_levels():
    halves = []
    half = HG_CHUNK // 2
    while half >= HG_DIAG:
        halves.append(half)
        half //= 2
    return halves


def _hgrn_consts(rev):
    c = HG_CHUNK
    levels = {}
    for half in _hgrn_levels():
        blk = 2 * half
        levels[half] = []
        for g in range(c // SUBLANES):
            t = lax.broadcasted_iota(jnp.int32, (SUBLANES, c), 0) + g * SUBLANES
            s = lax.broadcasted_iota(jnp.int32, (SUBLANES, c), 1)
            same = t // blk == s // blk
            if rev:
                levels[half].append(same & (t % blk < half) & (s % blk >= half))
            else:
                levels[half].append(same & (t % blk >= half) & (s % blk < half))
    t = lax.broadcasted_iota(jnp.int32, (c, c), 0)
    s = lax.broadcasted_iota(jnp.int32, (c, c), 1)
    band = []
    for d in range(HG_DIAG):
        if rev:
            band.append((s == t + d) & (t % HG_DIAG + d < HG_DIAG))
        else:
            band.append((s == t - d) & (t % HG_DIAG >= d))
    return levels, band


def _hgrn_scores(q, k, b, brow, rev, consts):
    levels, band = consts
    c = HG_CHUNK
    nt = (((1,), (1,)), ((), ()))
    groups = c // SUBLANES
    row8 = lax.broadcasted_iota(jnp.int32, (SUBLANES, HEAD_DIM), 0)
    rows = [slice(g * SUBLANES, (g + 1) * SUBLANES) for g in range(groups)]

    a = jnp.zeros((c, c), F32)
    for d in range(HG_DIAG):
        if d == 0:
            prod = q * k
        else:
            shift = c - d if rev else d
            prod = q * pltpu.roll(k, shift, axis=0) * jnp.exp2(b - pltpu.roll(b, shift, axis=0))
        a = jnp.where(band[d], prod.sum(axis=-1, keepdims=True), a)
    rows_a = [a[r] for r in rows]

    for half in _hgrn_levels():
        blk = 2 * half
        ref_off = half if rev else half - 1
        zs = []
        owners = []
        for g in range(groups):
            t0 = g * SUBLANES
            if half >= SUBLANES:
                ref = brow(t0 // blk * blk + ref_off)
                attending = (t0 % blk < half) if rev else (t0 % blk >= half)
                x = q[rows[g]] if attending else k[rows[g]]
                e = jnp.exp2(b[rows[g]] - ref if attending else ref - b[rows[g]])
                if attending:
                    owners.append(g)
            else:
                ref = brow(t0 + ref_off)
                for i in range(1, SUBLANES // blk):
                    ref = jnp.where(row8 >= i * blk, brow(t0 + i * blk + ref_off), ref)
                attending = (row8 % blk < half) if rev else (row8 % blk >= half)
                x = jnp.where(attending, q[rows[g]], k[rows[g]])
                e = jnp.exp2(-jnp.abs(b[rows[g]] - ref))
                owners.append(g)
            zs.append(x * e)
        z = jnp.concatenate(zs, axis=0).astype(BF16)
        lv = lax.dot_general(z, z, nt, preferred_element_type=F32)
        for g in owners:
            rows_a[g] = jnp.where(levels[half][g], lv[rows[g]], rows_a[g])
    return jnp.concatenate(rows_a, axis=0)


def _hgrn_kernel(qf_ref, kf_ref, bf_ref, vf_ref, qb_ref, kb_ref, bb_ref, vb_ref,
                 of_ref, ob_ref, s_scr):
    @pl.when(pl.program_id(0) == 0)
    def _():
        s_scr[...] = jnp.zeros(s_scr.shape, F32)

    c = HG_CHUNK
    n_chunks = HG_BLOCK // c
    consts = (_hgrn_consts(False), _hgrn_consts(True))
    nt = (((1,), (1,)), ((), ()))
    tn = (((0,), (0,)), ((), ()))
    dirs = ((qf_ref, kf_ref, bf_ref, vf_ref, of_ref, False),
            (qb_ref, kb_ref, bb_ref, vb_ref, ob_ref, True))

    def head_body(h, carry):
        jobs = []
        for di, (q_ref, k_ref, b_ref, v_ref, o_ref, rev) in enumerate(dirs):
            order = reversed(range(n_chunks)) if rev else range(n_chunks)
            for ci in order:
                rows = slice(ci * c, (ci + 1) * c)
                q = q_ref[h, rows, :].astype(F32)
                k = k_ref[h, rows, :].astype(F32)
                b = b_ref[h, rows, :]
                v = v_ref[h, rows, :]
                tot = b[0:1] if rev else b[c - 1:c]
                q_in = (q * jnp.exp2(b)).astype(BF16)
                k_out = (k * jnp.exp2(tot - b)).astype(BF16)
                upd = lax.dot_general(v, k_out, tn, preferred_element_type=F32)

                def brow(r, b_ref=b_ref, base=ci * c):
                    return b_ref[h, base + r:base + r + 1, :]

                a = _hgrn_scores(q, k, b, brow, rev, consts[di]).astype(BF16)
                jobs.append((di, o_ref, rows, q_in, jnp.exp2(tot), upd, a, v))
        st = [s_scr[0, h], s_scr[1, h]]
        for di, o_ref, rows, q_in, decay, upd, a, v in jobs:
            o = lax.dot_general(q_in, st[di].astype(BF16), nt, preferred_element_type=F32)
            st[di] = decay * st[di] + upd
            o_ref[h, rows, :] = o + jnp.dot(a, v, preferred_element_type=F32)
        s_scr[0, h] = st[0]
        s_scr[1, h] = st[1]
        return carry

    lax.fori_loop(0, HG_HEADS, head_body, 0)


def _hgrn2(hq, kf, bf, kb, bb, hv):
    nb = T_ROWS // HG_BLOCK
    fwd = lambda i: (0, i, 0)
    bwd = lambda i: (0, jnp.where(i == 0, 0, nb - i), 0)
    blk = (HG_HEADS, HG_BLOCK, HEAD_DIM)
    out = jax.ShapeDtypeStruct((HG_HEADS, T_ROWS, HEAD_DIM), F32)
    return pl.pallas_call(
        _hgrn_kernel,
        out_shape=(out, out),
        grid=(nb,),
        in_specs=[pl.BlockSpec(blk, fwd)] * 4 + [pl.BlockSpec(blk, bwd)] * 4,
        out_specs=(pl.BlockSpec(blk, fwd), pl.BlockSpec(blk, bwd)),
        scratch_shapes=[pltpu.VMEM((2, HG_HEADS, HEAD_DIM, HEAD_DIM), F32)],
        compiler_params=_cparams(1),
        name="hgrn2_scan",
    )(hq, kf, bf, hv, hq, kb, bb, hv)


def _outproj_kernel(attn_ref, of_ref, ob_ref, gate_ref, hnw_ref, w_ref, x_ref, mod_ref, n2w_ref,
                    xo_ref, h2_ref, mix_scr):
    m = pl.program_id(0)
    bm = x_ref.shape[0]
    acc = jnp.dot(attn_ref[...], w_ref[0:ATTN_WIDTH, :], preferred_element_type=F32)
    for h in range(HG_HEADS):
        y = _rms(of_ref[h] + ob_ref[h], hnw_ref[...])
        mix_scr[:, h * HEAD_DIM:(h + 1) * HEAD_DIM] = (y * gate_ref[h].astype(F32)).astype(BF16)
    acc = acc + jnp.dot(mix_scr[...], w_ref[ATTN_WIDTH:ATTN_WIDTH + HG_WIDTH, :],
                        preferred_element_type=F32)
    is_ctx = _row_is_ctx(m, bm)
    x_new = x_ref[...] + _mod_rows(mod_ref, is_ctx, 2) * acc
    xo_ref[...] = x_new
    h2 = _rms(x_new, n2w_ref[...]) * (1.0 + _mod_rows(mod_ref, is_ctx, 4)) + _mod_rows(mod_ref, is_ctx, 3)
    h2_ref[...] = h2.astype(BF16)


def _output_projection(layer, attn, o_f, o_b, gate, hnw, w_out, xs, mod, n2w):
    bm = OUT_ROW_TILE
    head_spec = pl.BlockSpec((HG_HEADS, bm, HEAD_DIM), lambda m: (0, m, 0))
    row_spec = pl.BlockSpec((bm, D_MODEL), lambda m: (m, 0))
    return pl.pallas_call(
        _outproj_kernel,
        out_shape=(jax.ShapeDtypeStruct((T_ROWS, D_MODEL), F32),
                   jax.ShapeDtypeStruct((T_ROWS, D_MODEL), BF16)),
        grid=(T_ROWS // bm,),
        in_specs=[
            pl.BlockSpec((bm, ATTN_WIDTH), lambda m: (m, 0)),
            head_spec, head_spec, head_spec,
            pl.BlockSpec((1, HEAD_DIM), lambda m: (0, 0)),
            pl.BlockSpec((None, ATTN_WIDTH + HG_WIDTH, D_MODEL), lambda m: (layer, 0, 0)),
            row_spec,
            pl.BlockSpec((None, SUBLANES, 6 * D_MODEL), lambda m: (layer, 0, 0)),
            pl.BlockSpec((1, D_MODEL), lambda m: (0, 0)),
        ],
        out_specs=(row_spec, row_spec),
        scratch_shapes=[pltpu.VMEM((bm, HG_WIDTH), BF16)],
        compiler_params=_cparams(1),
        name="output_projection",
    )(attn, o_f, o_b, gate, hnw, w_out, xs, mod, n2w)


def _ffn_up_kernel(h_ref, hp_ref, hn_ref, wg_ref, wu_ref, cw_ref, cb_ref, u_ref, hcat_scr, g_scr):
    m = pl.program_id(0)
    n = pl.program_id(1)
    bm = h_ref.shape[0]
    hl = CONV_HALO

    @pl.when(n == 0)
    def _():
        hcat_scr[0:hl, :] = hp_ref[...]
        hcat_scr[hl:hl + bm, :] = h_ref[...]
        hcat_scr[hl + bm:hl + bm + hl, :] = hn_ref[...]

    g_scr[...] = jnp.dot(hcat_scr[...], wg_ref[...], preferred_element_type=F32)
    up = jnp.dot(h_ref[...], wu_ref[...], preferred_element_type=F32)
    r = m * bm + lax.broadcasted_iota(jnp.int32, (bm, 1), 0)
    has_prev = (r != 0) & (r != CTX_LEN)
    has_next = (r != CTX_LEN - 1) & (r != T_ROWS - 1)
    g_prev = jnp.where(has_prev, g_scr[hl - 1:hl - 1 + bm, :], 0.0)
    g_next = jnp.where(has_next, g_scr[hl + 1:hl + 1 + bm, :], 0.0)
    gate = (g_prev * cw_ref[0:1, :] + g_scr[hl:hl + bm, :] * cw_ref[1:2, :]
            + g_next * cw_ref[2:3, :] + cb_ref[...])
    u_ref[...] = (_silu(gate) * up).astype(BF16)


def _ffn_up(layer, h2, w_up, conv_w, conv_b):
    bm, bn, hl = ROW_TILE, FF_COL_TILE, CONV_HALO
    n_n = D_FF // bn
    per = bm // hl
    last = T_ROWS // hl - 1
    return pl.pallas_call(
        _ffn_up_kernel,
        out_shape=jax.ShapeDtypeStruct((T_ROWS, D_FF), BF16),
        grid=(T_ROWS // bm, n_n),
        in_specs=[
            pl.BlockSpec((bm, D_MODEL), lambda m, n: (m, 0)),
            pl.BlockSpec((hl, D_MODEL), lambda m, n: (jnp.maximum(m * per - 1, 0), 0)),
            pl.BlockSpec((hl, D_MODEL), lambda m, n: (jnp.minimum((m + 1) * per, last), 0)),
            pl.BlockSpec((None, D_MODEL, bn), lambda m, n: (layer, 0, n)),
            pl.BlockSpec((None, D_MODEL, bn), lambda m, n: (layer, 0, n_n + n)),
            pl.BlockSpec((3, bn), lambda m, n: (0, n)),
            pl.BlockSpec((1, bn), lambda m, n: (0, n)),
        ],
        out_specs=pl.BlockSpec((bm, bn), lambda m, n: (m, n)),
        scratch_shapes=[pltpu.VMEM((bm + 2 * hl, D_MODEL), BF16),
                        pltpu.VMEM((bm + 2 * hl, bn), F32)],
        compiler_params=_cparams(2),
        name="ffn_up_conv",
    )(h2, h2, h2, w_up, w_up, conv_w, conv_b)


def _ffn_down_kernel(u_ref, w_ref, x_ref, mod_ref, fw_ref, o_ref, *, final):
    m = pl.program_id(0)
    bm = x_ref.shape[0]

    def run():
        acc = jnp.dot(u_ref[...], w_ref[...], preferred_element_type=F32)
        x_new = x_ref[...] + _mod_rows(mod_ref, _row_is_ctx(m, bm), 5) * acc
        o_ref[...] = _rms(x_new, fw_ref[...]) if final else x_new

    if final:
        pl.when(m > 0)(run)
    else:
        run()


def _ffn_down(layer, u, w_down, xs, mod, final_w, final):
    bm = CTX_LEN if final else OUT_ROW_TILE
    row_spec = pl.BlockSpec((bm, D_MODEL), lambda m: (m, 0))
    if final:
        out_rows = SEQ
        out_spec = pl.BlockSpec((bm, D_MODEL), lambda m: (jnp.maximum(m - 1, 0), 0))
    else:
        out_rows, out_spec = T_ROWS, row_spec
    return pl.pallas_call(
        functools.partial(_ffn_down_kernel, final=final),
        out_shape=jax.ShapeDtypeStruct((out_rows, D_MODEL), F32),
        grid=(T_ROWS // bm,),
        in_specs=[
            pl.BlockSpec((bm, D_FF), lambda m: (m, 0)),
            pl.BlockSpec((None, D_FF, D_MODEL), lambda m: (layer, 0, 0),
                         pipeline_mode=pl.Buffered(1)),
            row_spec,
            pl.BlockSpec((None, SUBLANES, 6 * D_MODEL), lambda m: (layer, 0, 0)),
            pl.BlockSpec((1, D_MODEL), lambda m: (0, 0)),
        ],
        out_specs=out_spec,
        compiler_params=_cparams(1),
        name="ffn_down",
    )(u, w_down, xs, mod, final_w)


def _rope_tables():
    f32 = np.float32
    rows = SEQ // GRID_W
    row = np.repeat(np.arange(rows, dtype=f32), GRID_W)
    col = np.tile(np.arange(GRID_W, dtype=f32), rows)
    freqs = np.power(f32(ROPE_THETA), -np.arange(ROPE_FREQS, dtype=f32) / f32(ROPE_FREQS)).astype(f32)
    ar, ac = row[:, None] * freqs, col[:, None] * freqs
    cos = np.concatenate([np.cos(ar), np.cos(ar), np.cos(ac), np.cos(ac)], axis=1)
    sin = np.concatenate([np.sin(ar), np.sin(ar), np.sin(ac), np.sin(ac)], axis=1)
    cos = np.concatenate([np.ones((CTX_LEN, HEAD_DIM), f32), cos], axis=0)
    sin = np.concatenate([np.zeros((CTX_LEN, HEAD_DIM), f32), sin], axis=0)
    return jnp.asarray(cos, F32), jnp.asarray(sin, F32)


def kernel(x, c, ctx, c_ctx, w_mod, b_mod, norm1_w, norm2_w, w_in, q_norm_w, k_norm_w, hg_lb_logits,
           hg_norm_w, w_out, w_up, conv_w, conv_b, w_down, final_norm_w):
    assert x.shape == (1, SEQ, D_MODEL) and ctx.shape == (1, CTX_LEN, D_MODEL)
    xs = jnp.concatenate([ctx[0], x[0]], axis=0)
    mod = _modulation(jnp.stack([c[0], c_ctx], axis=1), w_mod, b_mod)
    cos_t, sin_t = _rope_tables()
    lb_sm = jax.nn.softmax(hg_lb_logits.astype(F32), axis=1)
    lb_all = jnp.cumsum(lb_sm, axis=1) - lb_sm[:, :1]

    w_in, w_out, w_up, w_down = (w.astype(BF16) for w in (w_in, w_out, w_up, w_down))
    for l in range(DEPTH):
        lb = jnp.concatenate([lb_all[0, l], lb_all[1, l]])[None, :]
        q, k, v, hq, kf, bf, kb, bb, hv, gate = _input_projection(
            l, xs, mod, norm1_w[l][None, :], w_in, q_norm_w[l][None, :],
            k_norm_w[l][None, :], lb, cos_t, sin_t)
        attn = _attention(q, k, v)
        o_f, o_b = _hgrn2(hq, kf, bf, kb, bb, hv)
        xs, h2 = _output_projection(l, attn, o_f, o_b, gate, hg_norm_w[l][None, :],
                                    w_out, xs, mod, norm2_w[l][None, :])
        u = _ffn_up(l, h2, w_up, conv_w[l], conv_b[l][None, :])
        xs = _ffn_down(l, u, w_down, xs, mod, final_norm_w[None, :], final=(l == DEPTH - 1))
    return xs[None]
```

```python
import functools

import jax
import jax.numpy as jnp
import numpy as np
from jax import lax
from jax.experimental import pallas as pl
from jax.experimental.pallas import tpu as pltpu

F32 = jnp.float32
BF16 = jnp.bfloat16

D_MODEL = 2048
SEQ = 8192
DEPTH = 4
CTX_LEN = 256
T_ROWS = CTX_LEN + SEQ
GRID_W = 64
HEAD_DIM = 128
ATTN_HEADS = 8
ATTN_KV_HEADS = 2
ATTN_GROUP = 4
ATTN_WIDTH = 1024
KV_WIDTH = 256
HG_HEADS = 8
HG_WIDTH = 1024
IN_WIDTH = 6656
ROPE_THETA = 10000.0
ROPE_FREQS = 32
D_FF = 5632
EPS = 1e-6
Q_SCALE = HEAD_DIM ** -0.5 * 1.4426950408889634

LANES = 128
SUBLANES = 8
VMEM_LIMIT = 56 * 1024 * 1024

MOD_SLAB = 256
ROW_TILE = 768
OUT_ROW_TILE = 384
IN_COL_TILE = 512
FF_COL_TILE = 512
ATTN_Q_TILE = 256
ATTN_K_TILE = 1408
ATTN_QK_PARTS = 2
HG_BLOCK = 256
HG_ITER_HEADS = 4
HG_CHUNK = 64
HG_DIAG = 2
CONV_HALO = 16


def _cparams(n_axes):
    return pltpu.CompilerParams(dimension_semantics=("arbitrary",) * n_axes,
                                vmem_limit_bytes=VMEM_LIMIT)


def _rms(x, w):
    return x * lax.rsqrt(jnp.mean(x * x, axis=-1, keepdims=True) + EPS) * w


def _silu(x):
    h = 0.5 * x
    return h * jnp.tanh(h) + h


def _row_is_ctx(tile_idx, rows):
    r = tile_idx * rows + lax.broadcasted_iota(jnp.int32, (rows, 1), 0)
    return r < CTX_LEN


def _mod_rows(mod_ref, is_ctx, k):
    lat = mod_ref[0:1, k * D_MODEL:(k + 1) * D_MODEL]
    ctx = mod_ref[1:2, k * D_MODEL:(k + 1) * D_MODEL]
    return jnp.where(is_ctx, ctx, lat)


def _mod_kernel(s_ref, w_ref, b_ref, o_ref, acc_scr):
    k = pl.program_id(1)
    n_out = w_ref.shape[2]
    cw = 4 * LANES

    @pl.when(k == 0)
    def _():
        acc_scr[...] = jnp.zeros(acc_scr.shape, F32)

    r0 = pl.multiple_of(k * MOD_SLAB, MOD_SLAB)
    s = _silu(s_ref[pl.ds(r0, MOD_SLAB), :])
    sb = [jnp.tile(jnp.broadcast_to(s[:, i:i + 1], (MOD_SLAB, LANES)), (1, cw // LANES))
          for i in range(2)]
    for c0 in range(0, n_out, cw):
        w = w_ref[0, :, c0:c0 + cw]
        for i in range(2):
            part = (w * sb[i]).reshape(MOD_SLAB // SUBLANES, SUBLANES, cw).sum(axis=0)
            acc_scr[i, :, c0:c0 + cw] += part

    @pl.when(k == pl.num_programs(1) - 1)
    def _():
        bias = b_ref[0]
        rows = [acc_scr[i].sum(axis=0, keepdims=True) + bias for i in range(2)]
        o_ref[0] = jnp.concatenate(rows + [jnp.zeros((SUBLANES - 2, n_out), F32)], axis=0)


def _modulation(s_in, w_mod, b_mod):
    n_out = 6 * D_MODEL
    return pl.pallas_call(
        _mod_kernel,
        out_shape=jax.ShapeDtypeStruct((DEPTH, SUBLANES, n_out), F32),
        grid=(DEPTH, D_MODEL // MOD_SLAB),
        in_specs=[
            pl.BlockSpec((D_MODEL, 2), lambda l, k: (0, 0)),
            pl.BlockSpec((1, MOD_SLAB, n_out), lambda l, k: (l, k, 0)),
            pl.BlockSpec((1, 1, n_out), lambda l, k: (l, 0, 0)),
        ],
        out_specs=pl.BlockSpec((1, SUBLANES, n_out), lambda l, k: (l, 0, 0)),
        scratch_shapes=[pltpu.VMEM((2, SUBLANES, n_out), F32)],
        compiler_params=_cparams(2),
        name="adaln_modulation",
    )(s_in, w_mod, b_mod.reshape(DEPTH, 1, n_out))


def _norm_rope_matrix():
    i = lax.broadcasted_iota(jnp.int32, (2 * HEAD_DIM, 2 * HEAD_DIM), 0)
    j = lax.broadcasted_iota(jnp.int32, (2 * HEAD_DIM, 2 * HEAD_DIM), 1)
    ones = (i < HEAD_DIM) & (j < HEAD_DIM)
    ii, jj = i - HEAD_DIM, j - HEAD_DIM
    low = (jj // ROPE_FREQS) % 2 == 0
    rot = jnp.where(low & (ii == jj + ROPE_FREQS), -1.0,
                    jnp.where(~low & (ii == jj - ROPE_FREQS), 1.0, 0.0))
    rot = jnp.where((i >= HEAD_DIM) & (j >= HEAD_DIM), rot, 0.0)
    return jnp.where(ones, 1.0, rot).astype(BF16)


def _norm_rope(p, w, mat_ref, cos, sin, scale):
    pw = p * w
    lhs = jnp.concatenate([(p * p).astype(BF16), pw.astype(BF16)], axis=1)
    red = jnp.dot(lhs, mat_ref[...], preferred_element_type=F32)
    r = lax.rsqrt(red[:, 0:HEAD_DIM] * (1.0 / HEAD_DIM) + EPS) * scale
    return r * (pw * cos + red[:, HEAD_DIM:2 * HEAD_DIM] * sin)


def _inproj_kernel(x_ref, mod_ref, n1w_ref, w_ref, qnw_ref, knw_ref, lb_ref, cos_ref, sin_ref, mat_ref,
                   q_ref, k_ref, v_ref, hq_ref, kf_ref, bf_ref, kb_ref, bb_ref, hv_ref, gate_ref,
                   h_scr):
    m = pl.program_id(0)
    n = pl.program_id(1)
    bm = x_ref.shape[0]

    def modulated(rows, mod_row):
        x = x_ref[rows, :]
        r = lax.rsqrt(jnp.mean(x * x, axis=-1, keepdims=True) + EPS)
        gain = n1w_ref[...] * (1.0 + mod_ref[mod_row:mod_row + 1, D_MODEL:2 * D_MODEL])
        return (x * r * gain + mod_ref[mod_row:mod_row + 1, 0:D_MODEL]).astype(BF16)

    @pl.when(n == 0)
    def _():
        h_scr[...] = modulated(slice(None), 0)

    @pl.when((n == 0) & (m == 0))
    def _():
        h_scr[0:CTX_LEN, :] = modulated(slice(0, CTX_LEN), 1)

    def project(epilogue):
        ch = IN_COL_TILE // 2
        hpt = ch // HEAD_DIM
        rh = bm // 2
        ps = []
        for ri in range(2):
            rows = slice(ri * rh, (ri + 1) * rh)
            h = h_scr[rows, :]
            ps.append((rows, [jnp.dot(h, w_ref[:, a * ch:(a + 1) * ch], preferred_element_type=F32)
                              for a in range(2)]))
        for rows, halves in ps:
            for a, p in enumerate(halves):
                for j in range(hpt):
                    epilogue(a * hpt + j, rows, p[:, j * HEAD_DIM:(j + 1) * HEAD_DIM])

    def q_out(j, rows, p):
        q_ref[j, rows, :] = _norm_rope(p, qnw_ref[...], mat_ref, cos_ref[rows, :], sin_ref[rows, :],
                                       Q_SCALE).astype(BF16)

    def kv_out(j, rows, p):
        if j < ATTN_KV_HEADS:
            k_ref[j, rows, :] = _norm_rope(p, knw_ref[...], mat_ref, cos_ref[rows, :],
                                           sin_ref[rows, :], 1.0).astype(BF16)
        else:
            v_ref[j - ATTN_KV_HEADS, rows, :] = p.astype(BF16)

    def silu_out(ref):
        def out(j, rows, p):
            ref[j, rows, :] = _silu(p).astype(BF16)
        return out

    def forget_out(k_out, b_out, rev):
        def out(j, rows, p):
            lb = lb_ref[:, j * HEAD_DIM:(j + 1) * HEAD_DIM]
            fg = lb + (1.0 - lb) * jax.nn.sigmoid(p)
            k_out[j, rows, :] = (1.0 - fg).astype(BF16)
            b_out[j, rows, :] = _chunk_cumsum(jnp.log2(fg), rev)
        return out

    def plain_out(j, rows, p):
        hv_ref[j, rows, :] = p.astype(BF16)

    pl.when(n < 2)(lambda: project(q_out))
    pl.when(n == 2)(lambda: project(kv_out))
    pl.when((n == 3) | (n == 4))(lambda: project(silu_out(hq_ref)))
    pl.when((n == 5) | (n == 6))(lambda: project(forget_out(kf_ref, bf_ref, False)))
    pl.when((n == 7) | (n == 8))(lambda: project(forget_out(kb_ref, bb_ref, True)))
    pl.when((n == 9) | (n == 10))(lambda: project(plain_out))
    pl.when(n >= 11)(lambda: project(silu_out(gate_ref)))


def _input_projection(layer, xs, mod, n1w, w_in, qnw, knw, lb, cos_t, sin_t):
    bm, bn = ROW_TILE, IN_COL_TILE
    n_m, n_n = T_ROWS // bm, IN_WIDTH // bn
    hpt = bn // HEAD_DIM

    def head_spec(first_tile):
        return pl.BlockSpec((hpt, bm, HEAD_DIM),
                            lambda m, n: (jnp.clip(n - first_tile, 0, 1), m, 0))

    def head_shape(dtype):
        return jax.ShapeDtypeStruct((HG_HEADS, T_ROWS, HEAD_DIM), dtype)

    kv_spec = pl.BlockSpec((ATTN_KV_HEADS, bm, HEAD_DIM), lambda m, n: (0, m, 0))
    kv_shape = jax.ShapeDtypeStruct((ATTN_KV_HEADS, T_ROWS, HEAD_DIM), BF16)
    return pl.pallas_call(
        _inproj_kernel,
        out_shape=(head_shape(BF16), kv_shape, kv_shape, head_shape(BF16),
                   head_shape(BF16), head_shape(F32), head_shape(BF16), head_shape(F32),
                   head_shape(BF16), head_shape(BF16)),
        grid=(n_m, n_n),
        in_specs=[
            pl.BlockSpec((bm, D_MODEL), lambda m, n: (m, 0)),
            pl.BlockSpec((None, SUBLANES, 6 * D_MODEL), lambda m, n: (layer, 0, 0)),
            pl.BlockSpec((1, D_MODEL), lambda m, n: (0, 0)),
            pl.BlockSpec((None, D_MODEL, bn), lambda m, n: (layer, 0, n)),
            pl.BlockSpec((1, HEAD_DIM), lambda m, n: (0, 0)),
            pl.BlockSpec((1, HEAD_DIM), lambda m, n: (0, 0)),
            pl.BlockSpec((1, bn), lambda m, n: (0, jnp.clip(n - 5, 0, 3))),
            pl.BlockSpec((bm, HEAD_DIM), lambda m, n: (m, 0)),
            pl.BlockSpec((bm, HEAD_DIM), lambda m, n: (m, 0)),
            pl.BlockSpec((2 * HEAD_DIM, 2 * HEAD_DIM), lambda m, n: (0, 0)),
        ],
        out_specs=(head_spec(0), kv_spec, kv_spec, head_spec(3),
                   head_spec(5), head_spec(5), head_spec(7), head_spec(7),
                   head_spec(9), head_spec(11)),
        scratch_shapes=[pltpu.VMEM((bm, D_MODEL), BF16)],
        compiler_params=_cparams(2),
        name="input_projection",
    )(xs, mod, n1w, w_in, qnw, knw, lb, cos_t, sin_t, _norm_rope_matrix())


def _attn_kernel(q_ref, k_ref, vt_ref, o_ref, acc_scr, st_scr):
    i = pl.program_id(1)
    bq = q_ref.shape[1]
    bk = ATTN_K_TILE
    n_chunks = T_ROWS // bk

    def scores(kc, r, slot):
        part = kc.shape[0] // ATTN_QK_PARTS
        mx = None
        for a in range(ATTN_QK_PARTS):
            st = lax.dot_general(kc[a * part:(a + 1) * part], q_ref[r], (((1,), (1,)), ((), ())),
                                 preferred_element_type=F32)
            st_scr[slot, a * part:(a + 1) * part, :] = st
            m_part = st.max(axis=0, keepdims=True)
            mx = m_part if mx is None else jnp.maximum(mx, m_part)
        return mx

    def update(vtc, r, slot, m_cur, m, l):
        width = vtc.shape[1]
        m_new = jnp.maximum(m, m_cur)
        alpha = jnp.exp2(m - m_new)
        p = jnp.exp2(st_scr[slot, 0:width, :] - m_new)
        l = alpha * l + p.sum(axis=0, keepdims=True)
        acc_scr[r] = alpha * acc_scr[r] + jnp.dot(vtc, p.astype(BF16), preferred_element_type=F32)
        return m_new, l

    def finish(l):
        for r in range(ATTN_GROUP):
            o_ref[:, r * HEAD_DIM:(r + 1) * HEAD_DIM] = (acc_scr[r] / l[r]).T.astype(BF16)

    acc_scr[...] = jnp.zeros(acc_scr.shape, F32)
    m0 = tuple(jnp.full((1, bq), -jnp.inf, F32) for _ in range(ATTN_GROUP))
    l0 = tuple(jnp.zeros((1, bq), F32) for _ in range(ATTN_GROUP))

    @pl.when(i == 0)
    def _():
        kc = k_ref[0, 0:CTX_LEN, :]
        vtc = vt_ref[0, 0, :, 0:CTX_LEN]
        l = []
        mc = scores(kc, 0, 0)
        for r in range(ATTN_GROUP):
            mc_next = scores(kc, r + 1, (r + 1) % 2) if r + 1 < ATTN_GROUP else None
            l.append(update(vtc, r, r % 2, mc, m0[r], l0[r])[1])
            mc = mc_next
        finish(l)

    @pl.when(i > 0)
    def _():
        def chunk(j):
            return k_ref[0, pl.ds(pl.multiple_of(j * bk, bk), bk), :]

        def body(j, carry):
            mc, m, l = carry
            m, l = list(m), list(l)
            vtc = vt_ref[0, j]
            for r in range(ATTN_GROUP):
                if r + 1 < ATTN_GROUP:
                    mc_next = scores(chunk(j), r + 1, (r + 1) % 2)
                else:
                    mc_next = scores(chunk(jnp.minimum(j + 1, n_chunks - 1)), 0, 0)
                m[r], l[r] = update(vtc, r, r % 2, mc, m[r], l[r])
                mc = mc_next
            return mc, tuple(m), tuple(l)

        _, _, l = lax.fori_loop(0, n_chunks, body, (scores(chunk(0), 0, 0), m0, l0), unroll=True)
        finish(l)


def _attention(q, k, v):
    bq, bk = ATTN_Q_TILE, ATTN_K_TILE
    n_chunks = T_ROWS // bk
    vt = v.reshape(ATTN_KV_HEADS, n_chunks, bk, HEAD_DIM).transpose(0, 1, 3, 2)
    return pl.pallas_call(
        _attn_kernel,
        out_shape=jax.ShapeDtypeStruct((T_ROWS, ATTN_WIDTH), BF16),
        grid=(ATTN_KV_HEADS, T_ROWS // bq),
        in_specs=[
            pl.BlockSpec((ATTN_GROUP, bq, HEAD_DIM), lambda g, i: (g, i, 0)),
            pl.BlockSpec((1, T_ROWS, HEAD_DIM), lambda g, i: (g, 0, 0)),
            pl.BlockSpec((1, n_chunks, HEAD_DIM, bk), lambda g, i: (g, 0, 0, 0)),
        ],
        out_specs=pl.BlockSpec((bq, ATTN_GROUP * HEAD_DIM), lambda g, i: (i, g)),
        scratch_shapes=[pltpu.VMEM((ATTN_GROUP, HEAD_DIM, bq), F32),
                        pltpu.VMEM((2, bk, bq), F32)],
        compiler_params=_cparams(2),
        name="gqa_attention",
    )(q, k, vt)


def _chunk_cumsum(g, rev):
    n_groups = g.shape[0] // SUBLANES
    x = g.reshape(n_groups, SUBLANES, g.shape[1])
    sub = lax.broadcasted_iota(jnp.int32, x.shape, 1)
    step = 1
    while step < SUBLANES:
        if rev:
            x = x + jnp.where(sub < SUBLANES - step, pltpu.roll(x, SUBLANES - step, axis=1), 0.0)
        else:
            x = x + jnp.where(sub >= step, pltpu.roll(x, step, axis=1), 0.0)
        step *= 2
    per = HG_CHUNK // SUBLANES
    xs = [x[v] for v in range(n_groups)]
    for c0 in range(0, n_groups, per):
        if rev:
            for v in range(c0 + per - 2, c0 - 1, -1):
                xs[v] = xs[v] + xs[v + 1][0:1]
        else:
            for v in range(c0 + 1, c0 + per):
                xs[v] = xs[v] + xs[v - 1][SUBLANES - 1:SUBLANES]
    return jnp.concatenate(xs, axis=0)


def _hgrn_levels():
    halves = []
    half = HG_CHUNK // 2
    while half >= HG_DIAG:
        halves.append(half)
        half //= 2
    return halves


def _hgrn_consts(rev):
    c = HG_CHUNK
    levels = {}
    for half in _hgrn_levels():
        blk = 2 * half
        levels[half] = []
        for g in range(c // SUBLANES):
            t = lax.broadcasted_iota(jnp.int32, (SUBLANES, c), 0) + g * SUBLANES
            s = lax.broadcasted_iota(jnp.int32, (SUBLANES, c), 1)
            same = t // blk == s // blk
            if rev:
                levels[half].append(same & (t % blk < half) & (s % blk >= half))
            else:
                levels[half].append(same & (t % blk >= half) & (s % blk < half))
    t = lax.broadcasted_iota(jnp.int32, (c, c), 0)
    s = lax.broadcasted_iota(jnp.int32, (c, c), 1)
    band = []
    for d in range(HG_DIAG):
        if rev:
            band.append((s == t + d) & (t % HG_DIAG + d < HG_DIAG))
        else:
            band.append((s == t - d) & (t % HG_DIAG >= d))
    return levels, band


def _hgrn_scores(q, k, b, brow, rev, consts):
    levels, band = consts
    c = HG_CHUNK
    nt = (((1,), (1,)), ((), ()))
    groups = c // SUBLANES
    row8 = lax.broadcasted_iota(jnp.int32, (SUBLANES, HEAD_DIM), 0)
    rows = [slice(g * SUBLANES, (g + 1) * SUBLANES) for g in range(groups)]

    a = jnp.zeros((c, c), F32)
    for d in range(HG_DIAG):
        if d == 0:
            prod = q * k
        else:
            shift = c - d if rev else d
            prod = q * pltpu.roll(k, shift, axis=0) * jnp.exp2(b - pltpu.roll(b, shift, axis=0))
        a = jnp.where(band[d], prod.sum(axis=-1, keepdims=True), a)
    rows_a = [a[r] for r in rows]

    for half in _hgrn_levels():
        blk = 2 * half
        ref_off = half if rev else half - 1
        zs = []
        owners = []
        for g in range(groups):
            t0 = g * SUBLANES
            if half >= SUBLANES:
                ref = brow(t0 // blk * blk + ref_off)
                attending = (t0 % blk < half) if rev else (t0 % blk >= half)
                x = q[rows[g]] if attending else k[rows[g]]
                e = jnp.exp2(b[rows[g]] - ref if attending else ref - b[rows[g]])
                if attending:
                    owners.append(g)
            else:
                ref = brow(t0 + ref_off)
                for i in range(1, SUBLANES // blk):
                    ref = jnp.where(row8 >= i * blk, brow(t0 + i * blk + ref_off), ref)
                attending = (row8 % blk < half) if rev else (row8 % blk >= half)
                x = jnp.where(attending, q[rows[g]], k[rows[g]])
                e = jnp.exp2(-jnp.abs(b[rows[g]] - ref))
                owners.append(g)
            zs.append(x * e)
        z = jnp.concatenate(zs, axis=0).astype(BF16)
        lv = lax.dot_general(z, z, nt, preferred_element_type=F32)
        for g in owners:
            rows_a[g] = jnp.where(levels[half][g], lv[rows[g]], rows_a[g])
    return jnp.concatenate(rows_a, axis=0)


def _hgrn_kernel(qf_ref, kf_ref, bf_ref, vf_ref, qb_ref, kb_ref, bb_ref, vb_ref,
                 of_ref, ob_ref, s_scr):
    @pl.when(pl.program_id(0) == 0)
    def _():
        s_scr[...] = jnp.zeros(s_scr.shape, F32)

    c = HG_CHUNK
    n_chunks = HG_BLOCK // c
    consts = (_hgrn_consts(False), _hgrn_consts(True))
    nt = (((1,), (1,)), ((), ()))
    tn = (((0,), (0,)), ((), ()))
    dirs = ((qf_ref, kf_ref, bf_ref, vf_ref, of_ref, False),
            (qb_ref, kb_ref, bb_ref, vb_ref, ob_ref, True))

    def head_body(it, carry):
        jobs = []
        for hi in range(HG_ITER_HEADS):
            h = it * HG_ITER_HEADS + hi
            for di, (q_ref, k_ref, b_ref, v_ref, o_ref, rev) in enumerate(dirs):
                order = reversed(range(n_chunks)) if rev else range(n_chunks)
                for ci in order:
                    rows = slice(ci * c, (ci + 1) * c)
                    q = q_ref[h, rows, :].astype(F32)
                    k = k_ref[h, rows, :].astype(F32)
                    b = b_ref[h, rows, :]
                    v = v_ref[h, rows, :]
                    tot = b[0:1] if rev else b[c - 1:c]
                    q_in = (q * jnp.exp2(b)).astype(BF16)
                    k_out = (k * jnp.exp2(tot - b)).astype(BF16)
                    upd = lax.dot_general(v, k_out, tn, preferred_element_type=F32)

                    def brow(r, b_ref=b_ref, base=ci * c, h=h):
                        return b_ref[h, base + r:base + r + 1, :]

                    a = _hgrn_scores(q, k, b, brow, rev, consts[di]).astype(BF16)
                    jobs.append((hi, h, di, o_ref, rows, q_in, jnp.exp2(tot), upd, a, v))
        st = {(hi, di): s_scr[di, it * HG_ITER_HEADS + hi]
              for hi in range(HG_ITER_HEADS) for di in range(2)}
        for hi, h, di, o_ref, rows, q_in, decay, upd, a, v in jobs:
            o = lax.dot_general(q_in, st[hi, di].astype(BF16), nt, preferred_element_type=F32)
            st[hi, di] = decay * st[hi, di] + upd
            o_ref[h, rows, :] = o + jnp.dot(a, v, preferred_element_type=F32)
        for (hi, di), s in st.items():
            s_scr[di, it * HG_ITER_HEADS + hi] = s
        return carry

    lax.fori_loop(0, HG_HEADS // HG_ITER_HEADS, head_body, 0)


def _hgrn2(hq, kf, bf, kb, bb, hv):
    nb = T_ROWS // HG_BLOCK
    fwd = lambda i: (0, i, 0)
    bwd = lambda i: (0, jnp.where(i == 0, 0, nb - i), 0)
    blk = (HG_HEADS, HG_BLOCK, HEAD_DIM)
    out = jax.ShapeDtypeStruct((HG_HEADS, T_ROWS, HEAD_DIM), F32)
    return pl.pallas_call(
        _hgrn_kernel,
        out_shape=(out, out),
        grid=(nb,),
        in_specs=[pl.BlockSpec(blk, fwd)] * 4 + [pl.BlockSpec(blk, bwd)] * 4,
        out_specs=(pl.BlockSpec(blk, fwd), pl.BlockSpec(blk, bwd)),
        scratch_shapes=[pltpu.VMEM((2, HG_HEADS, HEAD_DIM, HEAD_DIM), F32)],
        compiler_params=_cparams(1),
        name="hgrn2_scan",
    )(hq, kf, bf, hv, hq, kb, bb, hv)


def _outproj_kernel(attn_ref, of_ref, ob_ref, gate_ref, hnw_ref, w_ref, x_ref, mod_ref, n2w_ref,
                    xo_ref, h2_ref, mix_scr):
    m = pl.program_id(0)
    bm = x_ref.shape[0]
    acc = jnp.dot(attn_ref[...], w_ref[0:ATTN_WIDTH, :], preferred_element_type=F32)
    for h in range(HG_HEADS):
        y = _rms(of_ref[h] + ob_ref[h], hnw_ref[...])
        mix_scr[:, h * HEAD_DIM:(h + 1) * HEAD_DIM] = (y * gate_ref[h].astype(F32)).astype(BF16)
    acc = acc + jnp.dot(mix_scr[...], w_ref[ATTN_WIDTH:ATTN_WIDTH + HG_WIDTH, :],
                        preferred_element_type=F32)
    is_ctx = _row_is_ctx(m, bm)
    x_new = x_ref[...] + _mod_rows(mod_ref, is_ctx, 2) * acc
    xo_ref[...] = x_new
    h2 = _rms(x_new, n2w_ref[...]) * (1.0 + _mod_rows(mod_ref, is_ctx, 4)) + _mod_rows(mod_ref, is_ctx, 3)
    h2_ref[...] = h2.astype(BF16)


def _output_projection(layer, attn, o_f, o_b, gate, hnw, w_out, xs, mod, n2w):
    bm = OUT_ROW_TILE
    head_spec = pl.BlockSpec((HG_HEADS, bm, HEAD_DIM), lambda m: (0, m, 0))
    row_spec = pl.BlockSpec((bm, D_MODEL), lambda m: (m, 0))
    return pl.pallas_call(
        _outproj_kernel,
        out_shape=(jax.ShapeDtypeStruct((T_ROWS, D_MODEL), F32),
                   jax.ShapeDtypeStruct((T_ROWS, D_MODEL), BF16)),
        grid=(T_ROWS // bm,),
        in_specs=[
            pl.BlockSpec((bm, ATTN_WIDTH), lambda m: (m, 0)),
            head_spec, head_spec, head_spec,
            pl.BlockSpec((1, HEAD_DIM), lambda m: (0, 0)),
            pl.BlockSpec((None, ATTN_WIDTH + HG_WIDTH, D_MODEL), lambda m: (layer, 0, 0)),
            row_spec,
            pl.BlockSpec((None, SUBLANES, 6 * D_MODEL), lambda m: (layer, 0, 0)),
            pl.BlockSpec((1, D_MODEL), lambda m: (0, 0)),
        ],
        out_specs=(row_spec, row_spec),
        scratch_shapes=[pltpu.VMEM((bm, HG_WIDTH), BF16)],
        compiler_params=_cparams(1),
        name="output_projection",
    )(attn, o_f, o_b, gate, hnw, w_out, xs, mod, n2w)


def _ffn_up_kernel(h_ref, hp_ref, hn_ref, wg_ref, wu_ref, cw_ref, cb_ref, u_ref, hcat_scr, g_scr):
    m = pl.program_id(0)
    n = pl.program_id(1)
    bm = h_ref.shape[0]
    hl = CONV_HALO

    @pl.when(n == 0)
    def _():
        hcat_scr[0:hl, :] = hp_ref[...]
        hcat_scr[hl:hl + bm, :] = h_ref[...]
        hcat_scr[hl + bm:hl + bm + hl, :] = hn_ref[...]

    g_scr[...] = jnp.dot(hcat_scr[...], wg_ref[...], preferred_element_type=F32)
    up = jnp.dot(h_ref[...], wu_ref[...], preferred_element_type=F32)
    r = m * bm + lax.broadcasted_iota(jnp.int32, (bm, 1), 0)
    has_prev = (r != 0) & (r != CTX_LEN)
    has_next = (r != CTX_LEN - 1) & (r != T_ROWS - 1)
    g_prev = jnp.where(has_prev, g_scr[hl - 1:hl - 1 + bm, :], 0.0)
    g_next = jnp.where(has_next, g_scr[hl + 1:hl + 1 + bm, :], 0.0)
    gate = (g_prev * cw_ref[0:1, :] + g_scr[hl:hl + bm, :] * cw_ref[1:2, :]
            + g_next * cw_ref[2:3, :] + cb_ref[...])
    u_ref[...] = (_silu(gate) * up).astype(BF16)


def _ffn_up(layer, h2, w_up, conv_w, conv_b):
    bm, bn, hl = ROW_TILE, FF_COL_TILE, CONV_HALO
    n_n = D_FF // bn
    per = bm // hl
    last = T_ROWS // hl - 1
    return pl.pallas_call(
        _ffn_up_kernel,
        out_shape=jax.ShapeDtypeStruct((T_ROWS, D_FF), BF16),
        grid=(T_ROWS // bm, n_n),
        in_specs=[
            pl.BlockSpec((bm, D_MODEL), lambda m, n: (m, 0)),
            pl.BlockSpec((hl, D_MODEL), lambda m, n: (jnp.maximum(m * per - 1, 0), 0)),
            pl.BlockSpec((hl, D_MODEL), lambda m, n: (jnp.minimum((m + 1) * per, last), 0)),
            pl.BlockSpec((None, D_MODEL, bn), lambda m, n: (layer, 0, n)),
            pl.BlockSpec((None, D_MODEL, bn), lambda m, n: (layer, 0, n_n + n)),
            pl.BlockSpec((3, bn), lambda m, n: (0, n)),
            pl.BlockSpec((1, bn), lambda m, n: (0, n)),
        ],
        out_specs=pl.BlockSpec((bm, bn), lambda m, n: (m, n)),
        scratch_shapes=[pltpu.VMEM((bm + 2 * hl, D_MODEL), BF16),
                        pltpu.VMEM((bm + 2 * hl, bn), F32)],
        compiler_params=_cparams(2),
        name="ffn_up_conv",
    )(h2, h2, h2, w_up, w_up, conv_w, conv_b)


def _ffn_down_kernel(u_ref, w_ref, x_ref, mod_ref, fw_ref, o_ref, *, final):
    m = pl.program_id(0)
    bm = x_ref.shape[0]

    def run():
        acc = jnp.dot(u_ref[...], w_ref[...], preferred_element_type=F32)
        x_new = x_ref[...] + _mod_rows(mod_ref, _row_is_ctx(m, bm), 5) * acc
        o_ref[...] = _rms(x_new, fw_ref[...]) if final else x_new

    if final:
        pl.when(m > 0)(run)
    else:
        run()


def _ffn_down(layer, u, w_down, xs, mod, final_w, final):
    bm = CTX_LEN if final else OUT_ROW_TILE
    row_spec = pl.BlockSpec((bm, D_MODEL), lambda m: (m, 0))
    if final:
        out_rows = SEQ
        out_spec = pl.BlockSpec((bm, D_MODEL), lambda m: (jnp.maximum(m - 1, 0), 0))
    else:
        out_rows, out_spec = T_ROWS, row_spec
    return pl.pallas_call(
        functools.partial(_ffn_down_kernel, final=final),
        out_shape=jax.ShapeDtypeStruct((out_rows, D_MODEL), F32),
        grid=(T_ROWS // bm,),
        in_specs=[
            pl.BlockSpec((bm, D_FF), lambda m: (m, 0)),
            pl.BlockSpec((None, D_FF, D_MODEL), lambda m: (layer, 0, 0),
                         pipeline_mode=pl.Buffered(1)),
            row_spec,
            pl.BlockSpec((None, SUBLANES, 6 * D_MODEL), lambda m: (layer, 0, 0)),
            pl.BlockSpec((1, D_MODEL), lambda m: (0, 0)),
        ],
        out_specs=out_spec,
        compiler_params=_cparams(1),
        name="ffn_down",
    )(u, w_down, xs, mod, final_w)


def _rope_tables():
    f32 = np.float32
    rows = SEQ // GRID_W
    row = np.repeat(np.arange(rows, dtype=f32), GRID_W)
    col = np.tile(np.arange(GRID_W, dtype=f32), rows)
    freqs = np.power(f32(ROPE_THETA), -np.arange(ROPE_FREQS, dtype=f32) / f32(ROPE_FREQS)).astype(f32)
    ar, ac = row[:, None] * freqs, col[:, None] * freqs
    cos = np.concatenate([np.cos(ar), np.cos(ar), np.cos(ac), np.cos(ac)], axis=1)
    sin = np.concatenate([np.sin(ar), np.sin(ar), np.sin(ac), np.sin(ac)], axis=1)
    cos = np.concatenate([np.ones((CTX_LEN, HEAD_DIM), f32), cos], axis=0)
    sin = np.concatenate([np.zeros((CTX_LEN, HEAD_DIM), f32), sin], axis=0)
    return jnp.asarray(cos, F32), jnp.asarray(sin, F32)


def kernel(x, c, ctx, c_ctx, w_mod, b_mod, norm1_w, norm2_w, w_in, q_norm_w, k_norm_w, hg_lb_logits,
           hg_norm_w, w_out, w_up, conv_w, conv_b, w_down, final_norm_w):
    assert x.shape == (1, SEQ, D_MODEL) and ctx.shape == (1, CTX_LEN, D_MODEL)
    xs = jnp.concatenate([ctx[0], x[0]], axis=0)
    mod = _modulation(jnp.stack([c[0], c_ctx], axis=1), w_mod, b_mod)
    cos_t, sin_t = _rope_tables()
    lb_sm = jax.nn.softmax(hg_lb_logits.astype(F32), axis=1)
    lb_all = jnp.cumsum(lb_sm, axis=1) - lb_sm[:, :1]

    w_in, w_out, w_up, w_down = (w.astype(BF16) for w in (w_in, w_out, w_up, w_down))
    for l in range(DEPTH):
        lb = jnp.concatenate([lb_all[0, l], lb_all[1, l]])[None, :]
        q, k, v, hq, kf, bf, kb, bb, hv, gate = _input_projection(
            l, xs, mod, norm1_w[l][None, :], w_in, q_norm_w[l][None, :],
            k_norm_w[l][None, :], lb, cos_t, sin_t)
        attn = _attention(q, k, v)
        o_f, o_b = _hgrn2(hq, kf, bf, kb, bb, hv)
        xs, h2 = _output_projection(l, attn, o_f, o_b, gate, hg_norm_w[l][None, :],
                                    w_out, xs, mod, norm2_w[l][None, :])
        u = _ffn_up(l, h2, w_up, conv_w[l], conv_b[l][None, :])
        xs = _ffn_down(l, u, w_down, xs, mod, final_norm_w[None, :], final=(l == DEPTH - 1))
    return xs[None]
```

```python
import functools

import jax
import jax.numpy as jnp
import numpy as np
from jax import lax
from jax.experimental import pallas as pl
from jax.experimental.pallas import tpu as pltpu

F32 = jnp.float32
BF16 = jnp.bfloat16

D_MODEL = 2048
SEQ = 8192
DEPTH = 4
CTX_LEN = 256
T_ROWS = CTX_LEN + SEQ
GRID_W = 64
HEAD_DIM = 128
ATTN_HEADS = 8
ATTN_KV_HEADS = 2
ATTN_GROUP = 4
ATTN_WIDTH = 1024
KV_WIDTH = 256
HG_HEADS = 8
HG_WIDTH = 1024
IN_WIDTH = 6656
ROPE_THETA = 10000.0
ROPE_FREQS = 32
D_FF = 5632
EPS = 1e-6
Q_SCALE = HEAD_DIM ** -0.5 * 1.4426950408889634

LANES = 128
SUBLANES = 8
VMEM_LIMIT = 56 * 1024 * 1024

MOD_SLAB = 256
ROW_TILE = 768
OUT_ROW_TILE = 384
IN_COL_TILE = 512
FF_COL_TILE = 512
ATTN_Q_TILE = 256
ATTN_K_TILE = 1408
ATTN_QK_PARTS = 2
HG_BLOCK = 256
HG_ITER_HEADS = 4
HG_CHUNK = 64
HG_DIAG = 2
CONV_HALO = 16


def _cparams(n_axes):
    return pltpu.CompilerParams(dimension_semantics=("arbitrary",) * n_axes,
                                vmem_limit_bytes=VMEM_LIMIT)


def _rms(x, w):
    return x * lax.rsqrt(jnp.mean(x * x, axis=-1, keepdims=True) + EPS) * w


def _silu(x):
    h = 0.5 * x
    return h * jnp.tanh(h) + h


def _row_is_ctx(tile_idx, rows):
    r = tile_idx * rows + lax.broadcasted_iota(jnp.int32, (rows, 1), 0)
    return r < CTX_LEN


def _mod_rows(mod_ref, is_ctx, k):
    lat = mod_ref[0:1, k * D_MODEL:(k + 1) * D_MODEL]
    ctx = mod_ref[1:2, k * D_MODEL:(k + 1) * D_MODEL]
    return jnp.where(is_ctx, ctx, lat)


def _mod_kernel(s_ref, w_ref, b_ref, o_ref, acc_scr):
    k = pl.program_id(1)
    n_out = w_ref.shape[2]
    cw = 4 * LANES

    @pl.when(k == 0)
    def _():
        acc_scr[...] = jnp.zeros(acc_scr.shape, F32)

    r0 = pl.multiple_of(k * MOD_SLAB, MOD_SLAB)
    s = _silu(s_ref[pl.ds(r0, MOD_SLAB), :])
    sb = [jnp.tile(jnp.broadcast_to(s[:, i:i + 1], (MOD_SLAB, LANES)), (1, cw // LANES))
          for i in range(2)]
    for c0 in range(0, n_out, cw):
        w = w_ref[0, :, c0:c0 + cw]
        for i in range(2):
            part = (w * sb[i]).reshape(MOD_SLAB // SUBLANES, SUBLANES, cw).sum(axis=0)
            acc_scr[i, :, c0:c0 + cw] += part

    @pl.when(k == pl.num_programs(1) - 1)
    def _():
        bias = b_ref[0]
        rows = [acc_scr[i].sum(axis=0, keepdims=True) + bias for i in range(2)]
        o_ref[0] = jnp.concatenate(rows + [jnp.zeros((SUBLANES - 2, n_out), F32)], axis=0)


def _modulation(s_in, w_mod, b_mod):
    n_out = 6 * D_MODEL
    return pl.pallas_call(
        _mod_kernel,
        out_shape=jax.ShapeDtypeStruct((DEPTH, SUBLANES, n_out), F32),
        grid=(DEPTH, D_MODEL // MOD_SLAB),
        in_specs=[
            pl.BlockSpec((D_MODEL, 2), lambda l, k: (0, 0)),
            pl.BlockSpec((1, MOD_SLAB, n_out), lambda l, k: (l, k, 0)),
            pl.BlockSpec((1, 1, n_out), lambda l, k: (l, 0, 0)),
        ],
        out_specs=pl.BlockSpec((1, SUBLANES, n_out), lambda l, k: (l, 0, 0)),
        scratch_shapes=[pltpu.VMEM((2, SUBLANES, n_out), F32)],
        compiler_params=_cparams(2),
        name="adaln_modulation",
    )(s_in, w_mod, b_mod.reshape(DEPTH, 1, n_out))


def _norm_rope_matrix():
    i = lax.broadcasted_iota(jnp.int32, (2 * HEAD_DIM, 2 * HEAD_DIM), 0)
    j = lax.broadcasted_iota(jnp.int32, (2 * HEAD_DIM, 2 * HEAD_DIM), 1)
    ones = (i < HEAD_DIM) & (j < HEAD_DIM)
    ii, jj = i - HEAD_DIM, j - HEAD_DIM
    low = (jj // ROPE_FREQS) % 2 == 0
    rot = jnp.where(low & (ii == jj + ROPE_FREQS), -1.0,
                    jnp.where(~low & (ii == jj - ROPE_FREQS), 1.0, 0.0))
    rot = jnp.where((i >= HEAD_DIM) & (j >= HEAD_DIM), rot, 0.0)
    return jnp.where(ones, 1.0, rot).astype(BF16)


def _norm_rope(p, w, mat_ref, cos, sin, scale):
    pw = p * w
    lhs = jnp.concatenate([(p * p).astype(BF16), pw.astype(BF16)], axis=1)
    red = jnp.dot(lhs, mat_ref[...], preferred_element_type=F32)
    r = lax.rsqrt(red[:, 0:HEAD_DIM] * (1.0 / HEAD_DIM) + EPS) * scale
    return r * (pw * cos + red[:, HEAD_DIM:2 * HEAD_DIM] * sin)


def _inproj_kernel(x_ref, mod_ref, n1w_ref, w_ref, qnw_ref, knw_ref, lb_ref, cos_ref, sin_ref, mat_ref,
                   q_ref, k_ref, v_ref, hq_ref, kf_ref, bf_ref, kb_ref, bb_ref, hv_ref, gate_ref,
                   h_scr):
    m = pl.program_id(0)
    n = pl.program_id(1)
    bm = x_ref.shape[0]

    def modulated(rows, mod_row):
        x = x_ref[rows, :]
        r = lax.rsqrt(jnp.mean(x * x, axis=-1, keepdims=True) + EPS)
        gain = n1w_ref[...] * (1.0 + mod_ref[mod_row:mod_row + 1, D_MODEL:2 * D_MODEL])
        return (x * r * gain + mod_ref[mod_row:mod_row + 1, 0:D_MODEL]).astype(BF16)

    @pl.when(n == 0)
    def _():
        h_scr[...] = modulated(slice(None), 0)

    @pl.when((n == 0) & (m == 0))
    def _():
        h_scr[0:CTX_LEN, :] = modulated(slice(0, CTX_LEN), 1)

    def project(epilogue):
        ch = IN_COL_TILE // 2
        hpt = ch // HEAD_DIM
        rh = bm // 2
        ps = []
        for ri in range(2):
            rows = slice(ri * rh, (ri + 1) * rh)
            h = h_scr[rows, :]
            ps.append((rows, [jnp.dot(h, w_ref[:, a * ch:(a + 1) * ch], preferred_element_type=F32)
                              for a in range(2)]))
        for rows, halves in ps:
            for a, p in enumerate(halves):
                for j in range(hpt):
                    epilogue(a * hpt + j, rows, p[:, j * HEAD_DIM:(j + 1) * HEAD_DIM])

    def q_out(j, rows, p):
        q_ref[j, rows, :] = _norm_rope(p, qnw_ref[...], mat_ref, cos_ref[rows, :], sin_ref[rows, :],
                                       Q_SCALE).astype(BF16)

    def kv_out(j, rows, p):
        if j < ATTN_KV_HEADS:
            k_ref[j, rows, :] = _norm_rope(p, knw_ref[...], mat_ref, cos_ref[rows, :],
                                           sin_ref[rows, :], 1.0).astype(BF16)
        else:
            v_ref[j - ATTN_KV_HEADS, rows, :] = p.astype(BF16)

    def silu_out(ref):
        def out(j, rows, p):
            ref[j, rows, :] = _silu(p).astype(BF16)
        return out

    def forget_out(k_out, b_out, rev):
        def out(j, rows, p):
            lb = lb_ref[:, j * HEAD_DIM:(j + 1) * HEAD_DIM]
            fg = lb + (1.0 - lb) * jax.nn.sigmoid(p)
            k_out[j, rows, :] = (1.0 - fg).astype(BF16)
            b_out[j, rows, :] = _chunk_cumsum(jnp.log2(fg), rev)
        return out

    def plain_out(j, rows, p):
        hv_ref[j, rows, :] = p.astype(BF16)

    pl.when(n < 2)(lambda: project(q_out))
    pl.when(n == 2)(lambda: project(kv_out))
    pl.when((n == 3) | (n == 4))(lambda: project(silu_out(hq_ref)))
    pl.when((n == 5) | (n == 6))(lambda: project(forget_out(kf_ref, bf_ref, False)))
    pl.when((n == 7) | (n == 8))(lambda: project(forget_out(kb_ref, bb_ref, True)))
    pl.when((n == 9) | (n == 10))(lambda: project(plain_out))
    pl.when(n >= 11)(lambda: project(silu_out(gate_ref)))


def _input_projection(layer, xs, mod, n1w, w_in, qnw, knw, lb, cos_t, sin_t):
    bm, bn = ROW_TILE, IN_COL_TILE
    n_m, n_n = T_ROWS // bm, IN_WIDTH // bn
    hpt = bn // HEAD_DIM

    def head_spec(first_tile):
        return pl.BlockSpec((hpt, bm, HEAD_DIM),
                            lambda m, n: (jnp.clip(n - first_tile, 0, 1), m, 0))

    def head_shape(dtype):
        return jax.ShapeDtypeStruct((HG_HEADS, T_ROWS, HEAD_DIM), dtype)

    kv_spec = pl.BlockSpec((ATTN_KV_HEADS, bm, HEAD_DIM), lambda m, n: (0, m, 0))
    kv_shape = jax.ShapeDtypeStruct((ATTN_KV_HEADS, T_ROWS, HEAD_DIM), BF16)
    return pl.pallas_call(
        _inproj_kernel,
        out_shape=(head_shape(BF16), kv_shape, kv_shape, head_shape(BF16),
                   head_shape(BF16), head_shape(F32), head_shape(BF16), head_shape(F32),
                   head_shape(BF16), head_shape(BF16)),
        grid=(n_m, n_n),
        in_specs=[
            pl.BlockSpec((bm, D_MODEL), lambda m, n: (m, 0)),
            pl.BlockSpec((None, SUBLANES, 6 * D_MODEL), lambda m, n: (layer, 0, 0)),
            pl.BlockSpec((1, D_MODEL), lambda m, n: (0, 0)),
            pl.BlockSpec((None, D_MODEL, bn), lambda m, n: (layer, 0, n)),
            pl.BlockSpec((1, HEAD_DIM), lambda m, n: (0, 0)),
            pl.BlockSpec((1, HEAD_DIM), lambda m, n: (0, 0)),
            pl.BlockSpec((1, bn), lambda m, n: (0, jnp.clip(n - 5, 0, 3))),
            pl.BlockSpec((bm, HEAD_DIM), lambda m, n: (m, 0)),
            pl.BlockSpec((bm, HEAD_DIM), lambda m, n: (m, 0)),
            pl.BlockSpec((2 * HEAD_DIM, 2 * HEAD_DIM), lambda m, n: (0, 0)),
        ],
        out_specs=(head_spec(0), kv_spec, kv_spec, head_spec(3),
                   head_spec(5), head_spec(5), head_spec(7), head_spec(7),
                   head_spec(9), head_spec(11)),
        scratch_shapes=[pltpu.VMEM((bm, D_MODEL), BF16)],
        compiler_params=_cparams(2),
        name="input_projection",
    )(xs, mod, n1w, w_in, qnw, knw, lb, cos_t, sin_t, _norm_rope_matrix())


def _attn_kernel(q_ref, k_ref, vt_ref, o_ref, acc_scr, st_scr):
    i = pl.program_id(1)
    bq = q_ref.shape[1]
    bk = ATTN_K_TILE
    n_chunks = T_ROWS // bk

    def scores(kc, r, slot):
        part = kc.shape[0] // ATTN_QK_PARTS
        mx = None
        for a in range(ATTN_QK_PARTS):
            st = lax.dot_general(kc[a * part:(a + 1) * part], q_ref[r], (((1,), (1,)), ((), ())),
                                 preferred_element_type=F32)
            st_scr[slot, a * part:(a + 1) * part, :] = st
            m_part = st.max(axis=0, keepdims=True)
            mx = m_part if mx is None else jnp.maximum(mx, m_part)
        return mx

    def update(vtc, r, slot, m_cur, m, l):
        width = vtc.shape[1]
        m_new = jnp.maximum(m, m_cur)
        alpha = jnp.exp2(m - m_new)
        p = jnp.exp2(st_scr[slot, 0:width, :] - m_new)
        l = alpha * l + p.sum(axis=0, keepdims=True)
        acc_scr[r] = alpha * acc_scr[r] + jnp.dot(vtc, p.astype(BF16), preferred_element_type=F32)
        return m_new, l

    def finish(l):
        for r in range(ATTN_GROUP):
            o_ref[:, r * HEAD_DIM:(r + 1) * HEAD_DIM] = (acc_scr[r] / l[r]).T.astype(BF16)

    acc_scr[...] = jnp.zeros(acc_scr.shape, F32)
    m0 = tuple(jnp.full((1, bq), -jnp.inf, F32) for _ in range(ATTN_GROUP))
    l0 = tuple(jnp.zeros((1, bq), F32) for _ in range(ATTN_GROUP))

    @pl.when(i == 0)
    def _():
        kc = k_ref[0, 0:CTX_LEN, :]
        vtc = vt_ref[0, 0, :, 0:CTX_LEN]
        l = []
        mc = scores(kc, 0, 0)
        for r in range(ATTN_GROUP):
            mc_next = scores(kc, r + 1, (r + 1) % 2) if r + 1 < ATTN_GROUP else None
            l.append(update(vtc, r, r % 2, mc, m0[r], l0[r])[1])
            mc = mc_next
        finish(l)

    @pl.when(i > 0)
    def _():
        def chunk(j):
            return k_ref[0, pl.ds(pl.multiple_of(j * bk, bk), bk), :]

        def body(j, carry):
            mc, m, l = carry
            m, l = list(m), list(l)
            vtc = vt_ref[0, j]
            for r in range(ATTN_GROUP):
                if r + 1 < ATTN_GROUP:
                    mc_next = scores(chunk(j), r + 1, (r + 1) % 2)
                else:
                    mc_next = scores(chunk(jnp.minimum(j + 1, n_chunks - 1)), 0, 0)
                m[r], l[r] = update(vtc, r, r % 2, mc, m[r], l[r])
                mc = mc_next
            return mc, tuple(m), tuple(l)

        _, _, l = lax.fori_loop(0, n_chunks, body, (scores(chunk(0), 0, 0), m0, l0), unroll=True)
        finish(l)


def _attention(q, k, v):
    bq, bk = ATTN_Q_TILE, ATTN_K_TILE
    n_chunks = T_ROWS // bk
    vt = v.reshape(ATTN_KV_HEADS, n_chunks, bk, HEAD_DIM).transpose(0, 1, 3, 2)
    return pl.pallas_call(
        _attn_kernel,
        out_shape=jax.ShapeDtypeStruct((T_ROWS, ATTN_WIDTH), BF16),
        grid=(ATTN_KV_HEADS, T_ROWS // bq),
        in_specs=[
            pl.BlockSpec((ATTN_GROUP, bq, HEAD_DIM), lambda g, i: (g, i, 0)),
            pl.BlockSpec((1, T_ROWS, HEAD_DIM), lambda g, i: (g, 0, 0)),
            pl.BlockSpec((1, n_chunks, HEAD_DIM, bk), lambda g, i: (g, 0, 0, 0)),
        ],
        out_specs=pl.BlockSpec((bq, ATTN_GROUP * HEAD_DIM), lambda g, i: (i, g)),
        scratch_shapes=[pltpu.VMEM((ATTN_GROUP, HEAD_DIM, bq), F32),
                        pltpu.VMEM((2, bk, bq), F32)],
        compiler_params=_cparams(2),
        name="gqa_attention",
    )(q, k, vt)


def _chunk_cumsum(g, rev):
    n_groups = g.shape[0] // SUBLANES
    x = g.reshape(n_groups, SUBLANES, g.shape[1])
    sub = lax.broadcasted_iota(jnp.int32, x.shape, 1)
    step = 1
    while step < SUBLANES:
        if rev:
            x = x + jnp.where(sub < SUBLANES - step, pltpu.roll(x, SUBLANES - step, axis=1), 0.0)
        else:
            x = x + jnp.where(sub >= step, pltpu.roll(x, step, axis=1), 0.0)
        step *= 2
    per = HG_CHUNK // SUBLANES
    xs = [x[v] for v in range(n_groups)]
    for c0 in range(0, n_groups, per):
        if rev:
            for v in range(c0 + per - 2, c0 - 1, -1):
                xs[v] = xs[v] + xs[v + 1][0:1]
        else:
            for v in range(c0 + 1, c0 + per):
                xs[v] = xs[v] + xs[v - 1][SUBLANES - 1:SUBLANES]
    return jnp.concatenate(xs, axis=0)


def _hgrn_levels():
    halves = []
    half = HG_CHUNK // 2
    while half >= HG_DIAG:
        halves.append(half)
        half //= 2
    return halves


def _hgrn_consts(rev):
    c = HG_CHUNK
    levels = {}
    for half in _hgrn_levels():
        blk = 2 * half
        levels[half] = []
        for g in range(c // SUBLANES):
            t = lax.broadcasted_iota(jnp.int32, (SUBLANES, c), 0) + g * SUBLANES
            s = lax.broadcasted_iota(jnp.int32, (SUBLANES, c), 1)
            same = t // blk == s // blk
            if rev:
                levels[half].append(same & (t % blk < half) & (s % blk >= half))
            else:
                levels[half].append(same & (t % blk >= half) & (s % blk < half))
    t = lax.broadcasted_iota(jnp.int32, (c, c), 0)
    s = lax.broadcasted_iota(jnp.int32, (c, c), 1)
    band = []
    for d in range(HG_DIAG):
        if rev:
            band.append((s == t + d) & (t % HG_DIAG + d < HG_DIAG))
        else:
            band.append((s == t - d) & (t % HG_DIAG >= d))
    return levels, band


def _hgrn_scores(q, k, b, brow, rev, consts):
    levels, band = consts
    c = HG_CHUNK
    nt = (((1,), (1,)), ((), ()))
    groups = c // SUBLANES
    row8 = lax.broadcasted_iota(jnp.int32, (SUBLANES, HEAD_DIM), 0)
    rows = [slice(g * SUBLANES, (g + 1) * SUBLANES) for g in range(groups)]

    a = jnp.zeros((c, c), F32)
    for d in range(HG_DIAG):
        if d == 0:
            prod = q * k
        else:
            shift = c - d if rev else d
            prod = q * pltpu.roll(k, shift, axis=0) * jnp.exp2(b - pltpu.roll(b, shift, axis=0))
        a = jnp.where(band[d], prod.sum(axis=-1, keepdims=True), a)
    rows_a = [a[r] for r in rows]

    for half in _hgrn_levels():
        blk = 2 * half
        ref_off = half if rev else half - 1
        zs = []
        owners = []
        for g in range(groups):
            t0 = g * SUBLANES
            if half >= SUBLANES:
                ref = brow(t0 // blk * blk + ref_off)
                attending = (t0 % blk < half) if rev else (t0 % blk >= half)
                x = q[rows[g]] if attending else k[rows[g]]
                e = jnp.exp2(b[rows[g]] - ref if attending else ref - b[rows[g]])
                if attending:
                    owners.append(g)
            else:
                ref = brow(t0 + ref_off)
                for i in range(1, SUBLANES // blk):
                    ref = jnp.where(row8 >= i * blk, brow(t0 + i * blk + ref_off), ref)
                attending = (row8 % blk < half) if rev else (row8 % blk >= half)
                x = jnp.where(attending, q[rows[g]], k[rows[g]])
                e = jnp.exp2(-jnp.abs(b[rows[g]] - ref))
                owners.append(g)
            zs.append(x * e)
        z = jnp.concatenate(zs, axis=0).astype(BF16)
        lv = lax.dot_general(z, z, nt, preferred_element_type=F32)
        for g in owners:
            rows_a[g] = jnp.where(levels[half][g], lv[rows[g]], rows_a[g])
    return jnp.concatenate(rows_a, axis=0)


def _hgrn_kernel(qf_ref, kf_ref, bf_ref, vf_ref, qb_ref, kb_ref, bb_ref, vb_ref,
                 of_ref, ob_ref, s_scr):
    @pl.when(pl.program_id(0) == 0)
    def _():
        s_scr[...] = jnp.zeros(s_scr.shape, F32)

    c = HG_CHUNK
    n_chunks = HG_BLOCK // c
    consts = (_hgrn_consts(False), _hgrn_consts(True))
    nt = (((1,), (1,)), ((), ()))
    tn = (((0,), (0,)), ((), ()))
    dirs = ((qf_ref, kf_ref, bf_ref, vf_ref, of_ref, False),
            (qb_ref, kb_ref, bb_ref, vb_ref, ob_ref, True))

    def head_body(it, carry):
        jobs = []
        for hi in range(HG_ITER_HEADS):
            h = it * HG_ITER_HEADS + hi
            for di, (q_ref, k_ref, b_ref, v_ref, o_ref, rev) in enumerate(dirs):
                order = reversed(range(n_chunks)) if rev else range(n_chunks)
                for ci in order:
                    rows = slice(ci * c, (ci + 1) * c)
                    q = q_ref[h, rows, :].astype(F32)
                    k = k_ref[h, rows, :].astype(F32)
                    b = b_ref[h, rows, :]
                    v = v_ref[h, rows, :]
                    tot = b[0:1] if rev else b[c - 1:c]
                    q_in = (q * jnp.exp2(b)).astype(BF16)
                    k_out = (k * jnp.exp2(tot - b)).astype(BF16)
                    upd = lax.dot_general(v, k_out, tn, preferred_element_type=F32)

                    def brow(r, b_ref=b_ref, base=ci * c, h=h):
                        return b_ref[h, base + r:base + r + 1, :]

                    a = _hgrn_scores(q, k, b, brow, rev, consts[di]).astype(BF16)
                    jobs.append((hi, h, di, o_ref, rows, q_in, jnp.exp2(tot), upd, a, v))
        st = {(hi, di): s_scr[di, it * HG_ITER_HEADS + hi]
              for hi in range(HG_ITER_HEADS) for di in range(2)}
        for hi, h, di, o_ref, rows, q_in, decay, upd, a, v in jobs:
            o = lax.dot_general(q_in, st[hi, di].astype(BF16), nt, preferred_element_type=F32)
            st[hi, di] = decay * st[hi, di] + upd
            o_ref[h, rows, :] = o + jnp.dot(a, v, preferred_element_type=F32)
        for (hi, di), s in st.items():
            s_scr[di, it * HG_ITER_HEADS + hi] = s
        return carry

    lax.fori_loop(0, HG_HEADS // HG_ITER_HEADS, head_body, 0)


def _hgrn2(hq, kf, bf, kb, bb, hv):
    nb = T_ROWS // HG_BLOCK
    fwd = lambda i: (0, i, 0)
    bwd = lambda i: (0, jnp.where(i == 0, 0, nb - i), 0)
    blk = (HG_HEADS, HG_BLOCK, HEAD_DIM)
    out = jax.ShapeDtypeStruct((HG_HEADS, T_ROWS, HEAD_DIM), F32)
    return pl.pallas_call(
        _hgrn_kernel,
        out_shape=(out, out),
        grid=(nb,),
        in_specs=[pl.BlockSpec(blk, fwd)] * 4 + [pl.BlockSpec(blk, bwd)] * 4,
        out_specs=(pl.BlockSpec(blk, fwd), pl.BlockSpec(blk, bwd)),
        scratch_shapes=[pltpu.VMEM((2, HG_HEADS, HEAD_DIM, HEAD_DIM), F32)],
        compiler_params=_cparams(1),
        name="hgrn2_scan",
    )(hq, kf, bf, hv, hq, kb, bb, hv)


def _outproj_kernel(attn_ref, of_ref, ob_ref, gate_ref, hnw_ref, w_ref, x_ref, mod_ref, n2w_ref,
                    xo_ref, h2_ref, mix_scr):
    m = pl.program_id(0)
    bm = x_ref.shape[0]
    acc = jnp.dot(attn_ref[...], w_ref[0:ATTN_WIDTH, :], preferred_element_type=F32)
    for h in range(HG_HEADS):
        y = _rms(of_ref[h] + ob_ref[h], hnw_ref[...])
        mix_scr[:, h * HEAD_DIM:(h + 1) * HEAD_DIM] = (y * gate_ref[h].astype(F32)).astype(BF16)
    acc = acc + jnp.dot(mix_scr[...], w_ref[ATTN_WIDTH:ATTN_WIDTH + HG_WIDTH, :],
                        preferred_element_type=F32)

    def finish(rows, mod_row):
        def mod(k):
            return mod_ref[mod_row:mod_row + 1, k * D_MODEL:(k + 1) * D_MODEL]
        x_new = x_ref[rows, :] + mod(2) * acc[rows]
        xo_ref[rows, :] = x_new
        r = lax.rsqrt(jnp.mean(x_new * x_new, axis=-1, keepdims=True) + EPS)
        h2_ref[rows, :] = (x_new * r * (n2w_ref[...] * (1.0 + mod(4))) + mod(3)).astype(BF16)

    finish(slice(0, bm), 0)

    @pl.when(m == 0)
    def _():
        finish(slice(0, CTX_LEN), 1)


def _output_projection(layer, attn, o_f, o_b, gate, hnw, w_out, xs, mod, n2w):
    bm = OUT_ROW_TILE
    head_spec = pl.BlockSpec((HG_HEADS, bm, HEAD_DIM), lambda m: (0, m, 0))
    row_spec = pl.BlockSpec((bm, D_MODEL), lambda m: (m, 0))
    return pl.pallas_call(
        _outproj_kernel,
        out_shape=(jax.ShapeDtypeStruct((T_ROWS, D_MODEL), F32),
                   jax.ShapeDtypeStruct((T_ROWS, D_MODEL), BF16)),
        grid=(T_ROWS // bm,),
        in_specs=[
            pl.BlockSpec((bm, ATTN_WIDTH), lambda m: (m, 0)),
            head_spec, head_spec, head_spec,
            pl.BlockSpec((1, HEAD_DIM), lambda m: (0, 0)),
            pl.BlockSpec((None, ATTN_WIDTH + HG_WIDTH, D_MODEL), lambda m: (layer, 0, 0)),
            row_spec,
            pl.BlockSpec((None, SUBLANES, 6 * D_MODEL), lambda m: (layer, 0, 0)),
            pl.BlockSpec((1, D_MODEL), lambda m: (0, 0)),
        ],
        out_specs=(row_spec, row_spec),
        scratch_shapes=[pltpu.VMEM((bm, HG_WIDTH), BF16)],
        compiler_params=_cparams(1),
        name="output_projection",
    )(attn, o_f, o_b, gate, hnw, w_out, xs, mod, n2w)


def _ffn_up_kernel(h_ref, hp_ref, hn_ref, wg_ref, wu_ref, cw_ref, cb_ref, u_ref, hcat_scr, g_scr):
    m = pl.program_id(0)
    n = pl.program_id(1)
    bm = h_ref.shape[0]
    hl = CONV_HALO

    @pl.when(n == 0)
    def _():
        hcat_scr[0:hl, :] = hp_ref[...]
        hcat_scr[hl:hl + bm, :] = h_ref[...]
        hcat_scr[hl + bm:hl + bm + hl, :] = hn_ref[...]

    def step(edges):
        g_scr[...] = jnp.dot(hcat_scr[...], wg_ref[...], preferred_element_type=F32)
        up = jnp.dot(h_ref[...], wu_ref[...], preferred_element_type=F32)
        g_prev = g_scr[hl - 1:hl - 1 + bm, :]
        g_next = g_scr[hl + 1:hl + 1 + bm, :]
        if edges:
            r = m * bm + lax.broadcasted_iota(jnp.int32, (bm, 1), 0)
            g_prev = jnp.where((r != 0) & (r != CTX_LEN), g_prev, 0.0)
            g_next = jnp.where((r != CTX_LEN - 1) & (r != T_ROWS - 1), g_next, 0.0)
        gate = (g_prev * cw_ref[0:1, :] + g_scr[hl:hl + bm, :] * cw_ref[1:2, :]
                + g_next * cw_ref[2:3, :] + cb_ref[...])
        u_ref[...] = (_silu(gate) * up).astype(BF16)

    has_edge = (m == 0) | (m == pl.num_programs(0) - 1)
    pl.when(has_edge)(functools.partial(step, True))
    pl.when(jnp.logical_not(has_edge))(functools.partial(step, False))


def _ffn_up(layer, h2, w_up, conv_w, conv_b):
    bm, bn, hl = ROW_TILE, FF_COL_TILE, CONV_HALO
    n_n = D_FF // bn
    per = bm // hl
    last = T_ROWS // hl - 1
    return pl.pallas_call(
        _ffn_up_kernel,
        out_shape=jax.ShapeDtypeStruct((T_ROWS, D_FF), BF16),
        grid=(T_ROWS // bm, n_n),
        in_specs=[
            pl.BlockSpec((bm, D_MODEL), lambda m, n: (m, 0)),
            pl.BlockSpec((hl, D_MODEL), lambda m, n: (jnp.maximum(m * per - 1, 0), 0)),
            pl.BlockSpec((hl, D_MODEL), lambda m, n: (jnp.minimum((m + 1) * per, last), 0)),
            pl.BlockSpec((None, D_MODEL, bn), lambda m, n: (layer, 0, n)),
            pl.BlockSpec((None, D_MODEL, bn), lambda m, n: (layer, 0, n_n + n)),
            pl.BlockSpec((3, bn), lambda m, n: (0, n)),
            pl.BlockSpec((1, bn), lambda m, n: (0, n)),
        ],
        out_specs=pl.BlockSpec((bm, bn), lambda m, n: (m, n)),
        scratch_shapes=[pltpu.VMEM((bm + 2 * hl, D_MODEL), BF16),
                        pltpu.VMEM((bm + 2 * hl, bn), F32)],
        compiler_params=_cparams(2),
        name="ffn_up_conv",
    )(h2, h2, h2, w_up, w_up, conv_w, conv_b)


def _ffn_down_kernel(u_ref, w_ref, x_ref, mod_ref, fw_ref, o_ref, *, final):
    m = pl.program_id(0)
    bm = x_ref.shape[0]

    def gate(mod_row):
        return mod_ref[mod_row:mod_row + 1, 5 * D_MODEL:6 * D_MODEL]

    def run():
        acc = jnp.dot(u_ref[...], w_ref[...], preferred_element_type=F32)
        x_new = x_ref[...] + gate(0) * acc
        o_ref[...] = _rms(x_new, fw_ref[...]) if final else x_new
        return acc

    if final:
        @pl.when(m > 0)
        def _():
            run()
    else:
        acc = run()

        @pl.when(m == 0)
        def _():
            o_ref[0:CTX_LEN, :] = x_ref[0:CTX_LEN, :] + gate(1) * acc[0:CTX_LEN]


def _ffn_down(layer, u, w_down, xs, mod, final_w, final):
    bm = CTX_LEN if final else OUT_ROW_TILE
    row_spec = pl.BlockSpec((bm, D_MODEL), lambda m: (m, 0))
    if final:
        out_rows = SEQ
        out_spec = pl.BlockSpec((bm, D_MODEL), lambda m: (jnp.maximum(m - 1, 0), 0))
    else:
        out_rows, out_spec = T_ROWS, row_spec
    return pl.pallas_call(
        functools.partial(_ffn_down_kernel, final=final),
        out_shape=jax.ShapeDtypeStruct((out_rows, D_MODEL), F32),
        grid=(T_ROWS // bm,),
        in_specs=[
            pl.BlockSpec((bm, D_FF), lambda m: (m, 0)),
            pl.BlockSpec((None, D_FF, D_MODEL), lambda m: (layer, 0, 0),
                         pipeline_mode=pl.Buffered(1)),
            row_spec,
            pl.BlockSpec((None, SUBLANES, 6 * D_MODEL), lambda m: (layer, 0, 0)),
            pl.BlockSpec((1, D_MODEL), lambda m: (0, 0)),
        ],
        out_specs=out_spec,
        compiler_params=_cparams(1),
        name="ffn_down",
    )(u, w_down, xs, mod, final_w)


def _rope_tables():
    f32 = np.float32
    rows = SEQ // GRID_W
    row = np.repeat(np.arange(rows, dtype=f32), GRID_W)
    col = np.tile(np.arange(GRID_W, dtype=f32), rows)
    freqs = np.power(f32(ROPE_THETA), -np.arange(ROPE_FREQS, dtype=f32) / f32(ROPE_FREQS)).astype(f32)
    ar, ac = row[:, None] * freqs, col[:, None] * freqs
    cos = np.concatenate([np.cos(ar), np.cos(ar), np.cos(ac), np.cos(ac)], axis=1)
    sin = np.concatenate([np.sin(ar), np.sin(ar), np.sin(ac), np.sin(ac)], axis=1)
    cos = np.concatenate([np.ones((CTX_LEN, HEAD_DIM), f32), cos], axis=0)
    sin = np.concatenate([np.zeros((CTX_LEN, HEAD_DIM), f32), sin], axis=0)
    return jnp.asarray(cos, F32), jnp.asarray(sin, F32)


def kernel(x, c, ctx, c_ctx, w_mod, b_mod, norm1_w, norm2_w, w_in, q_norm_w, k_norm_w, hg_lb_logits,
           hg_norm_w, w_out, w_up, conv_w, conv_b, w_down, final_norm_w):
    assert x.shape == (1, SEQ, D_MODEL) and ctx.shape == (1, CTX_LEN, D_MODEL)
    xs = jnp.concatenate([ctx[0], x[0]], axis=0)
    mod = _modulation(jnp.stack([c[0], c_ctx], axis=1), w_mod, b_mod)
    cos_t, sin_t = _rope_tables()
    lb_sm = jax.nn.softmax(hg_lb_logits.astype(F32), axis=1)
    lb_all = jnp.cumsum(lb_sm, axis=1) - lb_sm[:, :1]

    w_in, w_out, w_up, w_down = (w.astype(BF16) for w in (w_in, w_out, w_up, w_down))
    for l in range(DEPTH):
        lb = jnp.concatenate([lb_all[0, l], lb_all[1, l]])[None, :]
        q, k, v, hq, kf, bf, kb, bb, hv, gate = _input_projection(
            l, xs, mod, norm1_w[l][None, :], w_in, q_norm_w[l][None, :],
            k_norm_w[l][None, :], lb, cos_t, sin_t)
        attn = _attention(q, k, v)
        o_f, o_b = _hgrn2(hq, kf, bf, kb, bb, hv)
        xs, h2 = _output_projection(l, attn, o_f, o_b, gate, hg_norm_w[l][None, :],
                                    w_out, xs, mod, norm2_w[l][None, :])
        u = _ffn_up(l, h2, w_up, conv_w[l], conv_b[l][None, :])
        xs = _ffn_down(l, u, w_down, xs, mod, final_norm_w[None, :], final=(l == DEPTH - 1))
    return xs[None]
```

```python
import functools

import jax
import jax.numpy as jnp
import numpy as np
from jax import lax
from jax.experimental import pallas as pl
from jax.experimental.pallas import tpu as pltpu

F32 = jnp.float32
BF16 = jnp.bfloat16

D_MODEL = 2048
SEQ = 8192
DEPTH = 4
CTX_LEN = 256
T_ROWS = CTX_LEN + SEQ
GRID_W = 64
HEAD_DIM = 128
ATTN_HEADS = 8
ATTN_KV_HEADS = 2
ATTN_GROUP = 4
ATTN_WIDTH = 1024
KV_WIDTH = 256
HG_HEADS = 8
HG_WIDTH = 1024
IN_WIDTH = 6656
ROPE_THETA = 10000.0
ROPE_FREQS = 32
D_FF = 5632
EPS = 1e-6
Q_SCALE = HEAD_DIM ** -0.5 * 1.4426950408889634

LANES = 128
SUBLANES = 8
VMEM_LIMIT = 56 * 1024 * 1024

MOD_SLAB = 256
ROW_TILE = 768
OUT_ROW_TILE = 384
IN_COL_TILE = 512
FF_COL_TILE = 512
ATTN_Q_TILE = 256
ATTN_K_TILE = 1408
ATTN_QK_PARTS = 2
HG_BLOCK = 256
HG_ITER_HEADS = 4
HG_CHUNK = 64
HG_DIAG = 2
CONV_HALO = 16


def _cparams(n_axes):
    return pltpu.CompilerParams(dimension_semantics=("arbitrary",) * n_axes,
                                vmem_limit_bytes=VMEM_LIMIT)


def _rms(x, w):
    return x * lax.rsqrt(jnp.mean(x * x, axis=-1, keepdims=True) + EPS) * w


def _silu(x):
    h = 0.5 * x
    return h * jnp.tanh(h) + h


def _mod_kernel(s_ref, w_ref, b_ref, o_ref, acc_scr):
    k = pl.program_id(1)
    n_out = w_ref.shape[2]
    cw = 4 * LANES

    @pl.when(k == 0)
    def _():
        acc_scr[...] = jnp.zeros(acc_scr.shape, F32)

    r0 = pl.multiple_of(k * MOD_SLAB, MOD_SLAB)
    s = _silu(s_ref[pl.ds(r0, MOD_SLAB), :])
    sb = [jnp.tile(jnp.broadcast_to(s[:, i:i + 1], (MOD_SLAB, LANES)), (1, cw // LANES))
          for i in range(2)]
    for c0 in range(0, n_out, cw):
        w = w_ref[0, :, c0:c0 + cw]
        for i in range(2):
            part = (w * sb[i]).reshape(MOD_SLAB // SUBLANES, SUBLANES, cw).sum(axis=0)
            acc_scr[i, :, c0:c0 + cw] += part

    @pl.when(k == pl.num_programs(1) - 1)
    def _():
        bias = b_ref[0]
        rows = [acc_scr[i].sum(axis=0, keepdims=True) + bias for i in range(2)]
        o_ref[0] = jnp.concatenate(rows + [jnp.zeros((SUBLANES - 2, n_out), F32)], axis=0)


def _modulation(s_in, w_mod, b_mod):
    n_out = 6 * D_MODEL
    return pl.pallas_call(
        _mod_kernel,
        out_shape=jax.ShapeDtypeStruct((DEPTH, SUBLANES, n_out), F32),
        grid=(DEPTH, D_MODEL // MOD_SLAB),
        in_specs=[
            pl.BlockSpec((D_MODEL, 2), lambda l, k: (0, 0)),
            pl.BlockSpec((1, MOD_SLAB, n_out), lambda l, k: (l, k, 0)),
            pl.BlockSpec((1, 1, n_out), lambda l, k: (l, 0, 0)),
        ],
        out_specs=pl.BlockSpec((1, SUBLANES, n_out), lambda l, k: (l, 0, 0)),
        scratch_shapes=[pltpu.VMEM((2, SUBLANES, n_out), F32)],
        compiler_params=_cparams(2),
        name="adaln_modulation",
    )(s_in, w_mod, b_mod.reshape(DEPTH, 1, n_out))


def _norm_rope_matrix():
    i = lax.broadcasted_iota(jnp.int32, (2 * HEAD_DIM, 2 * HEAD_DIM), 0)
    j = lax.broadcasted_iota(jnp.int32, (2 * HEAD_DIM, 2 * HEAD_DIM), 1)
    ones = (i < HEAD_DIM) & (j < HEAD_DIM)
    ii, jj = i - HEAD_DIM, j - HEAD_DIM
    low = (jj // ROPE_FREQS) % 2 == 0
    rot = jnp.where(low & (ii == jj + ROPE_FREQS), -1.0,
                    jnp.where(~low & (ii == jj - ROPE_FREQS), 1.0, 0.0))
    rot = jnp.where((i >= HEAD_DIM) & (j >= HEAD_DIM), rot, 0.0)
    return jnp.where(ones, 1.0, rot).astype(BF16)


def _norm_rope(p, w, mat_ref, cos, sin, scale):
    pw = p * w
    lhs = jnp.concatenate([(p * p).astype(BF16), pw.astype(BF16)], axis=1)
    red = jnp.dot(lhs, mat_ref[...], preferred_element_type=F32)
    r = lax.rsqrt(red[:, 0:HEAD_DIM] * (1.0 / HEAD_DIM) + EPS) * scale
    return r * (pw * cos + red[:, HEAD_DIM:2 * HEAD_DIM] * sin)


def _inproj_kernel(x_ref, mod_ref, n1w_ref, w_ref, qnw_ref, knw_ref, lb_ref, cos_ref, sin_ref, mat_ref,
                   q_ref, k_ref, v_ref, hq_ref, kf_ref, bf_ref, kb_ref, bb_ref, hv_ref, gate_ref,
                   h_scr):
    m = pl.program_id(0)
    n = pl.program_id(1)
    bm = x_ref.shape[0]

    def modulated(rows, mod_row):
        x = x_ref[rows, :]
        r = lax.rsqrt(jnp.mean(x * x, axis=-1, keepdims=True) + EPS)
        gain = n1w_ref[...] * (1.0 + mod_ref[mod_row:mod_row + 1, D_MODEL:2 * D_MODEL])
        return (x * r * gain + mod_ref[mod_row:mod_row + 1, 0:D_MODEL]).astype(BF16)

    @pl.when(n == 0)
    def _():
        h_scr[...] = modulated(slice(None), 0)

    @pl.when((n == 0) & (m == 0))
    def _():
        h_scr[0:CTX_LEN, :] = modulated(slice(0, CTX_LEN), 1)

    def project(epilogue):
        ch = IN_COL_TILE // 2
        hpt = ch // HEAD_DIM
        rh = bm // 2
        ps = []
        for ri in range(2):
            rows = slice(ri * rh, (ri + 1) * rh)
            h = h_scr[rows, :]
            ps.append((rows, [jnp.dot(h, w_ref[:, a * ch:(a + 1) * ch], preferred_element_type=F32)
                              for a in range(2)]))
        for rows, halves in ps:
            for a, p in enumerate(halves):
                for j in range(hpt):
                    epilogue(a * hpt + j, rows, p[:, j * HEAD_DIM:(j + 1) * HEAD_DIM])

    def q_out(j, rows, p):
        q_ref[j, rows, :] = _norm_rope(p, qnw_ref[...], mat_ref, cos_ref[rows, :], sin_ref[rows, :],
                                       Q_SCALE).astype(BF16)

    def kv_out(j, rows, p):
        if j < ATTN_KV_HEADS:
            k_ref[j, rows, :] = _norm_rope(p, knw_ref[...], mat_ref, cos_ref[rows, :],
                                           sin_ref[rows, :], 1.0).astype(BF16)
        else:
            v_ref[j - ATTN_KV_HEADS, rows, :] = p.astype(BF16)

    def silu_out(ref):
        def out(j, rows, p):
            ref[j, rows, :] = _silu(p).astype(BF16)
        return out

    def forget_out(k_out, b_out, rev):
        def out(j, rows, p):
            lb = lb_ref[:, j * HEAD_DIM:(j + 1) * HEAD_DIM]
            fg = lb + (1.0 - lb) * jax.nn.sigmoid(p)
            k_out[j, rows, :] = (1.0 - fg).astype(BF16)
            b_out[j, rows, :] = _chunk_cumsum(jnp.log2(fg), rev)
        return out

    def plain_out(j, rows, p):
        hv_ref[j, rows, :] = p.astype(BF16)

    pl.when(n < 2)(lambda: project(q_out))
    pl.when(n == 2)(lambda: project(kv_out))
    pl.when((n == 3) | (n == 4))(lambda: project(silu_out(hq_ref)))
    pl.when((n == 5) | (n == 6))(lambda: project(forget_out(kf_ref, bf_ref, False)))
    pl.when((n == 7) | (n == 8))(lambda: project(forget_out(kb_ref, bb_ref, True)))
    pl.when((n == 9) | (n == 10))(lambda: project(plain_out))
    pl.when(n >= 11)(lambda: project(silu_out(gate_ref)))


def _input_projection(layer, xs, mod, n1w, w_in, qnw, knw, lb, cos_t, sin_t):
    bm, bn = ROW_TILE, IN_COL_TILE
    n_m, n_n = T_ROWS // bm, IN_WIDTH // bn
    hpt = bn // HEAD_DIM

    def head_spec(first_tile):
        return pl.BlockSpec((hpt, bm, HEAD_DIM),
                            lambda m, n: (jnp.clip(n - first_tile, 0, 1), m, 0))

    def head_shape(dtype):
        return jax.ShapeDtypeStruct((HG_HEADS, T_ROWS, HEAD_DIM), dtype)

    kv_spec = pl.BlockSpec((ATTN_KV_HEADS, bm, HEAD_DIM), lambda m, n: (0, m, 0))
    kv_shape = jax.ShapeDtypeStruct((ATTN_KV_HEADS, T_ROWS, HEAD_DIM), BF16)
    return pl.pallas_call(
        _inproj_kernel,
        out_shape=(head_shape(BF16), kv_shape, kv_shape, head_shape(BF16),
                   head_shape(BF16), head_shape(F32), head_shape(BF16), head_shape(F32),
                   head_shape(BF16), head_shape(BF16)),
        grid=(n_m, n_n),
        in_specs=[
            pl.BlockSpec((bm, D_MODEL), lambda m, n: (m, 0)),
            pl.BlockSpec((None, SUBLANES, 6 * D_MODEL), lambda m, n: (layer, 0, 0)),
            pl.BlockSpec((1, D_MODEL), lambda m, n: (0, 0)),
            pl.BlockSpec((None, D_MODEL, bn), lambda m, n: (layer, 0, n)),
            pl.BlockSpec((1, HEAD_DIM), lambda m, n: (0, 0)),
            pl.BlockSpec((1, HEAD_DIM), lambda m, n: (0, 0)),
            pl.BlockSpec((1, bn), lambda m, n: (0, jnp.clip(n - 5, 0, 3))),
            pl.BlockSpec((bm, HEAD_DIM), lambda m, n: (m, 0)),
            pl.BlockSpec((bm, HEAD_DIM), lambda m, n: (m, 0)),
            pl.BlockSpec((2 * HEAD_DIM, 2 * HEAD_DIM), lambda m, n: (0, 0)),
        ],
        out_specs=(head_spec(0), kv_spec, kv_spec, head_spec(3),
                   head_spec(5), head_spec(5), head_spec(7), head_spec(7),
                   head_spec(9), head_spec(11)),
        scratch_shapes=[pltpu.VMEM((bm, D_MODEL), BF16)],
        compiler_params=_cparams(2),
        name="input_projection",
    )(xs, mod, n1w, w_in, qnw, knw, lb, cos_t, sin_t, _norm_rope_matrix())


def _attn_kernel(q_ref, k_ref, vt_ref, o_ref, acc_scr, st_scr):
    i = pl.program_id(1)
    bq = q_ref.shape[1]
    bk = ATTN_K_TILE
    n_chunks = T_ROWS // bk

    def scores(kc, r, slot):
        part = kc.shape[0] // ATTN_QK_PARTS
        mx = None
        for a in range(ATTN_QK_PARTS):
            st = lax.dot_general(kc[a * part:(a + 1) * part], q_ref[r], (((1,), (1,)), ((), ())),
                                 preferred_element_type=F32)
            st_scr[slot, a * part:(a + 1) * part, :] = st
            m_part = st.max(axis=0, keepdims=True)
            mx = m_part if mx is None else jnp.maximum(mx, m_part)
        return mx

    def update(vtc, r, slot, m_cur, m, l):
        width = vtc.shape[1]
        m_new = jnp.maximum(m, m_cur)
        alpha = jnp.exp2(m - m_new)
        p = jnp.exp2(st_scr[slot, 0:width, :] - m_new)
        l = alpha * l + p.sum(axis=0, keepdims=True)
        acc_scr[r] = alpha * acc_scr[r] + jnp.dot(vtc, p.astype(BF16), preferred_element_type=F32)
        return m_new, l

    def finish(l):
        for r in range(ATTN_GROUP):
            o_ref[:, r * HEAD_DIM:(r + 1) * HEAD_DIM] = (acc_scr[r] / l[r]).T.astype(BF16)

    acc_scr[...] = jnp.zeros(acc_scr.shape, F32)
    m0 = tuple(jnp.full((1, bq), -jnp.inf, F32) for _ in range(ATTN_GROUP))
    l0 = tuple(jnp.zeros((1, bq), F32) for _ in range(ATTN_GROUP))

    @pl.when(i == 0)
    def _():
        kc = k_ref[0, 0:CTX_LEN, :]
        vtc = vt_ref[0, 0, :, 0:CTX_LEN]
        l = []
        mc = scores(kc, 0, 0)
        for r in range(ATTN_GROUP):
            mc_next = scores(kc, r + 1, (r + 1) % 2) if r + 1 < ATTN_GROUP else None
            l.append(update(vtc, r, r % 2, mc, m0[r], l0[r])[1])
            mc = mc_next
        finish(l)

    @pl.when(i > 0)
    def _():
        def chunk(j):
            return k_ref[0, pl.ds(pl.multiple_of(j * bk, bk), bk), :]

        def body(j, carry):
            mc, m, l = carry
            m, l = list(m), list(l)
            vtc = vt_ref[0, j]
            for r in range(ATTN_GROUP):
                if r + 1 < ATTN_GROUP:
                    mc_next = scores(chunk(j), r + 1, (r + 1) % 2)
                else:
                    mc_next = scores(chunk(jnp.minimum(j + 1, n_chunks - 1)), 0, 0)
                m[r], l[r] = update(vtc, r, r % 2, mc, m[r], l[r])
                mc = mc_next
            return mc, tuple(m), tuple(l)

        _, _, l = lax.fori_loop(0, n_chunks, body, (scores(chunk(0), 0, 0), m0, l0), unroll=True)
        finish(l)


def _attention(q, k, v):
    bq, bk = ATTN_Q_TILE, ATTN_K_TILE
    n_chunks = T_ROWS // bk
    vt = v.reshape(ATTN_KV_HEADS, n_chunks, bk, HEAD_DIM).transpose(0, 1, 3, 2)
    return pl.pallas_call(
        _attn_kernel,
        out_shape=jax.ShapeDtypeStruct((T_ROWS, ATTN_WIDTH), BF16),
        grid=(ATTN_KV_HEADS, T_ROWS // bq),
        in_specs=[
            pl.BlockSpec((ATTN_GROUP, bq, HEAD_DIM), lambda g, i: (g, i, 0)),
            pl.BlockSpec((1, T_ROWS, HEAD_DIM), lambda g, i: (g, 0, 0)),
            pl.BlockSpec((1, n_chunks, HEAD_DIM, bk), lambda g, i: (g, 0, 0, 0)),
        ],
        out_specs=pl.BlockSpec((bq, ATTN_GROUP * HEAD_DIM), lambda g, i: (i, g)),
        scratch_shapes=[pltpu.VMEM((ATTN_GROUP, HEAD_DIM, bq), F32),
                        pltpu.VMEM((2, bk, bq), F32)],
        compiler_params=_cparams(2),
        name="gqa_attention",
    )(q, k, vt)


def _chunk_cumsum(g, rev):
    n_groups = g.shape[0] // SUBLANES
    x = g.reshape(n_groups, SUBLANES, g.shape[1])
    sub = lax.broadcasted_iota(jnp.int32, x.shape, 1)
    step = 1
    while step < SUBLANES:
        if rev:
            x = x + jnp.where(sub < SUBLANES - step, pltpu.roll(x, SUBLANES - step, axis=1), 0.0)
        else:
            x = x + jnp.where(sub >= step, pltpu.roll(x, step, axis=1), 0.0)
        step *= 2
    per = HG_CHUNK // SUBLANES
    xs = [x[v] for v in range(n_groups)]
    for c0 in range(0, n_groups, per):
        if rev:
            for v in range(c0 + per - 2, c0 - 1, -1):
                xs[v] = xs[v] + xs[v + 1][0:1]
        else:
            for v in range(c0 + 1, c0 + per):
                xs[v] = xs[v] + xs[v - 1][SUBLANES - 1:SUBLANES]
    return jnp.concatenate(xs, axis=0)


def _hgrn_levels():
    halves = []
    half = HG_CHUNK // 2
    while half >= HG_DIAG:
        halves.append(half)
        half //= 2
    return halves


def _hgrn_consts(rev):
    c = HG_CHUNK
    levels = {}
    for half in _hgrn_levels():
        blk = 2 * half
        levels[half] = []
        for g in range(c // SUBLANES):
            t = lax.broadcasted_iota(jnp.int32, (SUBLANES, c), 0) + g * SUBLANES
            s = lax.broadcasted_iota(jnp.int32, (SUBLANES, c), 1)
            same = t // blk == s // blk
            if rev:
                levels[half].append(same & (t % blk < half) & (s % blk >= half))
            else:
                levels[half].append(same & (t % blk >= half) & (s % blk < half))
    t = lax.broadcasted_iota(jnp.int32, (c, c), 0)
    s = lax.broadcasted_iota(jnp.int32, (c, c), 1)
    band = []
    for d in range(HG_DIAG):
        if rev:
            band.append((s == t + d) & (t % HG_DIAG + d < HG_DIAG))
        else:
            band.append((s == t - d) & (t % HG_DIAG >= d))
    return levels, band


def _hgrn_scores(q, k, b, brow, rev, consts):
    levels, band = consts
    c = HG_CHUNK
    nt = (((1,), (1,)), ((), ()))
    groups = c // SUBLANES
    row8 = lax.broadcasted_iota(jnp.int32, (SUBLANES, HEAD_DIM), 0)
    rows = [slice(g * SUBLANES, (g + 1) * SUBLANES) for g in range(groups)]

    a = jnp.zeros((c, c), F32)
    for d in range(HG_DIAG):
        if d == 0:
            prod = q * k
        else:
            shift = c - d if rev else d
            prod = q * pltpu.roll(k, shift, axis=0) * jnp.exp2(b - pltpu.roll(b, shift, axis=0))
        a = jnp.where(band[d], prod.sum(axis=-1, keepdims=True), a)
    rows_a = [a[r] for r in rows]

    for half in _hgrn_levels():
        blk = 2 * half
        ref_off = half if rev else half - 1
        zs = []
        owners = []
        for g in range(groups):
            t0 = g * SUBLANES
            if half >= SUBLANES:
                ref = brow(t0 // blk * blk + ref_off)
                attending = (t0 % blk < half) if rev else (t0 % blk >= half)
                x = q[rows[g]] if attending else k[rows[g]]
                e = jnp.exp2(b[rows[g]] - ref if attending else ref - b[rows[g]])
                if attending:
                    owners.append(g)
            else:
                ref = brow(t0 + ref_off)
                for i in range(1, SUBLANES // blk):
                    ref = jnp.where(row8 >= i * blk, brow(t0 + i * blk + ref_off), ref)
                attending = (row8 % blk < half) if rev else (row8 % blk >= half)
                x = jnp.where(attending, q[rows[g]], k[rows[g]])
                e = jnp.exp2(-jnp.abs(b[rows[g]] - ref))
                owners.append(g)
            zs.append(x * e)
        z = jnp.concatenate(zs, axis=0).astype(BF16)
        lv = lax.dot_general(z, z, nt, preferred_element_type=F32)
        for g in owners:
            rows_a[g] = jnp.where(levels[half][g], lv[rows[g]], rows_a[g])
    return jnp.concatenate(rows_a, axis=0)


def _hgrn_kernel(qf_ref, kf_ref, bf_ref, vf_ref, qb_ref, kb_ref, bb_ref, vb_ref,
                 of_ref, ob_ref, s_scr):
    @pl.when(pl.program_id(0) == 0)
    def _():
        s_scr[...] = jnp.zeros(s_scr.shape, F32)

    c = HG_CHUNK
    n_chunks = HG_BLOCK // c
    consts = (_hgrn_consts(False), _hgrn_consts(True))
    nt = (((1,), (1,)), ((), ()))
    tn = (((0,), (0,)), ((), ()))
    dirs = ((qf_ref, kf_ref, bf_ref, vf_ref, of_ref, False),
            (qb_ref, kb_ref, bb_ref, vb_ref, ob_ref, True))

    def head_body(it, carry):
        jobs = []
        for hi in range(HG_ITER_HEADS):
            h = it * HG_ITER_HEADS + hi
            for di, (q_ref, k_ref, b_ref, v_ref, o_ref, rev) in enumerate(dirs):
                order = reversed(range(n_chunks)) if rev else range(n_chunks)
                for ci in order:
                    rows = slice(ci * c, (ci + 1) * c)
                    q = q_ref[h, rows, :].astype(F32)
                    k = k_ref[h, rows, :].astype(F32)
                    b = b_ref[h, rows, :]
                    v = v_ref[h, rows, :]
                    tot = b[0:1] if rev else b[c - 1:c]
                    q_in = (q * jnp.exp2(b)).astype(BF16)
                    k_out = (k * jnp.exp2(tot - b)).astype(BF16)
                    upd = lax.dot_general(v, k_out, tn, preferred_element_type=F32)

                    def brow(r, b_ref=b_ref, base=ci * c, h=h):
                        return b_ref[h, base + r:base + r + 1, :]

                    a = _hgrn_scores(q, k, b, brow, rev, consts[di]).astype(BF16)
                    jobs.append((hi, h, di, o_ref, rows, q_in, jnp.exp2(tot), upd, a, v))
        st = {(hi, di): s_scr[di, it * HG_ITER_HEADS + hi]
              for hi in range(HG_ITER_HEADS) for di in range(2)}
        for hi, h, di, o_ref, rows, q_in, decay, upd, a, v in jobs:
            o = lax.dot_general(q_in, st[hi, di].astype(BF16), nt, preferred_element_type=F32)
            st[hi, di] = decay * st[hi, di] + upd
            o_ref[h, rows, :] = o + jnp.dot(a, v, preferred_element_type=F32)
        for (hi, di), s in st.items():
            s_scr[di, it * HG_ITER_HEADS + hi] = s
        return carry

    lax.fori_loop(0, HG_HEADS // HG_ITER_HEADS, head_body, 0)


def _hgrn2(hq, kf, bf, kb, bb, hv):
    nb = T_ROWS // HG_BLOCK
    fwd = lambda i: (0, i, 0)
    bwd = lambda i: (0, jnp.where(i == 0, 0, nb - i), 0)
    blk = (HG_HEADS, HG_BLOCK, HEAD_DIM)
    out = jax.ShapeDtypeStruct((HG_HEADS, T_ROWS, HEAD_DIM), F32)
    return pl.pallas_call(
        _hgrn_kernel,
        out_shape=(out, out),
        grid=(nb,),
        in_specs=[pl.BlockSpec(blk, fwd)] * 4 + [pl.BlockSpec(blk, bwd)] * 4,
        out_specs=(pl.BlockSpec(blk, fwd), pl.BlockSpec(blk, bwd)),
        scratch_shapes=[pltpu.VMEM((2, HG_HEADS, HEAD_DIM, HEAD_DIM), F32)],
        compiler_params=_cparams(1),
        name="hgrn2_scan",
    )(hq, kf, bf, hv, hq, kb, bb, hv)


def _outproj_kernel(attn_ref, of_ref, ob_ref, gate_ref, hnw_ref, w_ref, x_ref, mod_ref, n2w_ref,
                    xo_ref, h2_ref, mix_scr):
    m = pl.program_id(0)
    bm = x_ref.shape[0]
    acc = jnp.dot(attn_ref[...], w_ref[0:ATTN_WIDTH, :], preferred_element_type=F32)
    for h in range(HG_HEADS):
        y = _rms(of_ref[h] + ob_ref[h], hnw_ref[...])
        mix_scr[:, h * HEAD_DIM:(h + 1) * HEAD_DIM] = (y * gate_ref[h].astype(F32)).astype(BF16)
    acc = acc + jnp.dot(mix_scr[...], w_ref[ATTN_WIDTH:ATTN_WIDTH + HG_WIDTH, :],
                        preferred_element_type=F32)

    def finish(rows, mod_row):
        def mod(k):
            return mod_ref[mod_row:mod_row + 1, k * D_MODEL:(k + 1) * D_MODEL]
        x_new = x_ref[rows, :] + mod(2) * acc[rows]
        xo_ref[rows, :] = x_new
        r = lax.rsqrt(jnp.mean(x_new * x_new, axis=-1, keepdims=True) + EPS)
        h2_ref[rows, :] = (x_new * r * (n2w_ref[...] * (1.0 + mod(4))) + mod(3)).astype(BF16)

    finish(slice(0, bm), 0)

    @pl.when(m == 0)
    def _():
        finish(slice(0, CTX_LEN), 1)


def _output_projection(layer, attn, o_f, o_b, gate, hnw, w_out, xs, mod, n2w):
    bm = OUT_ROW_TILE
    head_spec = pl.BlockSpec((HG_HEADS, bm, HEAD_DIM), lambda m: (0, m, 0))
    row_spec = pl.BlockSpec((bm, D_MODEL), lambda m: (m, 0))
    return pl.pallas_call(
        _outproj_kernel,
        out_shape=(jax.ShapeDtypeStruct((T_ROWS, D_MODEL), F32),
                   jax.ShapeDtypeStruct((T_ROWS, D_MODEL), BF16)),
        grid=(T_ROWS // bm,),
        in_specs=[
            pl.BlockSpec((bm, ATTN_WIDTH), lambda m: (m, 0)),
            head_spec, head_spec, head_spec,
            pl.BlockSpec((1, HEAD_DIM), lambda m: (0, 0)),
            pl.BlockSpec((None, ATTN_WIDTH + HG_WIDTH, D_MODEL), lambda m: (layer, 0, 0)),
            row_spec,
            pl.BlockSpec((None, SUBLANES, 6 * D_MODEL), lambda m: (layer, 0, 0)),
            pl.BlockSpec((1, D_MODEL), lambda m: (0, 0)),
        ],
        out_specs=(row_spec, row_spec),
        scratch_shapes=[pltpu.VMEM((bm, HG_WIDTH), BF16)],
        compiler_params=_cparams(1),
        name="output_projection",
    )(attn, o_f, o_b, gate, hnw, w_out, xs, mod, n2w)


def _ffn_up_kernel(h_ref, hp_ref, hn_ref, wg_ref, wu_ref, cw_ref, cb_ref, u_ref, hcat_scr, g_scr):
    m = pl.program_id(0)
    n = pl.program_id(1)
    bm = h_ref.shape[0]
    hl = CONV_HALO

    @pl.when(n == 0)
    def _():
        hcat_scr[0:hl, :] = hp_ref[...]
        hcat_scr[hl:hl + bm, :] = h_ref[...]
        hcat_scr[hl + bm:hl + bm + hl, :] = hn_ref[...]

    g_scr[...] = jnp.dot(hcat_scr[...], wg_ref[...], preferred_element_type=F32)
    up = jnp.dot(h_ref[...], wu_ref[...], preferred_element_type=F32)
    r = m * bm + lax.broadcasted_iota(jnp.int32, (bm, 1), 0)
    has_prev = (r != 0) & (r != CTX_LEN)
    has_next = (r != CTX_LEN - 1) & (r != T_ROWS - 1)
    g_prev = jnp.where(has_prev, g_scr[hl - 1:hl - 1 + bm, :], 0.0)
    g_next = jnp.where(has_next, g_scr[hl + 1:hl + 1 + bm, :], 0.0)
    gate = (g_prev * cw_ref[0:1, :] + g_scr[hl:hl + bm, :] * cw_ref[1:2, :]
            + g_next * cw_ref[2:3, :] + cb_ref[...])
    u_ref[...] = (_silu(gate) * up).astype(BF16)


def _ffn_up(layer, h2, w_up, conv_w, conv_b):
    bm, bn, hl = ROW_TILE, FF_COL_TILE, CONV_HALO
    n_n = D_FF // bn
    per = bm // hl
    last = T_ROWS // hl - 1
    return pl.pallas_call(
        _ffn_up_kernel,
        out_shape=jax.ShapeDtypeStruct((T_ROWS, D_FF), BF16),
        grid=(T_ROWS // bm, n_n),
        in_specs=[
            pl.BlockSpec((bm, D_MODEL), lambda m, n: (m, 0)),
            pl.BlockSpec((hl, D_MODEL), lambda m, n: (jnp.maximum(m * per - 1, 0), 0)),
            pl.BlockSpec((hl, D_MODEL), lambda m, n: (jnp.minimum((m + 1) * per, last), 0)),
            pl.BlockSpec((None, D_MODEL, bn), lambda m, n: (layer, 0, n)),
            pl.BlockSpec((None, D_MODEL, bn), lambda m, n: (layer, 0, n_n + n)),
            pl.BlockSpec((3, bn), lambda m, n: (0, n)),
            pl.BlockSpec((1, bn), lambda m, n: (0, n)),
        ],
        out_specs=pl.BlockSpec((bm, bn), lambda m, n: (m, n)),
        scratch_shapes=[pltpu.VMEM((bm + 2 * hl, D_MODEL), BF16),
                        pltpu.VMEM((bm + 2 * hl, bn), F32)],
        compiler_params=_cparams(2),
        name="ffn_up_conv",
    )(h2, h2, h2, w_up, w_up, conv_w, conv_b)


def _ffn_down_kernel(u_ref, w_ref, x_ref, mod_ref, fw_ref, o_ref, *, final):
    m = pl.program_id(0)

    def gate(mod_row):
        return mod_ref[mod_row:mod_row + 1, 5 * D_MODEL:6 * D_MODEL]

    def run():
        acc = jnp.dot(u_ref[...], w_ref[...], preferred_element_type=F32)
        x_new = x_ref[...] + gate(0) * acc
        o_ref[...] = _rms(x_new, fw_ref[...]) if final else x_new
        return acc

    if final:
        @pl.when(m > 0)
        def _():
            run()
    else:
        acc = run()

        @pl.when(m == 0)
        def _():
            o_ref[0:CTX_LEN, :] = x_ref[0:CTX_LEN, :] + gate(1) * acc[0:CTX_LEN]


def _ffn_down(layer, u, w_down, xs, mod, final_w, final):
    bm = CTX_LEN if final else OUT_ROW_TILE
    row_spec = pl.BlockSpec((bm, D_MODEL), lambda m: (m, 0))
    if final:
        out_rows = SEQ
        out_spec = pl.BlockSpec((bm, D_MODEL), lambda m: (jnp.maximum(m - 1, 0), 0))
    else:
        out_rows, out_spec = T_ROWS, row_spec
    return pl.pallas_call(
        functools.partial(_ffn_down_kernel, final=final),
        out_shape=jax.ShapeDtypeStruct((out_rows, D_MODEL), F32),
        grid=(T_ROWS // bm,),
        in_specs=[
            pl.BlockSpec((bm, D_FF), lambda m: (m, 0)),
            pl.BlockSpec((None, D_FF, D_MODEL), lambda m: (layer, 0, 0),
                         pipeline_mode=pl.Buffered(1)),
            row_spec,
            pl.BlockSpec((None, SUBLANES, 6 * D_MODEL), lambda m: (layer, 0, 0)),
            pl.BlockSpec((1, D_MODEL), lambda m: (0, 0)),
        ],
        out_specs=out_spec,
        compiler_params=_cparams(1),
        name="ffn_down",
    )(u, w_down, xs, mod, final_w)


def _rope_tables():
    f32 = np.float32
    rows = SEQ // GRID_W
    row = np.repeat(np.arange(rows, dtype=f32), GRID_W)
    col = np.tile(np.arange(GRID_W, dtype=f32), rows)
    freqs = np.power(f32(ROPE_THETA), -np.arange(ROPE_FREQS, dtype=f32) / f32(ROPE_FREQS)).astype(f32)
    ar, ac = row[:, None] * freqs, col[:, None] * freqs
    cos = np.concatenate([np.cos(ar), np.cos(ar), np.cos(ac), np.cos(ac)], axis=1)
    sin = np.concatenate([np.sin(ar), np.sin(ar), np.sin(ac), np.sin(ac)], axis=1)
    cos = np.concatenate([np.ones((CTX_LEN, HEAD_DIM), f32), cos], axis=0)
    sin = np.concatenate([np.zeros((CTX_LEN, HEAD_DIM), f32), sin], axis=0)
    return jnp.asarray(cos, F32), jnp.asarray(sin, F32)


def kernel(x, c, ctx, c_ctx, w_mod, b_mod, norm1_w, norm2_w, w_in, q_norm_w, k_norm_w, hg_lb_logits,
           hg_norm_w, w_out, w_up, conv_w, conv_b, w_down, final_norm_w):
    assert x.shape == (1, SEQ, D_MODEL) and ctx.shape == (1, CTX_LEN, D_MODEL)
    xs = jnp.concatenate([ctx[0], x[0]], axis=0)
    mod = _modulation(jnp.stack([c[0], c_ctx], axis=1), w_mod, b_mod)
    cos_t, sin_t = _rope_tables()
    lb_sm = jax.nn.softmax(hg_lb_logits.astype(F32), axis=1)
    lb_all = jnp.cumsum(lb_sm, axis=1) - lb_sm[:, :1]

    w_in, w_out, w_up, w_down = (w.astype(BF16) for w in (w_in, w_out, w_up, w_down))
    for l in range(DEPTH):
        lb = jnp.concatenate([lb_all[0, l], lb_all[1, l]])[None, :]
        q, k, v, hq, kf, bf, kb, bb, hv, gate = _input_projection(
            l, xs, mod, norm1_w[l][None, :], w_in, q_norm_w[l][None, :],
            k_norm_w[l][None, :], lb, cos_t, sin_t)
        attn = _attention(q, k, v)
        o_f, o_b = _hgrn2(hq, kf, bf, kb, bb, hv)
        xs, h2 = _output_projection(l, attn, o_f, o_b, gate, hg_norm_w[l][None, :],
                                    w_out, xs, mod, norm2_w[l][None, :])
        u = _ffn_up(l, h2, w_up, conv_w[l], conv_b[l][None, :])
        xs = _ffn_down(l, u, w_down, xs, mod, final_norm_w[None, :], final=(l == DEPTH - 1))
    return xs[None]
```

```python
import functools

import jax
import jax.numpy as jnp
import numpy as np
from jax import lax
from jax.experimental import pallas as pl
from jax.experimental.pallas import tpu as pltpu

F32 = jnp.float32
BF16 = jnp.bfloat16

D_MODEL = 2048
SEQ = 8192
DEPTH = 4
CTX_LEN = 256
T_ROWS = CTX_LEN + SEQ
GRID_W = 64
HEAD_DIM = 128
ATTN_HEADS = 8
ATTN_KV_HEADS = 2
ATTN_GROUP = 4
ATTN_WIDTH = 1024
KV_WIDTH = 256
HG_HEADS = 8
HG_WIDTH = 1024
IN_WIDTH = 6656
ROPE_THETA = 10000.0
ROPE_FREQS = 32
D_FF = 5632
EPS = 1e-6
Q_SCALE = HEAD_DIM ** -0.5 * 1.4426950408889634

LANES = 128
SUBLANES = 8
VMEM_LIMIT = 56 * 1024 * 1024

MOD_SLAB = 256
ROW_TILE = 768
OUT_ROW_TILE = 384
IN_COL_TILE = 512
FF_COL_TILE = 512
ATTN_Q_TILE = 256
ATTN_K_TILE = 1408
ATTN_QK_PARTS = 2
HG_BLOCK = 256
HG_ITER_HEADS = 4
HG_CHUNK = 64
HG_DIAG = 2
CONV_HALO = 16


def _cparams(n_axes):
    return pltpu.CompilerParams(dimension_semantics=("arbitrary",) * n_axes,
                                vmem_limit_bytes=VMEM_LIMIT)


def _rms(x, w):
    return x * lax.rsqrt(jnp.mean(x * x, axis=-1, keepdims=True) + EPS) * w


def _silu(x):
    h = 0.5 * x
    return h * jnp.tanh(h) + h


def _mod_kernel(s_ref, w_ref, b_ref, o_ref, acc_scr):
    k = pl.program_id(1)
    n_out = w_ref.shape[2]
    cw = 4 * LANES

    @pl.when(k == 0)
    def _():
        acc_scr[...] = jnp.zeros(acc_scr.shape, F32)

    r0 = pl.multiple_of(k * MOD_SLAB, MOD_SLAB)
    s = _silu(s_ref[pl.ds(r0, MOD_SLAB), :])
    sb = [jnp.tile(jnp.broadcast_to(s[:, i:i + 1], (MOD_SLAB, LANES)), (1, cw // LANES))
          for i in range(2)]
    for c0 in range(0, n_out, cw):
        w = w_ref[0, :, c0:c0 + cw]
        for i in range(2):
            part = (w * sb[i]).reshape(MOD_SLAB // SUBLANES, SUBLANES, cw).sum(axis=0)
            acc_scr[i, :, c0:c0 + cw] += part

    @pl.when(k == pl.num_programs(1) - 1)
    def _():
        bias = b_ref[0]
        rows = [acc_scr[i].sum(axis=0, keepdims=True) + bias for i in range(2)]
        o_ref[0] = jnp.concatenate(rows + [jnp.zeros((SUBLANES - 2, n_out), F32)], axis=0)


def _modulation(s_in, w_mod, b_mod):
    n_out = 6 * D_MODEL
    return pl.pallas_call(
        _mod_kernel,
        out_shape=jax.ShapeDtypeStruct((DEPTH, SUBLANES, n_out), F32),
        grid=(DEPTH, D_MODEL // MOD_SLAB),
        in_specs=[
            pl.BlockSpec((D_MODEL, 2), lambda l, k: (0, 0)),
            pl.BlockSpec((1, MOD_SLAB, n_out), lambda l, k: (l, k, 0)),
            pl.BlockSpec((1, 1, n_out), lambda l, k: (l, 0, 0)),
        ],
        out_specs=pl.BlockSpec((1, SUBLANES, n_out), lambda l, k: (l, 0, 0)),
        scratch_shapes=[pltpu.VMEM((2, SUBLANES, n_out), F32)],
        compiler_params=_cparams(2),
        name="adaln_modulation",
    )(s_in, w_mod, b_mod.reshape(DEPTH, 1, n_out))


def _norm_rope_matrix():
    i = lax.broadcasted_iota(jnp.int32, (2 * HEAD_DIM, 2 * HEAD_DIM), 0)
    j = lax.broadcasted_iota(jnp.int32, (2 * HEAD_DIM, 2 * HEAD_DIM), 1)
    ones = (i < HEAD_DIM) & (j < HEAD_DIM)
    ii, jj = i - HEAD_DIM, j - HEAD_DIM
    low = (jj // ROPE_FREQS) % 2 == 0
    rot = jnp.where(low & (ii == jj + ROPE_FREQS), -1.0,
                    jnp.where(~low & (ii == jj - ROPE_FREQS), 1.0, 0.0))
    rot = jnp.where((i >= HEAD_DIM) & (j >= HEAD_DIM), rot, 0.0)
    return jnp.where(ones, 1.0, rot).astype(BF16)


def _norm_rope(p, w, mat_ref, cos, sin, scale):
    pw = p * w
    lhs = jnp.concatenate([(p * p).astype(BF16), pw.astype(BF16)], axis=1)
    red = jnp.dot(lhs, mat_ref[...], preferred_element_type=F32)
    r = lax.rsqrt(red[:, 0:HEAD_DIM] * (1.0 / HEAD_DIM) + EPS) * scale
    return r * (pw * cos + red[:, HEAD_DIM:2 * HEAD_DIM] * sin)


def _inproj_kernel(x_ref, mod_ref, n1w_ref, w_ref, qnw_ref, knw_ref, lb_ref, cos_ref, sin_ref, mat_ref,
                   q_ref, k_ref, v_ref, hq_ref, kf_ref, bf_ref, kb_ref, bb_ref, hv_ref, gate_ref,
                   h_scr):
    m = pl.program_id(0)
    n = pl.program_id(1)
    bm = x_ref.shape[0]

    def modulated(rows, mod_row):
        x = x_ref[rows, :]
        r = lax.rsqrt(jnp.mean(x * x, axis=-1, keepdims=True) + EPS)
        gain = n1w_ref[...] * (1.0 + mod_ref[mod_row:mod_row + 1, D_MODEL:2 * D_MODEL])
        return (x * r * gain + mod_ref[mod_row:mod_row + 1, 0:D_MODEL]).astype(BF16)

    @pl.when(n == 0)
    def _():
        h_scr[...] = modulated(slice(None), 0)

    @pl.when((n == 0) & (m == 0))
    def _():
        h_scr[0:CTX_LEN, :] = modulated(slice(0, CTX_LEN), 1)

    def project(epilogue):
        ch = IN_COL_TILE // 2
        hpt = ch // HEAD_DIM
        rh = bm // 2
        ps = []
        for ri in range(2):
            rows = slice(ri * rh, (ri + 1) * rh)
            h = h_scr[rows, :]
            ps.append((rows, [jnp.dot(h, w_ref[:, a * ch:(a + 1) * ch], preferred_element_type=F32)
                              for a in range(2)]))
        for rows, halves in ps:
            for a, p in enumerate(halves):
                for j in range(hpt):
                    epilogue(a * hpt + j, rows, p[:, j * HEAD_DIM:(j + 1) * HEAD_DIM])

    def q_out(j, rows, p):
        q_ref[j, rows, :] = _norm_rope(p, qnw_ref[...], mat_ref, cos_ref[rows, :], sin_ref[rows, :],
                                       Q_SCALE).astype(BF16)

    def kv_out(j, rows, p):
        if j < ATTN_KV_HEADS:
            k_ref[j, rows, :] = _norm_rope(p, knw_ref[...], mat_ref, cos_ref[rows, :],
                                           sin_ref[rows, :], 1.0).astype(BF16)
        else:
            v_ref[j - ATTN_KV_HEADS, rows, :] = p.astype(BF16)

    def silu_out(ref):
        def out(j, rows, p):
            ref[j, rows, :] = _silu(p).astype(BF16)
        return out

    def forget_out(k_out, b_out, rev):
        def out(j, rows, p):
            lb = lb_ref[:, j * HEAD_DIM:(j + 1) * HEAD_DIM]
            fg = lb + (1.0 - lb) * jax.nn.sigmoid(p)
            k_out[j, rows, :] = (1.0 - fg).astype(BF16)
            b_out[j, rows, :] = _chunk_cumsum(jnp.log2(fg), rev)
        return out

    def plain_out(j, rows, p):
        hv_ref[j, rows, :] = p.astype(BF16)

    pl.when(n < 2)(lambda: project(q_out))
    pl.when(n == 2)(lambda: project(kv_out))
    pl.when((n == 3) | (n == 4))(lambda: project(silu_out(hq_ref)))
    pl.when((n == 5) | (n == 6))(lambda: project(forget_out(kf_ref, bf_ref, False)))
    pl.when((n == 7) | (n == 8))(lambda: project(forget_out(kb_ref, bb_ref, True)))
    pl.when((n == 9) | (n == 10))(lambda: project(plain_out))
    pl.when(n >= 11)(lambda: project(silu_out(gate_ref)))


def _input_projection(layer, xs, mod, n1w, w_in, qnw, knw, lb, cos_t, sin_t):
    bm, bn = ROW_TILE, IN_COL_TILE
    n_m, n_n = T_ROWS // bm, IN_WIDTH // bn
    hpt = bn // HEAD_DIM

    def moved_on(m, n, last_tile):
        return (n > last_tile) & (m < n_m - 1)

    def head_spec(first_tile):
        def index(m, n):
            nxt = moved_on(m, n, first_tile + 1)
            return (jnp.where(nxt, 0, jnp.clip(n - first_tile, 0, 1)), jnp.where(nxt, m + 1, m), 0)
        return pl.BlockSpec((hpt, bm, HEAD_DIM), index)

    def head_shape(dtype):
        return jax.ShapeDtypeStruct((HG_HEADS, T_ROWS, HEAD_DIM), dtype)

    kv_spec = pl.BlockSpec((ATTN_KV_HEADS, bm, HEAD_DIM),
                           lambda m, n: (0, jnp.where(moved_on(m, n, 2), m + 1, m), 0))
    kv_shape = jax.ShapeDtypeStruct((ATTN_KV_HEADS, T_ROWS, HEAD_DIM), BF16)
    return pl.pallas_call(
        _inproj_kernel,
        out_shape=(head_shape(BF16), kv_shape, kv_shape, head_shape(BF16),
                   head_shape(BF16), head_shape(F32), head_shape(BF16), head_shape(F32),
                   head_shape(BF16), head_shape(BF16)),
        grid=(n_m, n_n),
        in_specs=[
            pl.BlockSpec((bm, D_MODEL), lambda m, n: (m, 0)),
            pl.BlockSpec((None, SUBLANES, 6 * D_MODEL), lambda m, n: (layer, 0, 0)),
            pl.BlockSpec((1, D_MODEL), lambda m, n: (0, 0)),
            pl.BlockSpec((None, D_MODEL, bn), lambda m, n: (layer, 0, n)),
            pl.BlockSpec((1, HEAD_DIM), lambda m, n: (0, 0)),
            pl.BlockSpec((1, HEAD_DIM), lambda m, n: (0, 0)),
            pl.BlockSpec((1, bn), lambda m, n: (0, jnp.clip(n - 5, 0, 3))),
            pl.BlockSpec((bm, HEAD_DIM), lambda m, n: (m, 0)),
            pl.BlockSpec((bm, HEAD_DIM), lambda m, n: (m, 0)),
            pl.BlockSpec((2 * HEAD_DIM, 2 * HEAD_DIM), lambda m, n: (0, 0)),
        ],
        out_specs=(head_spec(0), kv_spec, kv_spec, head_spec(3),
                   head_spec(5), head_spec(5), head_spec(7), head_spec(7),
                   head_spec(9), head_spec(11)),
        scratch_shapes=[pltpu.VMEM((bm, D_MODEL), BF16)],
        compiler_params=_cparams(2),
        name="input_projection",
    )(xs, mod, n1w, w_in, qnw, knw, lb, cos_t, sin_t, _norm_rope_matrix())


def _attn_kernel(q_ref, k_ref, vt_ref, o_ref, acc_scr, st_scr):
    i = pl.program_id(1)
    bq = q_ref.shape[1]
    bk = ATTN_K_TILE
    n_chunks = T_ROWS // bk

    def scores(kc, r, slot):
        part = kc.shape[0] // ATTN_QK_PARTS
        mx = None
        for a in range(ATTN_QK_PARTS):
            st = lax.dot_general(kc[a * part:(a + 1) * part], q_ref[r], (((1,), (1,)), ((), ())),
                                 preferred_element_type=F32)
            st_scr[slot, a * part:(a + 1) * part, :] = st
            m_part = st.max(axis=0, keepdims=True)
            mx = m_part if mx is None else jnp.maximum(mx, m_part)
        return mx

    def update(vtc, r, slot, m_cur, m, l):
        width = vtc.shape[1]
        m_new = jnp.maximum(m, m_cur)
        alpha = jnp.exp2(m - m_new)
        p = jnp.exp2(st_scr[slot, 0:width, :] - m_new)
        l = alpha * l + p.sum(axis=0, keepdims=True)
        acc_scr[r] = alpha * acc_scr[r] + jnp.dot(vtc, p.astype(BF16), preferred_element_type=F32)
        return m_new, l

    def finish(l):
        for r in range(ATTN_GROUP):
            o_ref[:, r * HEAD_DIM:(r + 1) * HEAD_DIM] = (acc_scr[r] / l[r]).T.astype(BF16)

    acc_scr[...] = jnp.zeros(acc_scr.shape, F32)
    m0 = tuple(jnp.full((1, bq), -jnp.inf, F32) for _ in range(ATTN_GROUP))
    l0 = tuple(jnp.zeros((1, bq), F32) for _ in range(ATTN_GROUP))

    @pl.when(i == 0)
    def _():
        kc = k_ref[0, 0:CTX_LEN, :]
        vtc = vt_ref[0, 0, :, 0:CTX_LEN]
        l = []
        mc = scores(kc, 0, 0)
        for r in range(ATTN_GROUP):
            mc_next = scores(kc, r + 1, (r + 1) % 2) if r + 1 < ATTN_GROUP else None
            l.append(update(vtc, r, r % 2, mc, m0[r], l0[r])[1])
            mc = mc_next
        finish(l)

    @pl.when(i > 0)
    def _():
        def chunk(j):
            return k_ref[0, pl.ds(pl.multiple_of(j * bk, bk), bk), :]

        def body(j, carry):
            mc, m, l = carry
            m, l = list(m), list(l)
            vtc = vt_ref[0, j]
            for r in range(ATTN_GROUP):
                if r + 1 < ATTN_GROUP:
                    mc_next = scores(chunk(j), r + 1, (r + 1) % 2)
                else:
                    mc_next = scores(chunk(jnp.minimum(j + 1, n_chunks - 1)), 0, 0)
                m[r], l[r] = update(vtc, r, r % 2, mc, m[r], l[r])
                mc = mc_next
            return mc, tuple(m), tuple(l)

        _, _, l = lax.fori_loop(0, n_chunks, body, (scores(chunk(0), 0, 0), m0, l0), unroll=True)
        finish(l)


def _attention(q, k, v):
    bq, bk = ATTN_Q_TILE, ATTN_K_TILE
    n_chunks = T_ROWS // bk
    vt = v.reshape(ATTN_KV_HEADS, n_chunks, bk, HEAD_DIM).transpose(0, 1, 3, 2)
    return pl.pallas_call(
        _attn_kernel,
        out_shape=jax.ShapeDtypeStruct((T_ROWS, ATTN_WIDTH), BF16),
        grid=(ATTN_KV_HEADS, T_ROWS // bq),
        in_specs=[
            pl.BlockSpec((ATTN_GROUP, bq, HEAD_DIM), lambda g, i: (g, i, 0)),
            pl.BlockSpec((1, T_ROWS, HEAD_DIM), lambda g, i: (g, 0, 0)),
            pl.BlockSpec((1, n_chunks, HEAD_DIM, bk), lambda g, i: (g, 0, 0, 0)),
        ],
        out_specs=pl.BlockSpec((bq, ATTN_GROUP * HEAD_DIM), lambda g, i: (i, g)),
        scratch_shapes=[pltpu.VMEM((ATTN_GROUP, HEAD_DIM, bq), F32),
                        pltpu.VMEM((2, bk, bq), F32)],
        compiler_params=_cparams(2),
        name="gqa_attention",
    )(q, k, vt)


def _chunk_cumsum(g, rev):
    n_groups = g.shape[0] // SUBLANES
    x = g.reshape(n_groups, SUBLANES, g.shape[1])
    sub = lax.broadcasted_iota(jnp.int32, x.shape, 1)
    step = 1
    while step < SUBLANES:
        if rev:
            x = x + jnp.where(sub < SUBLANES - step, pltpu.roll(x, SUBLANES - step, axis=1), 0.0)
        else:
            x = x + jnp.where(sub >= step, pltpu.roll(x, step, axis=1), 0.0)
        step *= 2
    per = HG_CHUNK // SUBLANES
    xs = [x[v] for v in range(n_groups)]
    for c0 in range(0, n_groups, per):
        if rev:
            for v in range(c0 + per - 2, c0 - 1, -1):
                xs[v] = xs[v] + xs[v + 1][0:1]
        else:
            for v in range(c0 + 1, c0 + per):
                xs[v] = xs[v] + xs[v - 1][SUBLANES - 1:SUBLANES]
    return jnp.concatenate(xs, axis=0)


def _hgrn_levels():
    halves = []
    half = HG_CHUNK // 2
    while half >= HG_DIAG:
        halves.append(half)
        half //= 2
    return halves


def _hgrn_consts(rev):
    c = HG_CHUNK
    levels = {}
    for half in _hgrn_levels():
        blk = 2 * half
        levels[half] = []
        for g in range(c // SUBLANES):
            t = lax.broadcasted_iota(jnp.int32, (SUBLANES, c), 0) + g * SUBLANES
            s = lax.broadcasted_iota(jnp.int32, (SUBLANES, c), 1)
            same = t // blk == s // blk
            if rev:
                levels[half].append(same & (t % blk < half) & (s % blk >= half))
            else:
                levels[half].append(same & (t % blk >= half) & (s % blk < half))
    t = lax.broadcasted_iota(jnp.int32, (c, c), 0)
    s = lax.broadcasted_iota(jnp.int32, (c, c), 1)
    band = []
    for d in range(HG_DIAG):
        if rev:
            band.append((s == t + d) & (t % HG_DIAG + d < HG_DIAG))
        else:
            band.append((s == t - d) & (t % HG_DIAG >= d))
    return levels, band


def _hgrn_scores(q, k, b, brow, rev, consts):
    levels, band = consts
    c = HG_CHUNK
    nt = (((1,), (1,)), ((), ()))
    groups = c // SUBLANES
    row8 = lax.broadcasted_iota(jnp.int32, (SUBLANES, HEAD_DIM), 0)
    rows = [slice(g * SUBLANES, (g + 1) * SUBLANES) for g in range(groups)]

    a = jnp.zeros((c, c), F32)
    for d in range(HG_DIAG):
        if d == 0:
            prod = q * k
        else:
            shift = c - d if rev else d
            prod = q * pltpu.roll(k, shift, axis=0) * jnp.exp2(b - pltpu.roll(b, shift, axis=0))
        a = jnp.where(band[d], prod.sum(axis=-1, keepdims=True), a)
    rows_a = [a[r] for r in rows]

    for half in _hgrn_levels():
        blk = 2 * half
        ref_off = half if rev else half - 1
        zs = []
        owners = []
        for g in range(groups):
            t0 = g * SUBLANES
            if half >= SUBLANES:
                ref = brow(t0 // blk * blk + ref_off)
                attending = (t0 % blk < half) if rev else (t0 % blk >= half)
                x = q[rows[g]] if attending else k[rows[g]]
                e = jnp.exp2(b[rows[g]] - ref if attending else ref - b[rows[g]])
                if attending:
                    owners.append(g)
            else:
                ref = brow(t0 + ref_off)
                for i in range(1, SUBLANES // blk):
                    ref = jnp.where(row8 >= i * blk, brow(t0 + i * blk + ref_off), ref)
                attending = (row8 % blk < half) if rev else (row8 % blk >= half)
                x = jnp.where(attending, q[rows[g]], k[rows[g]])
                e = jnp.exp2(-jnp.abs(b[rows[g]] - ref))
                owners.append(g)
            zs.append(x * e)
        z = jnp.concatenate(zs, axis=0).astype(BF16)
        lv = lax.dot_general(z, z, nt, preferred_element_type=F32)
        for g in owners:
            rows_a[g] = jnp.where(levels[half][g], lv[rows[g]], rows_a[g])
    return jnp.concatenate(rows_a, axis=0)


def _hgrn_kernel(qf_ref, kf_ref, bf_ref, vf_ref, qb_ref, kb_ref, bb_ref, vb_ref,
                 of_ref, ob_ref, s_scr):
    @pl.when(pl.program_id(0) == 0)
    def _():
        s_scr[...] = jnp.zeros(s_scr.shape, F32)

    c = HG_CHUNK
    n_chunks = HG_BLOCK // c
    consts = (_hgrn_consts(False), _hgrn_consts(True))
    nt = (((1,), (1,)), ((), ()))
    tn = (((0,), (0,)), ((), ()))
    dirs = ((qf_ref, kf_ref, bf_ref, vf_ref, of_ref, False),
            (qb_ref, kb_ref, bb_ref, vb_ref, ob_ref, True))

    def head_body(it, carry):
        jobs = []
        for hi in range(HG_ITER_HEADS):
            h = it * HG_ITER_HEADS + hi
            for di, (q_ref, k_ref, b_ref, v_ref, o_ref, rev) in enumerate(dirs):
                order = reversed(range(n_chunks)) if rev else range(n_chunks)
                for ci in order:
                    rows = slice(ci * c, (ci + 1) * c)
                    q = q_ref[h, rows, :].astype(F32)
                    k = k_ref[h, rows, :].astype(F32)
                    b = b_ref[h, rows, :]
                    v = v_ref[h, rows, :]
                    tot = b[0:1] if rev else b[c - 1:c]
                    q_in = (q * jnp.exp2(b)).astype(BF16)
                    k_out = (k * jnp.exp2(tot - b)).astype(BF16)
                    upd = lax.dot_general(v, k_out, tn, preferred_element_type=F32)

                    def brow(r, b_ref=b_ref, base=ci * c, h=h):
                        return b_ref[h, base + r:base + r + 1, :]

                    a = _hgrn_scores(q, k, b, brow, rev, consts[di]).astype(BF16)
                    jobs.append((hi, h, di, o_ref, rows, q_in, jnp.exp2(tot), upd, a, v))
        st = {(hi, di): s_scr[di, it * HG_ITER_HEADS + hi]
              for hi in range(HG_ITER_HEADS) for di in range(2)}
        for hi, h, di, o_ref, rows, q_in, decay, upd, a, v in jobs:
            o = lax.dot_general(q_in, st[hi, di].astype(BF16), nt, preferred_element_type=F32)
            st[hi, di] = decay * st[hi, di] + upd
            o_ref[h, rows, :] = o + jnp.dot(a, v, preferred_element_type=F32)
        for (hi, di), s in st.items():
            s_scr[di, it * HG_ITER_HEADS + hi] = s
        return carry

    lax.fori_loop(0, HG_HEADS // HG_ITER_HEADS, head_body, 0)


def _hgrn2(hq, kf, bf, kb, bb, hv):
    nb = T_ROWS // HG_BLOCK
    fwd = lambda i: (0, i, 0)
    bwd = lambda i: (0, jnp.where(i == 0, 0, nb - i), 0)
    blk = (HG_HEADS, HG_BLOCK, HEAD_DIM)
    out = jax.ShapeDtypeStruct((HG_HEADS, T_ROWS, HEAD_DIM), F32)
    return pl.pallas_call(
        _hgrn_kernel,
        out_shape=(out, out),
        grid=(nb,),
        in_specs=[pl.BlockSpec(blk, fwd)] * 4 + [pl.BlockSpec(blk, bwd)] * 4,
        out_specs=(pl.BlockSpec(blk, fwd), pl.BlockSpec(blk, bwd)),
        scratch_shapes=[pltpu.VMEM((2, HG_HEADS, HEAD_DIM, HEAD_DIM), F32)],
        compiler_params=_cparams(1),
        name="hgrn2_scan",
    )(hq, kf, bf, hv, hq, kb, bb, hv)


def _outproj_kernel(attn_ref, of_ref, ob_ref, gate_ref, hnw_ref, w_ref, x_ref, mod_ref, n2w_ref,
                    xo_ref, h2_ref, mix_scr):
    m = pl.program_id(0)
    bm = x_ref.shape[0]
    acc = jnp.dot(attn_ref[...], w_ref[0:ATTN_WIDTH, :], preferred_element_type=F32)
    for h in range(HG_HEADS):
        y = _rms(of_ref[h] + ob_ref[h], hnw_ref[...])
        mix_scr[:, h * HEAD_DIM:(h + 1) * HEAD_DIM] = (y * gate_ref[h].astype(F32)).astype(BF16)
    acc = acc + jnp.dot(mix_scr[...], w_ref[ATTN_WIDTH:ATTN_WIDTH + HG_WIDTH, :],
                        preferred_element_type=F32)

    def finish(rows, mod_row):
        def mod(k):
            return mod_ref[mod_row:mod_row + 1, k * D_MODEL:(k + 1) * D_MODEL]
        x_new = x_ref[rows, :] + mod(2) * acc[rows]
        xo_ref[rows, :] = x_new
        r = lax.rsqrt(jnp.mean(x_new * x_new, axis=-1, keepdims=True) + EPS)
        h2_ref[rows, :] = (x_new * r * (n2w_ref[...] * (1.0 + mod(4))) + mod(3)).astype(BF16)

    finish(slice(0, bm), 0)

    @pl.when(m == 0)
    def _():
        finish(slice(0, CTX_LEN), 1)


def _output_projection(layer, attn, o_f, o_b, gate, hnw, w_out, xs, mod, n2w):
    bm = OUT_ROW_TILE
    head_spec = pl.BlockSpec((HG_HEADS, bm, HEAD_DIM), lambda m: (0, m, 0))
    row_spec = pl.BlockSpec((bm, D_MODEL), lambda m: (m, 0))
    return pl.pallas_call(
        _outproj_kernel,
        out_shape=(jax.ShapeDtypeStruct((T_ROWS, D_MODEL), F32),
                   jax.ShapeDtypeStruct((T_ROWS, D_MODEL), BF16)),
        grid=(T_ROWS // bm,),
        in_specs=[
            pl.BlockSpec((bm, ATTN_WIDTH), lambda m: (m, 0)),
            head_spec, head_spec, head_spec,
            pl.BlockSpec((1, HEAD_DIM), lambda m: (0, 0)),
            pl.BlockSpec((None, ATTN_WIDTH + HG_WIDTH, D_MODEL), lambda m: (layer, 0, 0)),
            row_spec,
            pl.BlockSpec((None, SUBLANES, 6 * D_MODEL), lambda m: (layer, 0, 0)),
            pl.BlockSpec((1, D_MODEL), lambda m: (0, 0)),
        ],
        out_specs=(row_spec, row_spec),
        scratch_shapes=[pltpu.VMEM((bm, HG_WIDTH), BF16)],
        compiler_params=_cparams(1),
        name="output_projection",
    )(attn, o_f, o_b, gate, hnw, w_out, xs, mod, n2w)


def _ffn_up_kernel(h_ref, hp_ref, hn_ref, wg_ref, wu_ref, cw_ref, cb_ref, u_ref, hcat_scr, g_scr):
    m = pl.program_id(0)
    n = pl.program_id(1)
    bm = h_ref.shape[0]
    hl = CONV_HALO

    @pl.when(n == 0)
    def _():
        hcat_scr[0:hl, :] = hp_ref[...]
        hcat_scr[hl:hl + bm, :] = h_ref[...]
        hcat_scr[hl + bm:hl + bm + hl, :] = hn_ref[...]

    g_scr[...] = jnp.dot(hcat_scr[...], wg_ref[...], preferred_element_type=F32)
    up = jnp.dot(h_ref[...], wu_ref[...], preferred_element_type=F32)
    r = m * bm + lax.broadcasted_iota(jnp.int32, (bm, 1), 0)
    has_prev = (r != 0) & (r != CTX_LEN)
    has_next = (r != CTX_LEN - 1) & (r != T_ROWS - 1)
    g_prev = jnp.where(has_prev, g_scr[hl - 1:hl - 1 + bm, :], 0.0)
    g_next = jnp.where(has_next, g_scr[hl + 1:hl + 1 + bm, :], 0.0)
    gate = (g_prev * cw_ref[0:1, :] + g_scr[hl:hl + bm, :] * cw_ref[1:2, :]
            + g_next * cw_ref[2:3, :] + cb_ref[...])
    u_ref[...] = (_silu(gate) * up).astype(BF16)


def _ffn_up(layer, h2, w_up, conv_w, conv_b):
    bm, bn, hl = ROW_TILE, FF_COL_TILE, CONV_HALO
    n_n = D_FF // bn
    per = bm // hl
    last = T_ROWS // hl - 1
    return pl.pallas_call(
        _ffn_up_kernel,
        out_shape=jax.ShapeDtypeStruct((T_ROWS, D_FF), BF16),
        grid=(T_ROWS // bm, n_n),
        in_specs=[
            pl.BlockSpec((bm, D_MODEL), lambda m, n: (m, 0)),
            pl.BlockSpec((hl, D_MODEL), lambda m, n: (jnp.maximum(m * per - 1, 0), 0)),
            pl.BlockSpec((hl, D_MODEL), lambda m, n: (jnp.minimum((m + 1) * per, last), 0)),
            pl.BlockSpec((None, D_MODEL, bn), lambda m, n: (layer, 0, n)),
            pl.BlockSpec((None, D_MODEL, bn), lambda m, n: (layer, 0, n_n + n)),
            pl.BlockSpec((3, bn), lambda m, n: (0, n)),
            pl.BlockSpec((1, bn), lambda m, n: (0, n)),
        ],
        out_specs=pl.BlockSpec((bm, bn), lambda m, n: (m, n)),
        scratch_shapes=[pltpu.VMEM((bm + 2 * hl, D_MODEL), BF16),
                        pltpu.VMEM((bm + 2 * hl, bn), F32)],
        compiler_params=_cparams(2),
        name="ffn_up_conv",
    )(h2, h2, h2, w_up, w_up, conv_w, conv_b)


def _ffn_down_kernel(u_ref, w_ref, x_ref, mod_ref, fw_ref, o_ref, *, final):
    m = pl.program_id(0)

    def gate(mod_row):
        return mod_ref[mod_row:mod_row + 1, 5 * D_MODEL:6 * D_MODEL]

    def run():
        acc = jnp.dot(u_ref[...], w_ref[...], preferred_element_type=F32)
        x_new = x_ref[...] + gate(0) * acc
        o_ref[...] = _rms(x_new, fw_ref[...]) if final else x_new
        return acc

    if final:
        @pl.when(m > 0)
        def _():
            run()
    else:
        acc = run()

        @pl.when(m == 0)
        def _():
            o_ref[0:CTX_LEN, :] = x_ref[0:CTX_LEN, :] + gate(1) * acc[0:CTX_LEN]


def _ffn_down(layer, u, w_down, xs, mod, final_w, final):
    bm = CTX_LEN if final else OUT_ROW_TILE
    row_spec = pl.BlockSpec((bm, D_MODEL), lambda m: (m, 0))
    if final:
        out_rows = SEQ
        out_spec = pl.BlockSpec((bm, D_MODEL), lambda m: (jnp.maximum(m - 1, 0), 0))
    else:
        out_rows, out_spec = T_ROWS, row_spec
    return pl.pallas_call(
        functools.partial(_ffn_down_kernel, final=final),
        out_shape=jax.ShapeDtypeStruct((out_rows, D_MODEL), F32),
        grid=(T_ROWS // bm,),
        in_specs=[
            pl.BlockSpec((bm, D_FF), lambda m: (m, 0)),
            pl.BlockSpec((None, D_FF, D_MODEL), lambda m: (layer, 0, 0),
                         pipeline_mode=pl.Buffered(1)),
            row_spec,
            pl.BlockSpec((None, SUBLANES, 6 * D_MODEL), lambda m: (layer, 0, 0)),
            pl.BlockSpec((1, D_MODEL), lambda m: (0, 0)),
        ],
        out_specs=out_spec,
        compiler_params=_cparams(1),
        name="ffn_down",
    )(u, w_down, xs, mod, final_w)


def _rope_tables():
    f32 = np.float32
    rows = SEQ // GRID_W
    row = np.repeat(np.arange(rows, dtype=f32), GRID_W)
    col = np.tile(np.arange(GRID_W, dtype=f32), rows)
    freqs = np.power(f32(ROPE_THETA), -np.arange(ROPE_FREQS, dtype=f32) / f32(ROPE_FREQS)).astype(f32)
    ar, ac = row[:, None] * freqs, col[:, None] * freqs
    cos = np.concatenate([np.cos(ar), np.cos(ar), np.cos(ac), np.cos(ac)], axis=1)
    sin = np.concatenate([np.sin(ar), np.sin(ar), np.sin(ac), np.sin(ac)], axis=1)
    cos = np.concatenate([np.ones((CTX_LEN, HEAD_DIM), f32), cos], axis=0)
    sin = np.concatenate([np.zeros((CTX_LEN, HEAD_DIM), f32), sin], axis=0)
    return jnp.asarray(cos, F32), jnp.asarray(sin, F32)


def kernel(x, c, ctx, c_ctx, w_mod, b_mod, norm1_w, norm2_w, w_in, q_norm_w, k_norm_w, hg_lb_logits,
           hg_norm_w, w_out, w_up, conv_w, conv_b, w_down, final_norm_w):
    assert x.shape == (1, SEQ, D_MODEL) and ctx.shape == (1, CTX_LEN, D_MODEL)
    xs = jnp.concatenate([ctx[0], x[0]], axis=0)
    mod = _modulation(jnp.stack([c[0], c_ctx], axis=1), w_mod, b_mod)
    cos_t, sin_t = _rope_tables()
    lb_sm = jax.nn.softmax(hg_lb_logits.astype(F32), axis=1)
    lb_all = jnp.cumsum(lb_sm, axis=1) - lb_sm[:, :1]

    w_in, w_out, w_up, w_down = (w.astype(BF16) for w in (w_in, w_out, w_up, w_down))
    for l in range(DEPTH):
        lb = jnp.concatenate([lb_all[0, l], lb_all[1, l]])[None, :]
        q, k, v, hq, kf, bf, kb, bb, hv, gate = _input_projection(
            l, xs, mod, norm1_w[l][None, :], w_in, q_norm_w[l][None, :],
            k_norm_w[l][None, :], lb, cos_t, sin_t)
        attn = _attention(q, k, v)
        o_f, o_b = _hgrn2(hq, kf, bf, kb, bb, hv)
        xs, h2 = _output_projection(l, attn, o_f, o_b, gate, hg_norm_w[l][None, :],
                                    w_out, xs, mod, norm2_w[l][None, :])
        u = _ffn_up(l, h2, w_up, conv_w[l], conv_b[l][None, :])
        xs = _ffn_down(l, u, w_down, xs, mod, final_norm_w[None, :], final=(l == DEPTH - 1))
    return xs[None]
```

```python
import functools

import jax
import jax.numpy as jnp
import numpy as np
from jax import lax
from jax.experimental import pallas as pl
from jax.experimental.pallas import tpu as pltpu

F32 = jnp.float32
BF16 = jnp.bfloat16

D_MODEL = 2048
SEQ = 8192
DEPTH = 4
CTX_LEN = 256
T_ROWS = CTX_LEN + SEQ
GRID_W = 64
HEAD_DIM = 128
ATTN_HEADS = 8
ATTN_KV_HEADS = 2
ATTN_GROUP = 4
ATTN_WIDTH = 1024
KV_WIDTH = 256
HG_HEADS = 8
HG_WIDTH = 1024
IN_WIDTH = 6656
ROPE_THETA = 10000.0
ROPE_FREQS = 32
D_FF = 5632
EPS = 1e-6
Q_SCALE = HEAD_DIM ** -0.5 * 1.4426950408889634

LANES = 128
SUBLANES = 8
VMEM_LIMIT = 56 * 1024 * 1024

MOD_SLAB = 256
ROW_TILE = 768
OUT_ROW_TILE = 384
IN_COL_TILE = 512
FF_COL_TILE = 512
ATTN_Q_TILE = 256
ATTN_K_TILE = 1408
ATTN_QK_PARTS = 2
HG_BLOCK = 256
HG_ITER_HEADS = 4
HG_CHUNK = 64
HG_DIAG = 2
CONV_HALO = 16


def _cparams(n_axes):
    return pltpu.CompilerParams(dimension_semantics=("arbitrary",) * n_axes,
                                vmem_limit_bytes=VMEM_LIMIT)


def _rms(x, w):
    return x * lax.rsqrt(jnp.mean(x * x, axis=-1, keepdims=True) + EPS) * w


def _silu(x):
    h = 0.5 * x
    return h * jnp.tanh(h) + h


def _mod_kernel(s_ref, w_ref, b_ref, o_ref, acc_scr):
    k = pl.program_id(1)
    n_out = w_ref.shape[2]
    cw = 4 * LANES

    @pl.when(k == 0)
    def _():
        acc_scr[...] = jnp.zeros(acc_scr.shape, F32)

    r0 = pl.multiple_of(k * MOD_SLAB, MOD_SLAB)
    s = _silu(s_ref[pl.ds(r0, MOD_SLAB), :])
    sb = [jnp.tile(jnp.broadcast_to(s[:, i:i + 1], (MOD_SLAB, LANES)), (1, cw // LANES))
          for i in range(2)]
    for c0 in range(0, n_out, cw):
        w = w_ref[0, :, c0:c0 + cw]
        for i in range(2):
            part = (w * sb[i]).reshape(MOD_SLAB // SUBLANES, SUBLANES, cw).sum(axis=0)
            acc_scr[i, :, c0:c0 + cw] += part

    @pl.when(k == pl.num_programs(1) - 1)
    def _():
        bias = b_ref[0]
        rows = [acc_scr[i].sum(axis=0, keepdims=True) + bias for i in range(2)]
        o_ref[0] = jnp.concatenate(rows + [jnp.zeros((SUBLANES - 2, n_out), F32)], axis=0)


def _modulation(s_in, w_mod, b_mod):
    n_out = 6 * D_MODEL
    return pl.pallas_call(
        _mod_kernel,
        out_shape=jax.ShapeDtypeStruct((DEPTH, SUBLANES, n_out), F32),
        grid=(DEPTH, D_MODEL // MOD_SLAB),
        in_specs=[
            pl.BlockSpec((D_MODEL, 2), lambda l, k: (0, 0)),
            pl.BlockSpec((1, MOD_SLAB, n_out), lambda l, k: (l, k, 0)),
            pl.BlockSpec((1, 1, n_out), lambda l, k: (l, 0, 0)),
        ],
        out_specs=pl.BlockSpec((1, SUBLANES, n_out), lambda l, k: (l, 0, 0)),
        scratch_shapes=[pltpu.VMEM((2, SUBLANES, n_out), F32)],
        compiler_params=_cparams(2),
        name="adaln_modulation",
    )(s_in, w_mod, b_mod.reshape(DEPTH, 1, n_out))


def _norm_rope_matrix():
    i = lax.broadcasted_iota(jnp.int32, (2 * HEAD_DIM, 2 * HEAD_DIM), 0)
    j = lax.broadcasted_iota(jnp.int32, (2 * HEAD_DIM, 2 * HEAD_DIM), 1)
    ones = (i < HEAD_DIM) & (j < HEAD_DIM)
    ii, jj = i - HEAD_DIM, j - HEAD_DIM
    low = (jj // ROPE_FREQS) % 2 == 0
    rot = jnp.where(low & (ii == jj + ROPE_FREQS), -1.0,
                    jnp.where(~low & (ii == jj - ROPE_FREQS), 1.0, 0.0))
    rot = jnp.where((i >= HEAD_DIM) & (j >= HEAD_DIM), rot, 0.0)
    return jnp.where(ones, 1.0, rot).astype(BF16)


def _norm_rope(p, w, mat_ref, cos, sin, scale):
    pw = p * w
    lhs = jnp.concatenate([(p * p).astype(BF16), pw.astype(BF16)], axis=1)
    red = jnp.dot(lhs, mat_ref[...], preferred_element_type=F32)
    r = lax.rsqrt(red[:, 0:HEAD_DIM] * (1.0 / HEAD_DIM) + EPS) * scale
    return r * (pw * cos + red[:, HEAD_DIM:2 * HEAD_DIM] * sin)


def _inproj_kernel(x_ref, mod_ref, n1w_ref, w_ref, qnw_ref, knw_ref, lb_ref, cos_ref, sin_ref, mat_ref,
                   q_ref, k_ref, v_ref, hq_ref, kf_ref, bf_ref, kb_ref, bb_ref, hv_ref, gate_ref,
                   h_scr, p_scr):
    m = pl.program_id(0)
    n = pl.program_id(1)
    bm = x_ref.shape[0]

    def modulated(rows, mod_row):
        x = x_ref[rows, :]
        r = lax.rsqrt(jnp.mean(x * x, axis=-1, keepdims=True) + EPS)
        gain = n1w_ref[...] * (1.0 + mod_ref[mod_row:mod_row + 1, D_MODEL:2 * D_MODEL])
        return (x * r * gain + mod_ref[mod_row:mod_row + 1, 0:D_MODEL]).astype(BF16)

    @pl.when(n == 0)
    def _():
        h_scr[...] = modulated(slice(None), 0)

    @pl.when((n == 0) & (m == 0))
    def _():
        h_scr[0:CTX_LEN, :] = modulated(slice(0, CTX_LEN), 1)

    p_scr[...] = jnp.dot(h_scr[...], w_ref[...], preferred_element_type=F32)

    def project(epilogue):
        rh = bm // 2
        for ri in range(2):
            rows = slice(ri * rh, (ri + 1) * rh)
            for j in range(IN_COL_TILE // HEAD_DIM):
                epilogue(j, rows, p_scr[rows, j * HEAD_DIM:(j + 1) * HEAD_DIM])

    def q_out(j, rows, p):
        q_ref[j, rows, :] = _norm_rope(p, qnw_ref[...], mat_ref, cos_ref[rows, :], sin_ref[rows, :],
                                       Q_SCALE).astype(BF16)

    def kv_out(j, rows, p):
        if j < ATTN_KV_HEADS:
            k_ref[j, rows, :] = _norm_rope(p, knw_ref[...], mat_ref, cos_ref[rows, :],
                                           sin_ref[rows, :], 1.0).astype(BF16)
        else:
            v_ref[j - ATTN_KV_HEADS, rows, :] = p.astype(BF16)

    def silu_out(ref):
        def out(j, rows, p):
            ref[j, rows, :] = _silu(p).astype(BF16)
        return out

    def forget_out(k_out, b_out, rev):
        def out(j, rows, p):
            lb = lb_ref[:, j * HEAD_DIM:(j + 1) * HEAD_DIM]
            fg = lb + (1.0 - lb) * jax.nn.sigmoid(p)
            k_out[j, rows, :] = (1.0 - fg).astype(BF16)
            b_out[j, rows, :] = _chunk_cumsum(jnp.log2(fg), rev)
        return out

    def plain_out(j, rows, p):
        hv_ref[j, rows, :] = p.astype(BF16)

    pl.when(n < 2)(lambda: project(q_out))
    pl.when(n == 2)(lambda: project(kv_out))
    pl.when((n == 3) | (n == 4))(lambda: project(silu_out(hq_ref)))
    pl.when((n == 5) | (n == 6))(lambda: project(forget_out(kf_ref, bf_ref, False)))
    pl.when((n == 7) | (n == 8))(lambda: project(forget_out(kb_ref, bb_ref, True)))
    pl.when((n == 9) | (n == 10))(lambda: project(plain_out))
    pl.when(n >= 11)(lambda: project(silu_out(gate_ref)))


def _input_projection(layer, xs, mod, n1w, w_in, qnw, knw, lb, cos_t, sin_t):
    bm, bn = ROW_TILE, IN_COL_TILE
    n_m, n_n = T_ROWS // bm, IN_WIDTH // bn
    hpt = bn // HEAD_DIM

    def head_spec(first_tile):
        return pl.BlockSpec((hpt, bm, HEAD_DIM),
                            lambda m, n: (jnp.clip(n - first_tile, 0, 1), m, 0))

    def head_shape(dtype):
        return jax.ShapeDtypeStruct((HG_HEADS, T_ROWS, HEAD_DIM), dtype)

    kv_spec = pl.BlockSpec((ATTN_KV_HEADS, bm, HEAD_DIM), lambda m, n: (0, m, 0))
    kv_shape = jax.ShapeDtypeStruct((ATTN_KV_HEADS, T_ROWS, HEAD_DIM), BF16)
    return pl.pallas_call(
        _inproj_kernel,
        out_shape=(head_shape(BF16), kv_shape, kv_shape, head_shape(BF16),
                   head_shape(BF16), head_shape(F32), head_shape(BF16), head_shape(F32),
                   head_shape(BF16), head_shape(BF16)),
        grid=(n_m, n_n),
        in_specs=[
            pl.BlockSpec((bm, D_MODEL), lambda m, n: (m, 0)),
            pl.BlockSpec((None, SUBLANES, 6 * D_MODEL), lambda m, n: (layer, 0, 0)),
            pl.BlockSpec((1, D_MODEL), lambda m, n: (0, 0)),
            pl.BlockSpec((None, D_MODEL, bn), lambda m, n: (layer, 0, n)),
            pl.BlockSpec((1, HEAD_DIM), lambda m, n: (0, 0)),
            pl.BlockSpec((1, HEAD_DIM), lambda m, n: (0, 0)),
            pl.BlockSpec((1, bn), lambda m, n: (0, jnp.clip(n - 5, 0, 3))),
            pl.BlockSpec((bm, HEAD_DIM), lambda m, n: (m, 0)),
            pl.BlockSpec((bm, HEAD_DIM), lambda m, n: (m, 0)),
            pl.BlockSpec((2 * HEAD_DIM, 2 * HEAD_DIM), lambda m, n: (0, 0)),
        ],
        out_specs=(head_spec(0), kv_spec, kv_spec, head_spec(3),
                   head_spec(5), head_spec(5), head_spec(7), head_spec(7),
                   head_spec(9), head_spec(11)),
        scratch_shapes=[pltpu.VMEM((bm, D_MODEL), BF16), pltpu.VMEM((bm, bn), F32)],
        compiler_params=_cparams(2),
        name="input_projection",
    )(xs, mod, n1w, w_in, qnw, knw, lb, cos_t, sin_t, _norm_rope_matrix())


def _attn_kernel(q_ref, k_ref, vt_ref, o_ref, acc_scr, st_scr):
    i = pl.program_id(1)
    bq = q_ref.shape[1]
    bk = ATTN_K_TILE
    n_chunks = T_ROWS // bk

    def scores(kc, r, slot):
        part = kc.shape[0] // ATTN_QK_PARTS
        mx = None
        for a in range(ATTN_QK_PARTS):
            st = lax.dot_general(kc[a * part:(a + 1) * part], q_ref[r], (((1,), (1,)), ((), ())),
                                 preferred_element_type=F32)
            st_scr[slot, a * part:(a + 1) * part, :] = st
            m_part = st.max(axis=0, keepdims=True)
            mx = m_part if mx is None else jnp.maximum(mx, m_part)
        return mx

    def update(vtc, r, slot, m_cur, m, l):
        width = vtc.shape[1]
        m_new = jnp.maximum(m, m_cur)
        alpha = jnp.exp2(m - m_new)
        p = jnp.exp2(st_scr[slot, 0:width, :] - m_new)
        l = alpha * l + p.sum(axis=0, keepdims=True)
        acc_scr[r] = alpha * acc_scr[r] + jnp.dot(vtc, p.astype(BF16), preferred_element_type=F32)
        return m_new, l

    def finish(l):
        for r in range(ATTN_GROUP):
            o_ref[:, r * HEAD_DIM:(r + 1) * HEAD_DIM] = (acc_scr[r] / l[r]).T.astype(BF16)

    acc_scr[...] = jnp.zeros(acc_scr.shape, F32)
    m0 = tuple(jnp.full((1, bq), -jnp.inf, F32) for _ in range(ATTN_GROUP))
    l0 = tuple(jnp.zeros((1, bq), F32) for _ in range(ATTN_GROUP))

    @pl.when(i == 0)
    def _():
        kc = k_ref[0, 0:CTX_LEN, :]
        vtc = vt_ref[0, 0, :, 0:CTX_LEN]
        l = []
        mc = scores(kc, 0, 0)
        for r in range(ATTN_GROUP):
            mc_next = scores(kc, r + 1, (r + 1) % 2) if r + 1 < ATTN_GROUP else None
            l.append(update(vtc, r, r % 2, mc, m0[r], l0[r])[1])
            mc = mc_next
        finish(l)

    @pl.when(i > 0)
    def _():
        def chunk(j):
            return k_ref[0, pl.ds(pl.multiple_of(j * bk, bk), bk), :]

        def body(j, carry):
            mc, m, l = carry
            m, l = list(m), list(l)
            vtc = vt_ref[0, j]
            for r in range(ATTN_GROUP):
                if r + 1 < ATTN_GROUP:
                    mc_next = scores(chunk(j), r + 1, (r + 1) % 2)
                else:
                    mc_next = scores(chunk(jnp.minimum(j + 1, n_chunks - 1)), 0, 0)
                m[r], l[r] = update(vtc, r, r % 2, mc, m[r], l[r])
                mc = mc_next
            return mc, tuple(m), tuple(l)

        _, _, l = lax.fori_loop(0, n_chunks, body, (scores(chunk(0), 0, 0), m0, l0))
        finish(l)


def _attention(q, k, v):
    bq, bk = ATTN_Q_TILE, ATTN_K_TILE
    n_chunks = T_ROWS // bk
    vt = v.reshape(ATTN_KV_HEADS, n_chunks, bk, HEAD_DIM).transpose(0, 1, 3, 2)
    return pl.pallas_call(
        _attn_kernel,
        out_shape=jax.ShapeDtypeStruct((T_ROWS, ATTN_WIDTH), BF16),
        grid=(ATTN_KV_HEADS, T_ROWS // bq),
        in_specs=[
            pl.BlockSpec((ATTN_GROUP, bq, HEAD_DIM), lambda g, i: (g, i, 0)),
            pl.BlockSpec((1, T_ROWS, HEAD_DIM), lambda g, i: (g, 0, 0)),
            pl.BlockSpec((1, n_chunks, HEAD_DIM, bk), lambda g, i: (g, 0, 0, 0)),
        ],
        out_specs=pl.BlockSpec((bq, ATTN_GROUP * HEAD_DIM), lambda g, i: (i, g)),
        scratch_shapes=[pltpu.VMEM((ATTN_GROUP, HEAD_DIM, bq), F32),
                        pltpu.VMEM((2, bk, bq), F32)],
        compiler_params=_cparams(2),
        name="gqa_attention",
    )(q, k, vt)


def _chunk_cumsum(g, rev):
    n_groups = g.shape[0] // SUBLANES
    x = g.reshape(n_groups, SUBLANES, g.shape[1])
    sub = lax.broadcasted_iota(jnp.int32, x.shape, 1)
    step = 1
    while step < SUBLANES:
        if rev:
            x = x + jnp.where(sub < SUBLANES - step, pltpu.roll(x, SUBLANES - step, axis=1), 0.0)
        else:
            x = x + jnp.where(sub >= step, pltpu.roll(x, step, axis=1), 0.0)
        step *= 2
    per = HG_CHUNK // SUBLANES
    xs = [x[v] for v in range(n_groups)]
    for c0 in range(0, n_groups, per):
        if rev:
            for v in range(c0 + per - 2, c0 - 1, -1):
                xs[v] = xs[v] + xs[v + 1][0:1]
        else:
            for v in range(c0 + 1, c0 + per):
                xs[v] = xs[v] + xs[v - 1][SUBLANES - 1:SUBLANES]
    return jnp.concatenate(xs, axis=0)


def _hgrn_levels():
    halves = []
    half = HG_CHUNK // 2
    while half >= HG_DIAG:
        halves.append(half)
        half //= 2
    return halves


def _hgrn_consts(rev):
    c = HG_CHUNK
    levels = {}
    for half in _hgrn_levels():
        blk = 2 * half
        levels[half] = []
        for g in range(c // SUBLANES):
            t = lax.broadcasted_iota(jnp.int32, (SUBLANES, c), 0) + g * SUBLANES
            s = lax.broadcasted_iota(jnp.int32, (SUBLANES, c), 1)
            same = t // blk == s // blk
            if rev:
                levels[half].append(same & (t % blk < half) & (s % blk >= half))
            else:
                levels[half].append(same & (t % blk >= half) & (s % blk < half))
    t = lax.broadcasted_iota(jnp.int32, (c, c), 0)
    s = lax.broadcasted_iota(jnp.int32, (c, c), 1)
    band = []
    for d in range(HG_DIAG):
        if rev:
            band.append((s == t + d) & (t % HG_DIAG + d < HG_DIAG))
        else:
            band.append((s == t - d) & (t % HG_DIAG >= d))
    return levels, band


def _hgrn_scores(q, k, b, brow, rev, consts):
    levels, band = consts
    c = HG_CHUNK
    nt = (((1,), (1,)), ((), ()))
    groups = c // SUBLANES
    row8 = lax.broadcasted_iota(jnp.int32, (SUBLANES, HEAD_DIM), 0)
    rows = [slice(g * SUBLANES, (g + 1) * SUBLANES) for g in range(groups)]

    a = jnp.zeros((c, c), F32)
    for d in range(HG_DIAG):
        if d == 0:
            prod = q * k
        else:
            shift = c - d if rev else d
            prod = q * pltpu.roll(k, shift, axis=0) * jnp.exp2(b - pltpu.roll(b, shift, axis=0))
        a = jnp.where(band[d], prod.sum(axis=-1, keepdims=True), a)
    rows_a = [a[r] for r in rows]

    for half in _hgrn_levels():
        blk = 2 * half
        ref_off = half if rev else half - 1
        zs = []
        owners = []
        for g in range(groups):
            t0 = g * SUBLANES
            if half >= SUBLANES:
                ref = brow(t0 // blk * blk + ref_off)
                attending = (t0 % blk < half) if rev else (t0 % blk >= half)
                x = q[rows[g]] if attending else k[rows[g]]
                e = jnp.exp2(b[rows[g]] - ref if attending else ref - b[rows[g]])
                if attending:
                    owners.append(g)
            else:
                ref = brow(t0 + ref_off)
                for i in range(1, SUBLANES // blk):
                    ref = jnp.where(row8 >= i * blk, brow(t0 + i * blk + ref_off), ref)
                attending = (row8 % blk < half) if rev else (row8 % blk >= half)
                x = jnp.where(attending, q[rows[g]], k[rows[g]])
                e = jnp.exp2(-jnp.abs(b[rows[g]] - ref))
                owners.append(g)
            zs.append(x * e)
        z = jnp.concatenate(zs, axis=0).astype(BF16)
        lv = lax.dot_general(z, z, nt, preferred_element_type=F32)
        for g in owners:
            rows_a[g] = jnp.where(levels[half][g], lv[rows[g]], rows_a[g])
    return jnp.concatenate(rows_a, axis=0)


def _hgrn_kernel(qf_ref, kf_ref, bf_ref, vf_ref, qb_ref, kb_ref, bb_ref, vb_ref,
                 of_ref, ob_ref, s_scr):
    @pl.when(pl.program_id(0) == 0)
    def _():
        s_scr[...] = jnp.zeros(s_scr.shape, F32)

    c = HG_CHUNK
    n_chunks = HG_BLOCK // c
    consts = (_hgrn_consts(False), _hgrn_consts(True))
    nt = (((1,), (1,)), ((), ()))
    tn = (((0,), (0,)), ((), ()))
    dirs = ((qf_ref, kf_ref, bf_ref, vf_ref, of_ref, False),
            (qb_ref, kb_ref, bb_ref, vb_ref, ob_ref, True))

    def head_body(it, carry):
        jobs = []
        for hi in range(HG_ITER_HEADS):
            h = it * HG_ITER_HEADS + hi
            for di, (q_ref, k_ref, b_ref, v_ref, o_ref, rev) in enumerate(dirs):
                order = reversed(range(n_chunks)) if rev else range(n_chunks)
                for ci in order:
                    rows = slice(ci * c, (ci + 1) * c)
                    q = q_ref[h, rows, :].astype(F32)
                    k = k_ref[h, rows, :].astype(F32)
                    b = b_ref[h, rows, :]
                    v = v_ref[h, rows, :]
                    tot = b[0:1] if rev else b[c - 1:c]
                    q_in = (q * jnp.exp2(b)).astype(BF16)
                    k_out = (k * jnp.exp2(tot - b)).astype(BF16)
                    upd = lax.dot_general(v, k_out, tn, preferred_element_type=F32)

                    def brow(r, b_ref=b_ref, base=ci * c, h=h):
                        return b_ref[h, base + r:base + r + 1, :]

                    a = _hgrn_scores(q, k, b, brow, rev, consts[di]).astype(BF16)
                    jobs.append((hi, h, di, o_ref, rows, q_in, jnp.exp2(tot), upd, a, v))
        st = {(hi, di): s_scr[di, it * HG_ITER_HEADS + hi]
              for hi in range(HG_ITER_HEADS) for di in range(2)}
        for hi, h, di, o_ref, rows, q_in, decay, upd, a, v in jobs:
            o = lax.dot_general(q_in, st[hi, di].astype(BF16), nt, preferred_element_type=F32)
            st[hi, di] = decay * st[hi, di] + upd
            o_ref[h, rows, :] = o + jnp.dot(a, v, preferred_element_type=F32)
        for (hi, di), s in st.items():
            s_scr[di, it * HG_ITER_HEADS + hi] = s
        return carry

    lax.fori_loop(0, HG_HEADS // HG_ITER_HEADS, head_body, 0)


def _hgrn2(hq, kf, bf, kb, bb, hv):
    nb = T_ROWS // HG_BLOCK
    fwd = lambda i: (0, i, 0)
    bwd = lambda i: (0, jnp.where(i == 0, 0, nb - i), 0)
    blk = (HG_HEADS, HG_BLOCK, HEAD_DIM)
    out = jax.ShapeDtypeStruct((HG_HEADS, T_ROWS, HEAD_DIM), F32)
    return pl.pallas_call(
        _hgrn_kernel,
        out_shape=(out, out),
        grid=(nb,),
        in_specs=[pl.BlockSpec(blk, fwd)] * 4 + [pl.BlockSpec(blk, bwd)] * 4,
        out_specs=(pl.BlockSpec(blk, fwd), pl.BlockSpec(blk, bwd)),
        scratch_shapes=[pltpu.VMEM((2, HG_HEADS, HEAD_DIM, HEAD_DIM), F32)],
        compiler_params=_cparams(1),
        name="hgrn2_scan",
    )(hq, kf, bf, hv, hq, kb, bb, hv)


def _outproj_kernel(attn_ref, of_ref, ob_ref, gate_ref, hnw_ref, w_ref, x_ref, mod_ref, n2w_ref,
                    xo_ref, h2_ref, mix_scr):
    m = pl.program_id(0)
    bm = x_ref.shape[0]
    acc = jnp.dot(attn_ref[...], w_ref[0:ATTN_WIDTH, :], preferred_element_type=F32)
    for h in range(HG_HEADS):
        y = _rms(of_ref[h] + ob_ref[h], hnw_ref[...])
        mix_scr[:, h * HEAD_DIM:(h + 1) * HEAD_DIM] = (y * gate_ref[h].astype(F32)).astype(BF16)
    acc = acc + jnp.dot(mix_scr[...], w_ref[ATTN_WIDTH:ATTN_WIDTH + HG_WIDTH, :],
                        preferred_element_type=F32)

    def finish(rows, mod_row):
        def mod(k):
            return mod_ref[mod_row:mod_row + 1, k * D_MODEL:(k + 1) * D_MODEL]
        x_new = x_ref[rows, :] + mod(2) * acc[rows]
        xo_ref[rows, :] = x_new
        r = lax.rsqrt(jnp.mean(x_new * x_new, axis=-1, keepdims=True) + EPS)
        h2_ref[rows, :] = (x_new * r * (n2w_ref[...] * (1.0 + mod(4))) + mod(3)).astype(BF16)

    finish(slice(0, bm), 0)

    @pl.when(m == 0)
    def _():
        finish(slice(0, CTX_LEN), 1)


def _output_projection(layer, attn, o_f, o_b, gate, hnw, w_out, xs, mod, n2w):
    bm = OUT_ROW_TILE
    head_spec = pl.BlockSpec((HG_HEADS, bm, HEAD_DIM), lambda m: (0, m, 0))
    row_spec = pl.BlockSpec((bm, D_MODEL), lambda m: (m, 0))
    return pl.pallas_call(
        _outproj_kernel,
        out_shape=(jax.ShapeDtypeStruct((T_ROWS, D_MODEL), F32),
                   jax.ShapeDtypeStruct((T_ROWS, D_MODEL), BF16)),
        grid=(T_ROWS // bm,),
        in_specs=[
            pl.BlockSpec((bm, ATTN_WIDTH), lambda m: (m, 0)),
            head_spec, head_spec, head_spec,
            pl.BlockSpec((1, HEAD_DIM), lambda m: (0, 0)),
            pl.BlockSpec((None, ATTN_WIDTH + HG_WIDTH, D_MODEL), lambda m: (layer, 0, 0)),
            row_spec,
            pl.BlockSpec((None, SUBLANES, 6 * D_MODEL), lambda m: (layer, 0, 0)),
            pl.BlockSpec((1, D_MODEL), lambda m: (0, 0)),
        ],
        out_specs=(row_spec, row_spec),
        scratch_shapes=[pltpu.VMEM((bm, HG_WIDTH), BF16)],
        compiler_params=_cparams(1),
        name="output_projection",
    )(attn, o_f, o_b, gate, hnw, w_out, xs, mod, n2w)


def _ffn_up_kernel(h_ref, hp_ref, hn_ref, wg_ref, wu_ref, cw_ref, cb_ref, u_ref, hcat_scr, g_scr):
    m = pl.program_id(0)
    n = pl.program_id(1)
    bm = h_ref.shape[0]
    hl = CONV_HALO

    @pl.when(n == 0)
    def _():
        hcat_scr[0:hl, :] = hp_ref[...]
        hcat_scr[hl:hl + bm, :] = h_ref[...]
        hcat_scr[hl + bm:hl + bm + hl, :] = hn_ref[...]

    g_scr[...] = jnp.dot(hcat_scr[...], wg_ref[...], preferred_element_type=F32)
    up = jnp.dot(h_ref[...], wu_ref[...], preferred_element_type=F32)
    r = m * bm + lax.broadcasted_iota(jnp.int32, (bm, 1), 0)
    has_prev = (r != 0) & (r != CTX_LEN)
    has_next = (r != CTX_LEN - 1) & (r != T_ROWS - 1)
    g_prev = jnp.where(has_prev, g_scr[hl - 1:hl - 1 + bm, :], 0.0)
    g_next = jnp.where(has_next, g_scr[hl + 1:hl + 1 + bm, :], 0.0)
    gate = (g_prev * cw_ref[0:1, :] + g_scr[hl:hl + bm, :] * cw_ref[1:2, :]
            + g_next * cw_ref[2:3, :] + cb_ref[...])
    u_ref[...] = (_silu(gate) * up).astype(BF16)


def _ffn_up(layer, h2, w_up, conv_w, conv_b):
    bm, bn, hl = ROW_TILE, FF_COL_TILE, CONV_HALO
    n_n = D_FF // bn
    per = bm // hl
    last = T_ROWS // hl - 1
    return pl.pallas_call(
        _ffn_up_kernel,
        out_shape=jax.ShapeDtypeStruct((T_ROWS, D_FF), BF16),
        grid=(T_ROWS // bm, n_n),
        in_specs=[
            pl.BlockSpec((bm, D_MODEL), lambda m, n: (m, 0)),
            pl.BlockSpec((hl, D_MODEL), lambda m, n: (jnp.maximum(m * per - 1, 0), 0)),
            pl.BlockSpec((hl, D_MODEL), lambda m, n: (jnp.minimum((m + 1) * per, last), 0)),
            pl.BlockSpec((None, D_MODEL, bn), lambda m, n: (layer, 0, n)),
            pl.BlockSpec((None, D_MODEL, bn), lambda m, n: (layer, 0, n_n + n)),
            pl.BlockSpec((3, bn), lambda m, n: (0, n)),
            pl.BlockSpec((1, bn), lambda m, n: (0, n)),
        ],
        out_specs=pl.BlockSpec((bm, bn), lambda m, n: (m, n)),
        scratch_shapes=[pltpu.VMEM((bm + 2 * hl, D_MODEL), BF16),
                        pltpu.VMEM((bm + 2 * hl, bn), F32)],
        compiler_params=_cparams(2),
        name="ffn_up_conv",
    )(h2, h2, h2, w_up, w_up, conv_w, conv_b)


def _ffn_down_kernel(u_ref, w_ref, x_ref, mod_ref, fw_ref, o_ref, *, final):
    m = pl.program_id(0)

    def gate(mod_row):
        return mod_ref[mod_row:mod_row + 1, 5 * D_MODEL:6 * D_MODEL]

    def run():
        acc = jnp.dot(u_ref[...], w_ref[...], preferred_element_type=F32)
        x_new = x_ref[...] + gate(0) * acc
        o_ref[...] = _rms(x_new, fw_ref[...]) if final else x_new
        return acc

    if final:
        @pl.when(m > 0)
        def _():
            run()
    else:
        acc = run()

        @pl.when(m == 0)
        def _():
            o_ref[0:CTX_LEN, :] = x_ref[0:CTX_LEN, :] + gate(1) * acc[0:CTX_LEN]


def _ffn_down(layer, u, w_down, xs, mod, final_w, final):
    bm = CTX_LEN if final else OUT_ROW_TILE
    row_spec = pl.BlockSpec((bm, D_MODEL), lambda m: (m, 0))
    if final:
        out_rows = SEQ
        out_spec = pl.BlockSpec((bm, D_MODEL), lambda m: (jnp.maximum(m - 1, 0), 0))
    else:
        out_rows, out_spec = T_ROWS, row_spec
    return pl.pallas_call(
        functools.partial(_ffn_down_kernel, final=final),
        out_shape=jax.ShapeDtypeStruct((out_rows, D_MODEL), F32),
        grid=(T_ROWS // bm,),
        in_specs=[
            pl.BlockSpec((bm, D_FF), lambda m: (m, 0)),
            pl.BlockSpec((None, D_FF, D_MODEL), lambda m: (layer, 0, 0),
                         pipeline_mode=pl.Buffered(1)),
            row_spec,
            pl.BlockSpec((None, SUBLANES, 6 * D_MODEL), lambda m: (layer, 0, 0)),
            pl.BlockSpec((1, D_MODEL), lambda m: (0, 0)),
        ],
        out_specs=out_spec,
        compiler_params=_cparams(1),
        name="ffn_down",
    )(u, w_down, xs, mod, final_w)


def _rope_tables():
    f32 = np.float32
    rows = SEQ // GRID_W
    row = np.repeat(np.arange(rows, dtype=f32), GRID_W)
    col = np.tile(np.arange(GRID_W, dtype=f32), rows)
    freqs = np.power(f32(ROPE_THETA), -np.arange(ROPE_FREQS, dtype=f32) / f32(ROPE_FREQS)).astype(f32)
    ar, ac = row[:, None] * freqs, col[:, None] * freqs
    cos = np.concatenate([np.cos(ar), np.cos(ar), np.cos(ac), np.cos(ac)], axis=1)
    sin = np.concatenate([np.sin(ar), np.sin(ar), np.sin(ac), np.sin(ac)], axis=1)
    cos = np.concatenate([np.ones((CTX_LEN, HEAD_DIM), f32), cos], axis=0)
    sin = np.concatenate([np.zeros((CTX_LEN, HEAD_DIM), f32), sin], axis=0)
    return jnp.asarray(cos, F32), jnp.asarray(sin, F32)


def kernel(x, c, ctx, c_ctx, w_mod, b_mod, norm1_w, norm2_w, w_in, q_norm_w, k_norm_w, hg_lb_logits,
           hg_norm_w, w_out, w_up, conv_w, conv_b, w_down, final_norm_w):
    assert x.shape == (1, SEQ, D_MODEL) and ctx.shape == (1, CTX_LEN, D_MODEL)
    xs = jnp.concatenate([ctx[0], x[0]], axis=0)
    mod = _modulation(jnp.stack([c[0], c_ctx], axis=1), w_mod, b_mod)
    cos_t, sin_t = _rope_tables()
    lb_sm = jax.nn.softmax(hg_lb_logits.astype(F32), axis=1)
    lb_all = jnp.cumsum(lb_sm, axis=1) - lb_sm[:, :1]

    w_in, w_out, w_up, w_down = (w.astype(BF16) for w in (w_in, w_out, w_up, w_down))
    for l in range(DEPTH):
        lb = jnp.concatenate([lb_all[0, l], lb_all[1, l]])[None, :]
        q, k, v, hq, kf, bf, kb, bb, hv, gate = _input_projection(
            l, xs, mod, norm1_w[l][None, :], w_in, q_norm_w[l][None, :],
            k_norm_w[l][None, :], lb, cos_t, sin_t)
        attn = _attention(q, k, v)
        o_f, o_b = _hgrn2(hq, kf, bf, kb, bb, hv)
        xs, h2 = _output_projection(l, attn, o_f, o_b, gate, hg_norm_w[l][None, :],
                                    w_out, xs, mod, norm2_w[l][None, :])
        u = _ffn_up(l, h2, w_up, conv_w[l], conv_b[l][None, :])
        xs = _ffn_down(l, u, w_down, xs, mod, final_norm_w[None, :], final=(l == DEPTH - 1))
    return xs[None]
```

```python
import functools

import jax
import jax.numpy as jnp
import numpy as np
from jax import lax
from jax.experimental import pallas as pl
from jax.experimental.pallas import tpu as pltpu

F32 = jnp.float32
BF16 = jnp.bfloat16

D_MODEL = 2048
SEQ = 8192
DEPTH = 4
CTX_LEN = 256
T_ROWS = CTX_LEN + SEQ
GRID_W = 64
HEAD_DIM = 128
ATTN_HEADS = 8
ATTN_KV_HEADS = 2
ATTN_GROUP = 4
ATTN_WIDTH = 1024
KV_WIDTH = 256
HG_HEADS = 8
HG_WIDTH = 1024
IN_WIDTH = 6656
ROPE_THETA = 10000.0
ROPE_FREQS = 32
D_FF = 5632
EPS = 1e-6
Q_SCALE = HEAD_DIM ** -0.5 * 1.4426950408889634

LANES = 128
SUBLANES = 8
VMEM_LIMIT = 56 * 1024 * 1024

MOD_SLAB = 256
ROW_TILE = 768
OUT_ROW_TILE = 384
IN_COL_TILE = 512
FF_COL_TILE = 512
ATTN_Q_TILE = 256
ATTN_K_TILE = 2816
ATTN_QK_PARTS = 2
HG_BLOCK = 256
HG_ITER_HEADS = 4
HG_CHUNK = 64
HG_DIAG = 2
CONV_HALO = 16


def _cparams(n_axes):
    return pltpu.CompilerParams(dimension_semantics=("arbitrary",) * n_axes,
                                vmem_limit_bytes=VMEM_LIMIT)


def _rms(x, w):
    return x * lax.rsqrt(jnp.mean(x * x, axis=-1, keepdims=True) + EPS) * w


def _silu(x):
    h = 0.5 * x
    return h * jnp.tanh(h) + h


def _mod_kernel(s_ref, w_ref, b_ref, o_ref, acc_scr):
    k = pl.program_id(1)
    n_out = w_ref.shape[2]
    cw = 4 * LANES

    @pl.when(k == 0)
    def _():
        acc_scr[...] = jnp.zeros(acc_scr.shape, F32)

    r0 = pl.multiple_of(k * MOD_SLAB, MOD_SLAB)
    s = _silu(s_ref[pl.ds(r0, MOD_SLAB), :])
    sb = [jnp.tile(jnp.broadcast_to(s[:, i:i + 1], (MOD_SLAB, LANES)), (1, cw // LANES))
          for i in range(2)]
    for c0 in range(0, n_out, cw):
        w = w_ref[0, :, c0:c0 + cw]
        for i in range(2):
            part = (w * sb[i]).reshape(MOD_SLAB // SUBLANES, SUBLANES, cw).sum(axis=0)
            acc_scr[i, :, c0:c0 + cw] += part

    @pl.when(k == pl.num_programs(1) - 1)
    def _():
        bias = b_ref[0]
        rows = [acc_scr[i].sum(axis=0, keepdims=True) + bias for i in range(2)]
        o_ref[0] = jnp.concatenate(rows + [jnp.zeros((SUBLANES - 2, n_out), F32)], axis=0)


def _modulation(s_in, w_mod, b_mod):
    n_out = 6 * D_MODEL
    return pl.pallas_call(
        _mod_kernel,
        out_shape=jax.ShapeDtypeStruct((DEPTH, SUBLANES, n_out), F32),
        grid=(DEPTH, D_MODEL // MOD_SLAB),
        in_specs=[
            pl.BlockSpec((D_MODEL, 2), lambda l, k: (0, 0)),
            pl.BlockSpec((1, MOD_SLAB, n_out), lambda l, k: (l, k, 0)),
            pl.BlockSpec((1, 1, n_out), lambda l, k: (l, 0, 0)),
        ],
        out_specs=pl.BlockSpec((1, SUBLANES, n_out), lambda l, k: (l, 0, 0)),
        scratch_shapes=[pltpu.VMEM((2, SUBLANES, n_out), F32)],
        compiler_params=_cparams(2),
        name="adaln_modulation",
    )(s_in, w_mod, b_mod.reshape(DEPTH, 1, n_out))


def _norm_rope_matrix():
    i = lax.broadcasted_iota(jnp.int32, (2 * HEAD_DIM, 2 * HEAD_DIM), 0)
    j = lax.broadcasted_iota(jnp.int32, (2 * HEAD_DIM, 2 * HEAD_DIM), 1)
    ones = (i < HEAD_DIM) & (j < HEAD_DIM)
    ii, jj = i - HEAD_DIM, j - HEAD_DIM
    low = (jj // ROPE_FREQS) % 2 == 0
    rot = jnp.where(low & (ii == jj + ROPE_FREQS), -1.0,
                    jnp.where(~low & (ii == jj - ROPE_FREQS), 1.0, 0.0))
    rot = jnp.where((i >= HEAD_DIM) & (j >= HEAD_DIM), rot, 0.0)
    return jnp.where(ones, 1.0, rot).astype(BF16)


def _norm_rope(p, w, mat_ref, cos, sin, scale):
    pw = p * w
    lhs = jnp.concatenate([(p * p).astype(BF16), pw.astype(BF16)], axis=1)
    red = jnp.dot(lhs, mat_ref[...], preferred_element_type=F32)
    r = lax.rsqrt(red[:, 0:HEAD_DIM] * (1.0 / HEAD_DIM) + EPS) * scale
    return r * (pw * cos + red[:, HEAD_DIM:2 * HEAD_DIM] * sin)


def _inproj_kernel(x_ref, mod_ref, n1w_ref, w_ref, qnw_ref, knw_ref, lb_ref, cos_ref, sin_ref, mat_ref,
                   q_ref, k_ref, v_ref, hq_ref, kf_ref, bf_ref, kb_ref, bb_ref, hv_ref, gate_ref,
                   h_scr):
    m = pl.program_id(0)
    n = pl.program_id(1)
    bm = x_ref.shape[0]

    def modulated(rows, mod_row):
        x = x_ref[rows, :]
        r = lax.rsqrt(jnp.mean(x * x, axis=-1, keepdims=True) + EPS)
        gain = n1w_ref[...] * (1.0 + mod_ref[mod_row:mod_row + 1, D_MODEL:2 * D_MODEL])
        return (x * r * gain + mod_ref[mod_row:mod_row + 1, 0:D_MODEL]).astype(BF16)

    @pl.when(n == 0)
    def _():
        h_scr[...] = modulated(slice(None), 0)

    @pl.when((n == 0) & (m == 0))
    def _():
        h_scr[0:CTX_LEN, :] = modulated(slice(0, CTX_LEN), 1)

    def project(epilogue):
        ch = IN_COL_TILE // 2
        hpt = ch // HEAD_DIM
        rh = bm // 2
        ps = []
        for ri in range(2):
            rows = slice(ri * rh, (ri + 1) * rh)
            h = h_scr[rows, :]
            ps.append((rows, [jnp.dot(h, w_ref[:, a * ch:(a + 1) * ch], preferred_element_type=F32)
                              for a in range(2)]))
        for rows, halves in ps:
            for a, p in enumerate(halves):
                for j in range(hpt):
                    epilogue(a * hpt + j, rows, p[:, j * HEAD_DIM:(j + 1) * HEAD_DIM])

    def q_out(j, rows, p):
        q_ref[j, rows, :] = _norm_rope(p, qnw_ref[...], mat_ref, cos_ref[rows, :], sin_ref[rows, :],
                                       Q_SCALE).astype(BF16)

    def kv_out(j, rows, p):
        if j < ATTN_KV_HEADS:
            k_ref[j, rows, :] = _norm_rope(p, knw_ref[...], mat_ref, cos_ref[rows, :],
                                           sin_ref[rows, :], 1.0).astype(BF16)
        else:
            v_ref[j - ATTN_KV_HEADS, rows, :] = p.astype(BF16)

    def silu_out(ref):
        def out(j, rows, p):
            ref[j, rows, :] = _silu(p).astype(BF16)
        return out

    def forget_out(k_out, b_out, rev):
        def out(j, rows, p):
            lb = lb_ref[:, j * HEAD_DIM:(j + 1) * HEAD_DIM]
            fg = lb + (1.0 - lb) * jax.nn.sigmoid(p)
            k_out[j, rows, :] = (1.0 - fg).astype(BF16)
            b_out[j, rows, :] = _chunk_cumsum(jnp.log2(fg), rev)
        return out

    def plain_out(j, rows, p):
        hv_ref[j, rows, :] = p.astype(BF16)

    pl.when(n < 2)(lambda: project(q_out))
    pl.when(n == 2)(lambda: project(kv_out))
    pl.when((n == 3) | (n == 4))(lambda: project(silu_out(hq_ref)))
    pl.when((n == 5) | (n == 6))(lambda: project(forget_out(kf_ref, bf_ref, False)))
    pl.when((n == 7) | (n == 8))(lambda: project(forget_out(kb_ref, bb_ref, True)))
    pl.when((n == 9) | (n == 10))(lambda: project(plain_out))
    pl.when(n >= 11)(lambda: project(silu_out(gate_ref)))


def _input_projection(layer, xs, mod, n1w, w_in, qnw, knw, lb, cos_t, sin_t):
    bm, bn = ROW_TILE, IN_COL_TILE
    n_m, n_n = T_ROWS // bm, IN_WIDTH // bn
    hpt = bn // HEAD_DIM

    def head_spec(first_tile):
        return pl.BlockSpec((hpt, bm, HEAD_DIM),
                            lambda m, n: (jnp.clip(n - first_tile, 0, 1), m, 0))

    def head_shape(dtype):
        return jax.ShapeDtypeStruct((HG_HEADS, T_ROWS, HEAD_DIM), dtype)

    kv_spec = pl.BlockSpec((ATTN_KV_HEADS, bm, HEAD_DIM), lambda m, n: (0, m, 0))
    kv_shape = jax.ShapeDtypeStruct((ATTN_KV_HEADS, T_ROWS, HEAD_DIM), BF16)
    return pl.pallas_call(
        _inproj_kernel,
        out_shape=(head_shape(BF16), kv_shape, kv_shape, head_shape(BF16),
                   head_shape(BF16), head_shape(F32), head_shape(BF16), head_shape(F32),
                   head_shape(BF16), head_shape(BF16)),
        grid=(n_m, n_n),
        in_specs=[
            pl.BlockSpec((bm, D_MODEL), lambda m, n: (m, 0)),
            pl.BlockSpec((None, SUBLANES, 6 * D_MODEL), lambda m, n: (layer, 0, 0)),
            pl.BlockSpec((1, D_MODEL), lambda m, n: (0, 0)),
            pl.BlockSpec((None, D_MODEL, bn), lambda m, n: (layer, 0, n)),
            pl.BlockSpec((1, HEAD_DIM), lambda m, n: (0, 0)),
            pl.BlockSpec((1, HEAD_DIM), lambda m, n: (0, 0)),
            pl.BlockSpec((1, bn), lambda m, n: (0, jnp.clip(n - 5, 0, 3))),
            pl.BlockSpec((bm, HEAD_DIM), lambda m, n: (m, 0)),
            pl.BlockSpec((bm, HEAD_DIM), lambda m, n: (m, 0)),
            pl.BlockSpec((2 * HEAD_DIM, 2 * HEAD_DIM), lambda m, n: (0, 0)),
        ],
        out_specs=(head_spec(0), kv_spec, kv_spec, head_spec(3),
                   head_spec(5), head_spec(5), head_spec(7), head_spec(7),
                   head_spec(9), head_spec(11)),
        scratch_shapes=[pltpu.VMEM((bm, D_MODEL), BF16)],
        compiler_params=_cparams(2),
        name="input_projection",
    )(xs, mod, n1w, w_in, qnw, knw, lb, cos_t, sin_t, _norm_rope_matrix())


def _attn_kernel(q_ref, k_ref, vt_ref, o_ref, acc_scr, st_scr):
    i = pl.program_id(1)
    bq = q_ref.shape[1]
    bk = ATTN_K_TILE
    n_chunks = T_ROWS // bk

    def scores(kc, r, slot):
        part = kc.shape[0] // ATTN_QK_PARTS
        mx = None
        for a in range(ATTN_QK_PARTS):
            st = lax.dot_general(kc[a * part:(a + 1) * part], q_ref[r], (((1,), (1,)), ((), ())),
                                 preferred_element_type=F32)
            st_scr[slot, a * part:(a + 1) * part, :] = st
            m_part = st.max(axis=0, keepdims=True)
            mx = m_part if mx is None else jnp.maximum(mx, m_part)
        return mx

    def update(vtc, r, slot, m_cur, m, l):
        width = vtc.shape[1]
        m_new = jnp.maximum(m, m_cur)
        alpha = jnp.exp2(m - m_new)
        p = jnp.exp2(st_scr[slot, 0:width, :] - m_new)
        l = alpha * l + p.sum(axis=0, keepdims=True)
        acc_scr[r] = alpha * acc_scr[r] + jnp.dot(vtc, p.astype(BF16), preferred_element_type=F32)
        return m_new, l

    def finish(l):
        for r in range(ATTN_GROUP):
            o_ref[:, r * HEAD_DIM:(r + 1) * HEAD_DIM] = (acc_scr[r] / l[r]).T.astype(BF16)

    acc_scr[...] = jnp.zeros(acc_scr.shape, F32)
    m0 = tuple(jnp.full((1, bq), -jnp.inf, F32) for _ in range(ATTN_GROUP))
    l0 = tuple(jnp.zeros((1, bq), F32) for _ in range(ATTN_GROUP))

    @pl.when(i == 0)
    def _():
        kc = k_ref[0, 0:CTX_LEN, :]
        vtc = vt_ref[0, 0, :, 0:CTX_LEN]
        l = []
        mc = scores(kc, 0, 0)
        for r in range(ATTN_GROUP):
            mc_next = scores(kc, r + 1, (r + 1) % 2) if r + 1 < ATTN_GROUP else None
            l.append(update(vtc, r, r % 2, mc, m0[r], l0[r])[1])
            mc = mc_next
        finish(l)

    @pl.when(i > 0)
    def _():
        def chunk(j):
            return k_ref[0, pl.ds(pl.multiple_of(j * bk, bk), bk), :]

        def body(j, carry):
            mc, m, l = carry
            m, l = list(m), list(l)
            vtc = vt_ref[0, j]
            for r in range(ATTN_GROUP):
                if r + 1 < ATTN_GROUP:
                    mc_next = scores(chunk(j), r + 1, (r + 1) % 2)
                else:
                    mc_next = scores(chunk(jnp.minimum(j + 1, n_chunks - 1)), 0, 0)
                m[r], l[r] = update(vtc, r, r % 2, mc, m[r], l[r])
                mc = mc_next
            return mc, tuple(m), tuple(l)

        _, _, l = lax.fori_loop(0, n_chunks, body, (scores(chunk(0), 0, 0), m0, l0), unroll=True)
        finish(l)


def _attention(q, k, v):
    bq, bk = ATTN_Q_TILE, ATTN_K_TILE
    n_chunks = T_ROWS // bk
    vt = v.reshape(ATTN_KV_HEADS, n_chunks, bk, HEAD_DIM).transpose(0, 1, 3, 2)
    return pl.pallas_call(
        _attn_kernel,
        out_shape=jax.ShapeDtypeStruct((T_ROWS, ATTN_WIDTH), BF16),
        grid=(ATTN_KV_HEADS, T_ROWS // bq),
        in_specs=[
            pl.BlockSpec((ATTN_GROUP, bq, HEAD_DIM), lambda g, i: (g, i, 0)),
            pl.BlockSpec((1, T_ROWS, HEAD_DIM), lambda g, i: (g, 0, 0)),
            pl.BlockSpec((1, n_chunks, HEAD_DIM, bk), lambda g, i: (g, 0, 0, 0)),
        ],
        out_specs=pl.BlockSpec((bq, ATTN_GROUP * HEAD_DIM), lambda g, i: (i, g)),
        scratch_shapes=[pltpu.VMEM((ATTN_GROUP, HEAD_DIM, bq), F32),
                        pltpu.VMEM((2, bk, bq), F32)],
        compiler_params=_cparams(2),
        name="gqa_attention",
    )(q, k, vt)


def _chunk_cumsum(g, rev):
    n_groups = g.shape[0] // SUBLANES
    x = g.reshape(n_groups, SUBLANES, g.shape[1])
    sub = lax.broadcasted_iota(jnp.int32, x.shape, 1)
    step = 1
    while step < SUBLANES:
        if rev:
            x = x + jnp.where(sub < SUBLANES - step, pltpu.roll(x, SUBLANES - step, axis=1), 0.0)
        else:
            x = x + jnp.where(sub >= step, pltpu.roll(x, step, axis=1), 0.0)
        step *= 2
    per = HG_CHUNK // SUBLANES
    xs = [x[v] for v in range(n_groups)]
    for c0 in range(0, n_groups, per):
        if rev:
            for v in range(c0 + per - 2, c0 - 1, -1):
                xs[v] = xs[v] + xs[v + 1][0:1]
        else:
            for v in range(c0 + 1, c0 + per):
                xs[v] = xs[v] + xs[v - 1][SUBLANES - 1:SUBLANES]
    return jnp.concatenate(xs, axis=0)


def _hgrn_levels():
    halves = []
    half = HG_CHUNK // 2
    while half >= HG_DIAG:
        halves.append(half)
        half //= 2
    return halves


def _hgrn_consts(rev):
    c = HG_CHUNK
    levels = {}
    for half in _hgrn_levels():
        blk = 2 * half
        levels[half] = []
        for g in range(c // SUBLANES):
            t = lax.broadcasted_iota(jnp.int32, (SUBLANES, c), 0) + g * SUBLANES
            s = lax.broadcasted_iota(jnp.int32, (SUBLANES, c), 1)
            same = t // blk == s // blk
            if rev:
                levels[half].append(same & (t % blk < half) & (s % blk >= half))
            else:
                levels[half].append(same & (t % blk >= half) & (s % blk < half))
    t = lax.broadcasted_iota(jnp.int32, (c, c), 0)
    s = lax.broadcasted_iota(jnp.int32, (c, c), 1)
    band = []
    for d in range(HG_DIAG):
        if rev:
            band.append((s == t + d) & (t % HG_DIAG + d < HG_DIAG))
        else:
            band.append((s == t - d) & (t % HG_DIAG >= d))
    return levels, band


def _hgrn_scores(q, k, b, brow, rev, consts):
    levels, band = consts
    c = HG_CHUNK
    nt = (((1,), (1,)), ((), ()))
    groups = c // SUBLANES
    row8 = lax.broadcasted_iota(jnp.int32, (SUBLANES, HEAD_DIM), 0)
    rows = [slice(g * SUBLANES, (g + 1) * SUBLANES) for g in range(groups)]

    a = jnp.zeros((c, c), F32)
    for d in range(HG_DIAG):
        if d == 0:
            prod = q * k
        else:
            shift = c - d if rev else d
            prod = q * pltpu.roll(k, shift, axis=0) * jnp.exp2(b - pltpu.roll(b, shift, axis=0))
        a = jnp.where(band[d], prod.sum(axis=-1, keepdims=True), a)
    rows_a = [a[r] for r in rows]

    for half in _hgrn_levels():
        blk = 2 * half
        ref_off = half if rev else half - 1
        zs = []
        owners = []
        for g in range(groups):
            t0 = g * SUBLANES
            if half >= SUBLANES:
                ref = brow(t0 // blk * blk + ref_off)
                attending = (t0 % blk < half) if rev else (t0 % blk >= half)
                x = q[rows[g]] if attending else k[rows[g]]
                e = jnp.exp2(b[rows[g]] - ref if attending else ref - b[rows[g]])
                if attending:
                    owners.append(g)
            else:
                ref = brow(t0 + ref_off)
                for i in range(1, SUBLANES // blk):
                    ref = jnp.where(row8 >= i * blk, brow(t0 + i * blk + ref_off), ref)
                attending = (row8 % blk < half) if rev else (row8 % blk >= half)
                x = jnp.where(attending, q[rows[g]], k[rows[g]])
                e = jnp.exp2(-jnp.abs(b[rows[g]] - ref))
                owners.append(g)
            zs.append(x * e)
        z = jnp.concatenate(zs, axis=0).astype(BF16)
        lv = lax.dot_general(z, z, nt, preferred_element_type=F32)
        for g in owners:
            rows_a[g] = jnp.where(levels[half][g], lv[rows[g]], rows_a[g])
    return jnp.concatenate(rows_a, axis=0)


def _hgrn_kernel(qf_ref, kf_ref, bf_ref, vf_ref, qb_ref, kb_ref, bb_ref, vb_ref,
                 of_ref, ob_ref, s_scr):
    @pl.when(pl.program_id(0) == 0)
    def _():
        s_scr[...] = jnp.zeros(s_scr.shape, F32)

    c = HG_CHUNK
    n_chunks = HG_BLOCK // c
    consts = (_hgrn_consts(False), _hgrn_consts(True))
    nt = (((1,), (1,)), ((), ()))
    tn = (((0,), (0,)), ((), ()))
    dirs = ((qf_ref, kf_ref, bf_ref, vf_ref, of_ref, False),
            (qb_ref, kb_ref, bb_ref, vb_ref, ob_ref, True))

    def head_body(it, carry):
        jobs = []
        for hi in range(HG_ITER_HEADS):
            h = it * HG_ITER_HEADS + hi
            for di, (q_ref, k_ref, b_ref, v_ref, o_ref, rev) in enumerate(dirs):
                order = reversed(range(n_chunks)) if rev else range(n_chunks)
                for ci in order:
                    rows = slice(ci * c, (ci + 1) * c)
                    q = q_ref[h, rows, :].astype(F32)
                    k = k_ref[h, rows, :].astype(F32)
                    b = b_ref[h, rows, :]
                    v = v_ref[h, rows, :]
                    tot = b[0:1] if rev else b[c - 1:c]
                    q_in = (q * jnp.exp2(b)).astype(BF16)
                    k_out = (k * jnp.exp2(tot - b)).astype(BF16)
                    upd = lax.dot_general(v, k_out, tn, preferred_element_type=F32)

                    def brow(r, b_ref=b_ref, base=ci * c, h=h):
                        return b_ref[h, base + r:base + r + 1, :]

                    a = _hgrn_scores(q, k, b, brow, rev, consts[di]).astype(BF16)
                    jobs.append((hi, h, di, o_ref, rows, q_in, jnp.exp2(tot), upd, a, v))
        st = {(hi, di): s_scr[di, it * HG_ITER_HEADS + hi]
              for hi in range(HG_ITER_HEADS) for di in range(2)}
        for hi, h, di, o_ref, rows, q_in, decay, upd, a, v in jobs:
            o = lax.dot_general(q_in, st[hi, di].astype(BF16), nt, preferred_element_type=F32)
            st[hi, di] = decay * st[hi, di] + upd
            o_ref[h, rows, :] = o + jnp.dot(a, v, preferred_element_type=F32)
        for (hi, di), s in st.items():
            s_scr[di, it * HG_ITER_HEADS + hi] = s
        return carry

    lax.fori_loop(0, HG_HEADS // HG_ITER_HEADS, head_body, 0)


def _hgrn2(hq, kf, bf, kb, bb, hv):
    nb = T_ROWS // HG_BLOCK
    fwd = lambda i: (0, i, 0)
    bwd = lambda i: (0, jnp.where(i == 0, 0, nb - i), 0)
    blk = (HG_HEADS, HG_BLOCK, HEAD_DIM)
    out = jax.ShapeDtypeStruct((HG_HEADS, T_ROWS, HEAD_DIM), F32)
    return pl.pallas_call(
        _hgrn_kernel,
        out_shape=(out, out),
        grid=(nb,),
        in_specs=[pl.BlockSpec(blk, fwd)] * 4 + [pl.BlockSpec(blk, bwd)] * 4,
        out_specs=(pl.BlockSpec(blk, fwd), pl.BlockSpec(blk, bwd)),
        scratch_shapes=[pltpu.VMEM((2, HG_HEADS, HEAD_DIM, HEAD_DIM), F32)],
        compiler_params=_cparams(1),
        name="hgrn2_scan",
    )(hq, kf, bf, hv, hq, kb, bb, hv)


def _outproj_kernel(attn_ref, of_ref, ob_ref, gate_ref, hnw_ref, w_ref, x_ref, mod_ref, n2w_ref,
                    xo_ref, h2_ref, mix_scr):
    m = pl.program_id(0)
    bm = x_ref.shape[0]
    acc = jnp.dot(attn_ref[...], w_ref[0:ATTN_WIDTH, :], preferred_element_type=F32)
    for h in range(HG_HEADS):
        y = _rms(of_ref[h] + ob_ref[h], hnw_ref[...])
        mix_scr[:, h * HEAD_DIM:(h + 1) * HEAD_DIM] = (y * gate_ref[h].astype(F32)).astype(BF16)
    acc = acc + jnp.dot(mix_scr[...], w_ref[ATTN_WIDTH:ATTN_WIDTH + HG_WIDTH, :],
                        preferred_element_type=F32)

    def finish(rows, mod_row):
        def mod(k):
            return mod_ref[mod_row:mod_row + 1, k * D_MODEL:(k + 1) * D_MODEL]
        x_new = x_ref[rows, :] + mod(2) * acc[rows]
        xo_ref[rows, :] = x_new
        r = lax.rsqrt(jnp.mean(x_new * x_new, axis=-1, keepdims=True) + EPS)
        h2_ref[rows, :] = (x_new * r * (n2w_ref[...] * (1.0 + mod(4))) + mod(3)).astype(BF16)

    finish(slice(0, bm), 0)

    @pl.when(m == 0)
    def _():
        finish(slice(0, CTX_LEN), 1)


def _output_projection(layer, attn, o_f, o_b, gate, hnw, w_out, xs, mod, n2w):
    bm = OUT_ROW_TILE
    head_spec = pl.BlockSpec((HG_HEADS, bm, HEAD_DIM), lambda m: (0, m, 0))
    row_spec = pl.BlockSpec((bm, D_MODEL), lambda m: (m, 0))
    return pl.pallas_call(
        _outproj_kernel,
        out_shape=(jax.ShapeDtypeStruct((T_ROWS, D_MODEL), F32),
                   jax.ShapeDtypeStruct((T_ROWS, D_MODEL), BF16)),
        grid=(T_ROWS // bm,),
        in_specs=[
            pl.BlockSpec((bm, ATTN_WIDTH), lambda m: (m, 0)),
            head_spec, head_spec, head_spec,
            pl.BlockSpec((1, HEAD_DIM), lambda m: (0, 0)),
            pl.BlockSpec((None, ATTN_WIDTH + HG_WIDTH, D_MODEL), lambda m: (layer, 0, 0)),
            row_spec,
            pl.BlockSpec((None, SUBLANES, 6 * D_MODEL), lambda m: (layer, 0, 0)),
            pl.BlockSpec((1, D_MODEL), lambda m: (0, 0)),
        ],
        out_specs=(row_spec, row_spec),
        scratch_shapes=[pltpu.VMEM((bm, HG_WIDTH), BF16)],
        compiler_params=_cparams(1),
        name="output_projection",
    )(attn, o_f, o_b, gate, hnw, w_out, xs, mod, n2w)


def _ffn_up_kernel(h_ref, hp_ref, hn_ref, wg_ref, wu_ref, cw_ref, cb_ref, u_ref, hcat_scr, g_scr):
    m = pl.program_id(0)
    n = pl.program_id(1)
    bm = h_ref.shape[0]
    hl = CONV_HALO

    @pl.when(n == 0)
    def _():
        hcat_scr[0:hl, :] = hp_ref[...]
        hcat_scr[hl:hl + bm, :] = h_ref[...]
        hcat_scr[hl + bm:hl + bm + hl, :] = hn_ref[...]

    g_scr[...] = jnp.dot(hcat_scr[...], wg_ref[...], preferred_element_type=F32)
    up = jnp.dot(h_ref[...], wu_ref[...], preferred_element_type=F32)
    r = m * bm + lax.broadcasted_iota(jnp.int32, (bm, 1), 0)
    has_prev = (r != 0) & (r != CTX_LEN)
    has_next = (r != CTX_LEN - 1) & (r != T_ROWS - 1)
    g_prev = jnp.where(has_prev, g_scr[hl - 1:hl - 1 + bm, :], 0.0)
    g_next = jnp.where(has_next, g_scr[hl + 1:hl + 1 + bm, :], 0.0)
    gate = (g_prev * cw_ref[0:1, :] + g_scr[hl:hl + bm, :] * cw_ref[1:2, :]
            + g_next * cw_ref[2:3, :] + cb_ref[...])
    u_ref[...] = (_silu(gate) * up).astype(BF16)


def _ffn_up(layer, h2, w_up, conv_w, conv_b):
    bm, bn, hl = ROW_TILE, FF_COL_TILE, CONV_HALO
    n_n = D_FF // bn
    per = bm // hl
    last = T_ROWS // hl - 1
    return pl.pallas_call(
        _ffn_up_kernel,
        out_shape=jax.ShapeDtypeStruct((T_ROWS, D_FF), BF16),
        grid=(T_ROWS // bm, n_n),
        in_specs=[
            pl.BlockSpec((bm, D_MODEL), lambda m, n: (m, 0)),
            pl.BlockSpec((hl, D_MODEL), lambda m, n: (jnp.maximum(m * per - 1, 0), 0)),
            pl.BlockSpec((hl, D_MODEL), lambda m, n: (jnp.minimum((m + 1) * per, last), 0)),
            pl.BlockSpec((None, D_MODEL, bn), lambda m, n: (layer, 0, n)),
            pl.BlockSpec((None, D_MODEL, bn), lambda m, n: (layer, 0, n_n + n)),
            pl.BlockSpec((3, bn), lambda m, n: (0, n)),
            pl.BlockSpec((1, bn), lambda m, n: (0, n)),
        ],
        out_specs=pl.BlockSpec((bm, bn), lambda m, n: (m, n)),
        scratch_shapes=[pltpu.VMEM((bm + 2 * hl, D_MODEL), BF16),
                        pltpu.VMEM((bm + 2 * hl, bn), F32)],
        compiler_params=_cparams(2),
        name="ffn_up_conv",
    )(h2, h2, h2, w_up, w_up, conv_w, conv_b)


def _ffn_down_kernel(u_ref, w_ref, x_ref, mod_ref, fw_ref, o_ref, *, final):
    m = pl.program_id(0)

    def gate(mod_row):
        return mod_ref[mod_row:mod_row + 1, 5 * D_MODEL:6 * D_MODEL]

    def run():
        acc = jnp.dot(u_ref[...], w_ref[...], preferred_element_type=F32)
        x_new = x_ref[...] + gate(0) * acc
        o_ref[...] = _rms(x_new, fw_ref[...]) if final else x_new
        return acc

    if final:
        @pl.when(m > 0)
        def _():
            run()
    else:
        acc = run()

        @pl.when(m == 0)
        def _():
            o_ref[0:CTX_LEN, :] = x_ref[0:CTX_LEN, :] + gate(1) * acc[0:CTX_LEN]


def _ffn_down(layer, u, w_down, xs, mod, final_w, final):
    bm = CTX_LEN if final else OUT_ROW_TILE
    row_spec = pl.BlockSpec((bm, D_MODEL), lambda m: (m, 0))
    if final:
        out_rows = SEQ
        out_spec = pl.BlockSpec((bm, D_MODEL), lambda m: (jnp.maximum(m - 1, 0), 0))
    else:
        out_rows, out_spec = T_ROWS, row_spec
    return pl.pallas_call(
        functools.partial(_ffn_down_kernel, final=final),
        out_shape=jax.ShapeDtypeStruct((out_rows, D_MODEL), F32),
        grid=(T_ROWS // bm,),
        in_specs=[
            pl.BlockSpec((bm, D_FF), lambda m: (m, 0)),
            pl.BlockSpec((None, D_FF, D_MODEL), lambda m: (layer, 0, 0),
                         pipeline_mode=pl.Buffered(1)),
            row_spec,
            pl.BlockSpec((None, SUBLANES, 6 * D_MODEL), lambda m: (layer, 0, 0)),
            pl.BlockSpec((1, D_MODEL), lambda m: (0, 0)),
        ],
        out_specs=out_spec,
        compiler_params=_cparams(1),
        name="ffn_down",
    )(u, w_down, xs, mod, final_w)


def _rope_tables():
    f32 = np.float32
    rows = SEQ // GRID_W
    row = np.repeat(np.arange(rows, dtype=f32), GRID_W)
    col = np.tile(np.arange(GRID_W, dtype=f32), rows)
    freqs = np.power(f32(ROPE_THETA), -np.arange(ROPE_FREQS, dtype=f32) / f32(ROPE_FREQS)).astype(f32)
    ar, ac = row[:, None] * freqs, col[:, None] * freqs
    cos = np.concatenate([np.cos(ar), np.cos(ar), np.cos(ac), np.cos(ac)], axis=1)
    sin = np.concatenate([np.sin(ar), np.sin(ar), np.sin(ac), np.sin(ac)], axis=1)
    cos = np.concatenate([np.ones((CTX_LEN, HEAD_DIM), f32), cos], axis=0)
    sin = np.concatenate([np.zeros((CTX_LEN, HEAD_DIM), f32), sin], axis=0)
    return jnp.asarray(cos, F32), jnp.asarray(sin, F32)


def kernel(x, c, ctx, c_ctx, w_mod, b_mod, norm1_w, norm2_w, w_in, q_norm_w, k_norm_w, hg_lb_logits,
           hg_norm_w, w_out, w_up, conv_w, conv_b, w_down, final_norm_w):
    assert x.shape == (1, SEQ, D_MODEL) and ctx.shape == (1, CTX_LEN, D_MODEL)
    xs = jnp.concatenate([ctx[0], x[0]], axis=0)
    mod = _modulation(jnp.stack([c[0], c_ctx], axis=1), w_mod, b_mod)
    cos_t, sin_t = _rope_tables()
    lb_sm = jax.nn.softmax(hg_lb_logits.astype(F32), axis=1)
    lb_all = jnp.cumsum(lb_sm, axis=1) - lb_sm[:, :1]

    w_in, w_out, w_up, w_down = (w.astype(BF16) for w in (w_in, w_out, w_up, w_down))
    for l in range(DEPTH):
        lb = jnp.concatenate([lb_all[0, l], lb_all[1, l]])[None, :]
        q, k, v, hq, kf, bf, kb, bb, hv, gate = _input_projection(
            l, xs, mod, norm1_w[l][None, :], w_in, q_norm_w[l][None, :],
            k_norm_w[l][None, :], lb, cos_t, sin_t)
        attn = _attention(q, k, v)
        o_f, o_b = _hgrn2(hq, kf, bf, kb, bb, hv)
        xs, h2 = _output_projection(l, attn, o_f, o_b, gate, hg_norm_w[l][None, :],
                                    w_out, xs, mod, norm2_w[l][None, :])
        u = _ffn_up(l, h2, w_up, conv_w[l], conv_b[l][None, :])
        xs = _ffn_down(l, u, w_down, xs, mod, final_norm_w[None, :], final=(l == DEPTH - 1))
    return xs[None]
```

```python
import functools

import jax
import jax.numpy as jnp
import numpy as np
from jax import lax
from jax.experimental import pallas as pl
from jax.experimental.pallas import tpu as pltpu

F32 = jnp.float32
BF16 = jnp.bfloat16

D_MODEL = 2048
SEQ = 8192
DEPTH = 4
CTX_LEN = 256
T_ROWS = CTX_LEN + SEQ
GRID_W = 64
HEAD_DIM = 128
ATTN_HEADS = 8
ATTN_KV_HEADS = 2
ATTN_GROUP = 4
ATTN_WIDTH = 1024
KV_WIDTH = 256
HG_HEADS = 8
HG_WIDTH = 1024
IN_WIDTH = 6656
ROPE_THETA = 10000.0
ROPE_FREQS = 32
D_FF = 5632
EPS = 1e-6
Q_SCALE = HEAD_DIM ** -0.5 * 1.4426950408889634

LANES = 128
SUBLANES = 8
VMEM_LIMIT = 56 * 1024 * 1024

MOD_SLAB = 256
ROW_TILE = 768
OUT_ROW_TILE = 384
IN_COL_TILE = 512
FF_COL_TILE = 512
ATTN_Q_TILE = 256
ATTN_K_TILE = 1408
ATTN_QK_PARTS = 2
HG_BLOCK = 256
HG_ITER_HEADS = 4
HG_CHUNK = 64
HG_DIAG = 1
CONV_HALO = 16


def _cparams(n_axes):
    return pltpu.CompilerParams(dimension_semantics=("arbitrary",) * n_axes,
                                vmem_limit_bytes=VMEM_LIMIT)


def _rms(x, w):
    return x * lax.rsqrt(jnp.mean(x * x, axis=-1, keepdims=True) + EPS) * w


def _silu(x):
    h = 0.5 * x
    return h * jnp.tanh(h) + h


def _mod_kernel(s_ref, w_ref, b_ref, o_ref, acc_scr):
    k = pl.program_id(1)
    n_out = w_ref.shape[2]
    cw = 4 * LANES

    @pl.when(k == 0)
    def _():
        acc_scr[...] = jnp.zeros(acc_scr.shape, F32)

    r0 = pl.multiple_of(k * MOD_SLAB, MOD_SLAB)
    s = _silu(s_ref[pl.ds(r0, MOD_SLAB), :])
    sb = [jnp.tile(jnp.broadcast_to(s[:, i:i + 1], (MOD_SLAB, LANES)), (1, cw // LANES))
          for i in range(2)]
    for c0 in range(0, n_out, cw):
        w = w_ref[0, :, c0:c0 + cw]
        for i in range(2):
            part = (w * sb[i]).reshape(MOD_SLAB // SUBLANES, SUBLANES, cw).sum(axis=0)
            acc_scr[i, :, c0:c0 + cw] += part

    @pl.when(k == pl.num_programs(1) - 1)
    def _():
        bias = b_ref[0]
        rows = [acc_scr[i].sum(axis=0, keepdims=True) + bias for i in range(2)]
        o_ref[0] = jnp.concatenate(rows + [jnp.zeros((SUBLANES - 2, n_out), F32)], axis=0)


def _modulation(s_in, w_mod, b_mod):
    n_out = 6 * D_MODEL
    return pl.pallas_call(
        _mod_kernel,
        out_shape=jax.ShapeDtypeStruct((DEPTH, SUBLANES, n_out), F32),
        grid=(DEPTH, D_MODEL // MOD_SLAB),
        in_specs=[
            pl.BlockSpec((D_MODEL, 2), lambda l, k: (0, 0)),
            pl.BlockSpec((1, MOD_SLAB, n_out), lambda l, k: (l, k, 0)),
            pl.BlockSpec((1, 1, n_out), lambda l, k: (l, 0, 0)),
        ],
        out_specs=pl.BlockSpec((1, SUBLANES, n_out), lambda l, k: (l, 0, 0)),
        scratch_shapes=[pltpu.VMEM((2, SUBLANES, n_out), F32)],
        compiler_params=_cparams(2),
        name="adaln_modulation",
    )(s_in, w_mod, b_mod.reshape(DEPTH, 1, n_out))


def _norm_rope_matrix():
    i = lax.broadcasted_iota(jnp.int32, (2 * HEAD_DIM, 2 * HEAD_DIM), 0)
    j = lax.broadcasted_iota(jnp.int32, (2 * HEAD_DIM, 2 * HEAD_DIM), 1)
    ones = (i < HEAD_DIM) & (j < HEAD_DIM)
    ii, jj = i - HEAD_DIM, j - HEAD_DIM
    low = (jj // ROPE_FREQS) % 2 == 0
    rot = jnp.where(low & (ii == jj + ROPE_FREQS), -1.0,
                    jnp.where(~low & (ii == jj - ROPE_FREQS), 1.0, 0.0))
    rot = jnp.where((i >= HEAD_DIM) & (j >= HEAD_DIM), rot, 0.0)
    return jnp.where(ones, 1.0, rot).astype(BF16)


def _norm_rope(p, w, mat_ref, cos, sin, scale):
    pw = p * w
    lhs = jnp.concatenate([(p * p).astype(BF16), pw.astype(BF16)], axis=1)
    red = jnp.dot(lhs, mat_ref[...], preferred_element_type=F32)
    r = lax.rsqrt(red[:, 0:HEAD_DIM] * (1.0 / HEAD_DIM) + EPS) * scale
    return r * (pw * cos + red[:, HEAD_DIM:2 * HEAD_DIM] * sin)


def _inproj_kernel(x_ref, mod_ref, n1w_ref, w_ref, qnw_ref, knw_ref, lb_ref, cos_ref, sin_ref, mat_ref,
                   q_ref, k_ref, v_ref, hq_ref, kf_ref, bf_ref, kb_ref, bb_ref, hv_ref, gate_ref,
                   h_scr):
    m = pl.program_id(0)
    n = pl.program_id(1)
    bm = x_ref.shape[0]

    def modulated(rows, mod_row):
        x = x_ref[rows, :]
        r = lax.rsqrt(jnp.mean(x * x, axis=-1, keepdims=True) + EPS)
        gain = n1w_ref[...] * (1.0 + mod_ref[mod_row:mod_row + 1, D_MODEL:2 * D_MODEL])
        return (x * r * gain + mod_ref[mod_row:mod_row + 1, 0:D_MODEL]).astype(BF16)

    @pl.when(n == 0)
    def _():
        h_scr[...] = modulated(slice(None), 0)

    @pl.when((n == 0) & (m == 0))
    def _():
        h_scr[0:CTX_LEN, :] = modulated(slice(0, CTX_LEN), 1)

    def project(epilogue):
        ch = IN_COL_TILE // 2
        hpt = ch // HEAD_DIM
        rh = bm // 2
        ps = []
        for ri in range(2):
            rows = slice(ri * rh, (ri + 1) * rh)
            h = h_scr[rows, :]
            ps.append((rows, [jnp.dot(h, w_ref[:, a * ch:(a + 1) * ch], preferred_element_type=F32)
                              for a in range(2)]))
        for rows, halves in ps:
            for a, p in enumerate(halves):
                for j in range(hpt):
                    epilogue(a * hpt + j, rows, p[:, j * HEAD_DIM:(j + 1) * HEAD_DIM])

    def q_out(j, rows, p):
        q_ref[j, rows, :] = _norm_rope(p, qnw_ref[...], mat_ref, cos_ref[rows, :], sin_ref[rows, :],
                                       Q_SCALE).astype(BF16)

    def kv_out(j, rows, p):
        if j < ATTN_KV_HEADS:
            k_ref[j, rows, :] = _norm_rope(p, knw_ref[...], mat_ref, cos_ref[rows, :],
                                           sin_ref[rows, :], 1.0).astype(BF16)
        else:
            v_ref[j - ATTN_KV_HEADS, rows, :] = p.astype(BF16)

    def silu_out(ref):
        def out(j, rows, p):
            ref[j, rows, :] = _silu(p).astype(BF16)
        return out

    def forget_out(k_out, b_out, rev):
        def out(j, rows, p):
            lb = lb_ref[:, j * HEAD_DIM:(j + 1) * HEAD_DIM]
            fg = lb + (1.0 - lb) * jax.nn.sigmoid(p)
            k_out[j, rows, :] = (1.0 - fg).astype(BF16)
            b_out[j, rows, :] = _chunk_cumsum(jnp.log2(fg), rev)
        return out

    def plain_out(j, rows, p):
        hv_ref[j, rows, :] = p.astype(BF16)

    pl.when(n < 2)(lambda: project(q_out))
    pl.when(n == 2)(lambda: project(kv_out))
    pl.when((n == 3) | (n == 4))(lambda: project(silu_out(hq_ref)))
    pl.when((n == 5) | (n == 6))(lambda: project(forget_out(kf_ref, bf_ref, False)))
    pl.when((n == 7) | (n == 8))(lambda: project(forget_out(kb_ref, bb_ref, True)))
    pl.when((n == 9) | (n == 10))(lambda: project(plain_out))
    pl.when(n >= 11)(lambda: project(silu_out(gate_ref)))


def _input_projection(layer, xs, mod, n1w, w_in, qnw, knw, lb, cos_t, sin_t):
    bm, bn = ROW_TILE, IN_COL_TILE
    n_m, n_n = T_ROWS // bm, IN_WIDTH // bn
    hpt = bn // HEAD_DIM

    def head_spec(first_tile):
        return pl.BlockSpec((hpt, bm, HEAD_DIM),
                            lambda m, n: (jnp.clip(n - first_tile, 0, 1), m, 0))

    def head_shape(dtype):
        return jax.ShapeDtypeStruct((HG_HEADS, T_ROWS, HEAD_DIM), dtype)

    kv_spec = pl.BlockSpec((ATTN_KV_HEADS, bm, HEAD_DIM), lambda m, n: (0, m, 0))
    kv_shape = jax.ShapeDtypeStruct((ATTN_KV_HEADS, T_ROWS, HEAD_DIM), BF16)
    return pl.pallas_call(
        _inproj_kernel,
        out_shape=(head_shape(BF16), kv_shape, kv_shape, head_shape(BF16),
                   head_shape(BF16), head_shape(F32), head_shape(BF16), head_shape(F32),
                   head_shape(BF16), head_shape(BF16)),
        grid=(n_m, n_n),
        in_specs=[
            pl.BlockSpec((bm, D_MODEL), lambda m, n: (m, 0)),
            pl.BlockSpec((None, SUBLANES, 6 * D_MODEL), lambda m, n: (layer, 0, 0)),
            pl.BlockSpec((1, D_MODEL), lambda m, n: (0, 0)),
            pl.BlockSpec((None, D_MODEL, bn), lambda m, n: (layer, 0, n)),
            pl.BlockSpec((1, HEAD_DIM), lambda m, n: (0, 0)),
            pl.BlockSpec((1, HEAD_DIM), lambda m, n: (0, 0)),
            pl.BlockSpec((1, bn), lambda m, n: (0, jnp.clip(n - 5, 0, 3))),
            pl.BlockSpec((bm, HEAD_DIM), lambda m, n: (m, 0)),
            pl.BlockSpec((bm, HEAD_DIM), lambda m, n: (m, 0)),
            pl.BlockSpec((2 * HEAD_DIM, 2 * HEAD_DIM), lambda m, n: (0, 0)),
        ],
        out_specs=(head_spec(0), kv_spec, kv_spec, head_spec(3),
                   head_spec(5), head_spec(5), head_spec(7), head_spec(7),
                   head_spec(9), head_spec(11)),
        scratch_shapes=[pltpu.VMEM((bm, D_MODEL), BF16)],
        compiler_params=_cparams(2),
        name="input_projection",
    )(xs, mod, n1w, w_in, qnw, knw, lb, cos_t, sin_t, _norm_rope_matrix())


def _attn_kernel(q_ref, k_ref, vt_ref, o_ref, acc_scr, st_scr):
    i = pl.program_id(1)
    bq = q_ref.shape[1]
    bk = ATTN_K_TILE
    n_chunks = T_ROWS // bk

    def scores(kc, r, slot):
        part = kc.shape[0] // ATTN_QK_PARTS
        mx = None
        for a in range(ATTN_QK_PARTS):
            st = lax.dot_general(kc[a * part:(a + 1) * part], q_ref[r], (((1,), (1,)), ((), ())),
                                 preferred_element_type=F32)
            st_scr[slot, a * part:(a + 1) * part, :] = st
            m_part = st.max(axis=0, keepdims=True)
            mx = m_part if mx is None else jnp.maximum(mx, m_part)
        return mx

    def update(vtc, r, slot, m_cur, m, l):
        width = vtc.shape[1]
        m_new = jnp.maximum(m, m_cur)
        alpha = jnp.exp2(m - m_new)
        p = jnp.exp2(st_scr[slot, 0:width, :] - m_new)
        l = alpha * l + p.sum(axis=0, keepdims=True)
        acc_scr[r] = alpha * acc_scr[r] + jnp.dot(vtc, p.astype(BF16), preferred_element_type=F32)
        return m_new, l

    def finish(l):
        for r in range(ATTN_GROUP):
            o_ref[:, r * HEAD_DIM:(r + 1) * HEAD_DIM] = (acc_scr[r] / l[r]).T.astype(BF16)

    acc_scr[...] = jnp.zeros(acc_scr.shape, F32)
    m0 = tuple(jnp.full((1, bq), -jnp.inf, F32) for _ in range(ATTN_GROUP))
    l0 = tuple(jnp.zeros((1, bq), F32) for _ in range(ATTN_GROUP))

    @pl.when(i == 0)
    def _():
        kc = k_ref[0, 0:CTX_LEN, :]
        vtc = vt_ref[0, 0, :, 0:CTX_LEN]
        l = []
        mc = scores(kc, 0, 0)
        for r in range(ATTN_GROUP):
            mc_next = scores(kc, r + 1, (r + 1) % 2) if r + 1 < ATTN_GROUP else None
            l.append(update(vtc, r, r % 2, mc, m0[r], l0[r])[1])
            mc = mc_next
        finish(l)

    @pl.when(i > 0)
    def _():
        def chunk(j):
            return k_ref[0, pl.ds(pl.multiple_of(j * bk, bk), bk), :]

        def body(j, carry):
            mc, m, l = carry
            m, l = list(m), list(l)
            vtc = vt_ref[0, j]
            for r in range(ATTN_GROUP):
                if r + 1 < ATTN_GROUP:
                    mc_next = scores(chunk(j), r + 1, (r + 1) % 2)
                else:
                    mc_next = scores(chunk(jnp.minimum(j + 1, n_chunks - 1)), 0, 0)
                m[r], l[r] = update(vtc, r, r % 2, mc, m[r], l[r])
                mc = mc_next
            return mc, tuple(m), tuple(l)

        _, _, l = lax.fori_loop(0, n_chunks, body, (scores(chunk(0), 0, 0), m0, l0), unroll=True)
        finish(l)


def _attention(q, k, v):
    bq, bk = ATTN_Q_TILE, ATTN_K_TILE
    n_chunks = T_ROWS // bk
    vt = v.reshape(ATTN_KV_HEADS, n_chunks, bk, HEAD_DIM).transpose(0, 1, 3, 2)
    return pl.pallas_call(
        _attn_kernel,
        out_shape=jax.ShapeDtypeStruct((T_ROWS, ATTN_WIDTH), BF16),
        grid=(ATTN_KV_HEADS, T_ROWS // bq),
        in_specs=[
            pl.BlockSpec((ATTN_GROUP, bq, HEAD_DIM), lambda g, i: (g, i, 0)),
            pl.BlockSpec((1, T_ROWS, HEAD_DIM), lambda g, i: (g, 0, 0)),
            pl.BlockSpec((1, n_chunks, HEAD_DIM, bk), lambda g, i: (g, 0, 0, 0)),
        ],
        out_specs=pl.BlockSpec((bq, ATTN_GROUP * HEAD_DIM), lambda g, i: (i, g)),
        scratch_shapes=[pltpu.VMEM((ATTN_GROUP, HEAD_DIM, bq), F32),
                        pltpu.VMEM((2, bk, bq), F32)],
        compiler_params=_cparams(2),
        name="gqa_attention",
    )(q, k, vt)


def _chunk_cumsum(g, rev):
    n_groups = g.shape[0] // SUBLANES
    x = g.reshape(n_groups, SUBLANES, g.shape[1])
    sub = lax.broadcasted_iota(jnp.int32, x.shape, 1)
    step = 1
    while step < SUBLANES:
        if rev:
            x = x + jnp.where(sub < SUBLANES - step, pltpu.roll(x, SUBLANES - step, axis=1), 0.0)
        else:
            x = x + jnp.where(sub >= step, pltpu.roll(x, step, axis=1), 0.0)
        step *= 2
    per = HG_CHUNK // SUBLANES
    xs = [x[v] for v in range(n_groups)]
    for c0 in range(0, n_groups, per):
        if rev:
            for v in range(c0 + per - 2, c0 - 1, -1):
                xs[v] = xs[v] + xs[v + 1][0:1]
        else:
            for v in range(c0 + 1, c0 + per):
                xs[v] = xs[v] + xs[v - 1][SUBLANES - 1:SUBLANES]
    return jnp.concatenate(xs, axis=0)


def _hgrn_levels():
    halves = []
    half = HG_CHUNK // 2
    while half >= HG_DIAG:
        halves.append(half)
        half //= 2
    return halves


def _hgrn_consts(rev):
    c = HG_CHUNK
    levels = {}
    for half in _hgrn_levels():
        blk = 2 * half
        levels[half] = []
        for g in range(c // SUBLANES):
            t = lax.broadcasted_iota(jnp.int32, (SUBLANES, c), 0) + g * SUBLANES
            s = lax.broadcasted_iota(jnp.int32, (SUBLANES, c), 1)
            same = t // blk == s // blk
            if rev:
                levels[half].append(same & (t % blk < half) & (s % blk >= half))
            else:
                levels[half].append(same & (t % blk >= half) & (s % blk < half))
    t = lax.broadcasted_iota(jnp.int32, (c, c), 0)
    s = lax.broadcasted_iota(jnp.int32, (c, c), 1)
    band = []
    for d in range(HG_DIAG):
        if rev:
            band.append((s == t + d) & (t % HG_DIAG + d < HG_DIAG))
        else:
            band.append((s == t - d) & (t % HG_DIAG >= d))
    return levels, band


def _hgrn_scores(q, k, b, brow, rev, consts):
    levels, band = consts
    c = HG_CHUNK
    nt = (((1,), (1,)), ((), ()))
    groups = c // SUBLANES
    row8 = lax.broadcasted_iota(jnp.int32, (SUBLANES, HEAD_DIM), 0)
    rows = [slice(g * SUBLANES, (g + 1) * SUBLANES) for g in range(groups)]

    a = jnp.zeros((c, c), F32)
    for d in range(HG_DIAG):
        if d == 0:
            prod = q * k
        else:
            shift = c - d if rev else d
            prod = q * pltpu.roll(k, shift, axis=0) * jnp.exp2(b - pltpu.roll(b, shift, axis=0))
        a = jnp.where(band[d], prod.sum(axis=-1, keepdims=True), a)
    rows_a = [a[r] for r in rows]

    for half in _hgrn_levels():
        blk = 2 * half
        ref_off = half if rev else half - 1
        zs = []
        owners = []
        for g in range(groups):
            t0 = g * SUBLANES
            if half == 1:
                attending = (row8 % 2 == 0) if rev else (row8 % 2 == 1)
                zs.append(jnp.where(attending, q[rows[g]] * (1.0 - k[rows[g]]), k[rows[g]]))
                owners.append(g)
                continue
            if half >= SUBLANES:
                ref = brow(t0 // blk * blk + ref_off)
                attending = (t0 % blk < half) if rev else (t0 % blk >= half)
                x = q[rows[g]] if attending else k[rows[g]]
                e = jnp.exp2(b[rows[g]] - ref if attending else ref - b[rows[g]])
                if attending:
                    owners.append(g)
            else:
                ref = brow(t0 + ref_off)
                for i in range(1, SUBLANES // blk):
                    ref = jnp.where(row8 >= i * blk, brow(t0 + i * blk + ref_off), ref)
                attending = (row8 % blk < half) if rev else (row8 % blk >= half)
                x = jnp.where(attending, q[rows[g]], k[rows[g]])
                e = jnp.exp2(-jnp.abs(b[rows[g]] - ref))
                owners.append(g)
            zs.append(x * e)
        z = jnp.concatenate(zs, axis=0).astype(BF16)
        lv = lax.dot_general(z, z, nt, preferred_element_type=F32)
        for g in owners:
            rows_a[g] = jnp.where(levels[half][g], lv[rows[g]], rows_a[g])
    return jnp.concatenate(rows_a, axis=0)


def _hgrn_kernel(qf_ref, kf_ref, bf_ref, vf_ref, qb_ref, kb_ref, bb_ref, vb_ref,
                 of_ref, ob_ref, s_scr):
    @pl.when(pl.program_id(0) == 0)
    def _():
        s_scr[...] = jnp.zeros(s_scr.shape, F32)

    c = HG_CHUNK
    n_chunks = HG_BLOCK // c
    consts = (_hgrn_consts(False), _hgrn_consts(True))
    nt = (((1,), (1,)), ((), ()))
    tn = (((0,), (0,)), ((), ()))
    dirs = ((qf_ref, kf_ref, bf_ref, vf_ref, of_ref, False),
            (qb_ref, kb_ref, bb_ref, vb_ref, ob_ref, True))

    def head_body(it, carry):
        jobs = []
        for hi in range(HG_ITER_HEADS):
            h = it * HG_ITER_HEADS + hi
            for di, (q_ref, k_ref, b_ref, v_ref, o_ref, rev) in enumerate(dirs):
                order = reversed(range(n_chunks)) if rev else range(n_chunks)
                for ci in order:
                    rows = slice(ci * c, (ci + 1) * c)
                    q = q_ref[h, rows, :].astype(F32)
                    k = k_ref[h, rows, :].astype(F32)
                    b = b_ref[h, rows, :]
                    v = v_ref[h, rows, :]
                    tot = b[0:1] if rev else b[c - 1:c]
                    q_in = (q * jnp.exp2(b)).astype(BF16)
                    k_out = (k * jnp.exp2(tot - b)).astype(BF16)
                    upd = lax.dot_general(v, k_out, tn, preferred_element_type=F32)

                    def brow(r, b_ref=b_ref, base=ci * c, h=h):
                        return b_ref[h, base + r:base + r + 1, :]

                    a = _hgrn_scores(q, k, b, brow, rev, consts[di]).astype(BF16)
                    jobs.append((hi, h, di, o_ref, rows, q_in, jnp.exp2(tot), upd, a, v))
        st = {(hi, di): s_scr[di, it * HG_ITER_HEADS + hi]
              for hi in range(HG_ITER_HEADS) for di in range(2)}
        for hi, h, di, o_ref, rows, q_in, decay, upd, a, v in jobs:
            o = lax.dot_general(q_in, st[hi, di].astype(BF16), nt, preferred_element_type=F32)
            st[hi, di] = decay * st[hi, di] + upd
            o_ref[h, rows, :] = o + jnp.dot(a, v, preferred_element_type=F32)
        for (hi, di), s in st.items():
            s_scr[di, it * HG_ITER_HEADS + hi] = s
        return carry

    lax.fori_loop(0, HG_HEADS // HG_ITER_HEADS, head_body, 0)


def _hgrn2(hq, kf, bf, kb, bb, hv):
    nb = T_ROWS // HG_BLOCK
    fwd = lambda i: (0, i, 0)
    bwd = lambda i: (0, jnp.where(i == 0, 0, nb - i), 0)
    blk = (HG_HEADS, HG_BLOCK, HEAD_DIM)
    out = jax.ShapeDtypeStruct((HG_HEADS, T_ROWS, HEAD_DIM), F32)
    return pl.pallas_call(
        _hgrn_kernel,
        out_shape=(out, out),
        grid=(nb,),
        in_specs=[pl.BlockSpec(blk, fwd)] * 4 + [pl.BlockSpec(blk, bwd)] * 4,
        out_specs=(pl.BlockSpec(blk, fwd), pl.BlockSpec(blk, bwd)),
        scratch_shapes=[pltpu.VMEM((2, HG_HEADS, HEAD_DIM, HEAD_DIM), F32)],
        compiler_params=_cparams(1),
        name="hgrn2_scan",
    )(hq, kf, bf, hv, hq, kb, bb, hv)


def _outproj_kernel(attn_ref, of_ref, ob_ref, gate_ref, hnw_ref, w_ref, x_ref, mod_ref, n2w_ref,
                    xo_ref, h2_ref, mix_scr):
    m = pl.program_id(0)
    bm = x_ref.shape[0]
    acc = jnp.dot(attn_ref[...], w_ref[0:ATTN_WIDTH, :], preferred_element_type=F32)
    for h in range(HG_HEADS):
        y = _rms(of_ref[h] + ob_ref[h], hnw_ref[...])
        mix_scr[:, h * HEAD_DIM:(h + 1) * HEAD_DIM] = (y * gate_ref[h].astype(F32)).astype(BF16)
    acc = acc + jnp.dot(mix_scr[...], w_ref[ATTN_WIDTH:ATTN_WIDTH + HG_WIDTH, :],
                        preferred_element_type=F32)

    def finish(rows, mod_row):
        def mod(k):
            return mod_ref[mod_row:mod_row + 1, k * D_MODEL:(k + 1) * D_MODEL]
        x_new = x_ref[rows, :] + mod(2) * acc[rows]
        xo_ref[rows, :] = x_new
        r = lax.rsqrt(jnp.mean(x_new * x_new, axis=-1, keepdims=True) + EPS)
        h2_ref[rows, :] = (x_new * r * (n2w_ref[...] * (1.0 + mod(4))) + mod(3)).astype(BF16)

    finish(slice(0, bm), 0)

    @pl.when(m == 0)
    def _():
        finish(slice(0, CTX_LEN), 1)


def _output_projection(layer, attn, o_f, o_b, gate, hnw, w_out, xs, mod, n2w):
    bm = OUT_ROW_TILE
    head_spec = pl.BlockSpec((HG_HEADS, bm, HEAD_DIM), lambda m: (0, m, 0))
    row_spec = pl.BlockSpec((bm, D_MODEL), lambda m: (m, 0))
    return pl.pallas_call(
        _outproj_kernel,
        out_shape=(jax.ShapeDtypeStruct((T_ROWS, D_MODEL), F32),
                   jax.ShapeDtypeStruct((T_ROWS, D_MODEL), BF16)),
        grid=(T_ROWS // bm,),
        in_specs=[
            pl.BlockSpec((bm, ATTN_WIDTH), lambda m: (m, 0)),
            head_spec, head_spec, head_spec,
            pl.BlockSpec((1, HEAD_DIM), lambda m: (0, 0)),
            pl.BlockSpec((None, ATTN_WIDTH + HG_WIDTH, D_MODEL), lambda m: (layer, 0, 0)),
            row_spec,
            pl.BlockSpec((None, SUBLANES, 6 * D_MODEL), lambda m: (layer, 0, 0)),
            pl.BlockSpec((1, D_MODEL), lambda m: (0, 0)),
        ],
        out_specs=(row_spec, row_spec),
        scratch_shapes=[pltpu.VMEM((bm, HG_WIDTH), BF16)],
        compiler_params=_cparams(1),
        name="output_projection",
    )(attn, o_f, o_b, gate, hnw, w_out, xs, mod, n2w)


def _ffn_up_kernel(h_ref, hp_ref, hn_ref, wg_ref, wu_ref, cw_ref, cb_ref, u_ref, hcat_scr, g_scr):
    m = pl.program_id(0)
    n = pl.program_id(1)
    bm = h_ref.shape[0]
    hl = CONV_HALO

    @pl.when(n == 0)
    def _():
        hcat_scr[0:hl, :] = hp_ref[...]
        hcat_scr[hl:hl + bm, :] = h_ref[...]
        hcat_scr[hl + bm:hl + bm + hl, :] = hn_ref[...]

    g_scr[...] = jnp.dot(hcat_scr[...], wg_ref[...], preferred_element_type=F32)
    up = jnp.dot(h_ref[...], wu_ref[...], preferred_element_type=F32)
    r = m * bm + lax.broadcasted_iota(jnp.int32, (bm, 1), 0)
    has_prev = (r != 0) & (r != CTX_LEN)
    has_next = (r != CTX_LEN - 1) & (r != T_ROWS - 1)
    g_prev = jnp.where(has_prev, g_scr[hl - 1:hl - 1 + bm, :], 0.0)
    g_next = jnp.where(has_next, g_scr[hl + 1:hl + 1 + bm, :], 0.0)
    gate = (g_prev * cw_ref[0:1, :] + g_scr[hl:hl + bm, :] * cw_ref[1:2, :]
            + g_next * cw_ref[2:3, :] + cb_ref[...])
    u_ref[...] = (_silu(gate) * up).astype(BF16)


def _ffn_up(layer, h2, w_up, conv_w, conv_b):
    bm, bn, hl = ROW_TILE, FF_COL_TILE, CONV_HALO
    n_n = D_FF // bn
    per = bm // hl
    last = T_ROWS // hl - 1
    return pl.pallas_call(
        _ffn_up_kernel,
        out_shape=jax.ShapeDtypeStruct((T_ROWS, D_FF), BF16),
        grid=(T_ROWS // bm, n_n),
        in_specs=[
            pl.BlockSpec((bm, D_MODEL), lambda m, n: (m, 0)),
            pl.BlockSpec((hl, D_MODEL), lambda m, n: (jnp.maximum(m * per - 1, 0), 0)),
            pl.BlockSpec((hl, D_MODEL), lambda m, n: (jnp.minimum((m + 1) * per, last), 0)),
            pl.BlockSpec((None, D_MODEL, bn), lambda m, n: (layer, 0, n)),
            pl.BlockSpec((None, D_MODEL, bn), lambda m, n: (layer, 0, n_n + n)),
            pl.BlockSpec((3, bn), lambda m, n: (0, n)),
            pl.BlockSpec((1, bn), lambda m, n: (0, n)),
        ],
        out_specs=pl.BlockSpec((bm, bn), lambda m, n: (m, n)),
        scratch_shapes=[pltpu.VMEM((bm + 2 * hl, D_MODEL), BF16),
                        pltpu.VMEM((bm + 2 * hl, bn), F32)],
        compiler_params=_cparams(2),
        name="ffn_up_conv",
    )(h2, h2, h2, w_up, w_up, conv_w, conv_b)


def _ffn_down_kernel(u_ref, w_ref, x_ref, mod_ref, fw_ref, o_ref, *, final):
    m = pl.program_id(0)

    def gate(mod_row):
        return mod_ref[mod_row:mod_row + 1, 5 * D_MODEL:6 * D_MODEL]

    def run():
        acc = jnp.dot(u_ref[...], w_ref[...], preferred_element_type=F32)
        x_new = x_ref[...] + gate(0) * acc
        o_ref[...] = _rms(x_new, fw_ref[...]) if final else x_new
        return acc

    if final:
        @pl.when(m > 0)
        def _():
            run()
    else:
        acc = run()

        @pl.when(m == 0)
        def _():
            o_ref[0:CTX_LEN, :] = x_ref[0:CTX_LEN, :] + gate(1) * acc[0:CTX_LEN]


def _ffn_down(layer, u, w_down, xs, mod, final_w, final):
    bm = CTX_LEN if final else OUT_ROW_TILE
    row_spec = pl.BlockSpec((bm, D_MODEL), lambda m: (m, 0))
    if final:
        out_rows = SEQ
        out_spec = pl.BlockSpec((bm, D_MODEL), lambda m: (jnp.maximum(m - 1, 0), 0))
    else:
        out_rows, out_spec = T_ROWS, row_spec
    return pl.pallas_call(
        functools.partial(_ffn_down_kernel, final=final),
        out_shape=jax.ShapeDtypeStruct((out_rows, D_MODEL), F32),
        grid=(T_ROWS // bm,),
        in_specs=[
            pl.BlockSpec((bm, D_FF), lambda m: (m, 0)),
            pl.BlockSpec((None, D_FF, D_MODEL), lambda m: (layer, 0, 0),
                         pipeline_mode=pl.Buffered(1)),
            row_spec,
            pl.BlockSpec((None, SUBLANES, 6 * D_MODEL), lambda m: (layer, 0, 0)),
            pl.BlockSpec((1, D_MODEL), lambda m: (0, 0)),
        ],
        out_specs=out_spec,
        compiler_params=_cparams(1),
        name="ffn_down",
    )(u, w_down, xs, mod, final_w)


def _rope_tables():
    f32 = np.float32
    rows = SEQ // GRID_W
    row = np.repeat(np.arange(rows, dtype=f32), GRID_W)
    col = np.tile(np.arange(GRID_W, dtype=f32), rows)
    freqs = np.power(f32(ROPE_THETA), -np.arange(ROPE_FREQS, dtype=f32) / f32(ROPE_FREQS)).astype(f32)
    ar, ac = row[:, None] * freqs, col[:, None] * freqs
    cos = np.concatenate([np.cos(ar), np.cos(ar), np.cos(ac), np.cos(ac)], axis=1)
    sin = np.concatenate([np.sin(ar), np.sin(ar), np.sin(ac), np.sin(ac)], axis=1)
    cos = np.concatenate([np.ones((CTX_LEN, HEAD_DIM), f32), cos], axis=0)
    sin = np.concatenate([np.zeros((CTX_LEN, HEAD_DIM), f32), sin], axis=0)
    return jnp.asarray(cos, F32), jnp.asarray(sin, F32)


def kernel(x, c, ctx, c_ctx, w_mod, b_mod, norm1_w, norm2_w, w_in, q_norm_w, k_norm_w, hg_lb_logits,
           hg_norm_w, w_out, w_up, conv_w, conv_b, w_down, final_norm_w):
    assert x.shape == (1, SEQ, D_MODEL) and ctx.shape == (1, CTX_LEN, D_MODEL)
    xs = jnp.concatenate([ctx[0], x[0]], axis=0)
    mod = _modulation(jnp.stack([c[0], c_ctx], axis=1), w_mod, b_mod)
    cos_t, sin_t = _rope_tables()
    lb_sm = jax.nn.softmax(hg_lb_logits.astype(F32), axis=1)
    lb_all = jnp.cumsum(lb_sm, axis=1) - lb_sm[:, :1]

    w_in, w_out, w_up, w_down = (w.astype(BF16) for w in (w_in, w_out, w_up, w_down))
    for l in range(DEPTH):
        lb = jnp.concatenate([lb_all[0, l], lb_all[1, l]])[None, :]
        q, k, v, hq, kf, bf, kb, bb, hv, gate = _input_projection(
            l, xs, mod, norm1_w[l][None, :], w_in, q_norm_w[l][None, :],
            k_norm_w[l][None, :], lb, cos_t, sin_t)
        attn = _attention(q, k, v)
        o_f, o_b = _hgrn2(hq, kf, bf, kb, bb, hv)
        xs, h2 = _output_projection(l, attn, o_f, o_b, gate, hg_norm_w[l][None, :],
                                    w_out, xs, mod, norm2_w[l][None, :])
        u = _ffn_up(l, h2, w_up, conv_w[l], conv_b[l][None, :])
        xs = _ffn_down(l, u, w_down, xs, mod, final_norm_w[None, :], final=(l == DEPTH - 1))
    return xs[None]
```

```python
import functools

import jax
import jax.numpy as jnp
import numpy as np
from jax import lax
from jax.experimental import pallas as pl
from jax.experimental.pallas import tpu as pltpu

F32 = jnp.float32
BF16 = jnp.bfloat16

D_MODEL = 2048
SEQ = 8192
DEPTH = 4
CTX_LEN = 256
T_ROWS = CTX_LEN + SEQ
GRID_W = 64
HEAD_DIM = 128
ATTN_HEADS = 8
ATTN_KV_HEADS = 2
ATTN_GROUP = 4
ATTN_WIDTH = 1024
KV_WIDTH = 256
HG_HEADS = 8
HG_WIDTH = 1024
IN_WIDTH = 6656
ROPE_THETA = 10000.0
ROPE_FREQS = 32
D_FF = 5632
EPS = 1e-6
Q_SCALE = HEAD_DIM ** -0.5 * 1.4426950408889634

LANES = 128
SUBLANES = 8
VMEM_LIMIT = 56 * 1024 * 1024

MOD_SLAB = 256
ROW_TILE = 768
FF_ROW_TILE = 1408
OUT_ROW_TILE = 384
IN_COL_TILE = 512
FF_COL_TILE = 512
ATTN_Q_TILE = 256
ATTN_K_TILE = 1408
ATTN_QK_PARTS = 2
HG_BLOCK = 256
HG_ITER_HEADS = 4
HG_CHUNK = 64
HG_DIAG = 1
CONV_HALO = 16


def _cparams(n_axes):
    return pltpu.CompilerParams(dimension_semantics=("arbitrary",) * n_axes,
                                vmem_limit_bytes=VMEM_LIMIT)


def _rms(x, w):
    return x * lax.rsqrt(jnp.mean(x * x, axis=-1, keepdims=True) + EPS) * w


def _silu(x):
    h = 0.5 * x
    return h * jnp.tanh(h) + h


def _mod_kernel(s_ref, w_ref, b_ref, o_ref, acc_scr):
    k = pl.program_id(1)
    n_out = w_ref.shape[2]
    cw = 4 * LANES

    @pl.when(k == 0)
    def _():
        acc_scr[...] = jnp.zeros(acc_scr.shape, F32)

    r0 = pl.multiple_of(k * MOD_SLAB, MOD_SLAB)
    s = _silu(s_ref[pl.ds(r0, MOD_SLAB), :])
    sb = [jnp.tile(jnp.broadcast_to(s[:, i:i + 1], (MOD_SLAB, LANES)), (1, cw // LANES))
          for i in range(2)]
    for c0 in range(0, n_out, cw):
        w = w_ref[0, :, c0:c0 + cw]
        for i in range(2):
            part = (w * sb[i]).reshape(MOD_SLAB // SUBLANES, SUBLANES, cw).sum(axis=0)
            acc_scr[i, :, c0:c0 + cw] += part

    @pl.when(k == pl.num_programs(1) - 1)
    def _():
        bias = b_ref[0]
        rows = [acc_scr[i].sum(axis=0, keepdims=True) + bias for i in range(2)]
        o_ref[0] = jnp.concatenate(rows + [jnp.zeros((SUBLANES - 2, n_out), F32)], axis=0)


def _modulation(s_in, w_mod, b_mod):
    n_out = 6 * D_MODEL
    return pl.pallas_call(
        _mod_kernel,
        out_shape=jax.ShapeDtypeStruct((DEPTH, SUBLANES, n_out), F32),
        grid=(DEPTH, D_MODEL // MOD_SLAB),
        in_specs=[
            pl.BlockSpec((D_MODEL, 2), lambda l, k: (0, 0)),
            pl.BlockSpec((1, MOD_SLAB, n_out), lambda l, k: (l, k, 0)),
            pl.BlockSpec((1, 1, n_out), lambda l, k: (l, 0, 0)),
        ],
        out_specs=pl.BlockSpec((1, SUBLANES, n_out), lambda l, k: (l, 0, 0)),
        scratch_shapes=[pltpu.VMEM((2, SUBLANES, n_out), F32)],
        compiler_params=_cparams(2),
        name="adaln_modulation",
    )(s_in, w_mod, b_mod.reshape(DEPTH, 1, n_out))


def _norm_rope_matrix():
    i = lax.broadcasted_iota(jnp.int32, (2 * HEAD_DIM, 2 * HEAD_DIM), 0)
    j = lax.broadcasted_iota(jnp.int32, (2 * HEAD_DIM, 2 * HEAD_DIM), 1)
    ones = (i < HEAD_DIM) & (j < HEAD_DIM)
    ii, jj = i - HEAD_DIM, j - HEAD_DIM
    low = (jj // ROPE_FREQS) % 2 == 0
    rot = jnp.where(low & (ii == jj + ROPE_FREQS), -1.0,
                    jnp.where(~low & (ii == jj - ROPE_FREQS), 1.0, 0.0))
    rot = jnp.where((i >= HEAD_DIM) & (j >= HEAD_DIM), rot, 0.0)
    return jnp.where(ones, 1.0, rot).astype(BF16)


def _norm_rope(p, w, mat_ref, cos, sin, scale):
    pw = p * w
    lhs = jnp.concatenate([(p * p).astype(BF16), pw.astype(BF16)], axis=1)
    red = jnp.dot(lhs, mat_ref[...], preferred_element_type=F32)
    r = lax.rsqrt(red[:, 0:HEAD_DIM] * (1.0 / HEAD_DIM) + EPS) * scale
    return r * (pw * cos + red[:, HEAD_DIM:2 * HEAD_DIM] * sin)


def _inproj_kernel(x_ref, mod_ref, n1w_ref, w_ref, qnw_ref, knw_ref, lb_ref, cos_ref, sin_ref, mat_ref,
                   q_ref, k_ref, v_ref, hq_ref, kf_ref, bf_ref, kb_ref, bb_ref, hv_ref, gate_ref,
                   h_scr):
    m = pl.program_id(0)
    n = pl.program_id(1)
    bm = x_ref.shape[0]

    def modulated(rows, mod_row):
        x = x_ref[rows, :]
        r = lax.rsqrt(jnp.mean(x * x, axis=-1, keepdims=True) + EPS)
        gain = n1w_ref[...] * (1.0 + mod_ref[mod_row:mod_row + 1, D_MODEL:2 * D_MODEL])
        return (x * r * gain + mod_ref[mod_row:mod_row + 1, 0:D_MODEL]).astype(BF16)

    @pl.when(n == 0)
    def _():
        h_scr[...] = modulated(slice(None), 0)

    @pl.when((n == 0) & (m == 0))
    def _():
        h_scr[0:CTX_LEN, :] = modulated(slice(0, CTX_LEN), 1)

    def project(epilogue):
        ch = IN_COL_TILE // 2
        hpt = ch // HEAD_DIM
        rh = bm // 2
        ps = []
        for ri in range(2):
            rows = slice(ri * rh, (ri + 1) * rh)
            h = h_scr[rows, :]
            ps.append((rows, [jnp.dot(h, w_ref[:, a * ch:(a + 1) * ch], preferred_element_type=F32)
                              for a in range(2)]))
        for rows, halves in ps:
            for a, p in enumerate(halves):
                for j in range(hpt):
                    epilogue(a * hpt + j, rows, p[:, j * HEAD_DIM:(j + 1) * HEAD_DIM])

    def q_out(j, rows, p):
        q_ref[j, rows, :] = _norm_rope(p, qnw_ref[...], mat_ref, cos_ref[rows, :], sin_ref[rows, :],
                                       Q_SCALE).astype(BF16)

    def kv_out(j, rows, p):
        if j < ATTN_KV_HEADS:
            k_ref[j, rows, :] = _norm_rope(p, knw_ref[...], mat_ref, cos_ref[rows, :],
                                           sin_ref[rows, :], 1.0).astype(BF16)
        else:
            v_ref[j - ATTN_KV_HEADS, rows, :] = p.astype(BF16)

    def silu_out(ref):
        def out(j, rows, p):
            ref[j, rows, :] = _silu(p).astype(BF16)
        return out

    def forget_out(k_out, b_out, rev):
        def out(j, rows, p):
            lb = lb_ref[:, j * HEAD_DIM:(j + 1) * HEAD_DIM]
            fg = lb + (1.0 - lb) * jax.nn.sigmoid(p)
            k_out[j, rows, :] = (1.0 - fg).astype(BF16)
            b_out[j, rows, :] = _chunk_cumsum(jnp.log2(fg), rev)
        return out

    def plain_out(j, rows, p):
        hv_ref[j, rows, :] = p.astype(BF16)

    pl.when(n < 2)(lambda: project(q_out))
    pl.when(n == 2)(lambda: project(kv_out))
    pl.when((n == 3) | (n == 4))(lambda: project(silu_out(hq_ref)))
    pl.when((n == 5) | (n == 6))(lambda: project(forget_out(kf_ref, bf_ref, False)))
    pl.when((n == 7) | (n == 8))(lambda: project(forget_out(kb_ref, bb_ref, True)))
    pl.when((n == 9) | (n == 10))(lambda: project(plain_out))
    pl.when(n >= 11)(lambda: project(silu_out(gate_ref)))


def _input_projection(layer, xs, mod, n1w, w_in, qnw, knw, lb, cos_t, sin_t):
    bm, bn = ROW_TILE, IN_COL_TILE
    n_m, n_n = T_ROWS // bm, IN_WIDTH // bn
    hpt = bn // HEAD_DIM

    def head_spec(first_tile):
        return pl.BlockSpec((hpt, bm, HEAD_DIM),
                            lambda m, n: (jnp.clip(n - first_tile, 0, 1), m, 0))

    def head_shape(dtype):
        return jax.ShapeDtypeStruct((HG_HEADS, T_ROWS, HEAD_DIM), dtype)

    kv_spec = pl.BlockSpec((ATTN_KV_HEADS, bm, HEAD_DIM), lambda m, n: (0, m, 0))
    kv_shape = jax.ShapeDtypeStruct((ATTN_KV_HEADS, T_ROWS, HEAD_DIM), BF16)
    return pl.pallas_call(
        _inproj_kernel,
        out_shape=(head_shape(BF16), kv_shape, kv_shape, head_shape(BF16),
                   head_shape(BF16), head_shape(F32), head_shape(BF16), head_shape(F32),
                   head_shape(BF16), head_shape(BF16)),
        grid=(n_m, n_n),
        in_specs=[
            pl.BlockSpec((bm, D_MODEL), lambda m, n: (m, 0)),
            pl.BlockSpec((None, SUBLANES, 6 * D_MODEL), lambda m, n: (layer, 0, 0)),
            pl.BlockSpec((1, D_MODEL), lambda m, n: (0, 0)),
            pl.BlockSpec((None, D_MODEL, bn), lambda m, n: (layer, 0, n)),
            pl.BlockSpec((1, HEAD_DIM), lambda m, n: (0, 0)),
            pl.BlockSpec((1, HEAD_DIM), lambda m, n: (0, 0)),
            pl.BlockSpec((1, bn), lambda m, n: (0, jnp.clip(n - 5, 0, 3))),
            pl.BlockSpec((bm, HEAD_DIM), lambda m, n: (m, 0)),
            pl.BlockSpec((bm, HEAD_DIM), lambda m, n: (m, 0)),
            pl.BlockSpec((2 * HEAD_DIM, 2 * HEAD_DIM), lambda m, n: (0, 0)),
        ],
        out_specs=(head_spec(0), kv_spec, kv_spec, head_spec(3),
                   head_spec(5), head_spec(5), head_spec(7), head_spec(7),
                   head_spec(9), head_spec(11)),
        scratch_shapes=[pltpu.VMEM((bm, D_MODEL), BF16)],
        compiler_params=_cparams(2),
        name="input_projection",
    )(xs, mod, n1w, w_in, qnw, knw, lb, cos_t, sin_t, _norm_rope_matrix())


def _attn_kernel(q_ref, k_ref, vt_ref, o_ref, acc_scr, st_scr):
    i = pl.program_id(1)
    bq = q_ref.shape[1]
    bk = ATTN_K_TILE
    n_chunks = T_ROWS // bk

    def scores(kc, r, slot):
        part = kc.shape[0] // ATTN_QK_PARTS
        mx = None
        for a in range(ATTN_QK_PARTS):
            st = lax.dot_general(kc[a * part:(a + 1) * part], q_ref[r], (((1,), (1,)), ((), ())),
                                 preferred_element_type=F32)
            st_scr[slot, a * part:(a + 1) * part, :] = st
            m_part = st.max(axis=0, keepdims=True)
            mx = m_part if mx is None else jnp.maximum(mx, m_part)
        return mx

    def update(vtc, r, slot, m_cur, m, l):
        width = vtc.shape[1]
        m_new = jnp.maximum(m, m_cur)
        alpha = jnp.exp2(m - m_new)
        p = jnp.exp2(st_scr[slot, 0:width, :] - m_new)
        l = alpha * l + p.sum(axis=0, keepdims=True)
        acc_scr[r] = alpha * acc_scr[r] + jnp.dot(vtc, p.astype(BF16), preferred_element_type=F32)
        return m_new, l

    def finish(l):
        for r in range(ATTN_GROUP):
            o_ref[:, r * HEAD_DIM:(r + 1) * HEAD_DIM] = (acc_scr[r] / l[r]).T.astype(BF16)

    acc_scr[...] = jnp.zeros(acc_scr.shape, F32)
    m0 = tuple(jnp.full((1, bq), -jnp.inf, F32) for _ in range(ATTN_GROUP))
    l0 = tuple(jnp.zeros((1, bq), F32) for _ in range(ATTN_GROUP))

    @pl.when(i == 0)
    def _():
        kc = k_ref[0, 0:CTX_LEN, :]
        vtc = vt_ref[0, 0, :, 0:CTX_LEN]
        l = []
        mc = scores(kc, 0, 0)
        for r in range(ATTN_GROUP):
            mc_next = scores(kc, r + 1, (r + 1) % 2) if r + 1 < ATTN_GROUP else None
            l.append(update(vtc, r, r % 2, mc, m0[r], l0[r])[1])
            mc = mc_next
        finish(l)

    @pl.when(i > 0)
    def _():
        def chunk(j):
            return k_ref[0, pl.ds(pl.multiple_of(j * bk, bk), bk), :]

        def body(j, carry):
            mc, m, l = carry
            m, l = list(m), list(l)
            vtc = vt_ref[0, j]
            for r in range(ATTN_GROUP):
                if r + 1 < ATTN_GROUP:
                    mc_next = scores(chunk(j), r + 1, (r + 1) % 2)
                else:
                    mc_next = scores(chunk(jnp.minimum(j + 1, n_chunks - 1)), 0, 0)
                m[r], l[r] = update(vtc, r, r % 2, mc, m[r], l[r])
                mc = mc_next
            return mc, tuple(m), tuple(l)

        _, _, l = lax.fori_loop(0, n_chunks, body, (scores(chunk(0), 0, 0), m0, l0), unroll=True)
        finish(l)


def _attention(q, k, v):
    bq, bk = ATTN_Q_TILE, ATTN_K_TILE
    n_chunks = T_ROWS // bk
    vt = v.reshape(ATTN_KV_HEADS, n_chunks, bk, HEAD_DIM).transpose(0, 1, 3, 2)
    return pl.pallas_call(
        _attn_kernel,
        out_shape=jax.ShapeDtypeStruct((T_ROWS, ATTN_WIDTH), BF16),
        grid=(ATTN_KV_HEADS, T_ROWS // bq),
        in_specs=[
            pl.BlockSpec((ATTN_GROUP, bq, HEAD_DIM), lambda g, i: (g, i, 0)),
            pl.BlockSpec((1, T_ROWS, HEAD_DIM), lambda g, i: (g, 0, 0)),
            pl.BlockSpec((1, n_chunks, HEAD_DIM, bk), lambda g, i: (g, 0, 0, 0)),
        ],
        out_specs=pl.BlockSpec((bq, ATTN_GROUP * HEAD_DIM), lambda g, i: (i, g)),
        scratch_shapes=[pltpu.VMEM((ATTN_GROUP, HEAD_DIM, bq), F32),
                        pltpu.VMEM((2, bk, bq), F32)],
        compiler_params=_cparams(2),
        name="gqa_attention",
    )(q, k, vt)


def _chunk_cumsum(g, rev):
    n_groups = g.shape[0] // SUBLANES
    x = g.reshape(n_groups, SUBLANES, g.shape[1])
    sub = lax.broadcasted_iota(jnp.int32, x.shape, 1)
    step = 1
    while step < SUBLANES:
        if rev:
            x = x + jnp.where(sub < SUBLANES - step, pltpu.roll(x, SUBLANES - step, axis=1), 0.0)
        else:
            x = x + jnp.where(sub >= step, pltpu.roll(x, step, axis=1), 0.0)
        step *= 2
    per = HG_CHUNK // SUBLANES
    xs = [x[v] for v in range(n_groups)]
    for c0 in range(0, n_groups, per):
        if rev:
            for v in range(c0 + per - 2, c0 - 1, -1):
                xs[v] = xs[v] + xs[v + 1][0:1]
        else:
            for v in range(c0 + 1, c0 + per):
                xs[v] = xs[v] + xs[v - 1][SUBLANES - 1:SUBLANES]
    return jnp.concatenate(xs, axis=0)


def _hgrn_levels():
    halves = []
    half = HG_CHUNK // 2
    while half >= HG_DIAG:
        halves.append(half)
        half //= 2
    return halves


def _hgrn_consts(rev):
    c = HG_CHUNK
    levels = {}
    for half in _hgrn_levels():
        blk = 2 * half
        levels[half] = []
        for g in range(c // SUBLANES):
            t = lax.broadcasted_iota(jnp.int32, (SUBLANES, c), 0) + g * SUBLANES
            s = lax.broadcasted_iota(jnp.int32, (SUBLANES, c), 1)
            same = t // blk == s // blk
            if rev:
                levels[half].append(same & (t % blk < half) & (s % blk >= half))
            else:
                levels[half].append(same & (t % blk >= half) & (s % blk < half))
    t = lax.broadcasted_iota(jnp.int32, (c, c), 0)
    s = lax.broadcasted_iota(jnp.int32, (c, c), 1)
    band = []
    for d in range(HG_DIAG):
        if rev:
            band.append((s == t + d) & (t % HG_DIAG + d < HG_DIAG))
        else:
            band.append((s == t - d) & (t % HG_DIAG >= d))
    return levels, band


def _hgrn_scores(q, k, b, brow, rev, consts):
    levels, band = consts
    c = HG_CHUNK
    nt = (((1,), (1,)), ((), ()))
    groups = c // SUBLANES
    row8 = lax.broadcasted_iota(jnp.int32, (SUBLANES, HEAD_DIM), 0)
    rows = [slice(g * SUBLANES, (g + 1) * SUBLANES) for g in range(groups)]

    a = jnp.zeros((c, c), F32)
    for d in range(HG_DIAG):
        if d == 0:
            prod = q * k
        else:
            shift = c - d if rev else d
            prod = q * pltpu.roll(k, shift, axis=0) * jnp.exp2(b - pltpu.roll(b, shift, axis=0))
        a = jnp.where(band[d], prod.sum(axis=-1, keepdims=True), a)
    rows_a = [a[r] for r in rows]

    for half in _hgrn_levels():
        blk = 2 * half
        ref_off = half if rev else half - 1
        zs = []
        owners = []
        for g in range(groups):
            t0 = g * SUBLANES
            if half == 1:
                attending = (row8 % 2 == 0) if rev else (row8 % 2 == 1)
                zs.append(jnp.where(attending, q[rows[g]] * (1.0 - k[rows[g]]), k[rows[g]]))
                owners.append(g)
                continue
            if half >= SUBLANES:
                ref = brow(t0 // blk * blk + ref_off)
                attending = (t0 % blk < half) if rev else (t0 % blk >= half)
                x = q[rows[g]] if attending else k[rows[g]]
                e = jnp.exp2(b[rows[g]] - ref if attending else ref - b[rows[g]])
                if attending:
                    owners.append(g)
            else:
                ref = brow(t0 + ref_off)
                for i in range(1, SUBLANES // blk):
                    ref = jnp.where(row8 >= i * blk, brow(t0 + i * blk + ref_off), ref)
                attending = (row8 % blk < half) if rev else (row8 % blk >= half)
                x = jnp.where(attending, q[rows[g]], k[rows[g]])
                e = jnp.exp2(-jnp.abs(b[rows[g]] - ref))
                owners.append(g)
            zs.append(x * e)
        z = jnp.concatenate(zs, axis=0).astype(BF16)
        lv = lax.dot_general(z, z, nt, preferred_element_type=F32)
        for g in owners:
            rows_a[g] = jnp.where(levels[half][g], lv[rows[g]], rows_a[g])
    return jnp.concatenate(rows_a, axis=0)


def _hgrn_kernel(qf_ref, kf_ref, bf_ref, vf_ref, qb_ref, kb_ref, bb_ref, vb_ref,
                 of_ref, ob_ref, s_scr):
    @pl.when(pl.program_id(0) == 0)
    def _():
        s_scr[...] = jnp.zeros(s_scr.shape, F32)

    c = HG_CHUNK
    n_chunks = HG_BLOCK // c
    consts = (_hgrn_consts(False), _hgrn_consts(True))
    nt = (((1,), (1,)), ((), ()))
    tn = (((0,), (0,)), ((), ()))
    dirs = ((qf_ref, kf_ref, bf_ref, vf_ref, of_ref, False),
            (qb_ref, kb_ref, bb_ref, vb_ref, ob_ref, True))

    def head_body(it, carry):
        jobs = []
        for hi in range(HG_ITER_HEADS):
            h = it * HG_ITER_HEADS + hi
            for di, (q_ref, k_ref, b_ref, v_ref, o_ref, rev) in enumerate(dirs):
                order = reversed(range(n_chunks)) if rev else range(n_chunks)
                for ci in order:
                    rows = slice(ci * c, (ci + 1) * c)
                    q = q_ref[h, rows, :].astype(F32)
                    k = k_ref[h, rows, :].astype(F32)
                    b = b_ref[h, rows, :]
                    v = v_ref[h, rows, :]
                    tot = b[0:1] if rev else b[c - 1:c]
                    q_in = (q * jnp.exp2(b)).astype(BF16)
                    k_out = (k * jnp.exp2(tot - b)).astype(BF16)
                    upd = lax.dot_general(v, k_out, tn, preferred_element_type=F32)

                    def brow(r, b_ref=b_ref, base=ci * c, h=h):
                        return b_ref[h, base + r:base + r + 1, :]

                    a = _hgrn_scores(q, k, b, brow, rev, consts[di]).astype(BF16)
                    jobs.append((hi, h, di, o_ref, rows, q_in, jnp.exp2(tot), upd, a, v))
        st = {(hi, di): s_scr[di, it * HG_ITER_HEADS + hi]
              for hi in range(HG_ITER_HEADS) for di in range(2)}
        for hi, h, di, o_ref, rows, q_in, decay, upd, a, v in jobs:
            o = lax.dot_general(q_in, st[hi, di].astype(BF16), nt, preferred_element_type=F32)
            st[hi, di] = decay * st[hi, di] + upd
            o_ref[h, rows, :] = o + jnp.dot(a, v, preferred_element_type=F32)
        for (hi, di), s in st.items():
            s_scr[di, it * HG_ITER_HEADS + hi] = s
        return carry

    lax.fori_loop(0, HG_HEADS // HG_ITER_HEADS, head_body, 0)


def _hgrn2(hq, kf, bf, kb, bb, hv):
    nb = T_ROWS // HG_BLOCK
    fwd = lambda i: (0, i, 0)
    bwd = lambda i: (0, jnp.where(i == 0, 0, nb - i), 0)
    blk = (HG_HEADS, HG_BLOCK, HEAD_DIM)
    out = jax.ShapeDtypeStruct((HG_HEADS, T_ROWS, HEAD_DIM), F32)
    return pl.pallas_call(
        _hgrn_kernel,
        out_shape=(out, out),
        grid=(nb,),
        in_specs=[pl.BlockSpec(blk, fwd)] * 4 + [pl.BlockSpec(blk, bwd)] * 4,
        out_specs=(pl.BlockSpec(blk, fwd), pl.BlockSpec(blk, bwd)),
        scratch_shapes=[pltpu.VMEM((2, HG_HEADS, HEAD_DIM, HEAD_DIM), F32)],
        compiler_params=_cparams(1),
        name="hgrn2_scan",
    )(hq, kf, bf, hv, hq, kb, bb, hv)


def _outproj_kernel(attn_ref, of_ref, ob_ref, gate_ref, hnw_ref, w_ref, x_ref, mod_ref, n2w_ref,
                    xo_ref, h2_ref, mix_scr):
    m = pl.program_id(0)
    bm = x_ref.shape[0]
    acc = jnp.dot(attn_ref[...], w_ref[0:ATTN_WIDTH, :], preferred_element_type=F32)
    for h in range(HG_HEADS):
        y = _rms(of_ref[h] + ob_ref[h], hnw_ref[...])
        mix_scr[:, h * HEAD_DIM:(h + 1) * HEAD_DIM] = (y * gate_ref[h].astype(F32)).astype(BF16)
    acc = acc + jnp.dot(mix_scr[...], w_ref[ATTN_WIDTH:ATTN_WIDTH + HG_WIDTH, :],
                        preferred_element_type=F32)

    def finish(rows, mod_row):
        def mod(k):
            return mod_ref[mod_row:mod_row + 1, k * D_MODEL:(k + 1) * D_MODEL]
        x_new = x_ref[rows, :] + mod(2) * acc[rows]
        xo_ref[rows, :] = x_new
        r = lax.rsqrt(jnp.mean(x_new * x_new, axis=-1, keepdims=True) + EPS)
        h2_ref[rows, :] = (x_new * r * (n2w_ref[...] * (1.0 + mod(4))) + mod(3)).astype(BF16)

    finish(slice(0, bm), 0)

    @pl.when(m == 0)
    def _():
        finish(slice(0, CTX_LEN), 1)


def _output_projection(layer, attn, o_f, o_b, gate, hnw, w_out, xs, mod, n2w):
    bm = OUT_ROW_TILE
    head_spec = pl.BlockSpec((HG_HEADS, bm, HEAD_DIM), lambda m: (0, m, 0))
    row_spec = pl.BlockSpec((bm, D_MODEL), lambda m: (m, 0))
    return pl.pallas_call(
        _outproj_kernel,
        out_shape=(jax.ShapeDtypeStruct((T_ROWS, D_MODEL), F32),
                   jax.ShapeDtypeStruct((T_ROWS, D_MODEL), BF16)),
        grid=(T_ROWS // bm,),
        in_specs=[
            pl.BlockSpec((bm, ATTN_WIDTH), lambda m: (m, 0)),
            head_spec, head_spec, head_spec,
            pl.BlockSpec((1, HEAD_DIM), lambda m: (0, 0)),
            pl.BlockSpec((None, ATTN_WIDTH + HG_WIDTH, D_MODEL), lambda m: (layer, 0, 0)),
            row_spec,
            pl.BlockSpec((None, SUBLANES, 6 * D_MODEL), lambda m: (layer, 0, 0)),
            pl.BlockSpec((1, D_MODEL), lambda m: (0, 0)),
        ],
        out_specs=(row_spec, row_spec),
        scratch_shapes=[pltpu.VMEM((bm, HG_WIDTH), BF16)],
        compiler_params=_cparams(1),
        name="output_projection",
    )(attn, o_f, o_b, gate, hnw, w_out, xs, mod, n2w)


def _ffn_up_kernel(h_ref, hp_ref, hn_ref, wg_ref, wu_ref, cw_ref, cb_ref, u_ref, hcat_scr, g_scr):
    m = pl.program_id(0)
    n = pl.program_id(1)
    bm = h_ref.shape[0]
    hl = CONV_HALO

    @pl.when(n == 0)
    def _():
        hcat_scr[0:hl, :] = hp_ref[...]
        hcat_scr[hl:hl + bm, :] = h_ref[...]
        hcat_scr[hl + bm:hl + bm + hl, :] = hn_ref[...]

    g_scr[...] = jnp.dot(hcat_scr[...], wg_ref[...], preferred_element_type=F32)
    up = jnp.dot(h_ref[...], wu_ref[...], preferred_element_type=F32)
    r = m * bm + lax.broadcasted_iota(jnp.int32, (bm, 1), 0)
    has_prev = (r != 0) & (r != CTX_LEN)
    has_next = (r != CTX_LEN - 1) & (r != T_ROWS - 1)
    g_prev = jnp.where(has_prev, g_scr[hl - 1:hl - 1 + bm, :], 0.0)
    g_next = jnp.where(has_next, g_scr[hl + 1:hl + 1 + bm, :], 0.0)
    gate = (g_prev * cw_ref[0:1, :] + g_scr[hl:hl + bm, :] * cw_ref[1:2, :]
            + g_next * cw_ref[2:3, :] + cb_ref[...])
    u_ref[...] = (_silu(gate) * up).astype(BF16)


def _ffn_up(layer, h2, w_up, conv_w, conv_b):
    bm, bn, hl = FF_ROW_TILE, FF_COL_TILE, CONV_HALO
    n_n = D_FF // bn
    per = bm // hl
    last = T_ROWS // hl - 1
    return pl.pallas_call(
        _ffn_up_kernel,
        out_shape=jax.ShapeDtypeStruct((T_ROWS, D_FF), BF16),
        grid=(T_ROWS // bm, n_n),
        in_specs=[
            pl.BlockSpec((bm, D_MODEL), lambda m, n: (m, 0)),
            pl.BlockSpec((hl, D_MODEL), lambda m, n: (jnp.maximum(m * per - 1, 0), 0)),
            pl.BlockSpec((hl, D_MODEL), lambda m, n: (jnp.minimum((m + 1) * per, last), 0)),
            pl.BlockSpec((None, D_MODEL, bn), lambda m, n: (layer, 0, n)),
            pl.BlockSpec((None, D_MODEL, bn), lambda m, n: (layer, 0, n_n + n)),
            pl.BlockSpec((3, bn), lambda m, n: (0, n)),
            pl.BlockSpec((1, bn), lambda m, n: (0, n)),
        ],
        out_specs=pl.BlockSpec((bm, bn), lambda m, n: (m, n)),
        scratch_shapes=[pltpu.VMEM((bm + 2 * hl, D_MODEL), BF16),
                        pltpu.VMEM((bm + 2 * hl, bn), F32)],
        compiler_params=_cparams(2),
        name="ffn_up_conv",
    )(h2, h2, h2, w_up, w_up, conv_w, conv_b)


def _ffn_down_kernel(u_ref, w_ref, x_ref, mod_ref, fw_ref, o_ref, *, final):
    m = pl.program_id(0)

    def gate(mod_row):
        return mod_ref[mod_row:mod_row + 1, 5 * D_MODEL:6 * D_MODEL]

    def run():
        acc = jnp.dot(u_ref[...], w_ref[...], preferred_element_type=F32)
        x_new = x_ref[...] + gate(0) * acc
        o_ref[...] = _rms(x_new, fw_ref[...]) if final else x_new
        return acc

    if final:
        @pl.when(m > 0)
        def _():
            run()
    else:
        acc = run()

        @pl.when(m == 0)
        def _():
            o_ref[0:CTX_LEN, :] = x_ref[0:CTX_LEN, :] + gate(1) * acc[0:CTX_LEN]


def _ffn_down(layer, u, w_down, xs, mod, final_w, final):
    bm = CTX_LEN if final else OUT_ROW_TILE
    row_spec = pl.BlockSpec((bm, D_MODEL), lambda m: (m, 0))
    if final:
        out_rows = SEQ
        out_spec = pl.BlockSpec((bm, D_MODEL), lambda m: (jnp.maximum(m - 1, 0), 0))
    else:
        out_rows, out_spec = T_ROWS, row_spec
    return pl.pallas_call(
        functools.partial(_ffn_down_kernel, final=final),
        out_shape=jax.ShapeDtypeStruct((out_rows, D_MODEL), F32),
        grid=(T_ROWS // bm,),
        in_specs=[
            pl.BlockSpec((bm, D_FF), lambda m: (m, 0)),
            pl.BlockSpec((None, D_FF, D_MODEL), lambda m: (layer, 0, 0),
                         pipeline_mode=pl.Buffered(1)),
            row_spec,
            pl.BlockSpec((None, SUBLANES, 6 * D_MODEL), lambda m: (layer, 0, 0)),
            pl.BlockSpec((1, D_MODEL), lambda m: (0, 0)),
        ],
        out_specs=out_spec,
        compiler_params=_cparams(1),
        name="ffn_down",
    )(u, w_down, xs, mod, final_w)


def _rope_tables():
    f32 = np.float32
    rows = SEQ // GRID_W
    row = np.repeat(np.arange(rows, dtype=f32), GRID_W)
    col = np.tile(np.arange(GRID_W, dtype=f32), rows)
    freqs = np.power(f32(ROPE_THETA), -np.arange(ROPE_FREQS, dtype=f32) / f32(ROPE_FREQS)).astype(f32)
    ar, ac = row[:, None] * freqs, col[:, None] * freqs
    cos = np.concatenate([np.cos(ar), np.cos(ar), np.cos(ac), np.cos(ac)], axis=1)
    sin = np.concatenate([np.sin(ar), np.sin(ar), np.sin(ac), np.sin(ac)], axis=1)
    cos = np.concatenate([np.ones((CTX_LEN, HEAD_DIM), f32), cos], axis=0)
    sin = np.concatenate([np.zeros((CTX_LEN, HEAD_DIM), f32), sin], axis=0)
    return jnp.asarray(cos, F32), jnp.asarray(sin, F32)


def kernel(x, c, ctx, c_ctx, w_mod, b_mod, norm1_w, norm2_w, w_in, q_norm_w, k_norm_w, hg_lb_logits,
           hg_norm_w, w_out, w_up, conv_w, conv_b, w_down, final_norm_w):
    assert x.shape == (1, SEQ, D_MODEL) and ctx.shape == (1, CTX_LEN, D_MODEL)
    xs = jnp.concatenate([ctx[0], x[0]], axis=0)
    mod = _modulation(jnp.stack([c[0], c_ctx], axis=1), w_mod, b_mod)
    cos_t, sin_t = _rope_tables()
    lb_sm = jax.nn.softmax(hg_lb_logits.astype(F32), axis=1)
    lb_all = jnp.cumsum(lb_sm, axis=1) - lb_sm[:, :1]

    w_in, w_out, w_up, w_down = (w.astype(BF16) for w in (w_in, w_out, w_up, w_down))
    for l in range(DEPTH):
        lb = jnp.concatenate([lb_all[0, l], lb_all[1, l]])[None, :]
        q, k, v, hq, kf, bf, kb, bb, hv, gate = _input_projection(
            l, xs, mod, norm1_w[l][None, :], w_in, q_norm_w[l][None, :],
            k_norm_w[l][None, :], lb, cos_t, sin_t)
        attn = _attention(q, k, v)
        o_f, o_b = _hgrn2(hq, kf, bf, kb, bb, hv)
        xs, h2 = _output_projection(l, attn, o_f, o_b, gate, hg_norm_w[l][None, :],
                                    w_out, xs, mod, norm2_w[l][None, :])
        u = _ffn_up(l, h2, w_up, conv_w[l], conv_b[l][None, :])
        xs = _ffn_down(l, u, w_down, xs, mod, final_norm_w[None, :], final=(l == DEPTH - 1))
    return xs[None]
```

```python
import functools

import jax
import jax.numpy as jnp
import numpy as np
from jax import lax
from jax.experimental import pallas as pl
from jax.experimental.pallas import tpu as pltpu

F32 = jnp.float32
BF16 = jnp.bfloat16

D_MODEL = 2048
SEQ = 8192
DEPTH = 4
CTX_LEN = 256
T_ROWS = CTX_LEN + SEQ
GRID_W = 64
HEAD_DIM = 128
ATTN_HEADS = 8
ATTN_KV_HEADS = 2
ATTN_GROUP = 4
ATTN_WIDTH = 1024
KV_WIDTH = 256
HG_HEADS = 8
HG_WIDTH = 1024
IN_WIDTH = 6656
ROPE_THETA = 10000.0
ROPE_FREQS = 32
D_FF = 5632
EPS = 1e-6
Q_SCALE = HEAD_DIM ** -0.5 * 1.4426950408889634

LANES = 128
SUBLANES = 8
VMEM_LIMIT = 58 * 1024 * 1024

MOD_SLAB = 256
ROW_TILE = 1408
FF_ROW_TILE = 1408
OUT_ROW_TILE = 384
IN_COL_TILE = 512
FF_COL_TILE = 512
ATTN_Q_TILE = 256
ATTN_K_TILE = 1408
ATTN_QK_PARTS = 2
HG_BLOCK = 256
HG_ITER_HEADS = 4
HG_CHUNK = 64
HG_DIAG = 1
CONV_HALO = 16


def _cparams(n_axes):
    return pltpu.CompilerParams(dimension_semantics=("arbitrary",) * n_axes,
                                vmem_limit_bytes=VMEM_LIMIT)


def _rms(x, w):
    return x * lax.rsqrt(jnp.mean(x * x, axis=-1, keepdims=True) + EPS) * w


def _silu(x):
    h = 0.5 * x
    return h * jnp.tanh(h) + h


def _mod_kernel(s_ref, w_ref, b_ref, o_ref, acc_scr):
    k = pl.program_id(1)
    n_out = w_ref.shape[2]
    cw = 4 * LANES

    @pl.when(k == 0)
    def _():
        acc_scr[...] = jnp.zeros(acc_scr.shape, F32)

    r0 = pl.multiple_of(k * MOD_SLAB, MOD_SLAB)
    s = _silu(s_ref[pl.ds(r0, MOD_SLAB), :])
    sb = [jnp.tile(jnp.broadcast_to(s[:, i:i + 1], (MOD_SLAB, LANES)), (1, cw // LANES))
          for i in range(2)]
    for c0 in range(0, n_out, cw):
        w = w_ref[0, :, c0:c0 + cw]
        for i in range(2):
            part = (w * sb[i]).reshape(MOD_SLAB // SUBLANES, SUBLANES, cw).sum(axis=0)
            acc_scr[i, :, c0:c0 + cw] += part

    @pl.when(k == pl.num_programs(1) - 1)
    def _():
        bias = b_ref[0]
        rows = [acc_scr[i].sum(axis=0, keepdims=True) + bias for i in range(2)]
        o_ref[0] = jnp.concatenate(rows + [jnp.zeros((SUBLANES - 2, n_out), F32)], axis=0)


def _modulation(s_in, w_mod, b_mod):
    n_out = 6 * D_MODEL
    return pl.pallas_call(
        _mod_kernel,
        out_shape=jax.ShapeDtypeStruct((DEPTH, SUBLANES, n_out), F32),
        grid=(DEPTH, D_MODEL // MOD_SLAB),
        in_specs=[
            pl.BlockSpec((D_MODEL, 2), lambda l, k: (0, 0)),
            pl.BlockSpec((1, MOD_SLAB, n_out), lambda l, k: (l, k, 0)),
            pl.BlockSpec((1, 1, n_out), lambda l, k: (l, 0, 0)),
        ],
        out_specs=pl.BlockSpec((1, SUBLANES, n_out), lambda l, k: (l, 0, 0)),
        scratch_shapes=[pltpu.VMEM((2, SUBLANES, n_out), F32)],
        compiler_params=_cparams(2),
        name="adaln_modulation",
    )(s_in, w_mod, b_mod.reshape(DEPTH, 1, n_out))


def _norm_rope_matrix():
    i = lax.broadcasted_iota(jnp.int32, (2 * HEAD_DIM, 2 * HEAD_DIM), 0)
    j = lax.broadcasted_iota(jnp.int32, (2 * HEAD_DIM, 2 * HEAD_DIM), 1)
    ones = (i < HEAD_DIM) & (j < HEAD_DIM)
    ii, jj = i - HEAD_DIM, j - HEAD_DIM
    low = (jj // ROPE_FREQS) % 2 == 0
    rot = jnp.where(low & (ii == jj + ROPE_FREQS), -1.0,
                    jnp.where(~low & (ii == jj - ROPE_FREQS), 1.0, 0.0))
    rot = jnp.where((i >= HEAD_DIM) & (j >= HEAD_DIM), rot, 0.0)
    return jnp.where(ones, 1.0, rot).astype(BF16)


def _norm_rope(p, w, mat_ref, cos, sin, scale):
    pw = p * w
    lhs = jnp.concatenate([(p * p).astype(BF16), pw.astype(BF16)], axis=1)
    red = jnp.dot(lhs, mat_ref[...], preferred_element_type=F32)
    r = lax.rsqrt(red[:, 0:HEAD_DIM] * (1.0 / HEAD_DIM) + EPS) * scale
    return r * (pw * cos + red[:, HEAD_DIM:2 * HEAD_DIM] * sin)


def _inproj_kernel(x_ref, mod_ref, n1w_ref, w_ref, qnw_ref, knw_ref, lb_ref, cos_ref, sin_ref, mat_ref,
                   q_ref, k_ref, v_ref, hq_ref, kf_ref, bf_ref, kb_ref, bb_ref, hv_ref, gate_ref,
                   h_scr):
    m = pl.program_id(0)
    n = pl.program_id(1)
    bm = x_ref.shape[0]

    def modulated(rows, mod_row):
        x = x_ref[rows, :]
        r = lax.rsqrt(jnp.mean(x * x, axis=-1, keepdims=True) + EPS)
        gain = n1w_ref[...] * (1.0 + mod_ref[mod_row:mod_row + 1, D_MODEL:2 * D_MODEL])
        return (x * r * gain + mod_ref[mod_row:mod_row + 1, 0:D_MODEL]).astype(BF16)

    @pl.when(n == 0)
    def _():
        h_scr[...] = modulated(slice(None), 0)

    @pl.when((n == 0) & (m == 0))
    def _():
        h_scr[0:CTX_LEN, :] = modulated(slice(0, CTX_LEN), 1)

    def project(epilogue):
        ch = IN_COL_TILE // 2
        hpt = ch // HEAD_DIM
        rh = bm // 2
        ps = []
        for ri in range(2):
            rows = slice(ri * rh, (ri + 1) * rh)
            h = h_scr[rows, :]
            ps.append((rows, [jnp.dot(h, w_ref[:, a * ch:(a + 1) * ch], preferred_element_type=F32)
                              for a in range(2)]))
        for rows, halves in ps:
            for a, p in enumerate(halves):
                for j in range(hpt):
                    epilogue(a * hpt + j, rows, p[:, j * HEAD_DIM:(j + 1) * HEAD_DIM])

    def q_out(j, rows, p):
        q_ref[j, rows, :] = _norm_rope(p, qnw_ref[...], mat_ref, cos_ref[rows, :], sin_ref[rows, :],
                                       Q_SCALE).astype(BF16)

    def kv_out(j, rows, p):
        if j < ATTN_KV_HEADS:
            k_ref[j, rows, :] = _norm_rope(p, knw_ref[...], mat_ref, cos_ref[rows, :],
                                           sin_ref[rows, :], 1.0).astype(BF16)
        else:
            v_ref[j - ATTN_KV_HEADS, rows, :] = p.astype(BF16)

    def silu_out(ref):
        def out(j, rows, p):
            ref[j, rows, :] = _silu(p).astype(BF16)
        return out

    def forget_out(k_out, b_out, rev):
        def out(j, rows, p):
            lb = lb_ref[:, j * HEAD_DIM:(j + 1) * HEAD_DIM]
            fg = lb + (1.0 - lb) * jax.nn.sigmoid(p)
            k_out[j, rows, :] = (1.0 - fg).astype(BF16)
            b_out[j, rows, :] = _chunk_cumsum(jnp.log2(fg), rev)
        return out

    def plain_out(j, rows, p):
        hv_ref[j, rows, :] = p.astype(BF16)

    pl.when(n < 2)(lambda: project(q_out))
    pl.when(n == 2)(lambda: project(kv_out))
    pl.when((n == 3) | (n == 4))(lambda: project(silu_out(hq_ref)))
    pl.when((n == 5) | (n == 6))(lambda: project(forget_out(kf_ref, bf_ref, False)))
    pl.when((n == 7) | (n == 8))(lambda: project(forget_out(kb_ref, bb_ref, True)))
    pl.when((n == 9) | (n == 10))(lambda: project(plain_out))
    pl.when(n >= 11)(lambda: project(silu_out(gate_ref)))


def _input_projection(layer, xs, mod, n1w, w_in, qnw, knw, lb, cos_t, sin_t):
    bm, bn = ROW_TILE, IN_COL_TILE
    n_m, n_n = T_ROWS // bm, IN_WIDTH // bn
    hpt = bn // HEAD_DIM

    def head_spec(first_tile):
        return pl.BlockSpec((hpt, bm, HEAD_DIM),
                            lambda m, n: (jnp.clip(n - first_tile, 0, 1), m, 0),
                            pipeline_mode=pl.Buffered(1))

    def head_shape(dtype):
        return jax.ShapeDtypeStruct((HG_HEADS, T_ROWS, HEAD_DIM), dtype)

    kv_spec = pl.BlockSpec((ATTN_KV_HEADS, bm, HEAD_DIM), lambda m, n: (0, m, 0),
                           pipeline_mode=pl.Buffered(1))
    kv_shape = jax.ShapeDtypeStruct((ATTN_KV_HEADS, T_ROWS, HEAD_DIM), BF16)
    return pl.pallas_call(
        _inproj_kernel,
        out_shape=(head_shape(BF16), kv_shape, kv_shape, head_shape(BF16),
                   head_shape(BF16), head_shape(F32), head_shape(BF16), head_shape(F32),
                   head_shape(BF16), head_shape(BF16)),
        grid=(n_m, n_n),
        in_specs=[
            pl.BlockSpec((bm, D_MODEL), lambda m, n: (m, 0)),
            pl.BlockSpec((None, SUBLANES, 6 * D_MODEL), lambda m, n: (layer, 0, 0)),
            pl.BlockSpec((1, D_MODEL), lambda m, n: (0, 0)),
            pl.BlockSpec((None, D_MODEL, bn), lambda m, n: (layer, 0, n)),
            pl.BlockSpec((1, HEAD_DIM), lambda m, n: (0, 0)),
            pl.BlockSpec((1, HEAD_DIM), lambda m, n: (0, 0)),
            pl.BlockSpec((1, bn), lambda m, n: (0, jnp.clip(n - 5, 0, 3))),
            pl.BlockSpec((bm, HEAD_DIM), lambda m, n: (m, 0)),
            pl.BlockSpec((bm, HEAD_DIM), lambda m, n: (m, 0)),
            pl.BlockSpec((2 * HEAD_DIM, 2 * HEAD_DIM), lambda m, n: (0, 0)),
        ],
        out_specs=(head_spec(0), kv_spec, kv_spec, head_spec(3),
                   head_spec(5), head_spec(5), head_spec(7), head_spec(7),
                   head_spec(9), head_spec(11)),
        scratch_shapes=[pltpu.VMEM((bm, D_MODEL), BF16)],
        compiler_params=_cparams(2),
        name="input_projection",
    )(xs, mod, n1w, w_in, qnw, knw, lb, cos_t, sin_t, _norm_rope_matrix())


def _attn_kernel(q_ref, k_ref, vt_ref, o_ref, acc_scr, st_scr):
    i = pl.program_id(1)
    bq = q_ref.shape[1]
    bk = ATTN_K_TILE
    n_chunks = T_ROWS // bk

    def scores(kc, r, slot):
        part = kc.shape[0] // ATTN_QK_PARTS
        mx = None
        for a in range(ATTN_QK_PARTS):
            st = lax.dot_general(kc[a * part:(a + 1) * part], q_ref[r], (((1,), (1,)), ((), ())),
                                 preferred_element_type=F32)
            st_scr[slot, a * part:(a + 1) * part, :] = st
            m_part = st.max(axis=0, keepdims=True)
            mx = m_part if mx is None else jnp.maximum(mx, m_part)
        return mx

    def update(vtc, r, slot, m_cur, m, l):
        width = vtc.shape[1]
        m_new = jnp.maximum(m, m_cur)
        alpha = jnp.exp2(m - m_new)
        p = jnp.exp2(st_scr[slot, 0:width, :] - m_new)
        l = alpha * l + p.sum(axis=0, keepdims=True)
        acc_scr[r] = alpha * acc_scr[r] + jnp.dot(vtc, p.astype(BF16), preferred_element_type=F32)
        return m_new, l

    def finish(l):
        for r in range(ATTN_GROUP):
            o_ref[:, r * HEAD_DIM:(r + 1) * HEAD_DIM] = (acc_scr[r] / l[r]).T.astype(BF16)

    acc_scr[...] = jnp.zeros(acc_scr.shape, F32)
    m0 = tuple(jnp.full((1, bq), -jnp.inf, F32) for _ in range(ATTN_GROUP))
    l0 = tuple(jnp.zeros((1, bq), F32) for _ in range(ATTN_GROUP))

    @pl.when(i == 0)
    def _():
        kc = k_ref[0, 0:CTX_LEN, :]
        vtc = vt_ref[0, 0, :, 0:CTX_LEN]
        l = []
        mc = scores(kc, 0, 0)
        for r in range(ATTN_GROUP):
            mc_next = scores(kc, r + 1, (r + 1) % 2) if r + 1 < ATTN_GROUP else None
            l.append(update(vtc, r, r % 2, mc, m0[r], l0[r])[1])
            mc = mc_next
        finish(l)

    @pl.when(i > 0)
    def _():
        def chunk(j):
            return k_ref[0, pl.ds(pl.multiple_of(j * bk, bk), bk), :]

        def body(j, carry):
            mc, m, l = carry
            m, l = list(m), list(l)
            vtc = vt_ref[0, j]
            for r in range(ATTN_GROUP):
                if r + 1 < ATTN_GROUP:
                    mc_next = scores(chunk(j), r + 1, (r + 1) % 2)
                else:
                    mc_next = scores(chunk(jnp.minimum(j + 1, n_chunks - 1)), 0, 0)
                m[r], l[r] = update(vtc, r, r % 2, mc, m[r], l[r])
                mc = mc_next
            return mc, tuple(m), tuple(l)

        _, _, l = lax.fori_loop(0, n_chunks, body, (scores(chunk(0), 0, 0), m0, l0), unroll=True)
        finish(l)


def _attention(q, k, v):
    bq, bk = ATTN_Q_TILE, ATTN_K_TILE
    n_chunks = T_ROWS // bk
    vt = v.reshape(ATTN_KV_HEADS, n_chunks, bk, HEAD_DIM).transpose(0, 1, 3, 2)
    return pl.pallas_call(
        _attn_kernel,
        out_shape=jax.ShapeDtypeStruct((T_ROWS, ATTN_WIDTH), BF16),
        grid=(ATTN_KV_HEADS, T_ROWS // bq),
        in_specs=[
            pl.BlockSpec((ATTN_GROUP, bq, HEAD_DIM), lambda g, i: (g, i, 0)),
            pl.BlockSpec((1, T_ROWS, HEAD_DIM), lambda g, i: (g, 0, 0)),
            pl.BlockSpec((1, n_chunks, HEAD_DIM, bk), lambda g, i: (g, 0, 0, 0)),
        ],
        out_specs=pl.BlockSpec((bq, ATTN_GROUP * HEAD_DIM), lambda g, i: (i, g)),
        scratch_shapes=[pltpu.VMEM((ATTN_GROUP, HEAD_DIM, bq), F32),
                        pltpu.VMEM((2, bk, bq), F32)],
        compiler_params=_cparams(2),
        name="gqa_attention",
    )(q, k, vt)


def _chunk_cumsum(g, rev):
    n_groups = g.shape[0] // SUBLANES
    x = g.reshape(n_groups, SUBLANES, g.shape[1])
    sub = lax.broadcasted_iota(jnp.int32, x.shape, 1)
    step = 1
    while step < SUBLANES:
        if rev:
            x = x + jnp.where(sub < SUBLANES - step, pltpu.roll(x, SUBLANES - step, axis=1), 0.0)
        else:
            x = x + jnp.where(sub >= step, pltpu.roll(x, step, axis=1), 0.0)
        step *= 2
    per = HG_CHUNK // SUBLANES
    xs = [x[v] for v in range(n_groups)]
    for c0 in range(0, n_groups, per):
        if rev:
            for v in range(c0 + per - 2, c0 - 1, -1):
                xs[v] = xs[v] + xs[v + 1][0:1]
        else:
            for v in range(c0 + 1, c0 + per):
                xs[v] = xs[v] + xs[v - 1][SUBLANES - 1:SUBLANES]
    return jnp.concatenate(xs, axis=0)


def _hgrn_levels():
    halves = []
    half = HG_CHUNK // 2
    while half >= HG_DIAG:
        halves.append(half)
        half //= 2
    return halves


def _hgrn_consts(rev):
    c = HG_CHUNK
    levels = {}
    for half in _hgrn_levels():
        blk = 2 * half
        levels[half] = []
        for g in range(c // SUBLANES):
            t = lax.broadcasted_iota(jnp.int32, (SUBLANES, c), 0) + g * SUBLANES
            s = lax.broadcasted_iota(jnp.int32, (SUBLANES, c), 1)
            same = t // blk == s // blk
            if rev:
                levels[half].append(same & (t % blk < half) & (s % blk >= half))
            else:
                levels[half].append(same & (t % blk >= half) & (s % blk < half))
    t = lax.broadcasted_iota(jnp.int32, (c, c), 0)
    s = lax.broadcasted_iota(jnp.int32, (c, c), 1)
    band = []
    for d in range(HG_DIAG):
        if rev:
            band.append((s == t + d) & (t % HG_DIAG + d < HG_DIAG))
        else:
            band.append((s == t - d) & (t % HG_DIAG >= d))
    return levels, band


def _hgrn_scores(q, k, b, brow, rev, consts):
    levels, band = consts
    c = HG_CHUNK
    nt = (((1,), (1,)), ((), ()))
    groups = c // SUBLANES
    row8 = lax.broadcasted_iota(jnp.int32, (SUBLANES, HEAD_DIM), 0)
    rows = [slice(g * SUBLANES, (g + 1) * SUBLANES) for g in range(groups)]

    a = jnp.zeros((c, c), F32)
    for d in range(HG_DIAG):
        if d == 0:
            prod = q * k
        else:
            shift = c - d if rev else d
            prod = q * pltpu.roll(k, shift, axis=0) * jnp.exp2(b - pltpu.roll(b, shift, axis=0))
        a = jnp.where(band[d], prod.sum(axis=-1, keepdims=True), a)
    rows_a = [a[r] for r in rows]

    for half in _hgrn_levels():
        blk = 2 * half
        ref_off = half if rev else half - 1
        zs = []
        owners = []
        for g in range(groups):
            t0 = g * SUBLANES
            if half == 1:
                attending = (row8 % 2 == 0) if rev else (row8 % 2 == 1)
                zs.append(jnp.where(attending, q[rows[g]] * (1.0 - k[rows[g]]), k[rows[g]]))
                owners.append(g)
                continue
            if half >= SUBLANES:
                ref = brow(t0 // blk * blk + ref_off)
                attending = (t0 % blk < half) if rev else (t0 % blk >= half)
                x = q[rows[g]] if attending else k[rows[g]]
                e = jnp.exp2(b[rows[g]] - ref if attending else ref - b[rows[g]])
                if attending:
                    owners.append(g)
            else:
                ref = brow(t0 + ref_off)
                for i in range(1, SUBLANES // blk):
                    ref = jnp.where(row8 >= i * blk, brow(t0 + i * blk + ref_off), ref)
                attending = (row8 % blk < half) if rev else (row8 % blk >= half)
                x = jnp.where(attending, q[rows[g]], k[rows[g]])
                e = jnp.exp2(-jnp.abs(b[rows[g]] - ref))
                owners.append(g)
            zs.append(x * e)
        z = jnp.concatenate(zs, axis=0).astype(BF16)
        lv = lax.dot_general(z, z, nt, preferred_element_type=F32)
        for g in owners:
            rows_a[g] = jnp.where(levels[half][g], lv[rows[g]], rows_a[g])
    return jnp.concatenate(rows_a, axis=0)


def _hgrn_kernel(qf_ref, kf_ref, bf_ref, vf_ref, qb_ref, kb_ref, bb_ref, vb_ref,
                 of_ref, ob_ref, s_scr):
    @pl.when(pl.program_id(0) == 0)
    def _():
        s_scr[...] = jnp.zeros(s_scr.shape, F32)

    c = HG_CHUNK
    n_chunks = HG_BLOCK // c
    consts = (_hgrn_consts(False), _hgrn_consts(True))
    nt = (((1,), (1,)), ((), ()))
    tn = (((0,), (0,)), ((), ()))
    dirs = ((qf_ref, kf_ref, bf_ref, vf_ref, of_ref, False),
            (qb_ref, kb_ref, bb_ref, vb_ref, ob_ref, True))

    def head_body(it, carry):
        jobs = []
        for hi in range(HG_ITER_HEADS):
            h = it * HG_ITER_HEADS + hi
            for di, (q_ref, k_ref, b_ref, v_ref, o_ref, rev) in enumerate(dirs):
                order = reversed(range(n_chunks)) if rev else range(n_chunks)
                for ci in order:
                    rows = slice(ci * c, (ci + 1) * c)
                    q = q_ref[h, rows, :].astype(F32)
                    k = k_ref[h, rows, :].astype(F32)
                    b = b_ref[h, rows, :]
                    v = v_ref[h, rows, :]
                    tot = b[0:1] if rev else b[c - 1:c]
                    q_in = (q * jnp.exp2(b)).astype(BF16)
                    k_out = (k * jnp.exp2(tot - b)).astype(BF16)
                    upd = lax.dot_general(v, k_out, tn, preferred_element_type=F32)

                    def brow(r, b_ref=b_ref, base=ci * c, h=h):
                        return b_ref[h, base + r:base + r + 1, :]

                    a = _hgrn_scores(q, k, b, brow, rev, consts[di]).astype(BF16)
                    jobs.append((hi, h, di, o_ref, rows, q_in, jnp.exp2(tot), upd, a, v))
        st = {(hi, di): s_scr[di, it * HG_ITER_HEADS + hi]
              for hi in range(HG_ITER_HEADS) for di in range(2)}
        for hi, h, di, o_ref, rows, q_in, decay, upd, a, v in jobs:
            o = lax.dot_general(q_in, st[hi, di].astype(BF16), nt, preferred_element_type=F32)
            st[hi, di] = decay * st[hi, di] + upd
            o_ref[h, rows, :] = o + jnp.dot(a, v, preferred_element_type=F32)
        for (hi, di), s in st.items():
            s_scr[di, it * HG_ITER_HEADS + hi] = s
        return carry

    lax.fori_loop(0, HG_HEADS // HG_ITER_HEADS, head_body, 0)


def _hgrn2(hq, kf, bf, kb, bb, hv):
    nb = T_ROWS // HG_BLOCK
    fwd = lambda i: (0, i, 0)
    bwd = lambda i: (0, jnp.where(i == 0, 0, nb - i), 0)
    blk = (HG_HEADS, HG_BLOCK, HEAD_DIM)
    out = jax.ShapeDtypeStruct((HG_HEADS, T_ROWS, HEAD_DIM), F32)
    return pl.pallas_call(
        _hgrn_kernel,
        out_shape=(out, out),
        grid=(nb,),
        in_specs=[pl.BlockSpec(blk, fwd)] * 4 + [pl.BlockSpec(blk, bwd)] * 4,
        out_specs=(pl.BlockSpec(blk, fwd), pl.BlockSpec(blk, bwd)),
        scratch_shapes=[pltpu.VMEM((2, HG_HEADS, HEAD_DIM, HEAD_DIM), F32)],
        compiler_params=_cparams(1),
        name="hgrn2_scan",
    )(hq, kf, bf, hv, hq, kb, bb, hv)


def _outproj_kernel(attn_ref, of_ref, ob_ref, gate_ref, hnw_ref, w_ref, x_ref, mod_ref, n2w_ref,
                    xo_ref, h2_ref, mix_scr):
    m = pl.program_id(0)
    bm = x_ref.shape[0]
    acc = jnp.dot(attn_ref[...], w_ref[0:ATTN_WIDTH, :], preferred_element_type=F32)
    for h in range(HG_HEADS):
        y = _rms(of_ref[h] + ob_ref[h], hnw_ref[...])
        mix_scr[:, h * HEAD_DIM:(h + 1) * HEAD_DIM] = (y * gate_ref[h].astype(F32)).astype(BF16)
    acc = acc + jnp.dot(mix_scr[...], w_ref[ATTN_WIDTH:ATTN_WIDTH + HG_WIDTH, :],
                        preferred_element_type=F32)

    def finish(rows, mod_row):
        def mod(k):
            return mod_ref[mod_row:mod_row + 1, k * D_MODEL:(k + 1) * D_MODEL]
        x_new = x_ref[rows, :] + mod(2) * acc[rows]
        xo_ref[rows, :] = x_new
        r = lax.rsqrt(jnp.mean(x_new * x_new, axis=-1, keepdims=True) + EPS)
        h2_ref[rows, :] = (x_new * r * (n2w_ref[...] * (1.0 + mod(4))) + mod(3)).astype(BF16)

    finish(slice(0, bm), 0)

    @pl.when(m == 0)
    def _():
        finish(slice(0, CTX_LEN), 1)


def _output_projection(layer, attn, o_f, o_b, gate, hnw, w_out, xs, mod, n2w):
    bm = OUT_ROW_TILE
    head_spec = pl.BlockSpec((HG_HEADS, bm, HEAD_DIM), lambda m: (0, m, 0))
    row_spec = pl.BlockSpec((bm, D_MODEL), lambda m: (m, 0))
    return pl.pallas_call(
        _outproj_kernel,
        out_shape=(jax.ShapeDtypeStruct((T_ROWS, D_MODEL), F32),
                   jax.ShapeDtypeStruct((T_ROWS, D_MODEL), BF16)),
        grid=(T_ROWS // bm,),
        in_specs=[
            pl.BlockSpec((bm, ATTN_WIDTH), lambda m: (m, 0)),
            head_spec, head_spec, head_spec,
            pl.BlockSpec((1, HEAD_DIM), lambda m: (0, 0)),
            pl.BlockSpec((None, ATTN_WIDTH + HG_WIDTH, D_MODEL), lambda m: (layer, 0, 0)),
            row_spec,
            pl.BlockSpec((None, SUBLANES, 6 * D_MODEL), lambda m: (layer, 0, 0)),
            pl.BlockSpec((1, D_MODEL), lambda m: (0, 0)),
        ],
        out_specs=(row_spec, row_spec),
        scratch_shapes=[pltpu.VMEM((bm, HG_WIDTH), BF16)],
        compiler_params=_cparams(1),
        name="output_projection",
    )(attn, o_f, o_b, gate, hnw, w_out, xs, mod, n2w)


def _ffn_up_kernel(h_ref, hp_ref, hn_ref, wg_ref, wu_ref, cw_ref, cb_ref, u_ref, hcat_scr, g_scr):
    m = pl.program_id(0)
    n = pl.program_id(1)
    bm = h_ref.shape[0]
    hl = CONV_HALO

    @pl.when(n == 0)
    def _():
        hcat_scr[0:hl, :] = hp_ref[...]
        hcat_scr[hl:hl + bm, :] = h_ref[...]
        hcat_scr[hl + bm:hl + bm + hl, :] = hn_ref[...]

    g_scr[...] = jnp.dot(hcat_scr[...], wg_ref[...], preferred_element_type=F32)
    up = jnp.dot(h_ref[...], wu_ref[...], preferred_element_type=F32)
    r = m * bm + lax.broadcasted_iota(jnp.int32, (bm, 1), 0)
    has_prev = (r != 0) & (r != CTX_LEN)
    has_next = (r != CTX_LEN - 1) & (r != T_ROWS - 1)
    g_prev = jnp.where(has_prev, g_scr[hl - 1:hl - 1 + bm, :], 0.0)
    g_next = jnp.where(has_next, g_scr[hl + 1:hl + 1 + bm, :], 0.0)
    gate = (g_prev * cw_ref[0:1, :] + g_scr[hl:hl + bm, :] * cw_ref[1:2, :]
            + g_next * cw_ref[2:3, :] + cb_ref[...])
    u_ref[...] = (_silu(gate) * up).astype(BF16)


def _ffn_up(layer, h2, w_up, conv_w, conv_b):
    bm, bn, hl = FF_ROW_TILE, FF_COL_TILE, CONV_HALO
    n_n = D_FF // bn
    per = bm // hl
    last = T_ROWS // hl - 1
    return pl.pallas_call(
        _ffn_up_kernel,
        out_shape=jax.ShapeDtypeStruct((T_ROWS, D_FF), BF16),
        grid=(T_ROWS // bm, n_n),
        in_specs=[
            pl.BlockSpec((bm, D_MODEL), lambda m, n: (m, 0)),
            pl.BlockSpec((hl, D_MODEL), lambda m, n: (jnp.maximum(m * per - 1, 0), 0)),
            pl.BlockSpec((hl, D_MODEL), lambda m, n: (jnp.minimum((m + 1) * per, last), 0)),
            pl.BlockSpec((None, D_MODEL, bn), lambda m, n: (layer, 0, n)),
            pl.BlockSpec((None, D_MODEL, bn), lambda m, n: (layer, 0, n_n + n)),
            pl.BlockSpec((3, bn), lambda m, n: (0, n)),
            pl.BlockSpec((1, bn), lambda m, n: (0, n)),
        ],
        out_specs=pl.BlockSpec((bm, bn), lambda m, n: (m, n)),
        scratch_shapes=[pltpu.VMEM((bm + 2 * hl, D_MODEL), BF16),
                        pltpu.VMEM((bm + 2 * hl, bn), F32)],
        compiler_params=_cparams(2),
        name="ffn_up_conv",
    )(h2, h2, h2, w_up, w_up, conv_w, conv_b)


def _ffn_down_kernel(u_ref, w_ref, x_ref, mod_ref, fw_ref, o_ref, *, final):
    m = pl.program_id(0)

    def gate(mod_row):
        return mod_ref[mod_row:mod_row + 1, 5 * D_MODEL:6 * D_MODEL]

    def run():
        acc = jnp.dot(u_ref[...], w_ref[...], preferred_element_type=F32)
        x_new = x_ref[...] + gate(0) * acc
        o_ref[...] = _rms(x_new, fw_ref[...]) if final else x_new
        return acc

    if final:
        @pl.when(m > 0)
        def _():
            run()
    else:
        acc = run()

        @pl.when(m == 0)
        def _():
            o_ref[0:CTX_LEN, :] = x_ref[0:CTX_LEN, :] + gate(1) * acc[0:CTX_LEN]


def _ffn_down(layer, u, w_down, xs, mod, final_w, final):
    bm = CTX_LEN if final else OUT_ROW_TILE
    row_spec = pl.BlockSpec((bm, D_MODEL), lambda m: (m, 0))
    if final:
        out_rows = SEQ
        out_spec = pl.BlockSpec((bm, D_MODEL), lambda m: (jnp.maximum(m - 1, 0), 0))
    else:
        out_rows, out_spec = T_ROWS, row_spec
    return pl.pallas_call(
        functools.partial(_ffn_down_kernel, final=final),
        out_shape=jax.ShapeDtypeStruct((out_rows, D_MODEL), F32),
        grid=(T_ROWS // bm,),
        in_specs=[
            pl.BlockSpec((bm, D_FF), lambda m: (m, 0)),
            pl.BlockSpec((None, D_FF, D_MODEL), lambda m: (layer, 0, 0),
                         pipeline_mode=pl.Buffered(1)),
            row_spec,
            pl.BlockSpec((None, SUBLANES, 6 * D_MODEL), lambda m: (layer, 0, 0)),
            pl.BlockSpec((1, D_MODEL), lambda m: (0, 0)),
        ],
        out_specs=out_spec,
        compiler_params=_cparams(1),
        name="ffn_down",
    )(u, w_down, xs, mod, final_w)


def _rope_tables():
    f32 = np.float32
    rows = SEQ // GRID_W
    row = np.repeat(np.arange(rows, dtype=f32), GRID_W)
    col = np.tile(np.arange(GRID_W, dtype=f32), rows)
    freqs = np.power(f32(ROPE_THETA), -np.arange(ROPE_FREQS, dtype=f32) / f32(ROPE_FREQS)).astype(f32)
    ar, ac = row[:, None] * freqs, col[:, None] * freqs
    cos = np.concatenate([np.cos(ar), np.cos(ar), np.cos(ac), np.cos(ac)], axis=1)
    sin = np.concatenate([np.sin(ar), np.sin(ar), np.sin(ac), np.sin(ac)], axis=1)
    cos = np.concatenate([np.ones((CTX_LEN, HEAD_DIM), f32), cos], axis=0)
    sin = np.concatenate([np.zeros((CTX_LEN, HEAD_DIM), f32), sin], axis=0)
    return jnp.asarray(cos, F32), jnp.asarray(sin, F32)


def kernel(x, c, ctx, c_ctx, w_mod, b_mod, norm1_w, norm2_w, w_in, q_norm_w, k_norm_w, hg_lb_logits,
           hg_norm_w, w_out, w_up, conv_w, conv_b, w_down, final_norm_w):
    assert x.shape == (1, SEQ, D_MODEL) and ctx.shape == (1, CTX_LEN, D_MODEL)
    xs = jnp.concatenate([ctx[0], x[0]], axis=0)
    mod = _modulation(jnp.stack([c[0], c_ctx], axis=1), w_mod, b_mod)
    cos_t, sin_t = _rope_tables()
    lb_sm = jax.nn.softmax(hg_lb_logits.astype(F32), axis=1)
    lb_all = jnp.cumsum(lb_sm, axis=1) - lb_sm[:, :1]

    w_in, w_out, w_up, w_down = (w.astype(BF16) for w in (w_in, w_out, w_up, w_down))
    for l in range(DEPTH):
        lb = jnp.concatenate([lb_all[0, l], lb_all[1, l]])[None, :]
        q, k, v, hq, kf, bf, kb, bb, hv, gate = _input_projection(
            l, xs, mod, norm1_w[l][None, :], w_in, q_norm_w[l][None, :],
            k_norm_w[l][None, :], lb, cos_t, sin_t)
        attn = _attention(q, k, v)
        o_f, o_b = _hgrn2(hq, kf, bf, kb, bb, hv)
        xs, h2 = _output_projection(l, attn, o_f, o_b, gate, hg_norm_w[l][None, :],
                                    w_out, xs, mod, norm2_w[l][None, :])
        u = _ffn_up(l, h2, w_up, conv_w[l], conv_b[l][None, :])
        xs = _ffn_down(l, u, w_down, xs, mod, final_norm_w[None, :], final=(l == DEPTH - 1))
    return xs[None]
```

```python
import functools

import jax
import jax.numpy as jnp
import numpy as np
from jax import lax
from jax.experimental import pallas as pl
from jax.experimental.pallas import tpu as pltpu

F32 = jnp.float32
BF16 = jnp.bfloat16

D_MODEL = 2048
SEQ = 8192
DEPTH = 4
CTX_LEN = 256
T_ROWS = CTX_LEN + SEQ
GRID_W = 64
HEAD_DIM = 128
ATTN_HEADS = 8
ATTN_KV_HEADS = 2
ATTN_GROUP = 4
ATTN_WIDTH = 1024
KV_WIDTH = 256
HG_HEADS = 8
HG_WIDTH = 1024
IN_WIDTH = 6656
ROPE_THETA = 10000.0
ROPE_FREQS = 32
D_FF = 5632
EPS = 1e-6
Q_SCALE = HEAD_DIM ** -0.5 * 1.4426950408889634

LANES = 128
SUBLANES = 8
VMEM_LIMIT = 56 * 1024 * 1024

MOD_SLAB = 256
ROW_TILE = 768
FF_ROW_TILE = 1408
OUT_ROW_TILE = 384
IN_COL_TILE = 512
FF_COL_TILE = 512
ATTN_Q_TILE = 256
ATTN_K_TILE = 1408
ATTN_QK_PARTS = 2
HG_BLOCK = 256
HG_ITER_HEADS = 4
HG_CHUNK = 64
HG_DIAG = 1
CONV_HALO = 16


def _cparams(n_axes):
    return pltpu.CompilerParams(dimension_semantics=("arbitrary",) * n_axes,
                                vmem_limit_bytes=VMEM_LIMIT)


def _rms(x, w):
    return x * lax.rsqrt(jnp.mean(x * x, axis=-1, keepdims=True) + EPS) * w


def _silu(x):
    h = 0.5 * x
    return h * jnp.tanh(h) + h


def _mod_kernel(s_ref, w_ref, b_ref, o_ref, acc_scr):
    k = pl.program_id(1)
    n_out = w_ref.shape[2]
    cw = 4 * LANES

    @pl.when(k == 0)
    def _():
        acc_scr[...] = jnp.zeros(acc_scr.shape, F32)

    r0 = pl.multiple_of(k * MOD_SLAB, MOD_SLAB)
    s = _silu(s_ref[pl.ds(r0, MOD_SLAB), :])
    sb = [jnp.tile(jnp.broadcast_to(s[:, i:i + 1], (MOD_SLAB, LANES)), (1, cw // LANES))
          for i in range(2)]
    for c0 in range(0, n_out, cw):
        w = w_ref[0, :, c0:c0 + cw]
        for i in range(2):
            part = (w * sb[i]).reshape(MOD_SLAB // SUBLANES, SUBLANES, cw).sum(axis=0)
            acc_scr[i, :, c0:c0 + cw] += part

    @pl.when(k == pl.num_programs(1) - 1)
    def _():
        bias = b_ref[0]
        rows = [acc_scr[i].sum(axis=0, keepdims=True) + bias for i in range(2)]
        o_ref[0] = jnp.concatenate(rows + [jnp.zeros((SUBLANES - 2, n_out), F32)], axis=0)


def _modulation(s_in, w_mod, b_mod):
    n_out = 6 * D_MODEL
    return pl.pallas_call(
        _mod_kernel,
        out_shape=jax.ShapeDtypeStruct((DEPTH, SUBLANES, n_out), F32),
        grid=(DEPTH, D_MODEL // MOD_SLAB),
        in_specs=[
            pl.BlockSpec((D_MODEL, 2), lambda l, k: (0, 0)),
            pl.BlockSpec((1, MOD_SLAB, n_out), lambda l, k: (l, k, 0)),
            pl.BlockSpec((1, 1, n_out), lambda l, k: (l, 0, 0)),
        ],
        out_specs=pl.BlockSpec((1, SUBLANES, n_out), lambda l, k: (l, 0, 0)),
        scratch_shapes=[pltpu.VMEM((2, SUBLANES, n_out), F32)],
        compiler_params=_cparams(2),
        name="adaln_modulation",
    )(s_in, w_mod, b_mod.reshape(DEPTH, 1, n_out))


def _norm_rope_matrix():
    i = lax.broadcasted_iota(jnp.int32, (2 * HEAD_DIM, 2 * HEAD_DIM), 0)
    j = lax.broadcasted_iota(jnp.int32, (2 * HEAD_DIM, 2 * HEAD_DIM), 1)
    ones = (i < HEAD_DIM) & (j < HEAD_DIM)
    ii, jj = i - HEAD_DIM, j - HEAD_DIM
    low = (jj // ROPE_FREQS) % 2 == 0
    rot = jnp.where(low & (ii == jj + ROPE_FREQS), -1.0,
                    jnp.where(~low & (ii == jj - ROPE_FREQS), 1.0, 0.0))
    rot = jnp.where((i >= HEAD_DIM) & (j >= HEAD_DIM), rot, 0.0)
    return jnp.where(ones, 1.0, rot).astype(BF16)


def _norm_rope(p, w, mat_ref, cos, sin, scale):
    pw = p * w
    lhs = jnp.concatenate([(p * p).astype(BF16), pw.astype(BF16)], axis=1)
    red = jnp.dot(lhs, mat_ref[...], preferred_element_type=F32)
    r = lax.rsqrt(red[:, 0:HEAD_DIM] * (1.0 / HEAD_DIM) + EPS) * scale
    return r * (pw * cos + red[:, HEAD_DIM:2 * HEAD_DIM] * sin)


def _inproj_kernel(x_ref, mod_ref, n1w_ref, w_ref, qnw_ref, knw_ref, lb_ref, cos_ref, sin_ref, mat_ref,
                   q_ref, k_ref, v_ref, hq_ref, kf_ref, bf_ref, kb_ref, bb_ref, hv_ref, gate_ref,
                   h_scr):
    m = pl.program_id(0)
    n = pl.program_id(1)
    bm = x_ref.shape[0]

    def modulated(rows, mod_row):
        x = x_ref[rows, :]
        r = lax.rsqrt(jnp.mean(x * x, axis=-1, keepdims=True) + EPS)
        gain = n1w_ref[...] * (1.0 + mod_ref[mod_row:mod_row + 1, D_MODEL:2 * D_MODEL])
        return (x * r * gain + mod_ref[mod_row:mod_row + 1, 0:D_MODEL]).astype(BF16)

    @pl.when(n == 0)
    def _():
        h_scr[...] = modulated(slice(None), 0)

    @pl.when((n == 0) & (m == 0))
    def _():
        h_scr[0:CTX_LEN, :] = modulated(slice(0, CTX_LEN), 1)

    def project(epilogue):
        ch = IN_COL_TILE // 2
        hpt = ch // HEAD_DIM
        rh = bm // 2
        ps = []
        for ri in range(2):
            rows = slice(ri * rh, (ri + 1) * rh)
            h = h_scr[rows, :]
            ps.append((rows, [jnp.dot(h, w_ref[:, a * ch:(a + 1) * ch], preferred_element_type=F32)
                              for a in range(2)]))
        for rows, halves in ps:
            for a, p in enumerate(halves):
                for j in range(hpt):
                    epilogue(a * hpt + j, rows, p[:, j * HEAD_DIM:(j + 1) * HEAD_DIM])

    def q_out(j, rows, p):
        q_ref[j, rows, :] = _norm_rope(p, qnw_ref[...], mat_ref, cos_ref[rows, :], sin_ref[rows, :],
                                       Q_SCALE).astype(BF16)

    def kv_out(j, rows, p):
        if j < ATTN_KV_HEADS:
            k_ref[j, rows, :] = _norm_rope(p, knw_ref[...], mat_ref, cos_ref[rows, :],
                                           sin_ref[rows, :], 1.0).astype(BF16)
        else:
            v_ref[j - ATTN_KV_HEADS, rows, :] = p.astype(BF16)

    def silu_out(ref):
        def out(j, rows, p):
            ref[j, rows, :] = _silu(p).astype(BF16)
        return out

    def forget_out(k_out, b_out, rev):
        def out(j, rows, p):
            lb = lb_ref[:, j * HEAD_DIM:(j + 1) * HEAD_DIM]
            fg = lb + (1.0 - lb) * jax.nn.sigmoid(p)
            k_out[j, rows, :] = (1.0 - fg).astype(BF16)
            b_out[j, rows, :] = _chunk_cumsum(jnp.log2(fg), rev)
        return out

    def plain_out(j, rows, p):
        hv_ref[j, rows, :] = p.astype(BF16)

    pl.when(n < 2)(lambda: project(q_out))
    pl.when(n == 2)(lambda: project(kv_out))
    pl.when((n == 3) | (n == 4))(lambda: project(silu_out(hq_ref)))
    pl.when((n == 5) | (n == 6))(lambda: project(forget_out(kf_ref, bf_ref, False)))
    pl.when((n == 7) | (n == 8))(lambda: project(forget_out(kb_ref, bb_ref, True)))
    pl.when((n == 9) | (n == 10))(lambda: project(plain_out))
    pl.when(n >= 11)(lambda: project(silu_out(gate_ref)))


def _input_projection(layer, xs, mod, n1w, w_in, qnw, knw, lb, cos_t, sin_t):
    bm, bn = ROW_TILE, IN_COL_TILE
    n_m, n_n = T_ROWS // bm, IN_WIDTH // bn
    hpt = bn // HEAD_DIM

    def head_spec(first_tile):
        return pl.BlockSpec((hpt, bm, HEAD_DIM),
                            lambda m, n: (jnp.clip(n - first_tile, 0, 1), m, 0))

    def head_shape(dtype):
        return jax.ShapeDtypeStruct((HG_HEADS, T_ROWS, HEAD_DIM), dtype)

    kv_spec = pl.BlockSpec((ATTN_KV_HEADS, bm, HEAD_DIM), lambda m, n: (0, m, 0))
    kv_shape = jax.ShapeDtypeStruct((ATTN_KV_HEADS, T_ROWS, HEAD_DIM), BF16)
    return pl.pallas_call(
        _inproj_kernel,
        out_shape=(head_shape(BF16), kv_shape, kv_shape, head_shape(BF16),
                   head_shape(BF16), head_shape(F32), head_shape(BF16), head_shape(F32),
                   head_shape(BF16), head_shape(BF16)),
        grid=(n_m, n_n),
        in_specs=[
            pl.BlockSpec((bm, D_MODEL), lambda m, n: (m, 0)),
            pl.BlockSpec((None, SUBLANES, 6 * D_MODEL), lambda m, n: (layer, 0, 0)),
            pl.BlockSpec((1, D_MODEL), lambda m, n: (0, 0)),
            pl.BlockSpec((None, D_MODEL, bn), lambda m, n: (layer, 0, n)),
            pl.BlockSpec((1, HEAD_DIM), lambda m, n: (0, 0)),
            pl.BlockSpec((1, HEAD_DIM), lambda m, n: (0, 0)),
            pl.BlockSpec((1, bn), lambda m, n: (0, jnp.clip(n - 5, 0, 3))),
            pl.BlockSpec((bm, HEAD_DIM), lambda m, n: (m, 0)),
            pl.BlockSpec((bm, HEAD_DIM), lambda m, n: (m, 0)),
            pl.BlockSpec((2 * HEAD_DIM, 2 * HEAD_DIM), lambda m, n: (0, 0)),
        ],
        out_specs=(head_spec(0), kv_spec, kv_spec, head_spec(3),
                   head_spec(5), head_spec(5), head_spec(7), head_spec(7),
                   head_spec(9), head_spec(11)),
        scratch_shapes=[pltpu.VMEM((bm, D_MODEL), BF16)],
        compiler_params=_cparams(2),
        name="input_projection",
    )(xs, mod, n1w, w_in, qnw, knw, lb, cos_t, sin_t, _norm_rope_matrix())


def _attn_kernel(q_ref, k_ref, vt_ref, o_ref, acc_scr, st_scr):
    i = pl.program_id(1)
    bq = q_ref.shape[1]
    bk = ATTN_K_TILE
    n_chunks = T_ROWS // bk

    def scores(kc, r, slot):
        part = kc.shape[0] // ATTN_QK_PARTS
        mx = None
        for a in range(ATTN_QK_PARTS):
            st = lax.dot_general(kc[a * part:(a + 1) * part], q_ref[r], (((1,), (1,)), ((), ())),
                                 preferred_element_type=F32)
            st_scr[slot, a * part:(a + 1) * part, :] = st
            m_part = st.max(axis=0, keepdims=True)
            mx = m_part if mx is None else jnp.maximum(mx, m_part)
        return mx

    def update(vtc, r, slot, m_cur, m, l):
        width = vtc.shape[1]
        m_new = jnp.maximum(m, m_cur)
        alpha = jnp.exp2(m - m_new)
        p = jnp.exp2(st_scr[slot, 0:width, :] - m_new)
        l = alpha * l + p.sum(axis=0, keepdims=True)
        acc_scr[r] = alpha * acc_scr[r] + jnp.dot(vtc, p.astype(BF16), preferred_element_type=F32)
        return m_new, l

    def finish(l):
        for r in range(ATTN_GROUP):
            o_ref[:, r * HEAD_DIM:(r + 1) * HEAD_DIM] = (acc_scr[r] / l[r]).T.astype(BF16)

    acc_scr[...] = jnp.zeros(acc_scr.shape, F32)
    m0 = tuple(jnp.full((1, bq), -jnp.inf, F32) for _ in range(ATTN_GROUP))
    l0 = tuple(jnp.zeros((1, bq), F32) for _ in range(ATTN_GROUP))

    @pl.when(i == 0)
    def _():
        kc = k_ref[0, 0:CTX_LEN, :]
        vtc = vt_ref[0, 0, :, 0:CTX_LEN]
        l = []
        mc = scores(kc, 0, 0)
        for r in range(ATTN_GROUP):
            mc_next = scores(kc, r + 1, (r + 1) % 2) if r + 1 < ATTN_GROUP else None
            l.append(update(vtc, r, r % 2, mc, m0[r], l0[r])[1])
            mc = mc_next
        finish(l)

    @pl.when(i > 0)
    def _():
        def chunk(j):
            return k_ref[0, pl.ds(pl.multiple_of(j * bk, bk), bk), :]

        def body(j, carry):
            mc, m, l = carry
            m, l = list(m), list(l)
            vtc = vt_ref[0, j]
            for r in range(ATTN_GROUP):
                if r + 1 < ATTN_GROUP:
                    mc_next = scores(chunk(j), r + 1, (r + 1) % 2)
                else:
                    mc_next = scores(chunk(jnp.minimum(j + 1, n_chunks - 1)), 0, 0)
                m[r], l[r] = update(vtc, r, r % 2, mc, m[r], l[r])
                mc = mc_next
            return mc, tuple(m), tuple(l)

        _, _, l = lax.fori_loop(0, n_chunks, body, (scores(chunk(0), 0, 0), m0, l0), unroll=True)
        finish(l)


def _attention(q, k, v):
    bq, bk = ATTN_Q_TILE, ATTN_K_TILE
    n_chunks = T_ROWS // bk
    vt = v.reshape(ATTN_KV_HEADS, n_chunks, bk, HEAD_DIM).transpose(0, 1, 3, 2)
    return pl.pallas_call(
        _attn_kernel,
        out_shape=jax.ShapeDtypeStruct((T_ROWS, ATTN_WIDTH), BF16),
        grid=(ATTN_KV_HEADS, T_ROWS // bq),
        in_specs=[
            pl.BlockSpec((ATTN_GROUP, bq, HEAD_DIM), lambda g, i: (g, i, 0)),
            pl.BlockSpec((1, T_ROWS, HEAD_DIM), lambda g, i: (g, 0, 0)),
            pl.BlockSpec((1, n_chunks, HEAD_DIM, bk), lambda g, i: (g, 0, 0, 0)),
        ],
        out_specs=pl.BlockSpec((bq, ATTN_GROUP * HEAD_DIM), lambda g, i: (i, g)),
        scratch_shapes=[pltpu.VMEM((ATTN_GROUP, HEAD_DIM, bq), F32),
                        pltpu.VMEM((2, bk, bq), F32)],
        compiler_params=_cparams(2),
        name="gqa_attention",
    )(q, k, vt)


def _chunk_cumsum(g, rev):
    n_groups = g.shape[0] // SUBLANES
    x = g.reshape(n_groups, SUBLANES, g.shape[1])
    sub = lax.broadcasted_iota(jnp.int32, x.shape, 1)
    step = 1
    while step < SUBLANES:
        if rev:
            x = x + jnp.where(sub < SUBLANES - step, pltpu.roll(x, SUBLANES - step, axis=1), 0.0)
        else:
            x = x + jnp.where(sub >= step, pltpu.roll(x, step, axis=1), 0.0)
        step *= 2
    per = HG_CHUNK // SUBLANES
    xs = [x[v] for v in range(n_groups)]
    for c0 in range(0, n_groups, per):
        if rev:
            for v in range(c0 + per - 2, c0 - 1, -1):
                xs[v] = xs[v] + xs[v + 1][0:1]
        else:
            for v in range(c0 + 1, c0 + per):
                xs[v] = xs[v] + xs[v - 1][SUBLANES - 1:SUBLANES]
    return jnp.concatenate(xs, axis=0)


def _hgrn_levels():
    halves = []
    half = HG_CHUNK // 2
    while half >= HG_DIAG:
        halves.append(half)
        half //= 2
    return halves


def _hgrn_consts(rev):
    c = HG_CHUNK
    levels = {}
    for half in _hgrn_levels():
        blk = 2 * half
        levels[half] = []
        for g in range(c // SUBLANES):
            t = lax.broadcasted_iota(jnp.int32, (SUBLANES, c), 0) + g * SUBLANES
            s = lax.broadcasted_iota(jnp.int32, (SUBLANES, c), 1)
            same = t // blk == s // blk
            if rev:
                levels[half].append(same & (t % blk < half) & (s % blk >= half))
            else:
                levels[half].append(same & (t % blk >= half) & (s % blk < half))
    t = lax.broadcasted_iota(jnp.int32, (c, c), 0)
    s = lax.broadcasted_iota(jnp.int32, (c, c), 1)
    band = []
    for d in range(HG_DIAG):
        if rev:
            band.append((s == t + d) & (t % HG_DIAG + d < HG_DIAG))
        else:
            band.append((s == t - d) & (t % HG_DIAG >= d))
    return levels, band


def _hgrn_scores(q, k, b, brow, rev, consts):
    levels, band = consts
    c = HG_CHUNK
    nt = (((1,), (1,)), ((), ()))
    groups = c // SUBLANES
    row8 = lax.broadcasted_iota(jnp.int32, (SUBLANES, HEAD_DIM), 0)
    rows = [slice(g * SUBLANES, (g + 1) * SUBLANES) for g in range(groups)]

    a = jnp.zeros((c, c), F32)
    for d in range(HG_DIAG):
        if d == 0:
            prod = q * k
        else:
            shift = c - d if rev else d
            prod = q * pltpu.roll(k, shift, axis=0) * jnp.exp2(b - pltpu.roll(b, shift, axis=0))
        a = jnp.where(band[d], prod.sum(axis=-1, keepdims=True), a)
    rows_a = [a[r] for r in rows]

    for half in _hgrn_levels():
        blk = 2 * half
        ref_off = half if rev else half - 1
        zs = []
        owners = []
        for g in range(groups):
            t0 = g * SUBLANES
            if half == 1:
                attending = (row8 % 2 == 0) if rev else (row8 % 2 == 1)
                zs.append(jnp.where(attending, q[rows[g]] * (1.0 - k[rows[g]]), k[rows[g]]))
                owners.append(g)
                continue
            if half >= SUBLANES:
                ref = brow(t0 // blk * blk + ref_off)
                attending = (t0 % blk < half) if rev else (t0 % blk >= half)
                x = q[rows[g]] if attending else k[rows[g]]
                e = jnp.exp2(b[rows[g]] - ref if attending else ref - b[rows[g]])
                if attending:
                    owners.append(g)
            else:
                ref = brow(t0 + ref_off)
                for i in range(1, SUBLANES // blk):
                    ref = jnp.where(row8 >= i * blk, brow(t0 + i * blk + ref_off), ref)
                attending = (row8 % blk < half) if rev else (row8 % blk >= half)
                x = jnp.where(attending, q[rows[g]], k[rows[g]])
                e = jnp.exp2(-jnp.abs(b[rows[g]] - ref))
                owners.append(g)
            zs.append(x * e)
        z = jnp.concatenate(zs, axis=0).astype(BF16)
        lv = lax.dot_general(z, z, nt, preferred_element_type=F32)
        for g in owners:
            rows_a[g] = jnp.where(levels[half][g], lv[rows[g]], rows_a[g])
    return jnp.concatenate(rows_a, axis=0)


def _hgrn_kernel(qf_ref, kf_ref, bf_ref, vf_ref, qb_ref, kb_ref, bb_ref, vb_ref,
                 of_ref, ob_ref, s_scr):
    @pl.when(pl.program_id(0) == 0)
    def _():
        s_scr[...] = jnp.zeros(s_scr.shape, F32)

    c = HG_CHUNK
    n_chunks = HG_BLOCK // c
    consts = (_hgrn_consts(False), _hgrn_consts(True))
    nt = (((1,), (1,)), ((), ()))
    tn = (((0,), (0,)), ((), ()))
    dirs = ((qf_ref, kf_ref, bf_ref, vf_ref, of_ref, False),
            (qb_ref, kb_ref, bb_ref, vb_ref, ob_ref, True))

    def head_body(it, carry):
        jobs = []
        for hi in range(HG_ITER_HEADS):
            h = it * HG_ITER_HEADS + hi
            for di, (q_ref, k_ref, b_ref, v_ref, o_ref, rev) in enumerate(dirs):
                order = reversed(range(n_chunks)) if rev else range(n_chunks)
                for ci in order:
                    rows = slice(ci * c, (ci + 1) * c)
                    q = q_ref[h, rows, :].astype(F32)
                    k = k_ref[h, rows, :].astype(F32)
                    b = b_ref[h, rows, :]
                    v = v_ref[h, rows, :]
                    tot = b[0:1] if rev else b[c - 1:c]
                    q_in = (q * jnp.exp2(b)).astype(BF16)
                    k_out = (k * jnp.exp2(tot - b)).astype(BF16)
                    upd = lax.dot_general(v, k_out, tn, preferred_element_type=F32)

                    def brow(r, b_ref=b_ref, base=ci * c, h=h):
                        return b_ref[h, base + r:base + r + 1, :]

                    a = _hgrn_scores(q, k, b, brow, rev, consts[di]).astype(BF16)
                    jobs.append((hi, h, di, o_ref, rows, q_in, jnp.exp2(tot), upd, a, v))
        st = {(hi, di): s_scr[di, it * HG_ITER_HEADS + hi]
              for hi in range(HG_ITER_HEADS) for di in range(2)}
        for hi, h, di, o_ref, rows, q_in, decay, upd, a, v in jobs:
            o = lax.dot_general(q_in, st[hi, di].astype(BF16), nt, preferred_element_type=F32)
            st[hi, di] = decay * st[hi, di] + upd
            o_ref[h, rows, :] = o + jnp.dot(a, v, preferred_element_type=F32)
        for (hi, di), s in st.items():
            s_scr[di, it * HG_ITER_HEADS + hi] = s
        return carry

    lax.fori_loop(0, HG_HEADS // HG_ITER_HEADS, head_body, 0)


def _hgrn2(hq, kf, bf, kb, bb, hv):
    nb = T_ROWS // HG_BLOCK
    fwd = lambda i: (0, i, 0)
    bwd = lambda i: (0, jnp.where(i == 0, 0, nb - i), 0)
    blk = (HG_HEADS, HG_BLOCK, HEAD_DIM)
    out = jax.ShapeDtypeStruct((HG_HEADS, T_ROWS, HEAD_DIM), F32)
    return pl.pallas_call(
        _hgrn_kernel,
        out_shape=(out, out),
        grid=(nb,),
        in_specs=[pl.BlockSpec(blk, fwd)] * 4 + [pl.BlockSpec(blk, bwd)] * 4,
        out_specs=(pl.BlockSpec(blk, fwd), pl.BlockSpec(blk, bwd)),
        scratch_shapes=[pltpu.VMEM((2, HG_HEADS, HEAD_DIM, HEAD_DIM), F32)],
        compiler_params=_cparams(1),
        name="hgrn2_scan",
    )(hq, kf, bf, hv, hq, kb, bb, hv)


def _outproj_kernel(attn_ref, of_ref, ob_ref, gate_ref, hnw_ref, w_ref, x_ref, mod_ref, n2w_ref,
                    xo_ref, h2_ref, mix_scr):
    m = pl.program_id(0)
    bm = x_ref.shape[0]
    acc = jnp.dot(attn_ref[...], w_ref[0:ATTN_WIDTH, :], preferred_element_type=F32)
    for h in range(HG_HEADS):
        y = _rms(of_ref[h] + ob_ref[h], hnw_ref[...])
        mix_scr[:, h * HEAD_DIM:(h + 1) * HEAD_DIM] = (y * gate_ref[h].astype(F32)).astype(BF16)
    acc = acc + jnp.dot(mix_scr[...], w_ref[ATTN_WIDTH:ATTN_WIDTH + HG_WIDTH, :],
                        preferred_element_type=F32)

    def finish(rows, mod_row):
        def mod(k):
            return mod_ref[mod_row:mod_row + 1, k * D_MODEL:(k + 1) * D_MODEL]
        x_new = x_ref[rows, :] + mod(2) * acc[rows]
        xo_ref[rows, :] = x_new
        r = lax.rsqrt(jnp.mean(x_new * x_new, axis=-1, keepdims=True) + EPS)
        h2_ref[rows, :] = (x_new * r * (n2w_ref[...] * (1.0 + mod(4))) + mod(3)).astype(BF16)

    finish(slice(0, bm), 0)

    @pl.when(m == 0)
    def _():
        finish(slice(0, CTX_LEN), 1)


def _output_projection(layer, attn, o_f, o_b, gate, hnw, w_out, xs, mod, n2w):
    bm = OUT_ROW_TILE
    head_spec = pl.BlockSpec((HG_HEADS, bm, HEAD_DIM), lambda m: (0, m, 0))
    row_spec = pl.BlockSpec((bm, D_MODEL), lambda m: (m, 0))
    return pl.pallas_call(
        _outproj_kernel,
        out_shape=(jax.ShapeDtypeStruct((T_ROWS, D_MODEL), F32),
                   jax.ShapeDtypeStruct((T_ROWS, D_MODEL), BF16)),
        grid=(T_ROWS // bm,),
        in_specs=[
            pl.BlockSpec((bm, ATTN_WIDTH), lambda m: (m, 0)),
            head_spec, head_spec, head_spec,
            pl.BlockSpec((1, HEAD_DIM), lambda m: (0, 0)),
            pl.BlockSpec((None, ATTN_WIDTH + HG_WIDTH, D_MODEL), lambda m: (layer, 0, 0)),
            row_spec,
            pl.BlockSpec((None, SUBLANES, 6 * D_MODEL), lambda m: (layer, 0, 0)),
            pl.BlockSpec((1, D_MODEL), lambda m: (0, 0)),
        ],
        out_specs=(row_spec, row_spec),
        scratch_shapes=[pltpu.VMEM((bm, HG_WIDTH), BF16)],
        compiler_params=_cparams(1),
        name="output_projection",
    )(attn, o_f, o_b, gate, hnw, w_out, xs, mod, n2w)


def _ffn_up_kernel(h_ref, hp_ref, hn_ref, wg_ref, wu_ref, cw_ref, cb_ref, u_ref, hcat_scr, g_scr):
    m = pl.program_id(0)
    n = pl.program_id(1)
    bm = h_ref.shape[0]
    hl = CONV_HALO

    @pl.when(n == 0)
    def _():
        hcat_scr[0:hl, :] = hp_ref[...]
        hcat_scr[hl:hl + bm, :] = h_ref[...]
        hcat_scr[hl + bm:hl + bm + hl, :] = hn_ref[...]

    g_scr[...] = jnp.dot(hcat_scr[...], wg_ref[...], preferred_element_type=F32)
    up = jnp.dot(h_ref[...], wu_ref[...], preferred_element_type=F32)
    r = m * bm + lax.broadcasted_iota(jnp.int32, (bm, 1), 0)
    has_prev = (r != 0) & (r != CTX_LEN)
    has_next = (r != CTX_LEN - 1) & (r != T_ROWS - 1)
    g_prev = jnp.where(has_prev, g_scr[hl - 1:hl - 1 + bm, :], 0.0)
    g_next = jnp.where(has_next, g_scr[hl + 1:hl + 1 + bm, :], 0.0)
    gate = (g_prev * cw_ref[0:1, :] + g_scr[hl:hl + bm, :] * cw_ref[1:2, :]
            + g_next * cw_ref[2:3, :] + cb_ref[...])
    u_ref[...] = (_silu(gate) * up).astype(BF16)


def _ffn_up(layer, h2, w_up, conv_w, conv_b):
    bm, bn, hl = FF_ROW_TILE, FF_COL_TILE, CONV_HALO
    n_n = D_FF // bn
    per = bm // hl
    last = T_ROWS // hl - 1
    return pl.pallas_call(
        _ffn_up_kernel,
        out_shape=jax.ShapeDtypeStruct((T_ROWS, D_FF), BF16),
        grid=(T_ROWS // bm, n_n),
        in_specs=[
            pl.BlockSpec((bm, D_MODEL), lambda m, n: (m, 0)),
            pl.BlockSpec((hl, D_MODEL), lambda m, n: (jnp.maximum(m * per - 1, 0), 0)),
            pl.BlockSpec((hl, D_MODEL), lambda m, n: (jnp.minimum((m + 1) * per, last), 0)),
            pl.BlockSpec((None, D_MODEL, bn), lambda m, n: (layer, 0, n)),
            pl.BlockSpec((None, D_MODEL, bn), lambda m, n: (layer, 0, n_n + n)),
            pl.BlockSpec((3, bn), lambda m, n: (0, n)),
            pl.BlockSpec((1, bn), lambda m, n: (0, n)),
        ],
        out_specs=pl.BlockSpec((bm, bn), lambda m, n: (m, n)),
        scratch_shapes=[pltpu.VMEM((bm + 2 * hl, D_MODEL), BF16),
                        pltpu.VMEM((bm + 2 * hl, bn), F32)],
        compiler_params=_cparams(2),
        name="ffn_up_conv",
    )(h2, h2, h2, w_up, w_up, conv_w, conv_b)


def _ffn_down_kernel(u_ref, w_ref, x_ref, mod_ref, fw_ref, o_ref, *, final):
    m = pl.program_id(0)

    def gate(mod_row):
        return mod_ref[mod_row:mod_row + 1, 5 * D_MODEL:6 * D_MODEL]

    def run():
        acc = jnp.dot(u_ref[...], w_ref[...], preferred_element_type=F32)
        x_new = x_ref[...] + gate(0) * acc
        o_ref[...] = _rms(x_new, fw_ref[...]) if final else x_new
        return acc

    if final:
        @pl.when(m > 0)
        def _():
            run()
    else:
        acc = run()

        @pl.when(m == 0)
        def _():
            o_ref[0:CTX_LEN, :] = x_ref[0:CTX_LEN, :] + gate(1) * acc[0:CTX_LEN]


def _ffn_down(layer, u, w_down, xs, mod, final_w, final):
    bm = CTX_LEN if final else OUT_ROW_TILE
    row_spec = pl.BlockSpec((bm, D_MODEL), lambda m: (m, 0))
    if final:
        out_rows = SEQ
        out_spec = pl.BlockSpec((bm, D_MODEL), lambda m: (jnp.maximum(m - 1, 0), 0))
    else:
        out_rows, out_spec = T_ROWS, row_spec
    return pl.pallas_call(
        functools.partial(_ffn_down_kernel, final=final),
        out_shape=jax.ShapeDtypeStruct((out_rows, D_MODEL), F32),
        grid=(T_ROWS // bm,),
        in_specs=[
            pl.BlockSpec((bm, D_FF), lambda m: (m, 0)),
            pl.BlockSpec((None, D_FF, D_MODEL), lambda m: (layer, 0, 0),
                         pipeline_mode=pl.Buffered(1)),
            row_spec,
            pl.BlockSpec((None, SUBLANES, 6 * D_MODEL), lambda m: (layer, 0, 0)),
            pl.BlockSpec((1, D_MODEL), lambda m: (0, 0)),
        ],
        out_specs=out_spec,
        compiler_params=_cparams(1),
        name="ffn_down",
    )(u, w_down, xs, mod, final_w)


def _rope_tables():
    f32 = np.float32
    rows = SEQ // GRID_W
    row = np.repeat(np.arange(rows, dtype=f32), GRID_W)
    col = np.tile(np.arange(GRID_W, dtype=f32), rows)
    freqs = np.power(f32(ROPE_THETA), -np.arange(ROPE_FREQS, dtype=f32) / f32(ROPE_FREQS)).astype(f32)
    ar, ac = row[:, None] * freqs, col[:, None] * freqs
    cos = np.concatenate([np.cos(ar), np.cos(ar), np.cos(ac), np.cos(ac)], axis=1)
    sin = np.concatenate([np.sin(ar), np.sin(ar), np.sin(ac), np.sin(ac)], axis=1)
    cos = np.concatenate([np.ones((CTX_LEN, HEAD_DIM), f32), cos], axis=0)
    sin = np.concatenate([np.zeros((CTX_LEN, HEAD_DIM), f32), sin], axis=0)
    return jnp.asarray(cos, F32), jnp.asarray(sin, F32)


def kernel(x, c, ctx, c_ctx, w_mod, b_mod, norm1_w, norm2_w, w_in, q_norm_w, k_norm_w, hg_lb_logits,
           hg_norm_w, w_out, w_up, conv_w, conv_b, w_down, final_norm_w):
    assert x.shape == (1, SEQ, D_MODEL) and ctx.shape == (1, CTX_LEN, D_MODEL)
    xs = jnp.concatenate([ctx[0], x[0]], axis=0)
    mod = _modulation(jnp.stack([c[0], c_ctx], axis=1), w_mod, b_mod)
    cos_t, sin_t = _rope_tables()
    lb_sm = jax.nn.softmax(hg_lb_logits.astype(F32), axis=1)
    lb_all = jnp.cumsum(lb_sm, axis=1) - lb_sm[:, :1]

    w_in, w_out, w_up, w_down = (w.astype(BF16) for w in (w_in, w_out, w_up, w_down))
    for l in range(DEPTH):
        lb = jnp.concatenate([lb_all[0, l], lb_all[1, l]])[None, :]
        q, k, v, hq, kf, bf, kb, bb, hv, gate = _input_projection(
            l, xs, mod, norm1_w[l][None, :], w_in, q_norm_w[l][None, :],
            k_norm_w[l][None, :], lb, cos_t, sin_t)
        attn = _attention(q, k, v)
        o_f, o_b = _hgrn2(hq, kf, bf, kb, bb, hv)
        xs, h2 = _output_projection(l, attn, o_f, o_b, gate, hg_norm_w[l][None, :],
                                    w_out, xs, mod, norm2_w[l][None, :])
        u = _ffn_up(l, h2, w_up, conv_w[l], conv_b[l][None, :])
        xs = _ffn_down(l, u, w_down, xs, mod, final_norm_w[None, :], final=(l == DEPTH - 1))
    return xs[None]
```

```python
import functools

import jax
import jax.numpy as jnp
import numpy as np
from jax import lax
from jax.experimental import pallas as pl
from jax.experimental.pallas import tpu as pltpu

F32 = jnp.float32
BF16 = jnp.bfloat16

D_MODEL = 2048
SEQ = 8192
DEPTH = 4
CTX_LEN = 256
T_ROWS = CTX_LEN + SEQ
GRID_W = 64
HEAD_DIM = 128
ATTN_HEADS = 8
ATTN_KV_HEADS = 2
ATTN_GROUP = 4
ATTN_WIDTH = 1024
KV_WIDTH = 256
HG_HEADS = 8
HG_WIDTH = 1024
IN_WIDTH = 6656
ROPE_THETA = 10000.0
ROPE_FREQS = 32
D_FF = 5632
EPS = 1e-6
Q_SCALE = HEAD_DIM ** -0.5 * 1.4426950408889634

LANES = 128
SUBLANES = 8
VMEM_LIMIT = 56 * 1024 * 1024

MOD_SLAB = 256
ROW_TILE = 768
FF_ROW_TILE = 1408
OUT_ROW_TILE = 384
IN_COL_TILE = 512
FF_COL_TILE = 512
ATTN_Q_TILE = 256
ATTN_K_TILE = 1408
ATTN_QK_PARTS = 2
HG_BLOCK = 256
HG_ITER_HEADS = 4
HG_CHUNK = 64
HG_DIAG = 1
EPI_ROWS = 128
CONV_HALO = 16


def _cparams(n_axes):
    return pltpu.CompilerParams(dimension_semantics=("arbitrary",) * n_axes,
                                vmem_limit_bytes=VMEM_LIMIT)


def _rms(x, w):
    return x * lax.rsqrt(jnp.mean(x * x, axis=-1, keepdims=True) + EPS) * w


def _silu(x):
    h = 0.5 * x
    return h * jnp.tanh(h) + h


def _mod_kernel(s_ref, w_ref, b_ref, o_ref, acc_scr):
    k = pl.program_id(1)
    n_out = w_ref.shape[2]
    cw = 4 * LANES

    @pl.when(k == 0)
    def _():
        acc_scr[...] = jnp.zeros(acc_scr.shape, F32)

    r0 = pl.multiple_of(k * MOD_SLAB, MOD_SLAB)
    s = _silu(s_ref[pl.ds(r0, MOD_SLAB), :])
    sb = [jnp.tile(jnp.broadcast_to(s[:, i:i + 1], (MOD_SLAB, LANES)), (1, cw // LANES))
          for i in range(2)]
    for c0 in range(0, n_out, cw):
        w = w_ref[0, :, c0:c0 + cw]
        for i in range(2):
            part = (w * sb[i]).reshape(MOD_SLAB // SUBLANES, SUBLANES, cw).sum(axis=0)
            acc_scr[i, :, c0:c0 + cw] += part

    @pl.when(k == pl.num_programs(1) - 1)
    def _():
        bias = b_ref[0]
        rows = [acc_scr[i].sum(axis=0, keepdims=True) + bias for i in range(2)]
        o_ref[0] = jnp.concatenate(rows + [jnp.zeros((SUBLANES - 2, n_out), F32)], axis=0)


def _modulation(s_in, w_mod, b_mod):
    n_out = 6 * D_MODEL
    return pl.pallas_call(
        _mod_kernel,
        out_shape=jax.ShapeDtypeStruct((DEPTH, SUBLANES, n_out), F32),
        grid=(DEPTH, D_MODEL // MOD_SLAB),
        in_specs=[
            pl.BlockSpec((D_MODEL, 2), lambda l, k: (0, 0)),
            pl.BlockSpec((1, MOD_SLAB, n_out), lambda l, k: (l, k, 0)),
            pl.BlockSpec((1, 1, n_out), lambda l, k: (l, 0, 0)),
        ],
        out_specs=pl.BlockSpec((1, SUBLANES, n_out), lambda l, k: (l, 0, 0)),
        scratch_shapes=[pltpu.VMEM((2, SUBLANES, n_out), F32)],
        compiler_params=_cparams(2),
        name="adaln_modulation",
    )(s_in, w_mod, b_mod.reshape(DEPTH, 1, n_out))


def _norm_rope_matrix():
    i = lax.broadcasted_iota(jnp.int32, (2 * HEAD_DIM, 2 * HEAD_DIM), 0)
    j = lax.broadcasted_iota(jnp.int32, (2 * HEAD_DIM, 2 * HEAD_DIM), 1)
    ones = (i < HEAD_DIM) & (j < HEAD_DIM)
    ii, jj = i - HEAD_DIM, j - HEAD_DIM
    low = (jj // ROPE_FREQS) % 2 == 0
    rot = jnp.where(low & (ii == jj + ROPE_FREQS), -1.0,
                    jnp.where(~low & (ii == jj - ROPE_FREQS), 1.0, 0.0))
    rot = jnp.where((i >= HEAD_DIM) & (j >= HEAD_DIM), rot, 0.0)
    return jnp.where(ones, 1.0, rot).astype(BF16)


def _norm_rope(p, w, mat_ref, cos, sin, scale):
    pw = p * w
    lhs = jnp.concatenate([(p * p).astype(BF16), pw.astype(BF16)], axis=1)
    red = jnp.dot(lhs, mat_ref[...], preferred_element_type=F32)
    r = lax.rsqrt(red[:, 0:HEAD_DIM] * (1.0 / HEAD_DIM) + EPS) * scale
    return r * (pw * cos + red[:, HEAD_DIM:2 * HEAD_DIM] * sin)


def _inproj_kernel(x_ref, mod_ref, n1w_ref, w_ref, qnw_ref, knw_ref, lb_ref, cos_ref, sin_ref, mat_ref,
                   q_ref, k_ref, v_ref, hq_ref, kf_ref, bf_ref, kb_ref, bb_ref, hv_ref, gate_ref,
                   h_scr):
    m = pl.program_id(0)
    n = pl.program_id(1)
    bm = x_ref.shape[0]

    def modulated(rows, mod_row):
        x = x_ref[rows, :]
        r = lax.rsqrt(jnp.mean(x * x, axis=-1, keepdims=True) + EPS)
        gain = n1w_ref[...] * (1.0 + mod_ref[mod_row:mod_row + 1, D_MODEL:2 * D_MODEL])
        return (x * r * gain + mod_ref[mod_row:mod_row + 1, 0:D_MODEL]).astype(BF16)

    @pl.when(n == 0)
    def _():
        for r0 in range(0, bm, EPI_ROWS):
            h_scr[r0:r0 + EPI_ROWS, :] = modulated(slice(r0, r0 + EPI_ROWS), 0)

    @pl.when((n == 0) & (m == 0))
    def _():
        h_scr[0:CTX_LEN, :] = modulated(slice(0, CTX_LEN), 1)

    def project(epilogue):
        ch = IN_COL_TILE // 2
        hpt = ch // HEAD_DIM
        rh = bm // 2
        ps = []
        for ri in range(2):
            rows = slice(ri * rh, (ri + 1) * rh)
            h = h_scr[rows, :]
            ps.append((rows, [jnp.dot(h, w_ref[:, a * ch:(a + 1) * ch], preferred_element_type=F32)
                              for a in range(2)]))
        for rows, halves in ps:
            for a, p in enumerate(halves):
                for j in range(hpt):
                    epilogue(a * hpt + j, rows, p[:, j * HEAD_DIM:(j + 1) * HEAD_DIM])

    def q_out(j, rows, p):
        q_ref[j, rows, :] = _norm_rope(p, qnw_ref[...], mat_ref, cos_ref[rows, :], sin_ref[rows, :],
                                       Q_SCALE).astype(BF16)

    def kv_out(j, rows, p):
        if j < ATTN_KV_HEADS:
            k_ref[j, rows, :] = _norm_rope(p, knw_ref[...], mat_ref, cos_ref[rows, :],
                                           sin_ref[rows, :], 1.0).astype(BF16)
        else:
            v_ref[j - ATTN_KV_HEADS, rows, :] = p.astype(BF16)

    def silu_out(ref):
        def out(j, rows, p):
            ref[j, rows, :] = _silu(p).astype(BF16)
        return out

    def forget_out(k_out, b_out, rev):
        def out(j, rows, p):
            lb = lb_ref[:, j * HEAD_DIM:(j + 1) * HEAD_DIM]
            fg = lb + (1.0 - lb) * jax.nn.sigmoid(p)
            k_out[j, rows, :] = (1.0 - fg).astype(BF16)
            b_out[j, rows, :] = _chunk_cumsum(jnp.log2(fg), rev)
        return out

    def plain_out(j, rows, p):
        hv_ref[j, rows, :] = p.astype(BF16)

    pl.when(n < 2)(lambda: project(q_out))
    pl.when(n == 2)(lambda: project(kv_out))
    pl.when((n == 3) | (n == 4))(lambda: project(silu_out(hq_ref)))
    pl.when((n == 5) | (n == 6))(lambda: project(forget_out(kf_ref, bf_ref, False)))
    pl.when((n == 7) | (n == 8))(lambda: project(forget_out(kb_ref, bb_ref, True)))
    pl.when((n == 9) | (n == 10))(lambda: project(plain_out))
    pl.when(n >= 11)(lambda: project(silu_out(gate_ref)))


def _input_projection(layer, xs, mod, n1w, w_in, qnw, knw, lb, cos_t, sin_t):
    bm, bn = ROW_TILE, IN_COL_TILE
    n_m, n_n = T_ROWS // bm, IN_WIDTH // bn
    hpt = bn // HEAD_DIM

    def head_spec(first_tile):
        return pl.BlockSpec((hpt, bm, HEAD_DIM),
                            lambda m, n: (jnp.clip(n - first_tile, 0, 1), m, 0))

    def head_shape(dtype):
        return jax.ShapeDtypeStruct((HG_HEADS, T_ROWS, HEAD_DIM), dtype)

    kv_spec = pl.BlockSpec((ATTN_KV_HEADS, bm, HEAD_DIM), lambda m, n: (0, m, 0))
    kv_shape = jax.ShapeDtypeStruct((ATTN_KV_HEADS, T_ROWS, HEAD_DIM), BF16)
    return pl.pallas_call(
        _inproj_kernel,
        out_shape=(head_shape(BF16), kv_shape, kv_shape, head_shape(BF16),
                   head_shape(BF16), head_shape(F32), head_shape(BF16), head_shape(F32),
                   head_shape(BF16), head_shape(BF16)),
        grid=(n_m, n_n),
        in_specs=[
            pl.BlockSpec((bm, D_MODEL), lambda m, n: (m, 0)),
            pl.BlockSpec((None, SUBLANES, 6 * D_MODEL), lambda m, n: (layer, 0, 0)),
            pl.BlockSpec((1, D_MODEL), lambda m, n: (0, 0)),
            pl.BlockSpec((None, D_MODEL, bn), lambda m, n: (layer, 0, n)),
            pl.BlockSpec((1, HEAD_DIM), lambda m, n: (0, 0)),
            pl.BlockSpec((1, HEAD_DIM), lambda m, n: (0, 0)),
            pl.BlockSpec((1, bn), lambda m, n: (0, jnp.clip(n - 5, 0, 3))),
            pl.BlockSpec((bm, HEAD_DIM), lambda m, n: (m, 0)),
            pl.BlockSpec((bm, HEAD_DIM), lambda m, n: (m, 0)),
            pl.BlockSpec((2 * HEAD_DIM, 2 * HEAD_DIM), lambda m, n: (0, 0)),
        ],
        out_specs=(head_spec(0), kv_spec, kv_spec, head_spec(3),
                   head_spec(5), head_spec(5), head_spec(7), head_spec(7),
                   head_spec(9), head_spec(11)),
        scratch_shapes=[pltpu.VMEM((bm, D_MODEL), BF16)],
        compiler_params=_cparams(2),
        name="input_projection",
    )(xs, mod, n1w, w_in, qnw, knw, lb, cos_t, sin_t, _norm_rope_matrix())


def _attn_kernel(q_ref, k_ref, vt_ref, o_ref, acc_scr, st_scr):
    i = pl.program_id(1)
    bq = q_ref.shape[1]
    bk = ATTN_K_TILE
    n_chunks = T_ROWS // bk

    def scores(kc, r, slot):
        part = kc.shape[0] // ATTN_QK_PARTS
        mx = None
        for a in range(ATTN_QK_PARTS):
            st = lax.dot_general(kc[a * part:(a + 1) * part], q_ref[r], (((1,), (1,)), ((), ())),
                                 preferred_element_type=F32)
            st_scr[slot, a * part:(a + 1) * part, :] = st
            m_part = st.max(axis=0, keepdims=True)
            mx = m_part if mx is None else jnp.maximum(mx, m_part)
        return mx

    def update(vtc, r, slot, m_cur, m, l):
        width = vtc.shape[1]
        m_new = jnp.maximum(m, m_cur)
        alpha = jnp.exp2(m - m_new)
        p = jnp.exp2(st_scr[slot, 0:width, :] - m_new)
        l = alpha * l + p.sum(axis=0, keepdims=True)
        acc_scr[r] = alpha * acc_scr[r] + jnp.dot(vtc, p.astype(BF16), preferred_element_type=F32)
        return m_new, l

    def finish(l):
        for r in range(ATTN_GROUP):
            o_ref[:, r * HEAD_DIM:(r + 1) * HEAD_DIM] = (acc_scr[r] / l[r]).T.astype(BF16)

    acc_scr[...] = jnp.zeros(acc_scr.shape, F32)
    m0 = tuple(jnp.full((1, bq), -jnp.inf, F32) for _ in range(ATTN_GROUP))
    l0 = tuple(jnp.zeros((1, bq), F32) for _ in range(ATTN_GROUP))

    @pl.when(i == 0)
    def _():
        kc = k_ref[0, 0:CTX_LEN, :]
        vtc = vt_ref[0, 0, :, 0:CTX_LEN]
        l = []
        mc = scores(kc, 0, 0)
        for r in range(ATTN_GROUP):
            mc_next = scores(kc, r + 1, (r + 1) % 2) if r + 1 < ATTN_GROUP else None
            l.append(update(vtc, r, r % 2, mc, m0[r], l0[r])[1])
            mc = mc_next
        finish(l)

    @pl.when(i > 0)
    def _():
        def chunk(j):
            return k_ref[0, pl.ds(pl.multiple_of(j * bk, bk), bk), :]

        def body(j, carry):
            mc, m, l = carry
            m, l = list(m), list(l)
            vtc = vt_ref[0, j]
            for r in range(ATTN_GROUP):
                if r + 1 < ATTN_GROUP:
                    mc_next = scores(chunk(j), r + 1, (r + 1) % 2)
                else:
                    mc_next = scores(chunk(jnp.minimum(j + 1, n_chunks - 1)), 0, 0)
                m[r], l[r] = update(vtc, r, r % 2, mc, m[r], l[r])
                mc = mc_next
            return mc, tuple(m), tuple(l)

        _, _, l = lax.fori_loop(0, n_chunks, body, (scores(chunk(0), 0, 0), m0, l0), unroll=True)
        finish(l)


def _attention(q, k, v):
    bq, bk = ATTN_Q_TILE, ATTN_K_TILE
    n_chunks = T_ROWS // bk
    vt = v.reshape(ATTN_KV_HEADS, n_chunks, bk, HEAD_DIM).transpose(0, 1, 3, 2)
    return pl.pallas_call(
        _attn_kernel,
        out_shape=jax.ShapeDtypeStruct((T_ROWS, ATTN_WIDTH), BF16),
        grid=(ATTN_KV_HEADS, T_ROWS // bq),
        in_specs=[
            pl.BlockSpec((ATTN_GROUP, bq, HEAD_DIM), lambda g, i: (g, i, 0)),
            pl.BlockSpec((1, T_ROWS, HEAD_DIM), lambda g, i: (g, 0, 0)),
            pl.BlockSpec((1, n_chunks, HEAD_DIM, bk), lambda g, i: (g, 0, 0, 0)),
        ],
        out_specs=pl.BlockSpec((bq, ATTN_GROUP * HEAD_DIM), lambda g, i: (i, g)),
        scratch_shapes=[pltpu.VMEM((ATTN_GROUP, HEAD_DIM, bq), F32),
                        pltpu.VMEM((2, bk, bq), F32)],
        compiler_params=_cparams(2),
        name="gqa_attention",
    )(q, k, vt)


def _chunk_cumsum(g, rev):
    n_groups = g.shape[0] // SUBLANES
    x = g.reshape(n_groups, SUBLANES, g.shape[1])
    sub = lax.broadcasted_iota(jnp.int32, x.shape, 1)
    step = 1
    while step < SUBLANES:
        if rev:
            x = x + jnp.where(sub < SUBLANES - step, pltpu.roll(x, SUBLANES - step, axis=1), 0.0)
        else:
            x = x + jnp.where(sub >= step, pltpu.roll(x, step, axis=1), 0.0)
        step *= 2
    per = HG_CHUNK // SUBLANES
    xs = [x[v] for v in range(n_groups)]
    for c0 in range(0, n_groups, per):
        if rev:
            for v in range(c0 + per - 2, c0 - 1, -1):
                xs[v] = xs[v] + xs[v + 1][0:1]
        else:
            for v in range(c0 + 1, c0 + per):
                xs[v] = xs[v] + xs[v - 1][SUBLANES - 1:SUBLANES]
    return jnp.concatenate(xs, axis=0)


def _hgrn_levels():
    halves = []
    half = HG_CHUNK // 2
    while half >= HG_DIAG:
        halves.append(half)
        half //= 2
    return halves


def _hgrn_consts(rev):
    c = HG_CHUNK
    levels = {}
    for half in _hgrn_levels():
        blk = 2 * half
        levels[half] = []
        for g in range(c // SUBLANES):
            t = lax.broadcasted_iota(jnp.int32, (SUBLANES, c), 0) + g * SUBLANES
            s = lax.broadcasted_iota(jnp.int32, (SUBLANES, c), 1)
            same = t // blk == s // blk
            if rev:
                levels[half].append(same & (t % blk < half) & (s % blk >= half))
            else:
                levels[half].append(same & (t % blk >= half) & (s % blk < half))
    t = lax.broadcasted_iota(jnp.int32, (c, c), 0)
    s = lax.broadcasted_iota(jnp.int32, (c, c), 1)
    band = []
    for d in range(HG_DIAG):
        if rev:
            band.append((s == t + d) & (t % HG_DIAG + d < HG_DIAG))
        else:
            band.append((s == t - d) & (t % HG_DIAG >= d))
    return levels, band


def _hgrn_scores(q, k, b, brow, rev, consts):
    levels, band = consts
    c = HG_CHUNK
    nt = (((1,), (1,)), ((), ()))
    groups = c // SUBLANES
    row8 = lax.broadcasted_iota(jnp.int32, (SUBLANES, HEAD_DIM), 0)
    rows = [slice(g * SUBLANES, (g + 1) * SUBLANES) for g in range(groups)]

    a = jnp.zeros((c, c), F32)
    for d in range(HG_DIAG):
        if d == 0:
            prod = q * k
        else:
            shift = c - d if rev else d
            prod = q * pltpu.roll(k, shift, axis=0) * jnp.exp2(b - pltpu.roll(b, shift, axis=0))
        a = jnp.where(band[d], prod.sum(axis=-1, keepdims=True), a)
    rows_a = [a[r] for r in rows]

    for half in _hgrn_levels():
        blk = 2 * half
        ref_off = half if rev else half - 1
        zs = []
        owners = []
        for g in range(groups):
            t0 = g * SUBLANES
            if half == 1:
                attending = (row8 % 2 == 0) if rev else (row8 % 2 == 1)
                zs.append(jnp.where(attending, q[rows[g]] * (1.0 - k[rows[g]]), k[rows[g]]))
                owners.append(g)
                continue
            if half >= SUBLANES:
                ref = brow(t0 // blk * blk + ref_off)
                attending = (t0 % blk < half) if rev else (t0 % blk >= half)
                x = q[rows[g]] if attending else k[rows[g]]
                e = jnp.exp2(b[rows[g]] - ref if attending else ref - b[rows[g]])
                if attending:
                    owners.append(g)
            else:
                ref = brow(t0 + ref_off)
                for i in range(1, SUBLANES // blk):
                    ref = jnp.where(row8 >= i * blk, brow(t0 + i * blk + ref_off), ref)
                attending = (row8 % blk < half) if rev else (row8 % blk >= half)
                x = jnp.where(attending, q[rows[g]], k[rows[g]])
                e = jnp.exp2(-jnp.abs(b[rows[g]] - ref))
                owners.append(g)
            zs.append(x * e)
        z = jnp.concatenate(zs, axis=0).astype(BF16)
        lv = lax.dot_general(z, z, nt, preferred_element_type=F32)
        for g in owners:
            rows_a[g] = jnp.where(levels[half][g], lv[rows[g]], rows_a[g])
    return jnp.concatenate(rows_a, axis=0)


def _hgrn_kernel(qf_ref, kf_ref, bf_ref, vf_ref, qb_ref, kb_ref, bb_ref, vb_ref,
                 of_ref, ob_ref, s_scr):
    @pl.when(pl.program_id(0) == 0)
    def _():
        s_scr[...] = jnp.zeros(s_scr.shape, F32)

    c = HG_CHUNK
    n_chunks = HG_BLOCK // c
    consts = (_hgrn_consts(False), _hgrn_consts(True))
    nt = (((1,), (1,)), ((), ()))
    tn = (((0,), (0,)), ((), ()))
    dirs = ((qf_ref, kf_ref, bf_ref, vf_ref, of_ref, False),
            (qb_ref, kb_ref, bb_ref, vb_ref, ob_ref, True))

    def head_body(it, carry):
        jobs = []
        for hi in range(HG_ITER_HEADS):
            h = it * HG_ITER_HEADS + hi
            for di, (q_ref, k_ref, b_ref, v_ref, o_ref, rev) in enumerate(dirs):
                order = reversed(range(n_chunks)) if rev else range(n_chunks)
                for ci in order:
                    rows = slice(ci * c, (ci + 1) * c)
                    q = q_ref[h, rows, :].astype(F32)
                    k = k_ref[h, rows, :].astype(F32)
                    b = b_ref[h, rows, :]
                    v = v_ref[h, rows, :]
                    tot = b[0:1] if rev else b[c - 1:c]
                    q_in = (q * jnp.exp2(b)).astype(BF16)
                    k_out = (k * jnp.exp2(tot - b)).astype(BF16)
                    upd = lax.dot_general(v, k_out, tn, preferred_element_type=F32)

                    def brow(r, b_ref=b_ref, base=ci * c, h=h):
                        return b_ref[h, base + r:base + r + 1, :]

                    a = _hgrn_scores(q, k, b, brow, rev, consts[di]).astype(BF16)
                    jobs.append((hi, h, di, o_ref, rows, q_in, jnp.exp2(tot), upd, a, v))
        st = {(hi, di): s_scr[di, it * HG_ITER_HEADS + hi]
              for hi in range(HG_ITER_HEADS) for di in range(2)}
        for hi, h, di, o_ref, rows, q_in, decay, upd, a, v in jobs:
            o = lax.dot_general(q_in, st[hi, di].astype(BF16), nt, preferred_element_type=F32)
            st[hi, di] = decay * st[hi, di] + upd
            o_ref[h, rows, :] = o + jnp.dot(a, v, preferred_element_type=F32)
        for (hi, di), s in st.items():
            s_scr[di, it * HG_ITER_HEADS + hi] = s
        return carry

    lax.fori_loop(0, HG_HEADS // HG_ITER_HEADS, head_body, 0)


def _hgrn2(hq, kf, bf, kb, bb, hv):
    nb = T_ROWS // HG_BLOCK
    fwd = lambda i: (0, i, 0)
    bwd = lambda i: (0, jnp.where(i == 0, 0, nb - i), 0)
    blk = (HG_HEADS, HG_BLOCK, HEAD_DIM)
    out = jax.ShapeDtypeStruct((HG_HEADS, T_ROWS, HEAD_DIM), F32)
    return pl.pallas_call(
        _hgrn_kernel,
        out_shape=(out, out),
        grid=(nb,),
        in_specs=[pl.BlockSpec(blk, fwd)] * 4 + [pl.BlockSpec(blk, bwd)] * 4,
        out_specs=(pl.BlockSpec(blk, fwd), pl.BlockSpec(blk, bwd)),
        scratch_shapes=[pltpu.VMEM((2, HG_HEADS, HEAD_DIM, HEAD_DIM), F32)],
        compiler_params=_cparams(1),
        name="hgrn2_scan",
    )(hq, kf, bf, hv, hq, kb, bb, hv)


def _outproj_kernel(attn_ref, of_ref, ob_ref, gate_ref, hnw_ref, w_ref, x_ref, mod_ref, n2w_ref,
                    xo_ref, h2_ref, mix_scr, acc_scr):
    m = pl.program_id(0)
    bm = x_ref.shape[0]
    acc = jnp.dot(attn_ref[...], w_ref[0:ATTN_WIDTH, :], preferred_element_type=F32)
    for h in range(HG_HEADS):
        y = _rms(of_ref[h] + ob_ref[h], hnw_ref[...])
        mix_scr[:, h * HEAD_DIM:(h + 1) * HEAD_DIM] = (y * gate_ref[h].astype(F32)).astype(BF16)
    acc_scr[...] = acc + jnp.dot(mix_scr[...], w_ref[ATTN_WIDTH:ATTN_WIDTH + HG_WIDTH, :],
                                 preferred_element_type=F32)

    def finish(rows, mod_row):
        def mod(k):
            return mod_ref[mod_row:mod_row + 1, k * D_MODEL:(k + 1) * D_MODEL]
        x_new = x_ref[rows, :] + mod(2) * acc_scr[rows, :]
        xo_ref[rows, :] = x_new
        r = lax.rsqrt(jnp.mean(x_new * x_new, axis=-1, keepdims=True) + EPS)
        h2_ref[rows, :] = (x_new * r * (n2w_ref[...] * (1.0 + mod(4))) + mod(3)).astype(BF16)

    for r0 in range(0, bm, EPI_ROWS):
        finish(slice(r0, r0 + EPI_ROWS), 0)

    @pl.when(m == 0)
    def _():
        for r0 in range(0, CTX_LEN, EPI_ROWS):
            finish(slice(r0, r0 + EPI_ROWS), 1)


def _output_projection(layer, attn, o_f, o_b, gate, hnw, w_out, xs, mod, n2w):
    bm = OUT_ROW_TILE
    head_spec = pl.BlockSpec((HG_HEADS, bm, HEAD_DIM), lambda m: (0, m, 0))
    row_spec = pl.BlockSpec((bm, D_MODEL), lambda m: (m, 0))
    return pl.pallas_call(
        _outproj_kernel,
        out_shape=(jax.ShapeDtypeStruct((T_ROWS, D_MODEL), F32),
                   jax.ShapeDtypeStruct((T_ROWS, D_MODEL), BF16)),
        grid=(T_ROWS // bm,),
        in_specs=[
            pl.BlockSpec((bm, ATTN_WIDTH), lambda m: (m, 0)),
            head_spec, head_spec, head_spec,
            pl.BlockSpec((1, HEAD_DIM), lambda m: (0, 0)),
            pl.BlockSpec((None, ATTN_WIDTH + HG_WIDTH, D_MODEL), lambda m: (layer, 0, 0)),
            row_spec,
            pl.BlockSpec((None, SUBLANES, 6 * D_MODEL), lambda m: (layer, 0, 0)),
            pl.BlockSpec((1, D_MODEL), lambda m: (0, 0)),
        ],
        out_specs=(row_spec, row_spec),
        scratch_shapes=[pltpu.VMEM((bm, HG_WIDTH), BF16), pltpu.VMEM((bm, D_MODEL), F32)],
        compiler_params=_cparams(1),
        name="output_projection",
    )(attn, o_f, o_b, gate, hnw, w_out, xs, mod, n2w)


def _ffn_up_kernel(h_ref, hp_ref, hn_ref, wg_ref, wu_ref, cw_ref, cb_ref, u_ref, hcat_scr, g_scr):
    m = pl.program_id(0)
    n = pl.program_id(1)
    bm = h_ref.shape[0]
    hl = CONV_HALO

    @pl.when(n == 0)
    def _():
        hcat_scr[0:hl, :] = hp_ref[...]
        hcat_scr[hl:hl + bm, :] = h_ref[...]
        hcat_scr[hl + bm:hl + bm + hl, :] = hn_ref[...]

    g_scr[...] = jnp.dot(hcat_scr[...], wg_ref[...], preferred_element_type=F32)
    up = jnp.dot(h_ref[...], wu_ref[...], preferred_element_type=F32)
    r = m * bm + lax.broadcasted_iota(jnp.int32, (bm, 1), 0)
    has_prev = (r != 0) & (r != CTX_LEN)
    has_next = (r != CTX_LEN - 1) & (r != T_ROWS - 1)
    g_prev = jnp.where(has_prev, g_scr[hl - 1:hl - 1 + bm, :], 0.0)
    g_next = jnp.where(has_next, g_scr[hl + 1:hl + 1 + bm, :], 0.0)
    gate = (g_prev * cw_ref[0:1, :] + g_scr[hl:hl + bm, :] * cw_ref[1:2, :]
            + g_next * cw_ref[2:3, :] + cb_ref[...])
    u_ref[...] = (_silu(gate) * up).astype(BF16)


def _ffn_up(layer, h2, w_up, conv_w, conv_b):
    bm, bn, hl = FF_ROW_TILE, FF_COL_TILE, CONV_HALO
    n_n = D_FF // bn
    per = bm // hl
    last = T_ROWS // hl - 1
    return pl.pallas_call(
        _ffn_up_kernel,
        out_shape=jax.ShapeDtypeStruct((T_ROWS, D_FF), BF16),
        grid=(T_ROWS // bm, n_n),
        in_specs=[
            pl.BlockSpec((bm, D_MODEL), lambda m, n: (m, 0)),
            pl.BlockSpec((hl, D_MODEL), lambda m, n: (jnp.maximum(m * per - 1, 0), 0)),
            pl.BlockSpec((hl, D_MODEL), lambda m, n: (jnp.minimum((m + 1) * per, last), 0)),
            pl.BlockSpec((None, D_MODEL, bn), lambda m, n: (layer, 0, n)),
            pl.BlockSpec((None, D_MODEL, bn), lambda m, n: (layer, 0, n_n + n)),
            pl.BlockSpec((3, bn), lambda m, n: (0, n)),
            pl.BlockSpec((1, bn), lambda m, n: (0, n)),
        ],
        out_specs=pl.BlockSpec((bm, bn), lambda m, n: (m, n)),
        scratch_shapes=[pltpu.VMEM((bm + 2 * hl, D_MODEL), BF16),
                        pltpu.VMEM((bm + 2 * hl, bn), F32)],
        compiler_params=_cparams(2),
        name="ffn_up_conv",
    )(h2, h2, h2, w_up, w_up, conv_w, conv_b)


def _ffn_down_kernel(u_ref, w_ref, x_ref, mod_ref, fw_ref, o_ref, *, final):
    m = pl.program_id(0)

    def gate(mod_row):
        return mod_ref[mod_row:mod_row + 1, 5 * D_MODEL:6 * D_MODEL]

    def run():
        acc = jnp.dot(u_ref[...], w_ref[...], preferred_element_type=F32)
        x_new = x_ref[...] + gate(0) * acc
        o_ref[...] = _rms(x_new, fw_ref[...]) if final else x_new
        return acc

    if final:
        @pl.when(m > 0)
        def _():
            run()
    else:
        acc = run()

        @pl.when(m == 0)
        def _():
            o_ref[0:CTX_LEN, :] = x_ref[0:CTX_LEN, :] + gate(1) * acc[0:CTX_LEN]


def _ffn_down(layer, u, w_down, xs, mod, final_w, final):
    bm = CTX_LEN if final else OUT_ROW_TILE
    row_spec = pl.BlockSpec((bm, D_MODEL), lambda m: (m, 0))
    if final:
        out_rows = SEQ
        out_spec = pl.BlockSpec((bm, D_MODEL), lambda m: (jnp.maximum(m - 1, 0), 0))
    else:
        out_rows, out_spec = T_ROWS, row_spec
    return pl.pallas_call(
        functools.partial(_ffn_down_kernel, final=final),
        out_shape=jax.ShapeDtypeStruct((out_rows, D_MODEL), F32),
        grid=(T_ROWS // bm,),
        in_specs=[
            pl.BlockSpec((bm, D_FF), lambda m: (m, 0)),
            pl.BlockSpec((None, D_FF, D_MODEL), lambda m: (layer, 0, 0),
                         pipeline_mode=pl.Buffered(1)),
            row_spec,
            pl.BlockSpec((None, SUBLANES, 6 * D_MODEL), lambda m: (layer, 0, 0)),
            pl.BlockSpec((1, D_MODEL), lambda m: (0, 0)),
        ],
        out_specs=out_spec,
        compiler_params=_cparams(1),
        name="ffn_down",
    )(u, w_down, xs, mod, final_w)


def _rope_tables():
    f32 = np.float32
    rows = SEQ // GRID_W
    row = np.repeat(np.arange(rows, dtype=f32), GRID_W)
    col = np.tile(np.arange(GRID_W, dtype=f32), rows)
    freqs = np.power(f32(ROPE_THETA), -np.arange(ROPE_FREQS, dtype=f32) / f32(ROPE_FREQS)).astype(f32)
    ar, ac = row[:, None] * freqs, col[:, None] * freqs
    cos = np.concatenate([np.cos(ar), np.cos(ar), np.cos(ac), np.cos(ac)], axis=1)
    sin = np.concatenate([np.sin(ar), np.sin(ar), np.sin(ac), np.sin(ac)], axis=1)
    cos = np.concatenate([np.ones((CTX_LEN, HEAD_DIM), f32), cos], axis=0)
    sin = np.concatenate([np.zeros((CTX_LEN, HEAD_DIM), f32), sin], axis=0)
    return jnp.asarray(cos, F32), jnp.asarray(sin, F32)


def kernel(x, c, ctx, c_ctx, w_mod, b_mod, norm1_w, norm2_w, w_in, q_norm_w, k_norm_w, hg_lb_logits,
           hg_norm_w, w_out, w_up, conv_w, conv_b, w_down, final_norm_w):
    assert x.shape == (1, SEQ, D_MODEL) and ctx.shape == (1, CTX_LEN, D_MODEL)
    xs = jnp.concatenate([ctx[0], x[0]], axis=0)
    mod = _modulation(jnp.stack([c[0], c_ctx], axis=1), w_mod, b_mod)
    cos_t, sin_t = _rope_tables()
    lb_sm = jax.nn.softmax(hg_lb_logits.astype(F32), axis=1)
    lb_all = jnp.cumsum(lb_sm, axis=1) - lb_sm[:, :1]

    w_in, w_out, w_up, w_down = (w.astype(BF16) for w in (w_in, w_out, w_up, w_down))
    for l in range(DEPTH):
        lb = jnp.concatenate([lb_all[0, l], lb_all[1, l]])[None, :]
        q, k, v, hq, kf, bf, kb, bb, hv, gate = _input_projection(
            l, xs, mod, norm1_w[l][None, :], w_in, q_norm_w[l][None, :],
            k_norm_w[l][None, :], lb, cos_t, sin_t)
        attn = _attention(q, k, v)
        o_f, o_b = _hgrn2(hq, kf, bf, kb, bb, hv)
        xs, h2 = _output_projection(l, attn, o_f, o_b, gate, hg_norm_w[l][None, :],
                                    w_out, xs, mod, norm2_w[l][None, :])
        u = _ffn_up(l, h2, w_up, conv_w[l], conv_b[l][None, :])
        xs = _ffn_down(l, u, w_down, xs, mod, final_norm_w[None, :], final=(l == DEPTH - 1))
    return xs[None]
```

```python
import functools

import jax
import jax.numpy as jnp
import numpy as np
from jax import lax
from jax.experimental import pallas as pl
from jax.experimental.pallas import tpu as pltpu

F32 = jnp.float32
BF16 = jnp.bfloat16

D_MODEL = 2048
SEQ = 8192
DEPTH = 4
CTX_LEN = 256
T_ROWS = CTX_LEN + SEQ
GRID_W = 64
HEAD_DIM = 128
ATTN_HEADS = 8
ATTN_KV_HEADS = 2
ATTN_GROUP = 4
ATTN_WIDTH = 1024
KV_WIDTH = 256
HG_HEADS = 8
HG_WIDTH = 1024
IN_WIDTH = 6656
ROPE_THETA = 10000.0
ROPE_FREQS = 32
D_FF = 5632
EPS = 1e-6
Q_SCALE = HEAD_DIM ** -0.5 * 1.4426950408889634

LANES = 128
SUBLANES = 8
VMEM_LIMIT = 56 * 1024 * 1024

MOD_SLAB = 256
ROW_TILE = 768
FF_ROW_TILE = 1408
OUT_ROW_TILE = 384
IN_COL_TILE = 512
FF_COL_TILE = 512
ATTN_Q_TILE = 256
ATTN_K_TILE = 1408
ATTN_QK_PARTS = 2
HG_BLOCK = 256
HG_ITER_HEADS = 4
HG_CHUNK = 64
HG_DIAG = 1
CONV_HALO = 16


def _cparams(n_axes):
    return pltpu.CompilerParams(dimension_semantics=("arbitrary",) * n_axes,
                                vmem_limit_bytes=VMEM_LIMIT)


def _rms(x, w):
    return x * lax.rsqrt(jnp.mean(x * x, axis=-1, keepdims=True) + EPS) * w


def _silu(x):
    h = 0.5 * x
    return h * jnp.tanh(h) + h


def _mod_kernel(s_ref, w_ref, b_ref, o_ref, acc_scr):
    k = pl.program_id(1)
    n_out = w_ref.shape[2]
    cw = 4 * LANES

    @pl.when(k == 0)
    def _():
        acc_scr[...] = jnp.zeros(acc_scr.shape, F32)

    r0 = pl.multiple_of(k * MOD_SLAB, MOD_SLAB)
    s = _silu(s_ref[pl.ds(r0, MOD_SLAB), :])
    sb = [jnp.tile(jnp.broadcast_to(s[:, i:i + 1], (MOD_SLAB, LANES)), (1, cw // LANES))
          for i in range(2)]
    for c0 in range(0, n_out, cw):
        w = w_ref[0, :, c0:c0 + cw]
        for i in range(2):
            part = (w * sb[i]).reshape(MOD_SLAB // SUBLANES, SUBLANES, cw).sum(axis=0)
            acc_scr[i, :, c0:c0 + cw] += part

    @pl.when(k == pl.num_programs(1) - 1)
    def _():
        bias = b_ref[0]
        rows = [acc_scr[i].sum(axis=0, keepdims=True) + bias for i in range(2)]
        o_ref[0] = jnp.concatenate(rows + [jnp.zeros((SUBLANES - 2, n_out), F32)], axis=0)


def _modulation(s_in, w_mod, b_mod):
    n_out = 6 * D_MODEL
    return pl.pallas_call(
        _mod_kernel,
        out_shape=jax.ShapeDtypeStruct((DEPTH, SUBLANES, n_out), F32),
        grid=(DEPTH, D_MODEL // MOD_SLAB),
        in_specs=[
            pl.BlockSpec((D_MODEL, 2), lambda l, k: (0, 0)),
            pl.BlockSpec((1, MOD_SLAB, n_out), lambda l, k: (l, k, 0)),
            pl.BlockSpec((1, 1, n_out), lambda l, k: (l, 0, 0)),
        ],
        out_specs=pl.BlockSpec((1, SUBLANES, n_out), lambda l, k: (l, 0, 0)),
        scratch_shapes=[pltpu.VMEM((2, SUBLANES, n_out), F32)],
        compiler_params=_cparams(2),
        name="adaln_modulation",
    )(s_in, w_mod, b_mod.reshape(DEPTH, 1, n_out))


def _norm_rope_matrix():
    i = lax.broadcasted_iota(jnp.int32, (2 * HEAD_DIM, 2 * HEAD_DIM), 0)
    j = lax.broadcasted_iota(jnp.int32, (2 * HEAD_DIM, 2 * HEAD_DIM), 1)
    ones = (i < HEAD_DIM) & (j < HEAD_DIM)
    ii, jj = i - HEAD_DIM, j - HEAD_DIM
    low = (jj // ROPE_FREQS) % 2 == 0
    rot = jnp.where(low & (ii == jj + ROPE_FREQS), -1.0,
                    jnp.where(~low & (ii == jj - ROPE_FREQS), 1.0, 0.0))
    rot = jnp.where((i >= HEAD_DIM) & (j >= HEAD_DIM), rot, 0.0)
    return jnp.where(ones, 1.0, rot).astype(BF16)


def _norm_rope(p, w, mat_ref, cos, sin, scale):
    pw = p * w
    lhs = jnp.concatenate([(p * p).astype(BF16), pw.astype(BF16)], axis=1)
    red = jnp.dot(lhs, mat_ref[...], preferred_element_type=F32)
    r = lax.rsqrt(red[:, 0:HEAD_DIM] * (1.0 / HEAD_DIM) + EPS) * scale
    return r * (pw * cos + red[:, HEAD_DIM:2 * HEAD_DIM] * sin)


def _inproj_kernel(x_ref, mod_ref, n1w_ref, w_ref, qnw_ref, knw_ref, lb_ref, cos_ref, sin_ref, mat_ref,
                   q_ref, k_ref, v_ref, hq_ref, kf_ref, bf_ref, kb_ref, bb_ref, hv_ref, gate_ref,
                   h_scr):
    m = pl.program_id(0)
    n = pl.program_id(1)
    bm = x_ref.shape[0]

    def modulated(rows, mod_row):
        x = x_ref[rows, :]
        r = lax.rsqrt(jnp.mean(x * x, axis=-1, keepdims=True) + EPS)
        gain = n1w_ref[...] * (1.0 + mod_ref[mod_row:mod_row + 1, D_MODEL:2 * D_MODEL])
        return (x * r * gain + mod_ref[mod_row:mod_row + 1, 0:D_MODEL]).astype(BF16)

    @pl.when(n == 0)
    def _():
        h_scr[...] = modulated(slice(None), 0)

    @pl.when((n == 0) & (m == 0))
    def _():
        h_scr[0:CTX_LEN, :] = modulated(slice(0, CTX_LEN), 1)

    def project(epilogue):
        ch = IN_COL_TILE // 2
        hpt = ch // HEAD_DIM
        rh = bm // 2
        ps = []
        for ri in range(2):
            rows = slice(ri * rh, (ri + 1) * rh)
            h = h_scr[rows, :]
            ps.append((rows, [jnp.dot(h, w_ref[:, a * ch:(a + 1) * ch], preferred_element_type=F32)
                              for a in range(2)]))
        for rows, halves in ps:
            for a, p in enumerate(halves):
                for j in range(hpt):
                    epilogue(a * hpt + j, rows, p[:, j * HEAD_DIM:(j + 1) * HEAD_DIM])

    def q_out(j, rows, p):
        q_ref[j, rows, :] = _norm_rope(p, qnw_ref[...], mat_ref, cos_ref[rows, :], sin_ref[rows, :],
                                       Q_SCALE).astype(BF16)

    def kv_out(j, rows, p):
        if j < ATTN_KV_HEADS:
            k_ref[j, rows, :] = _norm_rope(p, knw_ref[...], mat_ref, cos_ref[rows, :],
                                           sin_ref[rows, :], 1.0).astype(BF16)
        else:
            v_ref[j - ATTN_KV_HEADS, rows, :] = p.astype(BF16)

    def silu_out(ref):
        def out(j, rows, p):
            ref[j, rows, :] = _silu(p).astype(BF16)
        return out

    def forget_out(k_out, b_out, rev):
        def out(j, rows, p):
            lb = lb_ref[:, j * HEAD_DIM:(j + 1) * HEAD_DIM]
            fg = lb + (1.0 - lb) * jax.nn.sigmoid(p)
            k_out[j, rows, :] = (1.0 - fg).astype(BF16)
            b_out[j, rows, :] = _chunk_cumsum(jnp.log2(fg), rev)
        return out

    def plain_out(j, rows, p):
        hv_ref[j, rows, :] = p.astype(BF16)

    pl.when(n < 2)(lambda: project(q_out))
    pl.when(n == 2)(lambda: project(kv_out))
    pl.when((n == 3) | (n == 4))(lambda: project(silu_out(hq_ref)))
    pl.when((n == 5) | (n == 6))(lambda: project(forget_out(kf_ref, bf_ref, False)))
    pl.when((n == 7) | (n == 8))(lambda: project(forget_out(kb_ref, bb_ref, True)))
    pl.when((n == 9) | (n == 10))(lambda: project(plain_out))
    pl.when(n >= 11)(lambda: project(silu_out(gate_ref)))


def _input_projection(layer, xs, mod, n1w, w_in, qnw, knw, lb, cos_t, sin_t):
    bm, bn = ROW_TILE, IN_COL_TILE
    n_m, n_n = T_ROWS // bm, IN_WIDTH // bn
    hpt = bn // HEAD_DIM

    def head_spec(first_tile):
        return pl.BlockSpec((hpt, bm, HEAD_DIM),
                            lambda m, n: (jnp.clip(n - first_tile, 0, 1), m, 0))

    def head_shape(dtype):
        return jax.ShapeDtypeStruct((HG_HEADS, T_ROWS, HEAD_DIM), dtype)

    kv_spec = pl.BlockSpec((ATTN_KV_HEADS, bm, HEAD_DIM), lambda m, n: (0, m, 0))
    kv_shape = jax.ShapeDtypeStruct((ATTN_KV_HEADS, T_ROWS, HEAD_DIM), BF16)
    return pl.pallas_call(
        _inproj_kernel,
        out_shape=(head_shape(BF16), kv_shape, kv_shape, head_shape(BF16),
                   head_shape(BF16), head_shape(F32), head_shape(BF16), head_shape(F32),
                   head_shape(BF16), head_shape(BF16)),
        grid=(n_m, n_n),
        in_specs=[
            pl.BlockSpec((bm, D_MODEL), lambda m, n: (m, 0)),
            pl.BlockSpec((None, SUBLANES, 6 * D_MODEL), lambda m, n: (layer, 0, 0)),
            pl.BlockSpec((1, D_MODEL), lambda m, n: (0, 0)),
            pl.BlockSpec((None, D_MODEL, bn), lambda m, n: (layer, 0, n)),
            pl.BlockSpec((1, HEAD_DIM), lambda m, n: (0, 0)),
            pl.BlockSpec((1, HEAD_DIM), lambda m, n: (0, 0)),
            pl.BlockSpec((1, bn), lambda m, n: (0, jnp.clip(n - 5, 0, 3))),
            pl.BlockSpec((bm, HEAD_DIM), lambda m, n: (m, 0)),
            pl.BlockSpec((bm, HEAD_DIM), lambda m, n: (m, 0)),
            pl.BlockSpec((2 * HEAD_DIM, 2 * HEAD_DIM), lambda m, n: (0, 0)),
        ],
        out_specs=(head_spec(0), kv_spec, kv_spec, head_spec(3),
                   head_spec(5), head_spec(5), head_spec(7), head_spec(7),
                   head_spec(9), head_spec(11)),
        scratch_shapes=[pltpu.VMEM((bm, D_MODEL), BF16)],
        compiler_params=_cparams(2),
        name="input_projection",
    )(xs, mod, n1w, w_in, qnw, knw, lb, cos_t, sin_t, _norm_rope_matrix())


def _attn_kernel(q_ref, k_ref, vt_ref, o_ref, acc_scr, st_scr):
    i = pl.program_id(1)
    bq = q_ref.shape[1]
    bk = ATTN_K_TILE
    n_chunks = T_ROWS // bk

    def scores(kc, r, slot):
        part = kc.shape[0] // ATTN_QK_PARTS
        mx = None
        for a in range(ATTN_QK_PARTS):
            st = lax.dot_general(kc[a * part:(a + 1) * part], q_ref[r], (((1,), (1,)), ((), ())),
                                 preferred_element_type=F32)
            st_scr[slot, a * part:(a + 1) * part, :] = st
            m_part = st.max(axis=0, keepdims=True)
            mx = m_part if mx is None else jnp.maximum(mx, m_part)
        return mx

    def update(vtc, r, slot, m_cur, m, l):
        width = vtc.shape[1]
        m_new = jnp.maximum(m, m_cur)
        alpha = jnp.exp2(m - m_new)
        p = jnp.exp2(st_scr[slot, 0:width, :] - m_new)
        l = alpha * l + p.sum(axis=0, keepdims=True)
        acc_scr[r] = alpha * acc_scr[r] + jnp.dot(vtc, p.astype(BF16), preferred_element_type=F32)
        return m_new, l

    def finish(l):
        for r in range(ATTN_GROUP):
            o_ref[:, r * HEAD_DIM:(r + 1) * HEAD_DIM] = (acc_scr[r] / l[r]).T.astype(BF16)

    acc_scr[...] = jnp.zeros(acc_scr.shape, F32)
    m0 = tuple(jnp.full((1, bq), -jnp.inf, F32) for _ in range(ATTN_GROUP))
    l0 = tuple(jnp.zeros((1, bq), F32) for _ in range(ATTN_GROUP))

    @pl.when(i == 0)
    def _():
        kc = k_ref[0, 0:CTX_LEN, :]
        vtc = vt_ref[0, 0, :, 0:CTX_LEN]
        l = []
        mc = scores(kc, 0, 0)
        for r in range(ATTN_GROUP):
            mc_next = scores(kc, r + 1, (r + 1) % 2) if r + 1 < ATTN_GROUP else None
            l.append(update(vtc, r, r % 2, mc, m0[r], l0[r])[1])
            mc = mc_next
        finish(l)

    @pl.when(i > 0)
    def _():
        def chunk(j):
            return k_ref[0, pl.ds(pl.multiple_of(j * bk, bk), bk), :]

        def body(j, carry):
            mc, m, l = carry
            m, l = list(m), list(l)
            vtc = vt_ref[0, j]
            for r in range(ATTN_GROUP):
                if r + 1 < ATTN_GROUP:
                    mc_next = scores(chunk(j), r + 1, (r + 1) % 2)
                else:
                    mc_next = scores(chunk(jnp.minimum(j + 1, n_chunks - 1)), 0, 0)
                m[r], l[r] = update(vtc, r, r % 2, mc, m[r], l[r])
                mc = mc_next
            return mc, tuple(m), tuple(l)

        _, _, l = lax.fori_loop(0, n_chunks, body, (scores(chunk(0), 0, 0), m0, l0), unroll=True)
        finish(l)


def _attention(q, k, v):
    bq, bk = ATTN_Q_TILE, ATTN_K_TILE
    n_chunks = T_ROWS // bk
    vt = v.reshape(ATTN_KV_HEADS, n_chunks, bk, HEAD_DIM).transpose(0, 1, 3, 2)
    return pl.pallas_call(
        _attn_kernel,
        out_shape=jax.ShapeDtypeStruct((T_ROWS, ATTN_WIDTH), BF16),
        grid=(ATTN_KV_HEADS, T_ROWS // bq),
        in_specs=[
            pl.BlockSpec((ATTN_GROUP, bq, HEAD_DIM), lambda g, i: (g, i, 0)),
            pl.BlockSpec((1, T_ROWS, HEAD_DIM), lambda g, i: (g, 0, 0)),
            pl.BlockSpec((1, n_chunks, HEAD_DIM, bk), lambda g, i: (g, 0, 0, 0)),
        ],
        out_specs=pl.BlockSpec((bq, ATTN_GROUP * HEAD_DIM), lambda g, i: (i, g)),
        scratch_shapes=[pltpu.VMEM((ATTN_GROUP, HEAD_DIM, bq), F32),
                        pltpu.VMEM((2, bk, bq), F32)],
        compiler_params=_cparams(2),
        name="gqa_attention",
    )(q, k, vt)


def _chunk_cumsum(g, rev):
    n_groups = g.shape[0] // SUBLANES
    x = g.reshape(n_groups, SUBLANES, g.shape[1])
    sub = lax.broadcasted_iota(jnp.int32, x.shape, 1)
    step = 1
    while step < SUBLANES:
        if rev:
            x = x + jnp.where(sub < SUBLANES - step, pltpu.roll(x, SUBLANES - step, axis=1), 0.0)
        else:
            x = x + jnp.where(sub >= step, pltpu.roll(x, step, axis=1), 0.0)
        step *= 2
    per = HG_CHUNK // SUBLANES
    xs = [x[v] for v in range(n_groups)]
    for c0 in range(0, n_groups, per):
        if rev:
            for v in range(c0 + per - 2, c0 - 1, -1):
                xs[v] = xs[v] + xs[v + 1][0:1]
        else:
            for v in range(c0 + 1, c0 + per):
                xs[v] = xs[v] + xs[v - 1][SUBLANES - 1:SUBLANES]
    return jnp.concatenate(xs, axis=0)


def _hgrn_levels():
    halves = []
    half = HG_CHUNK // 2
    while half >= HG_DIAG:
        halves.append(half)
        half //= 2
    return halves


def _hgrn_consts(rev):
    c = HG_CHUNK
    levels = {}
    for half in _hgrn_levels():
        blk = 2 * half
        levels[half] = []
        for g in range(c // SUBLANES):
            t = lax.broadcasted_iota(jnp.int32, (SUBLANES, c), 0) + g * SUBLANES
            s = lax.broadcasted_iota(jnp.int32, (SUBLANES, c), 1)
            same = t // blk == s // blk
            if rev:
                levels[half].append(same & (t % blk < half) & (s % blk >= half))
            else:
                levels[half].append(same & (t % blk >= half) & (s % blk < half))
    t = lax.broadcasted_iota(jnp.int32, (c, c), 0)
    s = lax.broadcasted_iota(jnp.int32, (c, c), 1)
    band = []
    for d in range(HG_DIAG):
        if rev:
            band.append((s == t + d) & (t % HG_DIAG + d < HG_DIAG))
        else:
            band.append((s == t - d) & (t % HG_DIAG >= d))
    return levels, band


def _hgrn_scores(q, k, b, brow, rev, consts):
    levels, band = consts
    c = HG_CHUNK
    nt = (((1,), (1,)), ((), ()))
    groups = c // SUBLANES
    row8 = lax.broadcasted_iota(jnp.int32, (SUBLANES, HEAD_DIM), 0)
    rows = [slice(g * SUBLANES, (g + 1) * SUBLANES) for g in range(groups)]

    a = jnp.zeros((c, c), F32)
    for d in range(HG_DIAG):
        if d == 0:
            prod = q * k
        else:
            shift = c - d if rev else d
            prod = q * pltpu.roll(k, shift, axis=0) * jnp.exp2(b - pltpu.roll(b, shift, axis=0))
        a = jnp.where(band[d], prod.sum(axis=-1, keepdims=True), a)
    rows_a = [a[r] for r in rows]

    for half in _hgrn_levels():
        blk = 2 * half
        ref_off = half if rev else half - 1
        zs = []
        owners = []
        for g in range(groups):
            t0 = g * SUBLANES
            if half == 1:
                attending = (row8 % 2 == 0) if rev else (row8 % 2 == 1)
                zs.append(jnp.where(attending, q[rows[g]] * (1.0 - k[rows[g]]), k[rows[g]]))
                owners.append(g)
                continue
            if half >= SUBLANES:
                ref = brow(t0 // blk * blk + ref_off)
                attending = (t0 % blk < half) if rev else (t0 % blk >= half)
                x = q[rows[g]] if attending else k[rows[g]]
                e = jnp.exp2(b[rows[g]] - ref if attending else ref - b[rows[g]])
                if attending:
                    owners.append(g)
            else:
                ref = brow(t0 + ref_off)
                for i in range(1, SUBLANES // blk):
                    ref = jnp.where(row8 >= i * blk, brow(t0 + i * blk + ref_off), ref)
                attending = (row8 % blk < half) if rev else (row8 % blk >= half)
                x = jnp.where(attending, q[rows[g]], k[rows[g]])
                e = jnp.exp2(-jnp.abs(b[rows[g]] - ref))
                owners.append(g)
            zs.append(x * e)
        z = jnp.concatenate(zs, axis=0).astype(BF16)
        lv = lax.dot_general(z, z, nt, preferred_element_type=F32)
        for g in owners:
            rows_a[g] = jnp.where(levels[half][g], lv[rows[g]], rows_a[g])
    return jnp.concatenate(rows_a, axis=0)


def _hgrn_kernel(qf_ref, kf_ref, bf_ref, vf_ref, qb_ref, kb_ref, bb_ref, vb_ref,
                 of_ref, ob_ref, s_scr):
    @pl.when(pl.program_id(0) == 0)
    def _():
        s_scr[...] = jnp.zeros(s_scr.shape, F32)

    c = HG_CHUNK
    n_chunks = HG_BLOCK // c
    consts = (_hgrn_consts(False), _hgrn_consts(True))
    nt = (((1,), (1,)), ((), ()))
    tn = (((0,), (0,)), ((), ()))
    dirs = ((qf_ref, kf_ref, bf_ref, vf_ref, of_ref, False),
            (qb_ref, kb_ref, bb_ref, vb_ref, ob_ref, True))

    def head_body(it, carry):
        jobs = []
        for hi in range(HG_ITER_HEADS):
            h = it * HG_ITER_HEADS + hi
            for di, (q_ref, k_ref, b_ref, v_ref, o_ref, rev) in enumerate(dirs):
                order = reversed(range(n_chunks)) if rev else range(n_chunks)
                for ci in order:
                    rows = slice(ci * c, (ci + 1) * c)
                    q = q_ref[h, rows, :].astype(F32)
                    k = k_ref[h, rows, :].astype(F32)
                    b = b_ref[h, rows, :]
                    v = v_ref[h, rows, :]
                    tot = b[0:1] if rev else b[c - 1:c]
                    q_in = (q * jnp.exp2(b)).astype(BF16)
                    k_out = (k * jnp.exp2(tot - b)).astype(BF16)
                    upd = lax.dot_general(v, k_out, tn, preferred_element_type=F32)

                    def brow(r, b_ref=b_ref, base=ci * c, h=h):
                        return b_ref[h, base + r:base + r + 1, :]

                    a = _hgrn_scores(q, k, b, brow, rev, consts[di]).astype(BF16)
                    jobs.append((hi, h, di, o_ref, rows, q_in, jnp.exp2(tot), upd, a, v))
        st = {(hi, di): s_scr[di, it * HG_ITER_HEADS + hi]
              for hi in range(HG_ITER_HEADS) for di in range(2)}
        for hi, h, di, o_ref, rows, q_in, decay, upd, a, v in jobs:
            o = lax.dot_general(q_in, st[hi, di].astype(BF16), nt, preferred_element_type=F32)
            st[hi, di] = decay * st[hi, di] + upd
            o_ref[h, rows, :] = (o + jnp.dot(a, v, preferred_element_type=F32)).astype(BF16)
        for (hi, di), s in st.items():
            s_scr[di, it * HG_ITER_HEADS + hi] = s
        return carry

    lax.fori_loop(0, HG_HEADS // HG_ITER_HEADS, head_body, 0)


def _hgrn2(hq, kf, bf, kb, bb, hv):
    nb = T_ROWS // HG_BLOCK
    fwd = lambda i: (0, i, 0)
    bwd = lambda i: (0, jnp.where(i == 0, 0, nb - i), 0)
    blk = (HG_HEADS, HG_BLOCK, HEAD_DIM)
    out = jax.ShapeDtypeStruct((HG_HEADS, T_ROWS, HEAD_DIM), BF16)
    return pl.pallas_call(
        _hgrn_kernel,
        out_shape=(out, out),
        grid=(nb,),
        in_specs=[pl.BlockSpec(blk, fwd)] * 4 + [pl.BlockSpec(blk, bwd)] * 4,
        out_specs=(pl.BlockSpec(blk, fwd), pl.BlockSpec(blk, bwd)),
        scratch_shapes=[pltpu.VMEM((2, HG_HEADS, HEAD_DIM, HEAD_DIM), F32)],
        compiler_params=_cparams(1),
        name="hgrn2_scan",
    )(hq, kf, bf, hv, hq, kb, bb, hv)


def _outproj_kernel(attn_ref, of_ref, ob_ref, gate_ref, hnw_ref, w_ref, x_ref, mod_ref, n2w_ref,
                    xo_ref, h2_ref, mix_scr):
    m = pl.program_id(0)
    bm = x_ref.shape[0]
    acc = jnp.dot(attn_ref[...], w_ref[0:ATTN_WIDTH, :], preferred_element_type=F32)
    for h in range(HG_HEADS):
        y = _rms(of_ref[h].astype(F32) + ob_ref[h].astype(F32), hnw_ref[...])
        mix_scr[:, h * HEAD_DIM:(h + 1) * HEAD_DIM] = (y * gate_ref[h].astype(F32)).astype(BF16)
    acc = acc + jnp.dot(mix_scr[...], w_ref[ATTN_WIDTH:ATTN_WIDTH + HG_WIDTH, :],
                        preferred_element_type=F32)

    def finish(rows, mod_row):
        def mod(k):
            return mod_ref[mod_row:mod_row + 1, k * D_MODEL:(k + 1) * D_MODEL]
        x_new = x_ref[rows, :] + mod(2) * acc[rows]
        xo_ref[rows, :] = x_new
        r = lax.rsqrt(jnp.mean(x_new * x_new, axis=-1, keepdims=True) + EPS)
        h2_ref[rows, :] = (x_new * r * (n2w_ref[...] * (1.0 + mod(4))) + mod(3)).astype(BF16)

    finish(slice(0, bm), 0)

    @pl.when(m == 0)
    def _():
        finish(slice(0, CTX_LEN), 1)


def _output_projection(layer, attn, o_f, o_b, gate, hnw, w_out, xs, mod, n2w):
    bm = OUT_ROW_TILE
    head_spec = pl.BlockSpec((HG_HEADS, bm, HEAD_DIM), lambda m: (0, m, 0))
    row_spec = pl.BlockSpec((bm, D_MODEL), lambda m: (m, 0))
    return pl.pallas_call(
        _outproj_kernel,
        out_shape=(jax.ShapeDtypeStruct((T_ROWS, D_MODEL), F32),
                   jax.ShapeDtypeStruct((T_ROWS, D_MODEL), BF16)),
        grid=(T_ROWS // bm,),
        in_specs=[
            pl.BlockSpec((bm, ATTN_WIDTH), lambda m: (m, 0)),
            head_spec, head_spec, head_spec,
            pl.BlockSpec((1, HEAD_DIM), lambda m: (0, 0)),
            pl.BlockSpec((None, ATTN_WIDTH + HG_WIDTH, D_MODEL), lambda m: (layer, 0, 0)),
            row_spec,
            pl.BlockSpec((None, SUBLANES, 6 * D_MODEL), lambda m: (layer, 0, 0)),
            pl.BlockSpec((1, D_MODEL), lambda m: (0, 0)),
        ],
        out_specs=(row_spec, row_spec),
        scratch_shapes=[pltpu.VMEM((bm, HG_WIDTH), BF16)],
        compiler_params=_cparams(1),
        name="output_projection",
    )(attn, o_f, o_b, gate, hnw, w_out, xs, mod, n2w)


def _ffn_up_kernel(h_ref, hp_ref, hn_ref, wg_ref, wu_ref, cw_ref, cb_ref, u_ref, hcat_scr, g_scr):
    m = pl.program_id(0)
    n = pl.program_id(1)
    bm = h_ref.shape[0]
    hl = CONV_HALO

    @pl.when(n == 0)
    def _():
        hcat_scr[0:hl, :] = hp_ref[...]
        hcat_scr[hl:hl + bm, :] = h_ref[...]
        hcat_scr[hl + bm:hl + bm + hl, :] = hn_ref[...]

    g_scr[...] = jnp.dot(hcat_scr[...], wg_ref[...], preferred_element_type=F32)
    up = jnp.dot(h_ref[...], wu_ref[...], preferred_element_type=F32)
    r = m * bm + lax.broadcasted_iota(jnp.int32, (bm, 1), 0)
    has_prev = (r != 0) & (r != CTX_LEN)
    has_next = (r != CTX_LEN - 1) & (r != T_ROWS - 1)
    g_prev = jnp.where(has_prev, g_scr[hl - 1:hl - 1 + bm, :], 0.0)
    g_next = jnp.where(has_next, g_scr[hl + 1:hl + 1 + bm, :], 0.0)
    gate = (g_prev * cw_ref[0:1, :] + g_scr[hl:hl + bm, :] * cw_ref[1:2, :]
            + g_next * cw_ref[2:3, :] + cb_ref[...])
    u_ref[...] = (_silu(gate) * up).astype(BF16)


def _ffn_up(layer, h2, w_up, conv_w, conv_b):
    bm, bn, hl = FF_ROW_TILE, FF_COL_TILE, CONV_HALO
    n_n = D_FF // bn
    per = bm // hl
    last = T_ROWS // hl - 1
    return pl.pallas_call(
        _ffn_up_kernel,
        out_shape=jax.ShapeDtypeStruct((T_ROWS, D_FF), BF16),
        grid=(T_ROWS // bm, n_n),
        in_specs=[
            pl.BlockSpec((bm, D_MODEL), lambda m, n: (m, 0)),
            pl.BlockSpec((hl, D_MODEL), lambda m, n: (jnp.maximum(m * per - 1, 0), 0)),
            pl.BlockSpec((hl, D_MODEL), lambda m, n: (jnp.minimum((m + 1) * per, last), 0)),
            pl.BlockSpec((None, D_MODEL, bn), lambda m, n: (layer, 0, n)),
            pl.BlockSpec((None, D_MODEL, bn), lambda m, n: (layer, 0, n_n + n)),
            pl.BlockSpec((3, bn), lambda m, n: (0, n)),
            pl.BlockSpec((1, bn), lambda m, n: (0, n)),
        ],
        out_specs=pl.BlockSpec((bm, bn), lambda m, n: (m, n)),
        scratch_shapes=[pltpu.VMEM((bm + 2 * hl, D_MODEL), BF16),
                        pltpu.VMEM((bm + 2 * hl, bn), F32)],
        compiler_params=_cparams(2),
        name="ffn_up_conv",
    )(h2, h2, h2, w_up, w_up, conv_w, conv_b)


def _ffn_down_kernel(u_ref, w_ref, x_ref, mod_ref, fw_ref, o_ref, *, final):
    m = pl.program_id(0)

    def gate(mod_row):
        return mod_ref[mod_row:mod_row + 1, 5 * D_MODEL:6 * D_MODEL]

    def run():
        acc = jnp.dot(u_ref[...], w_ref[...], preferred_element_type=F32)
        x_new = x_ref[...] + gate(0) * acc
        o_ref[...] = _rms(x_new, fw_ref[...]) if final else x_new
        return acc

    if final:
        @pl.when(m > 0)
        def _():
            run()
    else:
        acc = run()

        @pl.when(m == 0)
        def _():
            o_ref[0:CTX_LEN, :] = x_ref[0:CTX_LEN, :] + gate(1) * acc[0:CTX_LEN]


def _ffn_down(layer, u, w_down, xs, mod, final_w, final):
    bm = CTX_LEN if final else OUT_ROW_TILE
    row_spec = pl.BlockSpec((bm, D_MODEL), lambda m: (m, 0))
    if final:
        out_rows = SEQ
        out_spec = pl.BlockSpec((bm, D_MODEL), lambda m: (jnp.maximum(m - 1, 0), 0))
    else:
        out_rows, out_spec = T_ROWS, row_spec
    return pl.pallas_call(
        functools.partial(_ffn_down_kernel, final=final),
        out_shape=jax.ShapeDtypeStruct((out_rows, D_MODEL), F32),
        grid=(T_ROWS // bm,),
        in_specs=[
            pl.BlockSpec((bm, D_FF), lambda m: (m, 0)),
            pl.BlockSpec((None, D_FF, D_MODEL), lambda m: (layer, 0, 0),
                         pipeline_mode=pl.Buffered(1)),
            row_spec,
            pl.BlockSpec((None, SUBLANES, 6 * D_MODEL), lambda m: (layer, 0, 0)),
            pl.BlockSpec((1, D_MODEL), lambda m: (0, 0)),
        ],
        out_specs=out_spec,
        compiler_params=_cparams(1),
        name="ffn_down",
    )(u, w_down, xs, mod, final_w)


def _rope_tables():
    f32 = np.float32
    rows = SEQ // GRID_W
    row = np.repeat(np.arange(rows, dtype=f32), GRID_W)
    col = np.tile(np.arange(GRID_W, dtype=f32), rows)
    freqs = np.power(f32(ROPE_THETA), -np.arange(ROPE_FREQS, dtype=f32) / f32(ROPE_FREQS)).astype(f32)
    ar, ac = row[:, None] * freqs, col[:, None] * freqs
    cos = np.concatenate([np.cos(ar), np.cos(ar), np.cos(ac), np.cos(ac)], axis=1)
    sin = np.concatenate([np.sin(ar), np.sin(ar), np.sin(ac), np.sin(ac)], axis=1)
    cos = np.concatenate([np.ones((CTX_LEN, HEAD_DIM), f32), cos], axis=0)
    sin = np.concatenate([np.zeros((CTX_LEN, HEAD_DIM), f32), sin], axis=0)
    return jnp.asarray(cos, F32), jnp.asarray(sin, F32)


def kernel(x, c, ctx, c_ctx, w_mod, b_mod, norm1_w, norm2_w, w_in, q_norm_w, k_norm_w, hg_lb_logits,
           hg_norm_w, w_out, w_up, conv_w, conv_b, w_down, final_norm_w):
    assert x.shape == (1, SEQ, D_MODEL) and ctx.shape == (1, CTX_LEN, D_MODEL)
    xs = jnp.concatenate([ctx[0], x[0]], axis=0)
    mod = _modulation(jnp.stack([c[0], c_ctx], axis=1), w_mod, b_mod)
    cos_t, sin_t = _rope_tables()
    lb_sm = jax.nn.softmax(hg_lb_logits.astype(F32), axis=1)
    lb_all = jnp.cumsum(lb_sm, axis=1) - lb_sm[:, :1]

    w_in, w_out, w_up, w_down = (w.astype(BF16) for w in (w_in, w_out, w_up, w_down))
    for l in range(DEPTH):
        lb = jnp.concatenate([lb_all[0, l], lb_all[1, l]])[None, :]
        q, k, v, hq, kf, bf, kb, bb, hv, gate = _input_projection(
            l, xs, mod, norm1_w[l][None, :], w_in, q_norm_w[l][None, :],
            k_norm_w[l][None, :], lb, cos_t, sin_t)
        attn = _attention(q, k, v)
        o_f, o_b = _hgrn2(hq, kf, bf, kb, bb, hv)
        xs, h2 = _output_projection(l, attn, o_f, o_b, gate, hg_norm_w[l][None, :],
                                    w_out, xs, mod, norm2_w[l][None, :])
        u = _ffn_up(l, h2, w_up, conv_w[l], conv_b[l][None, :])
        xs = _ffn_down(l, u, w_down, xs, mod, final_norm_w[None, :], final=(l == DEPTH - 1))
    return xs[None]
```
